```python
import math
import jax
import jax.numpy as jnp
from jax import lax
import numpy as np

D_MODEL = 2048
BATCH = 1
SEQ = 8192
DEPTH = 4

GRID_W = 64
CTX_LEN = 256
EPS = 1e-6
POS_BASE = 10000.0

D_S5 = D_MODEL // 4
S5_GROUP = 16
S5_GROUPS = D_S5 // S5_GROUP
S5_STATE = 64
S5_DT_MIN = 1e-3
S5_DT_MAX = 1e-1

GLA_HEADS = 4
D_GLA_K = D_MODEL // 4
D_GLA_V = D_MODEL // 2
GLA_DK = D_GLA_K // GLA_HEADS
GLA_DV = D_GLA_V // GLA_HEADS
GLA_RANK = 16
GLA_TAU = 16.0
GLA_CHUNK = 64

D_HY = D_MODEL // 4
HY_SHORT = 3
HY_BANDS = 16
HY_FEAT = 1 + 2 * HY_BANDS
HY_HIDDEN = 64
HY_TARGET = 1e-2
HY_MIN_DECAY = -math.log(HY_TARGET) / 1.5
HY_MAX_DECAY = -math.log(HY_TARGET) / 0.3
HY_FILTER_SCALE = 0.05

D_MIX = D_S5 + D_GLA_V + D_HY
IN_NAMES = ('s5', 'q', 'k', 'v', 'gate', 'alpha', 'hy')
IN_WIDTHS = (D_S5, D_GLA_K, D_GLA_K, D_GLA_V, D_GLA_V, 2 * GLA_RANK, 3 * D_HY)
IN_WIDTH = sum(IN_WIDTHS)

PEER_HEADS = 8
PEER_KEYS = 128
PEER_EXPERTS = PEER_KEYS * PEER_KEYS
PEER_DQ = 256
PEER_TOPK = 16
PEER_BLOCK = 128

F32 = jnp.float32

kernel_name = 'hybrid_s5_gla_hyena_peer_dit'


def rmsnorm(x, g):
    xf = x.astype(F32)
    y = xf * lax.rsqrt(jnp.mean(jnp.square(xf), axis=-1, keepdims=True) + EPS)
    return (y * g.astype(F32)).astype(x.dtype)


def modulate(x, g, shift, scale):
    return rmsnorm(x, g) * (1.0 + scale) + shift


def grid_sincos(n_tokens, dim, dtype):
    rows = n_tokens // GRID_W
    row = jnp.repeat(jnp.arange(rows), GRID_W).astype(F32)
    col = jnp.tile(jnp.arange(GRID_W), rows).astype(F32)
    n_freq = dim // 4
    omega = 1.0 / (POS_BASE ** (jnp.arange(n_freq, dtype=F32) / n_freq))
    def enc(p):
        a = p[:, None] * omega[None, :]
        return jnp.concatenate([jnp.sin(a), jnp.cos(a)], axis=-1)
    return jnp.concatenate([enc(row), enc(col)], axis=-1).astype(dtype)


def split_in(p):
    cuts = np.cumsum(IN_WIDTHS)[:-1].tolist()
    return dict(zip(IN_NAMES, jnp.split(p, cuts, axis=-1)))


def _rev(t, axis, rev):
    return jnp.flip(t, axis) if rev else t


def _lin_combine(e1, e2):
    a1, b1 = e1
    a2, b2 = e2
    return a2 * a1, a2 * b1 + b2


def s5_discretise(a_re, a_im, log_step, b_re, b_im):
    lam = lax.complex(a_re.astype(F32), a_im.astype(F32))
    lam_bar = jnp.exp(lam * jnp.exp(log_step.astype(F32))[:, None])
    b = lax.complex(b_re.astype(F32), b_im.astype(F32))
    b_bar = ((lam_bar - 1.0) / lam)[..., None] * b
    return lam_bar, b_bar


def s5_scan(u, lam_bar, b_bar, h0=None):
    bu = lax.complex(jnp.einsum('blgc,gpc->blgp', u, b_bar.real),
                     jnp.einsum('blgc,gpc->blgp', u, b_bar.imag))
    if h0 is not None:
        bu = bu.at[:, 0].add(lam_bar * h0)
    a = jnp.broadcast_to(lam_bar, bu.shape)
    _, h = lax.associative_scan(_lin_combine, (a, bu), axis=1)
    return h


def s5_readout(h, c_re, c_im):
    return (jnp.einsum('blgp,gcp->blgc', h.real, c_re)
            - jnp.einsum('blgp,gcp->blgc', h.imag, c_im))


def s5_mixer(u_c, u_l, a_re, a_im, log_step, b_re, b_im, c_re, c_im, d, w_glu, b_glu, ctx_out):
    dtype = u_l.dtype
    grp = lambda t: t.astype(F32).reshape(t.shape[0], t.shape[1], S5_GROUPS, S5_GROUP)
    uc, ul = grp(u_c), grp(u_l)
    y_c, y_l = 0.0, 0.0
    for dr in range(2):
        rev = dr == 1
        lam_bar, b_bar = s5_discretise(a_re[dr], a_im[dr], log_step[dr], b_re[dr], b_im[dr])
        cr, ci = c_re[dr].astype(F32), c_im[dr].astype(F32)
        h_c = s5_scan(_rev(uc, 1, rev), lam_bar, b_bar)
        h_l = s5_scan(_rev(ul, 1, rev), lam_bar, b_bar, h_c[:, -1])
        y_l = y_l + _rev(s5_readout(h_l, cr, ci), 1, rev)
        if ctx_out:
            y_c = y_c + _rev(s5_readout(h_c, cr, ci), 1, rev)
    def finish(y, u):
        y = y.reshape(u.shape) + d.astype(F32) * u.astype(F32)
        z = jax.nn.gelu(y)
        return (z * jax.nn.sigmoid(z @ w_glu.astype(F32) + b_glu.astype(F32))).astype(dtype)
    return (finish(y_c, u_c) if ctx_out else None), finish(y_l, u_l)


def _heads(t, dh):
    B, n, _ = t.shape
    return t.astype(F32).reshape(B, n, GLA_HEADS, dh).transpose(0, 2, 1, 3)


def _gla_prep(p, w_alpha, b_alpha):
    B, n, _ = p['q'].shape
    q = _heads(p['q'], GLA_DK) * GLA_DK ** -0.5
    k = _heads(p['k'], GLA_DK)
    v = _heads(p['v'], GLA_DV)
    lr = p['alpha'].astype(F32).reshape(B, n, 2, GLA_RANK)
    logs = [_heads(jax.nn.log_sigmoid(lr[:, :, dr] @ w_alpha[dr].astype(F32)
                                      + b_alpha[dr].astype(F32)) / GLA_TAU, GLA_DK)
            for dr in range(2)]
    return q, k, v, logs


def gla_chunked(q, k, v, g, s0):
    B, H, n, dk = q.shape
    nc = n // GLA_CHUNK
    rs = lambda t: t.reshape(B, H, nc, GLA_CHUNK, t.shape[-1])
    q, k, v, g = rs(q), rs(k), rs(v), rs(g)
    b = jnp.cumsum(g, axis=3)
    b_last = b[:, :, :, -1:]
    q_d = q * jnp.exp(b)
    att = jnp.einsum('bhnik,bhnjk->bhnij', q_d, k * jnp.exp(-b))
    mask = jnp.tril(jnp.ones((GLA_CHUNK, GLA_CHUNK), dtype=bool))
    o_intra = jnp.einsum('bhnij,bhnjv->bhniv', jnp.where(mask, att, 0.0), v)
    d_state = jnp.einsum('bhnck,bhncv->nbhkv', k * jnp.exp(b_last - b), v)
    decay = jnp.moveaxis(jnp.exp(b_last[:, :, :, 0]), 2, 0)
    def step(s, inp):
        dec, ds = inp
        return dec[..., None] * s + ds, s
    s_fin, s_prev = lax.scan(step, s0, (decay, d_state))
    o_inter = jnp.einsum('bhnck,nbhkv->bhncv', q_d, s_prev)
    return (o_intra + o_inter).reshape(B, H, n, v.shape[-1]), s_fin


def gla_final_state(k, v, g):
    cum = jnp.cumsum(g, axis=2)
    return jnp.einsum('bhlk,bhlv->bhkv', k * jnp.exp(cum[:, :, -1:] - cum), v)


def gla_mixer(p_c, p_l, w_alpha, b_alpha, g_norm, ctx_out):
    dtype = p_l['v'].dtype
    qc, kc, vc, gc = _gla_prep(p_c, w_alpha, b_alpha)
    ql, kl, vl, gl = _gla_prep(p_l, w_alpha, b_alpha)
    o_c, o_l = 0.0, 0.0
    for dr in range(2):
        rev = dr == 1
        f = lambda t, rev=rev: _rev(t, 2, rev)
        if ctx_out:
            s0 = jnp.zeros(kc.shape[:2] + (GLA_DK, GLA_DV), F32)
            oc, s_ctx = gla_chunked(f(qc), f(kc), f(vc), f(gc[dr]), s0)
            o_c = o_c + f(oc)
        else:
            s_ctx = gla_final_state(f(kc), f(vc), f(gc[dr]))
        ol, _ = gla_chunked(f(ql), f(kl), f(vl), f(gl[dr]), s_ctx)
        o_l = o_l + f(ol)
    def finish(o, gate):
        B, H, n, _ = o.shape
        o = rmsnorm(o, g_norm).transpose(0, 2, 1, 3).reshape(B, n, D_GLA_V)
        return (o * jax.nn.silu(gate.astype(F32))).astype(dtype)
    return (finish(o_c, p_c['gate']) if ctx_out else None), finish(o_l, p_l['gate'])


def short_conv(x, w, b):
    C = x.shape[-1]
    y = lax.conv_general_dilated(x, w.astype(F32)[:, None, :], window_strides=(1,),
                                 padding=[(HY_SHORT // 2, HY_SHORT // 2)],
                                 dimension_numbers=('NWC', 'WIO', 'NWC'),
                                 feature_group_count=C)
    return y + b.astype(F32)


def hyena_filter(n, w1, b1, w2, b2, w3, freq, decay):
    t = jnp.linspace(0.0, 1.0, n, dtype=F32)[:, None]
    w = 2.0 * math.pi * jnp.arange(n, dtype=F32)[:, None] / n
    bands = jnp.linspace(1e-4, HY_BANDS - 1, HY_BANDS, dtype=F32)[None, :]
    z = jnp.concatenate([t, jnp.cos(bands * w), -jnp.sin(bands * w)], axis=-1)
    freq = freq.astype(F32)
    h = jnp.sin(freq[0] * (z @ w1.astype(F32) + b1.astype(F32)))
    h = jnp.sin(freq[1] * (h @ w2.astype(F32) + b2.astype(F32)))
    h = (h @ w3.astype(F32)).reshape(n, 2, D_HY)
    return h * jnp.exp(-t[:, :, None] * jnp.abs(decay.astype(F32))[None])


def bidir_fftconv(u, h):
    n = u.shape[1]
    k = jnp.concatenate([h[:, 0], jnp.zeros_like(h[:1, 1]), jnp.flip(h[1:, 1], 0)], axis=0)
    u_f = jnp.fft.rfft(u, n=2 * n, axis=1)
    k_f = jnp.fft.rfft(k, n=2 * n, axis=0)
    return jnp.fft.irfft(u_f * k_f[None], n=2 * n, axis=1)[:, :n]


def hyena_branch(p, conv_w, conv_b, w1, b1, w2, b2, w3, freq, decay, bias):
    z = short_conv(p.astype(F32), conv_w, conv_b)
    x0, x1, v = jnp.split(z, 3, axis=-1)
    h = hyena_filter(p.shape[1], w1, b1, w2, b2, w3, freq, decay)
    s = x1 * v
    return (x0 * (bidir_fftconv(s, h) + bias.astype(F32) * s)).astype(p.dtype)


def peer_ffn(x, w_q, keys, u_tab, v_tab):
    T = x.shape[0]
    dtype = x.dtype
    keys = keys.astype(F32)
    def block(xt):
        q = (xt @ w_q).astype(F32).reshape(PEER_BLOCK, PEER_HEADS, 2, PEER_DQ // 2)
        s = jnp.einsum('thsd,hskd->thsk', q, keys)
        s1, i1 = lax.top_k(s[:, :, 0], PEER_TOPK)
        s2, i2 = lax.top_k(s[:, :, 1], PEER_TOPK)
        cand = (s1[..., :, None] + s2[..., None, :]).reshape(PEER_BLOCK, PEER_HEADS, PEER_TOPK ** 2)
        cidx = (i1[..., :, None] * PEER_KEYS + i2[..., None, :]).reshape(PEER_BLOCK, PEER_HEADS, PEER_TOPK ** 2)
        top, pos = lax.top_k(cand, PEER_TOPK)
        idx = jnp.take_along_axis(cidx, pos, axis=-1).reshape(PEER_BLOCK, PEER_HEADS * PEER_TOPK)
        gate = jax.nn.softmax(top, axis=-1).reshape(PEER_BLOCK, PEER_HEADS * PEER_TOPK)
        act = jax.nn.gelu(jnp.einsum('td,ted->te', xt, u_tab[idx]).astype(F32))
        return jnp.einsum('te,ted->td', (gate * act).astype(dtype), v_tab[idx])
    out = lax.map(block, x.reshape(T // PEER_BLOCK, PEER_BLOCK, D_MODEL))
    return out.reshape(T, D_MODEL)


def setup_inputs(seed: int = 0) -> dict:
    key = jax.random.key(seed)
    ks = iter(jax.random.split(key, 48))
    nrm = lambda shape, s: s * jax.random.normal(next(ks), shape, F32)
    G, P, Cg = S5_GROUPS, S5_STATE, S5_GROUP
    lo, hi = math.log(S5_DT_MIN), math.log(S5_DT_MAX)
    return {
        'x': nrm((BATCH, SEQ, D_MODEL), 1.0),
        'c': nrm((BATCH, D_MODEL), 1.0),
        'ctx': nrm((BATCH, CTX_LEN, D_MODEL), 1.0),
        'c_ctx': nrm((D_MODEL,), 1.0),
        'w_ada': nrm((DEPTH, D_MODEL, 6 * D_MODEL), 0.5 * D_MODEL ** -0.5),
        'b_ada': nrm((DEPTH, 6 * D_MODEL), 0.01),
        'g_norm1': 1.0 + nrm((DEPTH, D_MODEL), 0.01),
        'g_norm2': 1.0 + nrm((DEPTH, D_MODEL), 0.01),
        'w_in': nrm((DEPTH, D_MODEL, IN_WIDTH), D_MODEL ** -0.5),
        's5_a_re': -0.5 + nrm((DEPTH, 2, G, P), 0.01),
        's5_a_im': math.pi * jnp.arange(P, dtype=F32) + nrm((DEPTH, 2, G, P), 0.01),
        's5_log_step': lo + (hi - lo) * jax.random.uniform(next(ks), (DEPTH, 2, G), F32),
        's5_b_re': nrm((DEPTH, 2, G, P, Cg), (2 * Cg) ** -0.5),
        's5_b_im': nrm((DEPTH, 2, G, P, Cg), (2 * Cg) ** -0.5),
        's5_c_re': nrm((DEPTH, 2, G, Cg, P), P ** -0.5),
        's5_c_im': nrm((DEPTH, 2, G, Cg, P), P ** -0.5),
        's5_d': nrm((DEPTH, D_S5), 1.0),
        's5_w_glu': nrm((DEPTH, D_S5, D_S5), D_S5 ** -0.5),
        's5_b_glu': nrm((DEPTH, D_S5), 0.01),
        'gla_w_alpha': nrm((DEPTH, 2, GLA_RANK, D_GLA_K), GLA_RANK ** -0.5),
        'gla_b_alpha': nrm((DEPTH, 2, D_GLA_K), 0.01),
        'gla_g_norm': 1.0 + nrm((DEPTH, GLA_DV), 0.01),
        'hy_conv_w': nrm((DEPTH, HY_SHORT, 3 * D_HY), HY_SHORT ** -0.5),
        'hy_conv_b': nrm((DEPTH, 3 * D_HY), 0.01),
        'hy_f_w1': nrm((DEPTH, HY_FEAT, HY_HIDDEN), HY_FEAT ** -0.5),
        'hy_f_b1': nrm((DEPTH, HY_HIDDEN), 0.1),
        'hy_f_w2': nrm((DEPTH, HY_HIDDEN, HY_HIDDEN), HY_HIDDEN ** -0.5),
        'hy_f_b2': nrm((DEPTH, HY_HIDDEN), 0.1),
        'hy_f_w3': nrm((DEPTH, HY_HIDDEN, 2 * D_HY), HY_FILTER_SCALE * HY_HIDDEN ** -0.5),
        'hy_f_freq': 1.0 + nrm((DEPTH, 2, HY_HIDDEN), 0.01),
        'hy_decay': jnp.linspace(HY_MIN_DECAY, HY_MAX_DECAY, D_HY, dtype=F32) + nrm((DEPTH, 2, D_HY), 0.01),
        'hy_bias': nrm((DEPTH, D_HY), 1.0),
        'w_out': nrm((DEPTH, D_MIX, D_MODEL), D_MIX ** -0.5),
        'peer_w_q': nrm((DEPTH, D_MODEL, PEER_HEADS * PEER_DQ), D_MODEL ** -0.5),
        'peer_keys': nrm((DEPTH, PEER_HEADS, 2, PEER_KEYS, PEER_DQ // 2), (PEER_DQ // 2) ** -0.5),
        'peer_u': nrm((DEPTH, PEER_EXPERTS, D_MODEL), D_MODEL ** -0.5),
        'peer_v': nrm((DEPTH, PEER_EXPERTS, D_MODEL), 1.0),
        'g_final': 1.0 + nrm((D_MODEL,), 0.01),
    }


def reference(x, c, ctx, c_ctx, w_ada, b_ada, g_norm1, g_norm2, w_in,
              s5_a_re, s5_a_im, s5_log_step, s5_b_re, s5_b_im, s5_c_re, s5_c_im,
              s5_d, s5_w_glu, s5_b_glu, gla_w_alpha, gla_b_alpha, gla_g_norm,
              hy_conv_w, hy_conv_b, hy_f_w1, hy_f_b1, hy_f_w2, hy_f_b2, hy_f_w3,
              hy_f_freq, hy_decay, hy_bias, w_out, peer_w_q, peer_keys, peer_u, peer_v,
              g_final):
    B, L, _ = x.shape
    Lc = ctx.shape[1]
    h_l = x + grid_sincos(L, D_MODEL, x.dtype)[None]
    h_c = ctx
    for l in range(DEPTH):
        last = l == DEPTH - 1
        m_l = (jax.nn.silu(c) @ w_ada[l] + b_ada[l])[:, None, :]
        m_c = (jax.nn.silu(c_ctx) @ w_ada[l] + b_ada[l])[None, None, :]
        sh1_l, sc1_l, gt1_l, sh2_l, sc2_l, gt2_l = jnp.split(m_l, 6, axis=-1)
        sh1_c, sc1_c, gt1_c, sh2_c, sc2_c, gt2_c = jnp.split(m_c, 6, axis=-1)

        p_l = split_in(modulate(h_l, g_norm1[l], sh1_l, sc1_l) @ w_in[l])
        p_c = split_in(modulate(h_c, g_norm1[l], sh1_c, sc1_c) @ w_in[l])
        s5_c, s5_l = s5_mixer(p_c['s5'], p_l['s5'], s5_a_re[l], s5_a_im[l], s5_log_step[l],
                              s5_b_re[l], s5_b_im[l], s5_c_re[l], s5_c_im[l], s5_d[l],
                              s5_w_glu[l], s5_b_glu[l], not last)
        gla_c, gla_l = gla_mixer(p_c, p_l, gla_w_alpha[l], gla_b_alpha[l], gla_g_norm[l], not last)
        hy_args = (hy_conv_w[l], hy_conv_b[l], hy_f_w1[l], hy_f_b1[l], hy_f_w2[l], hy_f_b2[l],
                   hy_f_w3[l], hy_f_freq[l], hy_decay[l], hy_bias[l])
        hy_l = hyena_branch(p_l['hy'], *hy_args)
        h_l = h_l + gt1_l * (jnp.concatenate([s5_l, gla_l, hy_l], axis=-1) @ w_out[l])
        f_in_l = modulate(h_l, g_norm2[l], sh2_l, sc2_l)

        if last:
            f_l = peer_ffn(f_in_l.reshape(B * L, D_MODEL), peer_w_q[l], peer_keys[l], peer_u[l], peer_v[l])
            h_l = h_l + gt2_l * f_l.reshape(B, L, D_MODEL)
        else:
            hy_c = hyena_branch(p_c['hy'], *hy_args)
            h_c = h_c + gt1_c * (jnp.concatenate([s5_c, gla_c, hy_c], axis=-1) @ w_out[l])
            f_in_c = modulate(h_c, g_norm2[l], sh2_c, sc2_c)
            tok = jnp.concatenate([f_in_c, f_in_l], axis=1).reshape(B * (Lc + L), D_MODEL)
            f = peer_ffn(tok, peer_w_q[l], peer_keys[l], peer_u[l], peer_v[l]).reshape(B, Lc + L, D_MODEL)
            h_c = h_c + gt2_c * f[:, :Lc]
            h_l = h_l + gt2_l * f[:, Lc:]
    return rmsnorm(h_l, g_final)
```

```python
import functools
import math

import jax
import jax.numpy as jnp
from jax import lax
from jax.experimental import pallas as pl
from jax.experimental.pallas import tpu as pltpu

F32 = jnp.float32
BF16 = jnp.bfloat16

D_MODEL = 2048
DEPTH = 4
GRID_W = 64
EPS = 1e-6
POS_BASE = 10000.0

D_S5 = D_MODEL // 4
S5_GROUP = 16
S5_GROUPS = D_S5 // S5_GROUP
S5_STATE = 64
S5_BLK_GROUPS = 8
S5_BLK_CH = S5_BLK_GROUPS * S5_GROUP
S5_BLK_ST = S5_BLK_GROUPS * S5_STATE
S5_NBLK = S5_GROUPS // S5_BLK_GROUPS

GLA_HEADS = 4
D_GLA_K = D_MODEL // 4
D_GLA_V = D_MODEL // 2
GLA_DK = D_GLA_K // GLA_HEADS
GLA_DV = D_GLA_V // GLA_HEADS
GLA_RANK = 16
GLA_TAU = 16.0
GLA_CHUNK = 64

D_HY = D_MODEL // 4
HY_SHORT = 3
HY_BANDS = 16
HY_FEAT = 1 + 2 * HY_BANDS
HY_HIDDEN = 64
HY_TARGET = 1e-2
HY_MIN_DECAY = -math.log(HY_TARGET) / 1.5
HY_MAX_DECAY = -math.log(HY_TARGET) / 0.3

PEER_HEADS = 8
PEER_KEYS = 128
PEER_EXPERTS = PEER_KEYS * PEER_KEYS
PEER_DQ = 256
PEER_TOPK = 16

OFF_V = 0
OFF_GATE = OFF_V + D_GLA_V
OFF_S5 = OFF_GATE + D_GLA_V
OFF_Q = OFF_S5 + D_S5
OFF_K = OFF_Q + D_GLA_K
OFF_HY = OFF_K + D_GLA_K
OFF_ALPHA = OFF_HY + 3 * D_HY
LANE = 128
IN_PACKED = OFF_ALPHA + 2 * LANE

SEQ_TILE = 256
VMEM_LIMIT = 56 * 1024 * 1024

NT_DIMS = (((1,), (1,)), ((), ()))
TN_DIMS = (((0,), (0,)), ((), ()))


def _cparams(sem):
    return pltpu.CompilerParams(dimension_semantics=sem, vmem_limit_bytes=VMEM_LIMIT)


def _dot(a, b):
    return jnp.dot(a, b, preferred_element_type=F32)


def _gelu(x):
    c = math.sqrt(2.0 / math.pi)
    return 0.5 * x * (1.0 + jnp.tanh(c * (x + 0.044715 * (x * x * x))))


def _sigmoid(x):
    return 1.0 / (1.0 + jnp.exp(-x))


def _norm_mod_matmul_kernel(x_ref, g_ref, mod_ref, w_ref, o_ref, *rest, ctx_len, tm, emit_xn):
    if emit_xn:
        xo_ref, xn_ref = rest
    else:
        (xn_ref,) = rest
    i = pl.program_id(0)
    j = pl.program_id(1)

    @pl.when(j == 0)
    def _():
        x = x_ref[...]
        ms = jnp.mean(x * x, axis=-1, keepdims=True)
        y = x * lax.rsqrt(ms + EPS) * g_ref[...]
        row = i * tm + lax.broadcasted_iota(jnp.int32, (tm, 1), 0)
        is_ctx = row < ctx_len
        shift = jnp.where(is_ctx, mod_ref[0:1, :], mod_ref[2:3, :])
        scale = jnp.where(is_ctx, mod_ref[1:2, :], mod_ref[3:4, :])
        xn = (y * (1.0 + scale) + shift).astype(BF16)
        xn_ref[...] = xn
        if emit_xn:
            xo_ref[...] = xn

    o_ref[...] = _dot(xn_ref[...], w_ref[...])


def norm_mod_matmul(x, g, mod, w, *, ctx_len, tm, tn, emit_xn=False):
    T, D = x.shape
    N = w.shape[1]
    out_shape = [jax.ShapeDtypeStruct((T, N), F32)]
    out_specs = [pl.BlockSpec((tm, tn), lambda i, j: (i, j))]
    if emit_xn:
        out_shape.append(jax.ShapeDtypeStruct((T, D), BF16))
        out_specs.append(pl.BlockSpec((tm, D), lambda i, j: (i, 0)))
    res = pl.pallas_call(
        functools.partial(_norm_mod_matmul_kernel, ctx_len=ctx_len, tm=tm, emit_xn=emit_xn),
        grid=(T // tm, N // tn),
        in_specs=[
            pl.BlockSpec((tm, D), lambda i, j: (i, 0)),
            pl.BlockSpec((1, D), lambda i, j: (0, 0)),
            pl.BlockSpec((4, D), lambda i, j: (0, 0)),
            pl.BlockSpec((D, tn), lambda i, j: (0, j)),
        ],
        out_specs=out_specs,
        out_shape=out_shape,
        scratch_shapes=[pltpu.VMEM((tm, D), BF16)],
        compiler_params=_cparams(("arbitrary", "arbitrary")),
    )(x, g.reshape(1, D), mod, w)
    return res if emit_xn else res[0]


def _matmul_gated_res_kernel(a_ref, w_ref, r_ref, gate_ref, o_ref, *, ctx_len, tm):
    i = pl.program_id(0)
    row = i * tm + lax.broadcasted_iota(jnp.int32, (tm, 1), 0)
    gate = jnp.where(row < ctx_len, gate_ref[0:1, :], gate_ref[1:2, :])
    o_ref[...] = r_ref[...] + gate * _dot(a_ref[...], w_ref[...])


def matmul_gated_res(a, w, res, gates, *, ctx_len, tm, tn):
    T, K = a.shape
    N = w.shape[1]
    return pl.pallas_call(
        functools.partial(_matmul_gated_res_kernel, ctx_len=ctx_len, tm=tm),
        grid=(T // tm, N // tn),
        in_specs=[
            pl.BlockSpec((tm, K), lambda i, j: (i, 0)),
            pl.BlockSpec((K, tn), lambda i, j: (0, j)),
            pl.BlockSpec((tm, tn), lambda i, j: (i, j)),
            pl.BlockSpec((2, tn), lambda i, j: (0, j)),
        ],
        out_specs=pl.BlockSpec((tm, tn), lambda i, j: (i, j)),
        out_shape=jax.ShapeDtypeStruct((T, N), F32),
        compiler_params=_cparams(("arbitrary", "arbitrary")),
    )(a, w, res, gates)


def _time_tile(t, n_tiles, n_ctx_tiles, rev):
    if not rev:
        return t
    return jnp.where(t < n_ctx_tiles, n_ctx_tiles - 1 - t, n_tiles - 1 - (t - n_ctx_tiles))


def _s5_kernel(u_ref, bm_ref, cm_ref, pw_ref, y_ref, hre_ref, him_ref, cre_ref, cim_ref, *, rev, lt):
    t = pl.program_id(1)
    ns = S5_BLK_ST

    @pl.when(t == 0)
    def _():
        cre_ref[...] = jnp.zeros_like(cre_ref)
        cim_ref[...] = jnp.zeros_like(cim_ref)

    bu = _dot(u_ref[...].astype(BF16), bm_ref[...])
    hre_ref[...] = bu[:, :ns]
    him_ref[...] = bu[:, ns:]

    p_re = pw_ref[0:8, :ns]
    p_im = pw_ref[0:8, ns:]
    lam = [(pw_ref[8 + k:9 + k, :ns], pw_ref[8 + k:9 + k, ns:]) for k in range(3)]
    row = lax.broadcasted_iota(jnp.int32, (8, ns), 0)
    n_grp = lt // 8

    def body(g, carry):
        c_re, c_im = carry
        gi = (n_grp - 1 - g) if rev else g
        rows = pl.ds(pl.multiple_of(gi * 8, 8), 8)
        a_re = hre_ref[rows, :]
        a_im = him_ref[rows, :]
        for k, s in enumerate((1, 2, 4)):
            l_re, l_im = lam[k]
            if rev:
                keep = row < 8 - s
                s_re = jnp.where(keep, pltpu.roll(a_re, 8 - s, 0), 0.0)
                s_im = jnp.where(keep, pltpu.roll(a_im, 8 - s, 0), 0.0)
            else:
                keep = row >= s
                s_re = jnp.where(keep, pltpu.roll(a_re, s, 0), 0.0)
                s_im = jnp.where(keep, pltpu.roll(a_im, s, 0), 0.0)
            a_re = a_re + (l_re * s_re - l_im * s_im)
            a_im = a_im + (l_re * s_im + l_im * s_re)
        a_re = a_re + (p_re * c_re - p_im * c_im)
        a_im = a_im + (p_re * c_im + p_im * c_re)
        hre_ref[rows, :] = a_re
        him_ref[rows, :] = a_im
        if rev:
            return a_re[0:1, :], a_im[0:1, :]
        return a_re[7:8, :], a_im[7:8, :]

    c_re, c_im = lax.fori_loop(0, n_grp, body, (cre_ref[...], cim_ref[...]))
    cre_ref[...] = c_re
    cim_ref[...] = c_im

    y_ref[...] = (_dot(hre_ref[...].astype(BF16), cm_ref[:ns, :])
                  + _dot(him_ref[...].astype(BF16), cm_ref[ns:, :]))


def s5_scan(p, bm, cm, pw, *, ctx_len, rev, lt=SEQ_TILE):
    T = p.shape[0]
    n_tiles = T // lt
    n_ctx = ctx_len // lt
    tt = functools.partial(_time_tile, n_tiles=n_tiles, n_ctx_tiles=n_ctx, rev=rev)
    return pl.pallas_call(
        functools.partial(_s5_kernel, rev=rev, lt=lt),
        grid=(S5_NBLK, n_tiles),
        in_specs=[
            pl.BlockSpec((lt, S5_BLK_CH), lambda b, t: (tt(t), OFF_S5 // S5_BLK_CH + b)),
            pl.BlockSpec((None, S5_BLK_CH, 2 * S5_BLK_ST), lambda b, t: (b, 0, 0)),
            pl.BlockSpec((None, 2 * S5_BLK_ST, S5_BLK_CH), lambda b, t: (b, 0, 0)),
            pl.BlockSpec((None, 16, 2 * S5_BLK_ST), lambda b, t: (b, 0, 0)),
        ],
        out_specs=pl.BlockSpec((lt, S5_BLK_CH), lambda b, t: (tt(t), b)),
        out_shape=jax.ShapeDtypeStruct((T, D_S5), F32),
        scratch_shapes=[
            pltpu.VMEM((lt, S5_BLK_ST), F32),
            pltpu.VMEM((lt, S5_BLK_ST), F32),
            pltpu.VMEM((1, S5_BLK_ST), F32),
            pltpu.VMEM((1, S5_BLK_ST), F32),
        ],
        compiler_params=_cparams(("arbitrary", "arbitrary")),
    )(p, bm, cm, pw)


def _cmul(a, b):
    return a[0] * b[0] - a[1] * b[1], a[0] * b[1] + a[1] * b[0]


def s5_prepare(a_re, a_im, log_step, b_re, b_im, c_re, c_im):
    G, P, Cg = S5_GROUPS, S5_STATE, S5_GROUP
    dt = jnp.exp(log_step)[..., None]
    er = jnp.exp(a_re * dt)
    lam1 = (er * jnp.cos(a_im * dt), er * jnp.sin(a_im * dt))
    den = a_re * a_re + a_im * a_im
    xr, xi = lam1[0] - 1.0, lam1[1]
    coef = ((xr * a_re + xi * a_im) / den, (xi * a_re - xr * a_im) / den)
    bb_re = coef[0][..., None] * b_re - coef[1][..., None] * b_im
    bb_im = coef[0][..., None] * b_im + coef[1][..., None] * b_re
    pows = [lam1]
    for _ in range(7):
        pows.append(_cmul(pows[-1], lam1))

    eye = jnp.eye(S5_BLK_GROUPS, dtype=F32)

    def blockdiag_in(m):
        m = m.reshape(2, S5_NBLK, S5_BLK_GROUPS, P, Cg)
        return jnp.einsum('dbgpc,gh->dbgchp', m, eye).reshape(2, S5_NBLK, S5_BLK_CH, S5_BLK_ST)

    def blockdiag_out(m):
        m = m.reshape(2, S5_NBLK, S5_BLK_GROUPS, Cg, P)
        return jnp.einsum('dbgcp,gh->dbgphc', m, eye).reshape(2, S5_NBLK, S5_BLK_ST, S5_BLK_CH)

    bm = jnp.concatenate([blockdiag_in(bb_re), blockdiag_in(bb_im)], axis=-1).astype(BF16)
    cm = jnp.concatenate([blockdiag_out(c_re), blockdiag_out(-c_im)], axis=-2).astype(BF16)

    def lay(v):
        return v.reshape(2, S5_NBLK, S5_BLK_ST)

    def table(order):
        rows = [jnp.concatenate([lay(pows[k][0]), lay(pows[k][1])], axis=-1) for k in order]
        rows += [jnp.concatenate([lay(pows[k][0]), lay(pows[k][1])], axis=-1) for k in (0, 1, 3)]
        rows += [jnp.zeros_like(rows[0])] * (16 - len(rows))
        return jnp.stack(rows, axis=2)

    pw_f = table(range(8))
    pw_b = table(range(7, -1, -1))
    return bm, cm, pw_f, pw_b


def _log_sigmoid(z):
    return jnp.minimum(z, 0.0) - jnp.log(1.0 + jnp.exp(-jnp.abs(z)))


def _gla_kernel(q_ref, k_ref, v_ref, al_ref, wa_ref, ba_ref, o_ref, st_ref, *, rev, lt):
    t = pl.program_id(1)
    C = GLA_CHUNK

    @pl.when(t == 0)
    def _():
        st_ref[...] = jnp.zeros_like(st_ref)

    z = _dot(al_ref[...].astype(BF16), wa_ref[...]) + ba_ref[...]
    g_all = _log_sigmoid(z) * (1.0 / GLA_TAU)
    ri = lax.broadcasted_iota(jnp.int32, (C, C), 0)
    ci = lax.broadcasted_iota(jnp.int32, (C, C), 1)
    tri = (ci >= ri) if rev else (ci <= ri)
    tri_b = jnp.where(tri, 1.0, 0.0).astype(BF16)
    n_chunks = lt // C
    order = range(n_chunks - 1, -1, -1) if rev else range(n_chunks)
    for c in order:
        rows = slice(c * C, (c + 1) * C)
        g = g_all[rows, :]
        g_hi = g.astype(BF16)
        g_lo = (g - g_hi.astype(F32)).astype(BF16)
        b = _dot(tri_b, g_hi) + _dot(tri_b, g_lo)
        b_tot = b[0:1, :] if rev else b[C - 1:C, :]
        q = q_ref[rows, :] * (GLA_DK ** -0.5)
        k = k_ref[rows, :]
        v = v_ref[rows, :].astype(BF16)
        q_d = (q * jnp.exp(b)).astype(BF16)
        k_d = (k * jnp.exp(-b)).astype(BF16)
        k_s = (k * jnp.exp(b_tot - b)).astype(BF16)
        att = lax.dot_general(q_d, k_d, NT_DIMS, preferred_element_type=F32)
        att = jnp.where(tri, att, 0.0).astype(BF16)
        s_t = st_ref[...]
        o = _dot(att, v) + lax.dot_general(q_d, s_t.astype(BF16), NT_DIMS, preferred_element_type=F32)
        o_ref[rows, :] = o
        st_ref[...] = s_t * jnp.exp(b_tot) + lax.dot_general(v, k_s, TN_DIMS, preferred_element_type=F32)


def gla_scan(p, wa, ba, *, ctx_len, rev, lt=SEQ_TILE):
    T = p.shape[0]
    n_tiles = T // lt
    n_ctx = ctx_len // lt
    tt = functools.partial(_time_tile, n_tiles=n_tiles, n_ctx_tiles=n_ctx, rev=rev)
    return pl.pallas_call(
        functools.partial(_gla_kernel, rev=rev, lt=lt),
        grid=(GLA_HEADS, n_tiles),
        in_specs=[
            pl.BlockSpec((lt, GLA_DK), lambda h, t: (tt(t), OFF_Q // GLA_DK + h)),
            pl.BlockSpec((lt, GLA_DK), lambda h, t: (tt(t), OFF_K // GLA_DK + h)),
            pl.BlockSpec((lt, GLA_DV), lambda h, t: (tt(t), OFF_V // GLA_DV + h)),
            pl.BlockSpec((lt, LANE), lambda h, t: (tt(t), OFF_ALPHA // LANE)),
            pl.BlockSpec((LANE, GLA_DK), lambda h, t: (0, h)),
            pl.BlockSpec((1, GLA_DK), lambda h, t: (0, h)),
        ],
        out_specs=pl.BlockSpec((lt, GLA_DV), lambda h, t: (tt(t), h)),
        out_shape=jax.ShapeDtypeStruct((T, D_GLA_V), F32),
        scratch_shapes=[pltpu.VMEM((GLA_DV, GLA_DK), F32)],
        compiler_params=_cparams(("arbitrary", "arbitrary")),
    )(p, p, p, p, wa, ba)


def _finish_kernel(u_ref, ys_f_ref, ys_b_ref, d_ref, wglu_ref, bglu_ref,
                   og_f_ref, og_b_ref, gate_ref, gn_ref,
                   x0_ref, conv_ref, s_ref, hb_ref, o_ref):
    u = u_ref[...]
    y = ys_f_ref[...] + ys_b_ref[...] + d_ref[...] * u
    zz = _gelu(y)
    s5 = zz * _sigmoid(_dot(zz.astype(BF16), wglu_ref[...]) + bglu_ref[...])
    o_ref[:, 0:D_S5] = s5.astype(BF16)

    gate = gate_ref[...]
    for h in range(GLA_HEADS):
        cols = slice(h * GLA_DV, (h + 1) * GLA_DV)
        o = og_f_ref[:, cols] + og_b_ref[:, cols]
        ms = jnp.mean(o * o, axis=-1, keepdims=True)
        on = o * lax.rsqrt(ms + EPS) * gn_ref[...]
        gt = gate[:, cols]
        o_ref[:, D_S5 + h * GLA_DV:D_S5 + (h + 1) * GLA_DV] = (on * (gt * _sigmoid(gt))).astype(BF16)

    s = s_ref[...]
    hy = x0_ref[...] * (conv_ref[...] + hb_ref[...] * s)
    o_ref[:, D_S5 + D_GLA_V:] = hy.astype(BF16)


def mixer_finish(p, ys_f, ys_b, s5_d, w_glu, b_glu, og_f, og_b, gn, x0, conv, s, hy_bias, *, tm):
    T = p.shape[0]
    row = lambda w, off=0: pl.BlockSpec((tm, w), lambda i: (i, off))
    const = lambda r, w: pl.BlockSpec((r, w), lambda i: (0, 0))
    return pl.pallas_call(
        _finish_kernel,
        grid=(T // tm,),
        in_specs=[
            row(D_S5, OFF_S5 // D_S5), row(D_S5), row(D_S5), const(1, D_S5), const(D_S5, D_S5), const(1, D_S5),
            row(D_GLA_V), row(D_GLA_V), row(D_GLA_V, OFF_GATE // D_GLA_V), const(1, GLA_DV),
            row(D_HY), row(D_HY), row(D_HY), const(1, D_HY),
        ],
        out_specs=pl.BlockSpec((tm, D_MODEL), lambda i: (i, 0)),
        out_shape=jax.ShapeDtypeStruct((T, D_MODEL), BF16),
        compiler_params=_cparams(("arbitrary",)),
    )(p, ys_f, ys_b, s5_d.reshape(1, D_S5), w_glu, b_glu.reshape(1, D_S5),
      og_f, og_b, p, gn.reshape(1, GLA_DV), x0, conv, s, hy_bias.reshape(1, D_HY))


def _peer_kernel(x_ref, u_ref, vt_ref, s2_ref, e2_ref, s1_ref, e1_ref, tau_ref, o_ref,
                 act_ref, p_ref, *, n_i1, tm, cw):
    j = pl.program_id(1)

    @pl.when(j == 0)
    def _():
        o_ref[...] = jnp.zeros_like(o_ref)

    act_ref[...] = lax.dot_general(u_ref[...], x_ref[...], NT_DIMS, preferred_element_type=F32)
    K = PEER_KEYS
    for a in range(n_i1):
        for cb in range(tm // cw):
            cols = slice(cb * cw, (cb + 1) * cw)
            w = jnp.zeros((K, cw), F32)
            for h in range(PEER_HEADS):
                r = a * PEER_HEADS + h
                tot = s1_ref[r:r + 1, cols] + s2_ref[h * K:(h + 1) * K, cols]
                val = e1_ref[r:r + 1, cols] * e2_ref[h * K:(h + 1) * K, cols]
                w = w + jnp.where(tot >= tau_ref[h:h + 1, cols], val, 0.0)
            act = act_ref[a * K:(a + 1) * K, cols]
            p_ref[a * K:(a + 1) * K, cols] = (w * _gelu(act)).astype(BF16)
    o_ref[...] += _dot(vt_ref[...], p_ref[...])


def peer_dense(xn, u_tab, vt_tab, s2t, e2t, s1r, e1r, taut, *, tm, te, cw=256):
    T, D = xn.shape
    E = u_tab.shape[0]
    n_i1 = te // PEER_KEYS
    HK = PEER_HEADS * PEER_KEYS
    return pl.pallas_call(
        functools.partial(_peer_kernel, n_i1=n_i1, tm=tm, cw=cw),
        grid=(T // tm, E // te),
        in_specs=[
            pl.BlockSpec((tm, D), lambda i, j: (i, 0)),
            pl.BlockSpec((te, D), lambda i, j: (j, 0)),
            pl.BlockSpec((D, te), lambda i, j: (0, j)),
            pl.BlockSpec((HK, tm), lambda i, j: (0, i)),
            pl.BlockSpec((HK, tm), lambda i, j: (0, i)),
            pl.BlockSpec((n_i1 * PEER_HEADS, tm), lambda i, j: (j, i)),
            pl.BlockSpec((n_i1 * PEER_HEADS, tm), lambda i, j: (j, i)),
            pl.BlockSpec((PEER_HEADS, tm), lambda i, j: (0, i)),
        ],
        out_specs=pl.BlockSpec((D, tm), lambda i, j: (0, i)),
        out_shape=jax.ShapeDtypeStruct((D, T), F32),
        scratch_shapes=[pltpu.VMEM((te, tm), F32), pltpu.VMEM((te, tm), BF16)],
        compiler_params=_cparams(("arbitrary", "arbitrary")),
    )(xn, u_tab, vt_tab, s2t, e2t, s1r, e1r, taut)


def _transpose_gated_res_kernel(h_ref, ft_ref, gate_ref, o_ref, *, ctx_len, tm):
    i = pl.program_id(0)
    row = i * tm + lax.broadcasted_iota(jnp.int32, (tm, 1), 0)
    gate = jnp.where(row < ctx_len, gate_ref[0:1, :], gate_ref[1:2, :])
    o_ref[...] = h_ref[...] + gate * ft_ref[...].T


def transpose_gated_res(h, ft, gates, *, ctx_len, tm):
    T, D = h.shape
    return pl.pallas_call(
        functools.partial(_transpose_gated_res_kernel, ctx_len=ctx_len, tm=tm),
        grid=(T // tm,),
        in_specs=[
            pl.BlockSpec((tm, D), lambda i: (i, 0)),
            pl.BlockSpec((D, tm), lambda i: (0, i)),
            pl.BlockSpec((2, D), lambda i: (0, 0)),
        ],
        out_specs=pl.BlockSpec((tm, D), lambda i: (i, 0)),
        out_shape=jax.ShapeDtypeStruct((T, D), F32),
        compiler_params=_cparams(("arbitrary",)),
    )(h, ft, gates)


def _rmsnorm_kernel(x_ref, g_ref, o_ref):
    x = x_ref[...]
    ms = jnp.mean(x * x, axis=-1, keepdims=True)
    o_ref[...] = x * lax.rsqrt(ms + EPS) * g_ref[...]


def rmsnorm_rows(x, g, *, tm):
    T, D = x.shape
    return pl.pallas_call(
        _rmsnorm_kernel,
        grid=(T // tm,),
        in_specs=[pl.BlockSpec((tm, D), lambda i: (i, 0)), pl.BlockSpec((1, D), lambda i: (0, 0))],
        out_specs=pl.BlockSpec((tm, D), lambda i: (i, 0)),
        out_shape=jax.ShapeDtypeStruct((T, D), F32),
        compiler_params=_cparams(("arbitrary",)),
    )(x, g.reshape(1, D))


HI = lax.Precision.HIGHEST


def grid_sincos(n_tokens, dim):
    rows = n_tokens // GRID_W
    row = jnp.repeat(jnp.arange(rows), GRID_W).astype(F32)
    col = jnp.tile(jnp.arange(GRID_W), rows).astype(F32)
    n_freq = dim // 4
    omega = 1.0 / (POS_BASE ** (jnp.arange(n_freq, dtype=F32) / n_freq))

    def enc(pp):
        a = pp[:, None] * omega[None, :]
        return jnp.concatenate([jnp.sin(a), jnp.cos(a)], axis=-1)

    return jnp.concatenate([enc(row), enc(col)], axis=-1)


def hyena_filter(n, w1, b1, w2, b2, w3, freq, decay):
    t = jnp.linspace(0.0, 1.0, n, dtype=F32)[:, None]
    w = 2.0 * math.pi * jnp.arange(n, dtype=F32)[:, None] / n
    bands = jnp.linspace(1e-4, HY_BANDS - 1, HY_BANDS, dtype=F32)[None, :]
    z = jnp.concatenate([t, jnp.cos(bands * w), -jnp.sin(bands * w)], axis=-1)
    h = jnp.sin(freq[0] * (jnp.dot(z, w1, precision=HI) + b1))
    h = jnp.sin(freq[1] * (jnp.dot(h, w2, precision=HI) + b2))
    h = jnp.dot(h, w3, precision=HI).reshape(n, 2, D_HY)
    return h * jnp.exp(-t[:, :, None] * jnp.abs(decay)[None])


def hyena_conv_parts(phy, conv_w, conv_b, filt):
    n = phy.shape[0]
    zero = jnp.zeros((1, phy.shape[1]), F32)
    prev = jnp.concatenate([zero, phy[:-1]], axis=0)
    nxt = jnp.concatenate([phy[1:], zero], axis=0)
    z = conv_w[0] * prev + conv_w[1] * phy + conv_w[2] * nxt + conv_b
    x0, x1, v = jnp.split(z, 3, axis=-1)
    s = x1 * v
    h = hyena_filter(n, *filt)
    k = jnp.concatenate([h[:, 0], jnp.zeros_like(h[:1, 1]), jnp.flip(h[1:, 1], 0)], axis=0)
    u_f = jnp.fft.rfft(s, n=2 * n, axis=0)
    k_f = jnp.fft.rfft(k, n=2 * n, axis=0)
    conv = jnp.fft.irfft(u_f * k_f, n=2 * n, axis=0)[:n]
    return x0, conv, s


def peer_scores(q, keys):
    T = q.shape[0]
    H, K, R = PEER_HEADS, PEER_KEYS, PEER_TOPK
    qh = q.reshape(T, H, 2, PEER_DQ // 2)
    s = jnp.einsum('thsd,hskd->thsk', qh, keys, precision=HI)
    s1, s2 = s[:, :, 0], s[:, :, 1]
    a, _ = lax.top_k(s1, R)
    b, _ = lax.top_k(s2, R)
    cand = (a[..., :, None] + b[..., None, :]).reshape(T, H, R * R)
    top, _ = lax.top_k(cand, R)
    tau = top[..., R - 1]
    m = top[..., 0:1]
    zsum = jnp.sum(jnp.exp(top - m), axis=-1)
    e1 = jnp.exp(s1 - a[..., 0:1]) / zsum[..., None]
    e2 = jnp.exp(s2 - b[..., 0:1])
    s2t = s2.reshape(T, H * K).T
    e2t = e2.reshape(T, H * K).T
    s1r = s1.transpose(2, 1, 0).reshape(K * H, T)
    e1r = e1.transpose(2, 1, 0).reshape(K * H, T)
    return s2t, e2t, s1r, e1r, tau.T


def _pack_w_in(w_in_l):
    widths = (D_S5, D_GLA_K, D_GLA_K, D_GLA_V, D_GLA_V, 2 * GLA_RANK, 3 * D_HY)
    offs = [0]
    for wd in widths:
        offs.append(offs[-1] + wd)
    s5, q, k, v, gate, alpha, hy = (w_in_l[:, offs[i]:offs[i + 1]] for i in range(7))
    pad = jnp.zeros((w_in_l.shape[0], IN_PACKED - OFF_ALPHA - 2 * GLA_RANK), w_in_l.dtype)
    return jnp.concatenate([v, gate, s5, q, k, hy, alpha, pad], axis=1).astype(BF16)


def kernel(x, c, ctx, c_ctx, w_ada, b_ada, g_norm1, g_norm2, w_in, s5_a_re, s5_a_im, s5_log_step, s5_b_re, s5_b_im, s5_c_re, s5_c_im, s5_d, s5_w_glu, s5_b_glu, gla_w_alpha, gla_b_alpha, gla_g_norm, hy_conv_w, hy_conv_b, hy_f_w1, hy_f_b1, hy_f_w2, hy_f_b2, hy_f_w3, hy_f_freq, hy_decay, hy_bias, w_out, peer_w_q, peer_keys, peer_u, peer_v, g_final):
    L = x.shape[1]
    Lc = ctx.shape[1]
    T = L + Lc
    TM = 768
    h = jnp.concatenate([ctx[0], x[0] + grid_sincos(L, D_MODEL)], axis=0)
    cc = jnp.stack([c_ctx, c[0]], axis=0)
    cc = cc * _sigmoid(cc)

    for l in range(DEPTH):
        m = jnp.dot(cc, w_ada[l], precision=HI) + b_ada[l]
        sh1, sc1, gt1, sh2, sc2, gt2 = jnp.split(m, 6, axis=-1)
        mod1 = jnp.stack([sh1[0], sc1[0], sh1[1], sc1[1]], axis=0)
        mod2 = jnp.stack([sh2[0], sc2[0], sh2[1], sc2[1]], axis=0)

        p = norm_mod_matmul(h, g_norm1[l], mod1, _pack_w_in(w_in[l]), ctx_len=Lc, tm=TM, tn=768)

        bm, cm, pw_f, pw_b = s5_prepare(s5_a_re[l], s5_a_im[l], s5_log_step[l], s5_b_re[l], s5_b_im[l],
                                        s5_c_re[l], s5_c_im[l])
        ys_f = s5_scan(p, bm[0], cm[0], pw_f[0], ctx_len=Lc, rev=False)
        ys_b = s5_scan(p, bm[1], cm[1], pw_b[1], ctx_len=Lc, rev=True)

        wa = jnp.zeros((2, LANE, D_GLA_K), F32)
        wa = wa.at[0, 0:GLA_RANK].set(gla_w_alpha[l, 0]).at[1, GLA_RANK:2 * GLA_RANK].set(gla_w_alpha[l, 1])
        wa = wa.astype(BF16)
        og_f = gla_scan(p, wa[0], gla_b_alpha[l, 0].reshape(1, D_GLA_K), ctx_len=Lc, rev=False)
        og_b = gla_scan(p, wa[1], gla_b_alpha[l, 1].reshape(1, D_GLA_K), ctx_len=Lc, rev=True)

        filt = (hy_f_w1[l], hy_f_b1[l], hy_f_w2[l], hy_f_b2[l], hy_f_w3[l], hy_f_freq[l], hy_decay[l])
        phy = p[:, OFF_HY:OFF_HY + 3 * D_HY]
        parts_c = hyena_conv_parts(phy[:Lc], hy_conv_w[l], hy_conv_b[l], filt)
        parts_l = hyena_conv_parts(phy[Lc:], hy_conv_w[l], hy_conv_b[l], filt)
        x0, conv, s = (jnp.concatenate([a_, b_], axis=0) for a_, b_ in zip(parts_c, parts_l))

        mix = mixer_finish(p, ys_f, ys_b, s5_d[l], s5_w_glu[l].astype(BF16), s5_b_glu[l],
                           og_f, og_b, gla_g_norm[l], x0, conv, s, hy_bias[l], tm=256)
        h = matmul_gated_res(mix, w_out[l].astype(BF16), h, gt1, ctx_len=Lc, tm=TM, tn=1024)

        q, xn = norm_mod_matmul(h, g_norm2[l], mod2, peer_w_q[l].astype(BF16),
                                ctx_len=Lc, tm=TM, tn=1024, emit_xn=True)
        s2t, e2t, s1r, e1r, taut = peer_scores(q, peer_keys[l])
        ft = peer_dense(xn, peer_u[l].astype(BF16), peer_v[l].T.astype(BF16),
                        s2t, e2t, s1r, e1r, taut, tm=TM, te=512)
        h = transpose_gated_res(h, ft, gt2, ctx_len=Lc, tm=256)

    out = rmsnorm_rows(h[Lc:], g_final, tm=512)
    return out[None]
```

```python
import functools
import math

import jax
import jax.numpy as jnp
from jax import lax
from jax.experimental import pallas as pl
from jax.experimental.pallas import tpu as pltpu

F32 = jnp.float32
BF16 = jnp.bfloat16

D_MODEL = 2048
DEPTH = 4
GRID_W = 64
EPS = 1e-6
POS_BASE = 10000.0

D_S5 = D_MODEL // 4
S5_GROUP = 16
S5_GROUPS = D_S5 // S5_GROUP
S5_STATE = 64
S5_BLK_GROUPS = 8
S5_BLK_CH = S5_BLK_GROUPS * S5_GROUP
S5_BLK_ST = S5_BLK_GROUPS * S5_STATE
S5_NBLK = S5_GROUPS // S5_BLK_GROUPS

GLA_HEADS = 4
D_GLA_K = D_MODEL // 4
D_GLA_V = D_MODEL // 2
GLA_DK = D_GLA_K // GLA_HEADS
GLA_DV = D_GLA_V // GLA_HEADS
GLA_RANK = 16
GLA_TAU = 16.0
GLA_CHUNK = 64

D_HY = D_MODEL // 4
HY_SHORT = 3
HY_BANDS = 16
HY_FEAT = 1 + 2 * HY_BANDS
HY_HIDDEN = 64
HY_TARGET = 1e-2
HY_MIN_DECAY = -math.log(HY_TARGET) / 1.5
HY_MAX_DECAY = -math.log(HY_TARGET) / 0.3

PEER_HEADS = 8
PEER_KEYS = 128
PEER_EXPERTS = PEER_KEYS * PEER_KEYS
PEER_DQ = 256
PEER_TOPK = 16

OFF_V = 0
OFF_GATE = OFF_V + D_GLA_V
OFF_S5 = OFF_GATE + D_GLA_V
OFF_Q = OFF_S5 + D_S5
OFF_K = OFF_Q + D_GLA_K
OFF_HY = OFF_K + D_GLA_K
OFF_ALPHA = OFF_HY + 3 * D_HY
LANE = 128
SUBLANE = 8
IN_PACKED = OFF_ALPHA + 2 * LANE

SEQ_TILE = 256
VMEM_LIMIT = 56 * 1024 * 1024

NT_DIMS = (((1,), (1,)), ((), ()))
TN_DIMS = (((0,), (0,)), ((), ()))


def _cparams(sem):
    return pltpu.CompilerParams(dimension_semantics=sem, vmem_limit_bytes=VMEM_LIMIT)


def _dot(a, b):
    return jnp.dot(a, b, preferred_element_type=F32)


def _gelu(x):
    c = math.sqrt(2.0 / math.pi)
    return 0.5 * x * (1.0 + jnp.tanh(c * (x + 0.044715 * (x * x * x))))


def _sigmoid(x):
    return 1.0 / (1.0 + jnp.exp(-x))


def _norm_mod_matmul_kernel(x_ref, g_ref, mod_ref, w_ref, o_ref, *rest, ctx_len, tm, emit_xn):
    if emit_xn:
        xo_ref, xn_ref = rest
    else:
        (xn_ref,) = rest
    i = pl.program_id(0)
    j = pl.program_id(1)

    @pl.when(j == 0)
    def _():
        x = x_ref[...]
        ms = jnp.mean(x * x, axis=-1, keepdims=True)
        y = x * lax.rsqrt(ms + EPS) * g_ref[...]
        row = i * tm + lax.broadcasted_iota(jnp.int32, (tm, 1), 0)
        is_ctx = row < ctx_len
        shift = jnp.where(is_ctx, mod_ref[0:1, :], mod_ref[2:3, :])
        scale = jnp.where(is_ctx, mod_ref[1:2, :], mod_ref[3:4, :])
        xn = (y * (1.0 + scale) + shift).astype(BF16)
        xn_ref[...] = xn
        if emit_xn:
            xo_ref[...] = xn

    o_ref[...] = _dot(xn_ref[...], w_ref[...])


def norm_mod_matmul(x, g, mod, w, *, ctx_len, tm, tn, emit_xn=False):
    T, D = x.shape
    N = w.shape[1]
    out_shape = [jax.ShapeDtypeStruct((T, N), F32)]
    out_specs = [pl.BlockSpec((tm, tn), lambda i, j: (i, j))]
    if emit_xn:
        out_shape.append(jax.ShapeDtypeStruct((T, D), BF16))
        out_specs.append(pl.BlockSpec((tm, D), lambda i, j: (i, 0)))
    res = pl.pallas_call(
        functools.partial(_norm_mod_matmul_kernel, ctx_len=ctx_len, tm=tm, emit_xn=emit_xn),
        grid=(T // tm, N // tn),
        in_specs=[
            pl.BlockSpec((tm, D), lambda i, j: (i, 0)),
            pl.BlockSpec((1, D), lambda i, j: (0, 0)),
            pl.BlockSpec((4, D), lambda i, j: (0, 0)),
            pl.BlockSpec((D, tn), lambda i, j: (0, j)),
        ],
        out_specs=out_specs,
        out_shape=out_shape,
        scratch_shapes=[pltpu.VMEM((tm, D), BF16)],
        compiler_params=_cparams(("arbitrary", "arbitrary")),
        name="norm_mod_matmul",
    )(x, g.reshape(1, D), mod, w)
    return res if emit_xn else res[0]


def _matmul_gated_res_kernel(a_ref, w_ref, r_ref, gate_ref, o_ref, *, ctx_len, tm):
    i = pl.program_id(0)
    row = i * tm + lax.broadcasted_iota(jnp.int32, (tm, 1), 0)
    gate = jnp.where(row < ctx_len, gate_ref[0:1, :], gate_ref[1:2, :])
    o_ref[...] = r_ref[...] + gate * _dot(a_ref[...], w_ref[...])


def matmul_gated_res(a, w, res, gates, *, ctx_len, tm, tn):
    T, K = a.shape
    N = w.shape[1]
    return pl.pallas_call(
        functools.partial(_matmul_gated_res_kernel, ctx_len=ctx_len, tm=tm),
        grid=(T // tm, N // tn),
        in_specs=[
            pl.BlockSpec((tm, K), lambda i, j: (i, 0)),
            pl.BlockSpec((K, tn), lambda i, j: (0, j)),
            pl.BlockSpec((tm, tn), lambda i, j: (i, j)),
            pl.BlockSpec((2, tn), lambda i, j: (0, j)),
        ],
        out_specs=pl.BlockSpec((tm, tn), lambda i, j: (i, j)),
        out_shape=jax.ShapeDtypeStruct((T, N), F32),
        compiler_params=_cparams(("arbitrary", "arbitrary")),
        name="matmul_gated_res",
    )(a, w, res, gates)


def _time_tile(t, n_tiles, n_ctx_tiles, rev):
    if not rev:
        return t
    return jnp.where(t < n_ctx_tiles, n_ctx_tiles - 1 - t, n_tiles - 1 - (t - n_ctx_tiles))


def _s5_kernel(u_ref, bm_ref, cm_ref, pw_ref, y_ref, hre_ref, him_ref, cre_ref, cim_ref, *, rev, lt):
    t = pl.program_id(1)
    ns = S5_BLK_ST

    @pl.when(t == 0)
    def _():
        cre_ref[...] = jnp.zeros_like(cre_ref)
        cim_ref[...] = jnp.zeros_like(cim_ref)

    bu = _dot(u_ref[...].astype(BF16), bm_ref[...])
    hre_ref[...] = bu[:, :ns]
    him_ref[...] = bu[:, ns:]

    p_re = pw_ref[0:8, :ns]
    p_im = pw_ref[0:8, ns:]
    lam = [(pw_ref[8 + k:9 + k, :ns], pw_ref[8 + k:9 + k, ns:]) for k in range(3)]
    row = lax.broadcasted_iota(jnp.int32, (8, ns), 0)
    n_grp = lt // 8

    def body(g, carry):
        c_re, c_im = carry
        gi = (n_grp - 1 - g) if rev else g
        rows = pl.ds(pl.multiple_of(gi * 8, 8), 8)
        a_re = hre_ref[rows, :]
        a_im = him_ref[rows, :]
        for k, s in enumerate((1, 2, 4)):
            l_re, l_im = lam[k]
            if rev:
                keep = row < 8 - s
                s_re = jnp.where(keep, pltpu.roll(a_re, 8 - s, 0), 0.0)
                s_im = jnp.where(keep, pltpu.roll(a_im, 8 - s, 0), 0.0)
            else:
                keep = row >= s
                s_re = jnp.where(keep, pltpu.roll(a_re, s, 0), 0.0)
                s_im = jnp.where(keep, pltpu.roll(a_im, s, 0), 0.0)
            a_re = a_re + (l_re * s_re - l_im * s_im)
            a_im = a_im + (l_re * s_im + l_im * s_re)
        a_re = a_re + (p_re * c_re - p_im * c_im)
        a_im = a_im + (p_re * c_im + p_im * c_re)
        hre_ref[rows, :] = a_re
        him_ref[rows, :] = a_im
        if rev:
            return a_re[0:1, :], a_im[0:1, :]
        return a_re[7:8, :], a_im[7:8, :]

    c_re, c_im = lax.fori_loop(0, n_grp, body, (cre_ref[...], cim_ref[...]))
    cre_ref[...] = c_re
    cim_ref[...] = c_im

    y_ref[...] = (_dot(hre_ref[...].astype(BF16), cm_ref[:ns, :])
                  + _dot(him_ref[...].astype(BF16), cm_ref[ns:, :]))


def s5_scan(p, bm, cm, pw, *, ctx_len, rev, lt=SEQ_TILE):
    T = p.shape[0]
    n_tiles = T // lt
    n_ctx = ctx_len // lt
    tt = functools.partial(_time_tile, n_tiles=n_tiles, n_ctx_tiles=n_ctx, rev=rev)
    return pl.pallas_call(
        functools.partial(_s5_kernel, rev=rev, lt=lt),
        grid=(S5_NBLK, n_tiles),
        in_specs=[
            pl.BlockSpec((lt, S5_BLK_CH), lambda b, t: (tt(t), OFF_S5 // S5_BLK_CH + b)),
            pl.BlockSpec((None, S5_BLK_CH, 2 * S5_BLK_ST), lambda b, t: (b, 0, 0)),
            pl.BlockSpec((None, 2 * S5_BLK_ST, S5_BLK_CH), lambda b, t: (b, 0, 0)),
            pl.BlockSpec((None, 16, 2 * S5_BLK_ST), lambda b, t: (b, 0, 0)),
        ],
        out_specs=pl.BlockSpec((lt, S5_BLK_CH), lambda b, t: (tt(t), b)),
        out_shape=jax.ShapeDtypeStruct((T, D_S5), F32),
        scratch_shapes=[
            pltpu.VMEM((lt, S5_BLK_ST), F32),
            pltpu.VMEM((lt, S5_BLK_ST), F32),
            pltpu.VMEM((1, S5_BLK_ST), F32),
            pltpu.VMEM((1, S5_BLK_ST), F32),
        ],
        compiler_params=_cparams(("arbitrary", "arbitrary")),
        name="s5_scan_rev" if rev else "s5_scan_fwd",
    )(p, bm, cm, pw)


def _cmul(a, b):
    return a[0] * b[0] - a[1] * b[1], a[0] * b[1] + a[1] * b[0]


def s5_prepare(a_re, a_im, log_step, b_re, b_im, c_re, c_im):
    G, P, Cg = S5_GROUPS, S5_STATE, S5_GROUP
    dt = jnp.exp(log_step)[..., None]
    er = jnp.exp(a_re * dt)
    lam1 = (er * jnp.cos(a_im * dt), er * jnp.sin(a_im * dt))
    den = a_re * a_re + a_im * a_im
    xr, xi = lam1[0] - 1.0, lam1[1]
    coef = ((xr * a_re + xi * a_im) / den, (xi * a_re - xr * a_im) / den)
    bb_re = coef[0][..., None] * b_re - coef[1][..., None] * b_im
    bb_im = coef[0][..., None] * b_im + coef[1][..., None] * b_re
    pows = [lam1]
    for _ in range(7):
        pows.append(_cmul(pows[-1], lam1))

    eye = jnp.eye(S5_BLK_GROUPS, dtype=F32)

    def blockdiag_in(m):
        m = m.reshape(2, S5_NBLK, S5_BLK_GROUPS, P, Cg)
        return jnp.einsum('dbgpc,gh->dbgchp', m, eye).reshape(2, S5_NBLK, S5_BLK_CH, S5_BLK_ST)

    def blockdiag_out(m):
        m = m.reshape(2, S5_NBLK, S5_BLK_GROUPS, Cg, P)
        return jnp.einsum('dbgcp,gh->dbgphc', m, eye).reshape(2, S5_NBLK, S5_BLK_ST, S5_BLK_CH)

    bm = jnp.concatenate([blockdiag_in(bb_re), blockdiag_in(bb_im)], axis=-1).astype(BF16)
    cm = jnp.concatenate([blockdiag_out(c_re), blockdiag_out(-c_im)], axis=-2).astype(BF16)

    def lay(v):
        return v.reshape(2, S5_NBLK, S5_BLK_ST)

    def table(order):
        rows = [jnp.concatenate([lay(pows[k][0]), lay(pows[k][1])], axis=-1) for k in order]
        rows += [jnp.concatenate([lay(pows[k][0]), lay(pows[k][1])], axis=-1) for k in (0, 1, 3)]
        rows += [jnp.zeros_like(rows[0])] * (16 - len(rows))
        return jnp.stack(rows, axis=2)

    pw_f = table(range(8))
    pw_b = table(range(7, -1, -1))
    return bm, cm, pw_f, pw_b


def _log_sigmoid(z):
    return jnp.minimum(z, 0.0) - jnp.log(1.0 + jnp.exp(-jnp.abs(z)))


def _gla_kernel(q_ref, k_ref, v_ref, al_ref, wa_ref, ba_ref, o_ref, st_ref, *, rev, lt):
    t = pl.program_id(1)
    C = GLA_CHUNK

    @pl.when(t == 0)
    def _():
        st_ref[...] = jnp.zeros_like(st_ref)

    z = _dot(al_ref[...].astype(BF16), wa_ref[...]) + ba_ref[...]
    g_all = _log_sigmoid(z) * (1.0 / GLA_TAU)
    ri = lax.broadcasted_iota(jnp.int32, (C, C), 0)
    ci = lax.broadcasted_iota(jnp.int32, (C, C), 1)
    tri = (ci >= ri) if rev else (ci <= ri)
    tri_b = jnp.where(tri, 1.0, 0.0).astype(BF16)
    n_chunks = lt // C
    order = range(n_chunks - 1, -1, -1) if rev else range(n_chunks)
    for c in order:
        rows = slice(c * C, (c + 1) * C)
        g = g_all[rows, :]
        g_hi = g.astype(BF16)
        g_lo = (g - g_hi.astype(F32)).astype(BF16)
        b = _dot(tri_b, g_hi) + _dot(tri_b, g_lo)
        b_tot = b[0:1, :] if rev else b[C - 1:C, :]
        q = q_ref[rows, :] * (GLA_DK ** -0.5)
        k = k_ref[rows, :]
        v = v_ref[rows, :].astype(BF16)
        q_d = (q * jnp.exp(b)).astype(BF16)
        k_d = (k * jnp.exp(-b)).astype(BF16)
        k_s = (k * jnp.exp(b_tot - b)).astype(BF16)
        att = lax.dot_general(q_d, k_d, NT_DIMS, preferred_element_type=F32)
        att = jnp.where(tri, att, 0.0).astype(BF16)
        s_t = st_ref[...]
        o = _dot(att, v) + lax.dot_general(q_d, s_t.astype(BF16), NT_DIMS, preferred_element_type=F32)
        o_ref[rows, :] = o
        st_ref[...] = s_t * jnp.exp(b_tot) + lax.dot_general(v, k_s, TN_DIMS, preferred_element_type=F32)


def gla_scan(p, wa, ba, *, ctx_len, rev, lt=SEQ_TILE):
    T = p.shape[0]
    n_tiles = T // lt
    n_ctx = ctx_len // lt
    tt = functools.partial(_time_tile, n_tiles=n_tiles, n_ctx_tiles=n_ctx, rev=rev)
    return pl.pallas_call(
        functools.partial(_gla_kernel, rev=rev, lt=lt),
        grid=(GLA_HEADS, n_tiles),
        in_specs=[
            pl.BlockSpec((lt, GLA_DK), lambda h, t: (tt(t), OFF_Q // GLA_DK + h)),
            pl.BlockSpec((lt, GLA_DK), lambda h, t: (tt(t), OFF_K // GLA_DK + h)),
            pl.BlockSpec((lt, GLA_DV), lambda h, t: (tt(t), OFF_V // GLA_DV + h)),
            pl.BlockSpec((lt, LANE), lambda h, t: (tt(t), OFF_ALPHA // LANE)),
            pl.BlockSpec((LANE, GLA_DK), lambda h, t: (0, h)),
            pl.BlockSpec((1, GLA_DK), lambda h, t: (0, h)),
        ],
        out_specs=pl.BlockSpec((lt, GLA_DV), lambda h, t: (tt(t), h)),
        out_shape=jax.ShapeDtypeStruct((T, D_GLA_V), F32),
        scratch_shapes=[pltpu.VMEM((GLA_DV, GLA_DK), F32)],
        compiler_params=_cparams(("arbitrary", "arbitrary")),
        name="gla_scan_rev" if rev else "gla_scan_fwd",
    )(p, p, p, p, wa, ba)


def _finish_kernel(u_ref, ys_f_ref, ys_b_ref, d_ref, wglu_ref, bglu_ref,
                   og_f_ref, og_b_ref, gate_ref, gn_ref,
                   x0_ref, conv_ref, s_ref, hb_ref, o_ref):
    u = u_ref[...]
    y = ys_f_ref[...] + ys_b_ref[...] + d_ref[...] * u
    zz = _gelu(y)
    s5 = zz * _sigmoid(_dot(zz.astype(BF16), wglu_ref[...]) + bglu_ref[...])
    o_ref[:, 0:D_S5] = s5.astype(BF16)

    gate = gate_ref[...]
    for h in range(GLA_HEADS):
        cols = slice(h * GLA_DV, (h + 1) * GLA_DV)
        o = og_f_ref[:, cols] + og_b_ref[:, cols]
        ms = jnp.mean(o * o, axis=-1, keepdims=True)
        on = o * lax.rsqrt(ms + EPS) * gn_ref[...]
        gt = gate[:, cols]
        o_ref[:, D_S5 + h * GLA_DV:D_S5 + (h + 1) * GLA_DV] = (on * (gt * _sigmoid(gt))).astype(BF16)

    s = s_ref[...]
    hy = x0_ref[...] * (conv_ref[...] + hb_ref[...] * s)
    o_ref[:, D_S5 + D_GLA_V:] = hy.astype(BF16)


def mixer_finish(p, ys_f, ys_b, s5_d, w_glu, b_glu, og_f, og_b, gn, x0, conv, s, hy_bias, *, tm):
    T = p.shape[0]
    row = lambda w, off=0: pl.BlockSpec((tm, w), lambda i: (i, off))
    const = lambda r, w: pl.BlockSpec((r, w), lambda i: (0, 0))
    return pl.pallas_call(
        _finish_kernel,
        grid=(T // tm,),
        in_specs=[
            row(D_S5, OFF_S5 // D_S5), row(D_S5), row(D_S5), const(1, D_S5), const(D_S5, D_S5), const(1, D_S5),
            row(D_GLA_V), row(D_GLA_V), row(D_GLA_V, OFF_GATE // D_GLA_V), const(1, GLA_DV),
            row(D_HY), row(D_HY), row(D_HY), const(1, D_HY),
        ],
        out_specs=pl.BlockSpec((tm, D_MODEL), lambda i: (i, 0)),
        out_shape=jax.ShapeDtypeStruct((T, D_MODEL), BF16),
        compiler_params=_cparams(("arbitrary",)),
        name="mixer_finish",
    )(p, ys_f, ys_b, s5_d.reshape(1, D_S5), w_glu, b_glu.reshape(1, D_S5),
      og_f, og_b, p, gn.reshape(1, GLA_DV), x0, conv, s, hy_bias.reshape(1, D_HY))


def _peer_kernel(x_ref, u_ref, vt_ref, s2_ref, e2_ref, s1_ref, e1_ref, tau_ref, o_ref,
                 act_ref, p_ref, *, n_i1, tm, cw):
    j = pl.program_id(1)

    @pl.when(j == 0)
    def _():
        o_ref[...] = jnp.zeros_like(o_ref)

    act_ref[...] = lax.dot_general(u_ref[...], x_ref[...], NT_DIMS, preferred_element_type=F32)
    K = PEER_KEYS
    sub = SUBLANE // n_i1
    base = (j % sub) * n_i1 if sub > 1 else 0
    for a in range(n_i1):
        r1 = pl.ds(base + a, 1)
        for cb in range(tm // cw):
            cols = slice(cb * cw, (cb + 1) * cw)
            w = jnp.zeros((K, cw), F32)
            for h in range(PEER_HEADS):
                tot = s1_ref[h, r1, cols] + s2_ref[h, :, cols]
                val = e1_ref[h, r1, cols] * e2_ref[h, :, cols]
                w = w + jnp.where(tot >= tau_ref[h, :, cols], val, 0.0)
            act = act_ref[a * K:(a + 1) * K, cols]
            p_ref[a * K:(a + 1) * K, cols] = (w * _gelu(act)).astype(BF16)
    o_ref[...] += _dot(vt_ref[...], p_ref[...])


def peer_dense(xn, u_tab, vt_tab, s1t, e1t, s2t, e2t, taut, *, tm, te, cw=256):
    T, D = xn.shape
    E = u_tab.shape[0]
    H, K = PEER_HEADS, PEER_KEYS
    n_i1 = te // K
    sub = SUBLANE // n_i1
    return pl.pallas_call(
        functools.partial(_peer_kernel, n_i1=n_i1, tm=tm, cw=cw),
        grid=(T // tm, E // te),
        in_specs=[
            pl.BlockSpec((tm, D), lambda i, j: (i, 0)),
            pl.BlockSpec((te, D), lambda i, j: (j, 0)),
            pl.BlockSpec((D, te), lambda i, j: (0, j)),
            pl.BlockSpec((H, K, tm), lambda i, j: (0, 0, i)),
            pl.BlockSpec((H, K, tm), lambda i, j: (0, 0, i)),
            pl.BlockSpec((H, SUBLANE, tm), lambda i, j: (0, j // sub, i)),
            pl.BlockSpec((H, SUBLANE, tm), lambda i, j: (0, j // sub, i)),
            pl.BlockSpec((H, 1, tm), lambda i, j: (0, 0, i)),
        ],
        out_specs=pl.BlockSpec((D, tm), lambda i, j: (0, i)),
        out_shape=jax.ShapeDtypeStruct((D, T), F32),
        scratch_shapes=[pltpu.VMEM((te, tm), F32), pltpu.VMEM((te, tm), BF16)],
        compiler_params=_cparams(("arbitrary", "arbitrary")),
        name="peer_dense",
    )(xn, u_tab, vt_tab, s2t, e2t, s1t, e1t, taut)


NEG_BIG = -3.0e38
N_CAND = PEER_TOPK + 8 * 7 + 8


def _peer_prep_kernel(q_ref, k_ref, s1_ref, e1_ref, s2_ref, e2_ref, tau_ref, top_ref, cand_ref):
    R = PEER_TOPK
    half_w = PEER_DQ // 2

    def nt(a, b):
        return lax.dot_general(a, b, NT_DIMS, preferred_element_type=F32)

    def scores(half):
        qh = q_ref[:, half * half_w:(half + 1) * half_w]
        q_hi = qh.astype(BF16)
        q_lo = (qh - q_hi.astype(F32)).astype(BF16)
        kk = k_ref[half]
        k_hi = kk.astype(BF16)
        k_lo = (kk - k_hi.astype(F32)).astype(BF16)
        return nt(k_hi, q_hi) + (nt(k_hi, q_lo) + nt(k_lo, q_hi))

    def sorted_top(s, slot):
        work = s
        for r in range(R):
            m = jnp.max(work, axis=0, keepdims=True)
            top_ref[slot, r:r + 1, :] = m
            work = jnp.where(work == m, NEG_BIG, work)

    s1 = scores(0)
    s2 = scores(1)
    sorted_top(s1, 0)
    sorted_top(s2, 1)
    a = top_ref[0]
    b = top_ref[1]
    cand_ref[0:R, :] = a[0:1, :] + b
    for i in range(1, 8):
        cand_ref[R + 8 * (i - 1):R + 8 * i, :] = a[i:i + 1, :] + b[0:8, :]
    cand_ref[R + 56:R + 64, :] = a[8:16, :] + b[0:1, :]
    work = cand_ref[...]
    m0 = jnp.max(work, axis=0, keepdims=True)
    m = m0
    z = jnp.ones_like(m0)
    for r in range(1, R):
        work = jnp.where(work == m, NEG_BIG, work)
        m = jnp.max(work, axis=0, keepdims=True)
        z = z + jnp.exp(m - m0)
    tau_ref[0] = m
    s1_ref[0] = s1
    s2_ref[0] = s2
    e1_ref[0] = jnp.exp(s1 - a[0:1, :]) * (1.0 / z)
    e2_ref[0] = jnp.exp(s2 - b[0:1, :])


def peer_prep(q, keys, *, tm):
    T = q.shape[0]
    H, K = PEER_HEADS, PEER_KEYS
    tab = jax.ShapeDtypeStruct((H, K, T), F32)
    tab_spec = pl.BlockSpec((1, K, tm), lambda i, h: (h, 0, i))
    return pl.pallas_call(
        _peer_prep_kernel,
        grid=(T // tm, H),
        in_specs=[
            pl.BlockSpec((tm, PEER_DQ), lambda i, h: (i, h)),
            pl.BlockSpec((None, 2, K, PEER_DQ // 2), lambda i, h: (h, 0, 0, 0)),
        ],
        out_specs=[tab_spec, tab_spec, tab_spec, tab_spec, pl.BlockSpec((1, 1, tm), lambda i, h: (h, 0, i))],
        out_shape=[tab, tab, tab, tab, jax.ShapeDtypeStruct((H, 1, T), F32)],
        scratch_shapes=[pltpu.VMEM((2, PEER_TOPK, tm), F32), pltpu.VMEM((N_CAND, tm), F32)],
        compiler_params=_cparams(("arbitrary", "arbitrary")),
        name="peer_prep",
    )(q, keys)


def _transpose_gated_res_kernel(h_ref, ft_ref, gate_ref, o_ref, *, ctx_len, tm):
    i = pl.program_id(0)
    row = i * tm + lax.broadcasted_iota(jnp.int32, (tm, 1), 0)
    gate = jnp.where(row < ctx_len, gate_ref[0:1, :], gate_ref[1:2, :])
    o_ref[...] = h_ref[...] + gate * ft_ref[...].T


def transpose_gated_res(h, ft, gates, *, ctx_len, tm):
    T, D = h.shape
    return pl.pallas_call(
        functools.partial(_transpose_gated_res_kernel, ctx_len=ctx_len, tm=tm),
        grid=(T // tm,),
        in_specs=[
            pl.BlockSpec((tm, D), lambda i: (i, 0)),
            pl.BlockSpec((D, tm), lambda i: (0, i)),
            pl.BlockSpec((2, D), lambda i: (0, 0)),
        ],
        out_specs=pl.BlockSpec((tm, D), lambda i: (i, 0)),
        out_shape=jax.ShapeDtypeStruct((T, D), F32),
        compiler_params=_cparams(("arbitrary",)),
        name="transpose_gated_res",
    )(h, ft, gates)


def _rmsnorm_kernel(x_ref, g_ref, o_ref):
    x = x_ref[...]
    ms = jnp.mean(x * x, axis=-1, keepdims=True)
    o_ref[...] = x * lax.rsqrt(ms + EPS) * g_ref[...]


def rmsnorm_rows(x, g, *, tm):
    T, D = x.shape
    return pl.pallas_call(
        _rmsnorm_kernel,
        grid=(T // tm,),
        in_specs=[pl.BlockSpec((tm, D), lambda i: (i, 0)), pl.BlockSpec((1, D), lambda i: (0, 0))],
        out_specs=pl.BlockSpec((tm, D), lambda i: (i, 0)),
        out_shape=jax.ShapeDtypeStruct((T, D), F32),
        compiler_params=_cparams(("arbitrary",)),
        name="rmsnorm_rows",
    )(x, g.reshape(1, D))


HI = lax.Precision.HIGHEST


def grid_sincos(n_tokens, dim):
    rows = n_tokens // GRID_W
    row = jnp.repeat(jnp.arange(rows), GRID_W).astype(F32)
    col = jnp.tile(jnp.arange(GRID_W), rows).astype(F32)
    n_freq = dim // 4
    omega = 1.0 / (POS_BASE ** (jnp.arange(n_freq, dtype=F32) / n_freq))

    def enc(pp):
        a = pp[:, None] * omega[None, :]
        return jnp.concatenate([jnp.sin(a), jnp.cos(a)], axis=-1)

    return jnp.concatenate([enc(row), enc(col)], axis=-1)


def hyena_filter(n, w1, b1, w2, b2, w3, freq, decay):
    t = jnp.linspace(0.0, 1.0, n, dtype=F32)[:, None]
    w = 2.0 * math.pi * jnp.arange(n, dtype=F32)[:, None] / n
    bands = jnp.linspace(1e-4, HY_BANDS - 1, HY_BANDS, dtype=F32)[None, :]
    z = jnp.concatenate([t, jnp.cos(bands * w), -jnp.sin(bands * w)], axis=-1)
    h = jnp.sin(freq[0] * (jnp.dot(z, w1, precision=HI) + b1))
    h = jnp.sin(freq[1] * (jnp.dot(h, w2, precision=HI) + b2))
    h = jnp.dot(h, w3, precision=HI).reshape(n, 2, D_HY)
    return h * jnp.exp(-t[:, :, None] * jnp.abs(decay)[None])


def hyena_conv_parts(phy, conv_w, conv_b, filt):
    n = phy.shape[0]
    zero = jnp.zeros((1, phy.shape[1]), F32)
    prev = jnp.concatenate([zero, phy[:-1]], axis=0)
    nxt = jnp.concatenate([phy[1:], zero], axis=0)
    z = conv_w[0] * prev + conv_w[1] * phy + conv_w[2] * nxt + conv_b
    x0, x1, v = jnp.split(z, 3, axis=-1)
    s = x1 * v
    h = hyena_filter(n, *filt)
    k = jnp.concatenate([h[:, 0], jnp.zeros_like(h[:1, 1]), jnp.flip(h[1:, 1], 0)], axis=0)
    u_f = jnp.fft.rfft(s, n=2 * n, axis=0)
    k_f = jnp.fft.rfft(k, n=2 * n, axis=0)
    conv = jnp.fft.irfft(u_f * k_f, n=2 * n, axis=0)[:n]
    return x0, conv, s


def _pack_w_in(w_in_l):
    widths = (D_S5, D_GLA_K, D_GLA_K, D_GLA_V, D_GLA_V, 2 * GLA_RANK, 3 * D_HY)
    offs = [0]
    for wd in widths:
        offs.append(offs[-1] + wd)
    s5, q, k, v, gate, alpha, hy = (w_in_l[:, offs[i]:offs[i + 1]] for i in range(7))
    pad = jnp.zeros((w_in_l.shape[0], IN_PACKED - OFF_ALPHA - 2 * GLA_RANK), w_in_l.dtype)
    return jnp.concatenate([v, gate, s5, q, k, hy, alpha, pad], axis=1).astype(BF16)


def kernel(x, c, ctx, c_ctx, w_ada, b_ada, g_norm1, g_norm2, w_in, s5_a_re, s5_a_im, s5_log_step, s5_b_re, s5_b_im, s5_c_re, s5_c_im, s5_d, s5_w_glu, s5_b_glu, gla_w_alpha, gla_b_alpha, gla_g_norm, hy_conv_w, hy_conv_b, hy_f_w1, hy_f_b1, hy_f_w2, hy_f_b2, hy_f_w3, hy_f_freq, hy_decay, hy_bias, w_out, peer_w_q, peer_keys, peer_u, peer_v, g_final):
    L = x.shape[1]
    Lc = ctx.shape[1]
    T = L + Lc
    TM = 768
    h = jnp.concatenate([ctx[0], x[0] + grid_sincos(L, D_MODEL)], axis=0)
    cc = jnp.stack([c_ctx, c[0]], axis=0)
    cc = cc * _sigmoid(cc)

    for l in range(DEPTH):
        m = jnp.dot(cc, w_ada[l], precision=HI) + b_ada[l]
        sh1, sc1, gt1, sh2, sc2, gt2 = jnp.split(m, 6, axis=-1)
        mod1 = jnp.stack([sh1[0], sc1[0], sh1[1], sc1[1]], axis=0)
        mod2 = jnp.stack([sh2[0], sc2[0], sh2[1], sc2[1]], axis=0)

        p = norm_mod_matmul(h, g_norm1[l], mod1, _pack_w_in(w_in[l]), ctx_len=Lc, tm=TM, tn=768)

        bm, cm, pw_f, pw_b = s5_prepare(s5_a_re[l], s5_a_im[l], s5_log_step[l], s5_b_re[l], s5_b_im[l],
                                        s5_c_re[l], s5_c_im[l])
        ys_f = s5_scan(p, bm[0], cm[0], pw_f[0], ctx_len=Lc, rev=False)
        ys_b = s5_scan(p, bm[1], cm[1], pw_b[1], ctx_len=Lc, rev=True)

        wa = jnp.zeros((2, LANE, D_GLA_K), F32)
        wa = wa.at[0, 0:GLA_RANK].set(gla_w_alpha[l, 0]).at[1, GLA_RANK:2 * GLA_RANK].set(gla_w_alpha[l, 1])
        wa = wa.astype(BF16)
        og_f = gla_scan(p, wa[0], gla_b_alpha[l, 0].reshape(1, D_GLA_K), ctx_len=Lc, rev=False)
        og_b = gla_scan(p, wa[1], gla_b_alpha[l, 1].reshape(1, D_GLA_K), ctx_len=Lc, rev=True)

        filt = (hy_f_w1[l], hy_f_b1[l], hy_f_w2[l], hy_f_b2[l], hy_f_w3[l], hy_f_freq[l], hy_decay[l])
        phy = p[:, OFF_HY:OFF_HY + 3 * D_HY]
        parts_c = hyena_conv_parts(phy[:Lc], hy_conv_w[l], hy_conv_b[l], filt)
        parts_l = hyena_conv_parts(phy[Lc:], hy_conv_w[l], hy_conv_b[l], filt)
        x0, conv, s = (jnp.concatenate([a_, b_], axis=0) for a_, b_ in zip(parts_c, parts_l))

        mix = mixer_finish(p, ys_f, ys_b, s5_d[l], s5_w_glu[l].astype(BF16), s5_b_glu[l],
                           og_f, og_b, gla_g_norm[l], x0, conv, s, hy_bias[l], tm=256)
        h = matmul_gated_res(mix, w_out[l].astype(BF16), h, gt1, ctx_len=Lc, tm=TM, tn=1024)

        q, xn = norm_mod_matmul(h, g_norm2[l], mod2, peer_w_q[l].astype(BF16),
                                ctx_len=Lc, tm=TM, tn=1024, emit_xn=True)
        s1t, e1t, s2t, e2t, taut = peer_prep(q, peer_keys[l], tm=TM)
        ft = peer_dense(xn, peer_u[l].astype(BF16), peer_v[l].T.astype(BF16),
                        s1t, e1t, s2t, e2t, taut, tm=TM, te=512)
        h = transpose_gated_res(h, ft, gt2, ctx_len=Lc, tm=256)

    out = rmsnorm_rows(h[Lc:], g_final, tm=512)
    return out[None]
```

```python
import functools
import math

import jax
import jax.numpy as jnp
import numpy as np
from jax import lax
from jax.experimental import pallas as pl
from jax.experimental.pallas import tpu as pltpu

F32 = jnp.float32
BF16 = jnp.bfloat16

D_MODEL = 2048
DEPTH = 4
GRID_W = 64
EPS = 1e-6
POS_BASE = 10000.0

D_S5 = D_MODEL // 4
S5_GROUP = 16
S5_GROUPS = D_S5 // S5_GROUP
S5_STATE = 64
S5_BLK_GROUPS = 8
S5_BLK_CH = S5_BLK_GROUPS * S5_GROUP
S5_BLK_ST = S5_BLK_GROUPS * S5_STATE
S5_NBLK = S5_GROUPS // S5_BLK_GROUPS

GLA_HEADS = 4
D_GLA_K = D_MODEL // 4
D_GLA_V = D_MODEL // 2
GLA_DK = D_GLA_K // GLA_HEADS
GLA_DV = D_GLA_V // GLA_HEADS
GLA_RANK = 16
GLA_TAU = 16.0
GLA_CHUNK = 64

D_HY = D_MODEL // 4
HY_SHORT = 3
HY_BANDS = 16
HY_FEAT = 1 + 2 * HY_BANDS
HY_HIDDEN = 64
HY_TARGET = 1e-2
HY_MIN_DECAY = -math.log(HY_TARGET) / 1.5
HY_MAX_DECAY = -math.log(HY_TARGET) / 0.3

PEER_HEADS = 8
PEER_KEYS = 128
PEER_EXPERTS = PEER_KEYS * PEER_KEYS
PEER_DQ = 256
PEER_TOPK = 16

OFF_V = 0
OFF_GATE = OFF_V + D_GLA_V
OFF_S5 = OFF_GATE + D_GLA_V
OFF_Q = OFF_S5 + D_S5
OFF_K = OFF_Q + D_GLA_K
OFF_HY = OFF_K + D_GLA_K
OFF_ALPHA = OFF_HY + 3 * D_HY
LANE = 128
SUBLANE = 8
IN_PACKED = OFF_ALPHA + 2 * LANE

SEQ_TILE = 256
VMEM_LIMIT = 56 * 1024 * 1024

NT_DIMS = (((1,), (1,)), ((), ()))
TN_DIMS = (((0,), (0,)), ((), ()))


def _cparams(sem):
    return pltpu.CompilerParams(dimension_semantics=sem, vmem_limit_bytes=VMEM_LIMIT)


def _dot(a, b):
    return jnp.dot(a, b, preferred_element_type=F32)


def _gelu(x):
    c = math.sqrt(2.0 / math.pi)
    return 0.5 * x * (1.0 + jnp.tanh(c * (x + 0.044715 * (x * x * x))))


def _sigmoid(x):
    return 1.0 / (1.0 + jnp.exp(-x))


def _norm_mod_matmul_kernel(x_ref, g_ref, mod_ref, w_ref, o_ref, *rest, ctx_len, tm, emit_xn):
    if emit_xn:
        xo_ref, xn_ref = rest
    else:
        (xn_ref,) = rest
    i = pl.program_id(0)
    j = pl.program_id(1)

    @pl.when(j == 0)
    def _():
        x = x_ref[...]
        ms = jnp.mean(x * x, axis=-1, keepdims=True)
        y = x * lax.rsqrt(ms + EPS) * g_ref[...]
        row = i * tm + lax.broadcasted_iota(jnp.int32, (tm, 1), 0)
        is_ctx = row < ctx_len
        shift = jnp.where(is_ctx, mod_ref[0:1, :], mod_ref[2:3, :])
        scale = jnp.where(is_ctx, mod_ref[1:2, :], mod_ref[3:4, :])
        xn = (y * (1.0 + scale) + shift).astype(BF16)
        xn_ref[...] = xn
        if emit_xn:
            xo_ref[...] = xn

    o_ref[...] = _dot(xn_ref[...], w_ref[...])


def norm_mod_matmul(x, g, mod, w, *, ctx_len, tm, tn, emit_xn=False):
    T, D = x.shape
    N = w.shape[1]
    out_shape = [jax.ShapeDtypeStruct((T, N), F32)]
    out_specs = [pl.BlockSpec((tm, tn), lambda i, j: (i, j))]
    if emit_xn:
        out_shape.append(jax.ShapeDtypeStruct((T, D), BF16))
        out_specs.append(pl.BlockSpec((tm, D), lambda i, j: (i, 0)))
    res = pl.pallas_call(
        functools.partial(_norm_mod_matmul_kernel, ctx_len=ctx_len, tm=tm, emit_xn=emit_xn),
        grid=(T // tm, N // tn),
        in_specs=[
            pl.BlockSpec((tm, D), lambda i, j: (i, 0)),
            pl.BlockSpec((1, D), lambda i, j: (0, 0)),
            pl.BlockSpec((4, D), lambda i, j: (0, 0)),
            pl.BlockSpec((D, tn), lambda i, j: (0, j)),
        ],
        out_specs=out_specs,
        out_shape=out_shape,
        scratch_shapes=[pltpu.VMEM((tm, D), BF16)],
        compiler_params=_cparams(("arbitrary", "arbitrary")),
        name="norm_mod_matmul",
    )(x, g.reshape(1, D), mod, w)
    return res if emit_xn else res[0]


def _matmul_gated_res_kernel(a_ref, w_ref, r_ref, gate_ref, o_ref, *, ctx_len, tm):
    i = pl.program_id(0)
    row = i * tm + lax.broadcasted_iota(jnp.int32, (tm, 1), 0)
    gate = jnp.where(row < ctx_len, gate_ref[0:1, :], gate_ref[1:2, :])
    o_ref[...] = r_ref[...] + gate * _dot(a_ref[...], w_ref[...])


def matmul_gated_res(a, w, res, gates, *, ctx_len, tm, tn):
    T, K = a.shape
    N = w.shape[1]
    return pl.pallas_call(
        functools.partial(_matmul_gated_res_kernel, ctx_len=ctx_len, tm=tm),
        grid=(T // tm, N // tn),
        in_specs=[
            pl.BlockSpec((tm, K), lambda i, j: (i, 0)),
            pl.BlockSpec((K, tn), lambda i, j: (0, j)),
            pl.BlockSpec((tm, tn), lambda i, j: (i, j)),
            pl.BlockSpec((2, tn), lambda i, j: (0, j)),
        ],
        out_specs=pl.BlockSpec((tm, tn), lambda i, j: (i, j)),
        out_shape=jax.ShapeDtypeStruct((T, N), F32),
        compiler_params=_cparams(("arbitrary", "arbitrary")),
        name="matmul_gated_res",
    )(a, w, res, gates)


def _time_tile(t, n_tiles, n_ctx_tiles, rev):
    if not rev:
        return t
    return jnp.where(t < n_ctx_tiles, n_ctx_tiles - 1 - t, n_tiles - 1 - (t - n_ctx_tiles))


def _s5_kernel(u_ref, bm_ref, cm_ref, pw_ref, y_ref, hre_ref, him_ref, cre_ref, cim_ref, *, rev, lt):
    t = pl.program_id(1)
    ns = S5_BLK_ST

    @pl.when(t == 0)
    def _():
        cre_ref[...] = jnp.zeros_like(cre_ref)
        cim_ref[...] = jnp.zeros_like(cim_ref)

    bu = _dot(u_ref[...].astype(BF16), bm_ref[...])
    hre_ref[...] = bu[:, :ns]
    him_ref[...] = bu[:, ns:]

    p_re = pw_ref[0:8, :ns]
    p_im = pw_ref[0:8, ns:]
    lam = [(pw_ref[8 + k:9 + k, :ns], pw_ref[8 + k:9 + k, ns:]) for k in range(3)]
    row = lax.broadcasted_iota(jnp.int32, (8, ns), 0)
    n_grp = lt // 8

    def body(g, carry):
        c_re, c_im = carry
        gi = (n_grp - 1 - g) if rev else g
        rows = pl.ds(pl.multiple_of(gi * 8, 8), 8)
        a_re = hre_ref[rows, :]
        a_im = him_ref[rows, :]
        for k, s in enumerate((1, 2, 4)):
            l_re, l_im = lam[k]
            if rev:
                keep = row < 8 - s
                s_re = jnp.where(keep, pltpu.roll(a_re, 8 - s, 0), 0.0)
                s_im = jnp.where(keep, pltpu.roll(a_im, 8 - s, 0), 0.0)
            else:
                keep = row >= s
                s_re = jnp.where(keep, pltpu.roll(a_re, s, 0), 0.0)
                s_im = jnp.where(keep, pltpu.roll(a_im, s, 0), 0.0)
            a_re = a_re + (l_re * s_re - l_im * s_im)
            a_im = a_im + (l_re * s_im + l_im * s_re)
        a_re = a_re + (p_re * c_re - p_im * c_im)
        a_im = a_im + (p_re * c_im + p_im * c_re)
        hre_ref[rows, :] = a_re
        him_ref[rows, :] = a_im
        if rev:
            return a_re[0:1, :], a_im[0:1, :]
        return a_re[7:8, :], a_im[7:8, :]

    c_re, c_im = lax.fori_loop(0, n_grp, body, (cre_ref[...], cim_ref[...]))
    cre_ref[...] = c_re
    cim_ref[...] = c_im

    y_ref[...] = (_dot(hre_ref[...].astype(BF16), cm_ref[:ns, :])
                  + _dot(him_ref[...].astype(BF16), cm_ref[ns:, :]))


def s5_scan(p, bm, cm, pw, *, ctx_len, rev, lt=SEQ_TILE):
    T = p.shape[0]
    n_tiles = T // lt
    n_ctx = ctx_len // lt
    tt = functools.partial(_time_tile, n_tiles=n_tiles, n_ctx_tiles=n_ctx, rev=rev)
    return pl.pallas_call(
        functools.partial(_s5_kernel, rev=rev, lt=lt),
        grid=(S5_NBLK, n_tiles),
        in_specs=[
            pl.BlockSpec((lt, S5_BLK_CH), lambda b, t: (tt(t), OFF_S5 // S5_BLK_CH + b)),
            pl.BlockSpec((None, S5_BLK_CH, 2 * S5_BLK_ST), lambda b, t: (b, 0, 0)),
            pl.BlockSpec((None, 2 * S5_BLK_ST, S5_BLK_CH), lambda b, t: (b, 0, 0)),
            pl.BlockSpec((None, 16, 2 * S5_BLK_ST), lambda b, t: (b, 0, 0)),
        ],
        out_specs=pl.BlockSpec((lt, S5_BLK_CH), lambda b, t: (tt(t), b)),
        out_shape=jax.ShapeDtypeStruct((T, D_S5), F32),
        scratch_shapes=[
            pltpu.VMEM((lt, S5_BLK_ST), F32),
            pltpu.VMEM((lt, S5_BLK_ST), F32),
            pltpu.VMEM((1, S5_BLK_ST), F32),
            pltpu.VMEM((1, S5_BLK_ST), F32),
        ],
        compiler_params=_cparams(("arbitrary", "arbitrary")),
        name="s5_scan_rev" if rev else "s5_scan_fwd",
    )(p, bm, cm, pw)


def _cmul(a, b):
    return a[0] * b[0] - a[1] * b[1], a[0] * b[1] + a[1] * b[0]


def s5_prepare(a_re, a_im, log_step, b_re, b_im, c_re, c_im):
    G, P, Cg = S5_GROUPS, S5_STATE, S5_GROUP
    dt = jnp.exp(log_step)[..., None]
    er = jnp.exp(a_re * dt)
    lam1 = (er * jnp.cos(a_im * dt), er * jnp.sin(a_im * dt))
    den = a_re * a_re + a_im * a_im
    xr, xi = lam1[0] - 1.0, lam1[1]
    coef = ((xr * a_re + xi * a_im) / den, (xi * a_re - xr * a_im) / den)
    bb_re = coef[0][..., None] * b_re - coef[1][..., None] * b_im
    bb_im = coef[0][..., None] * b_im + coef[1][..., None] * b_re
    pows = [lam1]
    for _ in range(7):
        pows.append(_cmul(pows[-1], lam1))

    eye = jnp.eye(S5_BLK_GROUPS, dtype=F32)

    def blockdiag_in(m):
        m = m.reshape(2, S5_NBLK, S5_BLK_GROUPS, P, Cg)
        return jnp.einsum('dbgpc,gh->dbgchp', m, eye).reshape(2, S5_NBLK, S5_BLK_CH, S5_BLK_ST)

    def blockdiag_out(m):
        m = m.reshape(2, S5_NBLK, S5_BLK_GROUPS, Cg, P)
        return jnp.einsum('dbgcp,gh->dbgphc', m, eye).reshape(2, S5_NBLK, S5_BLK_ST, S5_BLK_CH)

    bm = jnp.concatenate([blockdiag_in(bb_re), blockdiag_in(bb_im)], axis=-1).astype(BF16)
    cm = jnp.concatenate([blockdiag_out(c_re), blockdiag_out(-c_im)], axis=-2).astype(BF16)

    def lay(v):
        return v.reshape(2, S5_NBLK, S5_BLK_ST)

    def table(order):
        rows = [jnp.concatenate([lay(pows[k][0]), lay(pows[k][1])], axis=-1) for k in order]
        rows += [jnp.concatenate([lay(pows[k][0]), lay(pows[k][1])], axis=-1) for k in (0, 1, 3)]
        rows += [jnp.zeros_like(rows[0])] * (16 - len(rows))
        return jnp.stack(rows, axis=2)

    pw_f = table(range(8))
    pw_b = table(range(7, -1, -1))
    return bm, cm, pw_f, pw_b


def _log_sigmoid(z):
    return jnp.minimum(z, 0.0) - jnp.log(1.0 + jnp.exp(-jnp.abs(z)))


def _gla_kernel(q_ref, k_ref, v_ref, al_ref, wa_ref, ba_ref, o_ref, st_ref, *, rev, lt):
    t = pl.program_id(1)
    C = GLA_CHUNK

    @pl.when(t == 0)
    def _():
        st_ref[...] = jnp.zeros_like(st_ref)

    z = _dot(al_ref[...].astype(BF16), wa_ref[...]) + ba_ref[...]
    g_all = _log_sigmoid(z) * (1.0 / GLA_TAU)
    ri = lax.broadcasted_iota(jnp.int32, (C, C), 0)
    ci = lax.broadcasted_iota(jnp.int32, (C, C), 1)
    tri = (ci >= ri) if rev else (ci <= ri)
    tri_b = jnp.where(tri, 1.0, 0.0).astype(BF16)
    n_chunks = lt // C
    order = range(n_chunks - 1, -1, -1) if rev else range(n_chunks)
    for c in order:
        rows = slice(c * C, (c + 1) * C)
        g = g_all[rows, :]
        g_hi = g.astype(BF16)
        g_lo = (g - g_hi.astype(F32)).astype(BF16)
        b = _dot(tri_b, g_hi) + _dot(tri_b, g_lo)
        b_tot = b[0:1, :] if rev else b[C - 1:C, :]
        q = q_ref[rows, :] * (GLA_DK ** -0.5)
        k = k_ref[rows, :]
        v = v_ref[rows, :].astype(BF16)
        q_d = (q * jnp.exp(b)).astype(BF16)
        k_d = (k * jnp.exp(-b)).astype(BF16)
        k_s = (k * jnp.exp(b_tot - b)).astype(BF16)
        att = lax.dot_general(q_d, k_d, NT_DIMS, preferred_element_type=F32)
        att = jnp.where(tri, att, 0.0).astype(BF16)
        s_t = st_ref[...]
        o = _dot(att, v) + lax.dot_general(q_d, s_t.astype(BF16), NT_DIMS, preferred_element_type=F32)
        o_ref[rows, :] = o
        st_ref[...] = s_t * jnp.exp(b_tot) + lax.dot_general(v, k_s, TN_DIMS, preferred_element_type=F32)


def gla_scan(p, wa, ba, *, ctx_len, rev, lt=SEQ_TILE):
    T = p.shape[0]
    n_tiles = T // lt
    n_ctx = ctx_len // lt
    tt = functools.partial(_time_tile, n_tiles=n_tiles, n_ctx_tiles=n_ctx, rev=rev)
    return pl.pallas_call(
        functools.partial(_gla_kernel, rev=rev, lt=lt),
        grid=(GLA_HEADS, n_tiles),
        in_specs=[
            pl.BlockSpec((lt, GLA_DK), lambda h, t: (tt(t), OFF_Q // GLA_DK + h)),
            pl.BlockSpec((lt, GLA_DK), lambda h, t: (tt(t), OFF_K // GLA_DK + h)),
            pl.BlockSpec((lt, GLA_DV), lambda h, t: (tt(t), OFF_V // GLA_DV + h)),
            pl.BlockSpec((lt, LANE), lambda h, t: (tt(t), OFF_ALPHA // LANE)),
            pl.BlockSpec((LANE, GLA_DK), lambda h, t: (0, h)),
            pl.BlockSpec((1, GLA_DK), lambda h, t: (0, h)),
        ],
        out_specs=pl.BlockSpec((lt, GLA_DV), lambda h, t: (tt(t), h)),
        out_shape=jax.ShapeDtypeStruct((T, D_GLA_V), F32),
        scratch_shapes=[pltpu.VMEM((GLA_DV, GLA_DK), F32)],
        compiler_params=_cparams(("arbitrary", "arbitrary")),
        name="gla_scan_rev" if rev else "gla_scan_fwd",
    )(p, p, p, p, wa, ba)


def _finish_kernel(u_ref, ys_f_ref, ys_b_ref, d_ref, wglu_ref, bglu_ref,
                   og_f_ref, og_b_ref, gate_ref, gn_ref,
                   x0_ref, conv_ref, s_ref, hb_ref, o_ref):
    u = u_ref[...]
    y = ys_f_ref[...] + ys_b_ref[...] + d_ref[...] * u
    zz = _gelu(y)
    s5 = zz * _sigmoid(_dot(zz.astype(BF16), wglu_ref[...]) + bglu_ref[...])
    o_ref[:, 0:D_S5] = s5.astype(BF16)

    gate = gate_ref[...]
    for h in range(GLA_HEADS):
        cols = slice(h * GLA_DV, (h + 1) * GLA_DV)
        o = og_f_ref[:, cols] + og_b_ref[:, cols]
        ms = jnp.mean(o * o, axis=-1, keepdims=True)
        on = o * lax.rsqrt(ms + EPS) * gn_ref[...]
        gt = gate[:, cols]
        o_ref[:, D_S5 + h * GLA_DV:D_S5 + (h + 1) * GLA_DV] = (on * (gt * _sigmoid(gt))).astype(BF16)

    s = s_ref[...]
    hy = x0_ref[...] * (conv_ref[...] + hb_ref[...] * s)
    o_ref[:, D_S5 + D_GLA_V:] = hy.astype(BF16)


def mixer_finish(p, ys_f, ys_b, s5_d, w_glu, b_glu, og_f, og_b, gn, x0, conv, s, hy_bias, *, tm):
    T = p.shape[0]
    row = lambda w, off=0: pl.BlockSpec((tm, w), lambda i: (i, off))
    const = lambda r, w: pl.BlockSpec((r, w), lambda i: (0, 0))
    return pl.pallas_call(
        _finish_kernel,
        grid=(T // tm,),
        in_specs=[
            row(D_S5, OFF_S5 // D_S5), row(D_S5), row(D_S5), const(1, D_S5), const(D_S5, D_S5), const(1, D_S5),
            row(D_GLA_V), row(D_GLA_V), row(D_GLA_V, OFF_GATE // D_GLA_V), const(1, GLA_DV),
            row(D_HY), row(D_HY), row(D_HY), const(1, D_HY),
        ],
        out_specs=pl.BlockSpec((tm, D_MODEL), lambda i: (i, 0)),
        out_shape=jax.ShapeDtypeStruct((T, D_MODEL), BF16),
        compiler_params=_cparams(("arbitrary",)),
        name="mixer_finish",
    )(p, ys_f, ys_b, s5_d.reshape(1, D_S5), w_glu, b_glu.reshape(1, D_S5),
      og_f, og_b, p, gn.reshape(1, GLA_DV), x0, conv, s, hy_bias.reshape(1, D_HY))


def _peer_kernel(x_ref, u_ref, vt_ref, s2_ref, e2_ref, s1_ref, e1_ref, tau_ref, o_ref,
                 act_ref, p_ref, *, n_i1, tm, cw):
    j = pl.program_id(1)

    @pl.when(j == 0)
    def _():
        o_ref[...] = jnp.zeros_like(o_ref)

    act_ref[...] = lax.dot_general(u_ref[...], x_ref[...], NT_DIMS, preferred_element_type=F32)
    K = PEER_KEYS
    sub = SUBLANE // n_i1
    base = (j % sub) * n_i1 if sub > 1 else 0
    for a in range(n_i1):
        r1 = pl.ds(base + a, 1)
        for cb in range(tm // cw):
            cols = slice(cb * cw, (cb + 1) * cw)
            w = jnp.zeros((K, cw), F32)
            for h in range(PEER_HEADS):
                tot = s1_ref[h, r1, cols] + s2_ref[h, :, cols]
                val = e1_ref[h, r1, cols] * e2_ref[h, :, cols]
                w = w + jnp.where(tot >= tau_ref[h, :, cols], val, 0.0)
            act = act_ref[a * K:(a + 1) * K, cols]
            p_ref[a * K:(a + 1) * K, cols] = (w * _gelu(act)).astype(BF16)
    o_ref[...] += _dot(vt_ref[...], p_ref[...])


def peer_dense(xn, u_tab, vt_tab, s1t, e1t, s2t, e2t, taut, *, tm, te, cw=256):
    T, D = xn.shape
    E = u_tab.shape[0]
    H, K = PEER_HEADS, PEER_KEYS
    n_i1 = te // K
    sub = SUBLANE // n_i1
    return pl.pallas_call(
        functools.partial(_peer_kernel, n_i1=n_i1, tm=tm, cw=cw),
        grid=(T // tm, E // te),
        in_specs=[
            pl.BlockSpec((tm, D), lambda i, j: (i, 0)),
            pl.BlockSpec((te, D), lambda i, j: (j, 0)),
            pl.BlockSpec((D, te), lambda i, j: (0, j)),
            pl.BlockSpec((H, K, tm), lambda i, j: (0, 0, i)),
            pl.BlockSpec((H, K, tm), lambda i, j: (0, 0, i)),
            pl.BlockSpec((H, SUBLANE, tm), lambda i, j: (0, j // sub, i)),
            pl.BlockSpec((H, SUBLANE, tm), lambda i, j: (0, j // sub, i)),
            pl.BlockSpec((H, 1, tm), lambda i, j: (0, 0, i)),
        ],
        out_specs=pl.BlockSpec((D, tm), lambda i, j: (0, i)),
        out_shape=jax.ShapeDtypeStruct((D, T), F32),
        scratch_shapes=[pltpu.VMEM((te, tm), F32), pltpu.VMEM((te, tm), BF16)],
        compiler_params=_cparams(("arbitrary", "arbitrary")),
        name="peer_dense",
    )(xn, u_tab, vt_tab, s2t, e2t, s1t, e1t, taut)


NEG_BIG = -3.0e38
N_CAND = PEER_TOPK + 8 * 7 + 8


def _peer_prep_kernel(q_ref, k_ref, s1_ref, e1_ref, s2_ref, e2_ref, tau_ref, top_ref, cand_ref):
    R = PEER_TOPK
    half_w = PEER_DQ // 2

    def nt(a, b):
        return lax.dot_general(a, b, NT_DIMS, preferred_element_type=F32)

    def scores(half):
        qh = q_ref[:, half * half_w:(half + 1) * half_w]
        q_hi = qh.astype(BF16)
        q_lo = (qh - q_hi.astype(F32)).astype(BF16)
        kk = k_ref[half]
        k_hi = kk.astype(BF16)
        k_lo = (kk - k_hi.astype(F32)).astype(BF16)
        return nt(k_hi, q_hi) + (nt(k_hi, q_lo) + nt(k_lo, q_hi))

    def sorted_top(s, slot):
        work = s
        for r in range(R):
            m = jnp.max(work, axis=0, keepdims=True)
            top_ref[slot, r:r + 1, :] = m
            work = jnp.where(work == m, NEG_BIG, work)

    s1 = scores(0)
    s2 = scores(1)
    sorted_top(s1, 0)
    sorted_top(s2, 1)
    a = top_ref[0]
    b = top_ref[1]
    cand_ref[0:R, :] = a[0:1, :] + b
    for i in range(1, 8):
        cand_ref[R + 8 * (i - 1):R + 8 * i, :] = a[i:i + 1, :] + b[0:8, :]
    cand_ref[R + 56:R + 64, :] = a[8:16, :] + b[0:1, :]
    work = cand_ref[...]
    m0 = jnp.max(work, axis=0, keepdims=True)
    m = m0
    z = jnp.ones_like(m0)
    for r in range(1, R):
        work = jnp.where(work == m, NEG_BIG, work)
        m = jnp.max(work, axis=0, keepdims=True)
        z = z + jnp.exp(m - m0)
    tau_ref[0] = m
    s1_ref[0] = s1
    s2_ref[0] = s2
    e1_ref[0] = jnp.exp(s1 - a[0:1, :]) * (1.0 / z)
    e2_ref[0] = jnp.exp(s2 - b[0:1, :])


def peer_prep(q, keys, *, tm):
    T = q.shape[0]
    H, K = PEER_HEADS, PEER_KEYS
    tab = jax.ShapeDtypeStruct((H, K, T), F32)
    tab_spec = pl.BlockSpec((1, K, tm), lambda i, h: (h, 0, i))
    return pl.pallas_call(
        _peer_prep_kernel,
        grid=(T // tm, H),
        in_specs=[
            pl.BlockSpec((tm, PEER_DQ), lambda i, h: (i, h)),
            pl.BlockSpec((None, 2, K, PEER_DQ // 2), lambda i, h: (h, 0, 0, 0)),
        ],
        out_specs=[tab_spec, tab_spec, tab_spec, tab_spec, pl.BlockSpec((1, 1, tm), lambda i, h: (h, 0, i))],
        out_shape=[tab, tab, tab, tab, jax.ShapeDtypeStruct((H, 1, T), F32)],
        scratch_shapes=[pltpu.VMEM((2, PEER_TOPK, tm), F32), pltpu.VMEM((N_CAND, tm), F32)],
        compiler_params=_cparams(("arbitrary", "arbitrary")),
        name="peer_prep",
    )(q, keys)


def _transpose_gated_res_kernel(h_ref, ft_ref, gate_ref, o_ref, *, ctx_len, tm):
    i = pl.program_id(0)
    row = i * tm + lax.broadcasted_iota(jnp.int32, (tm, 1), 0)
    gate = jnp.where(row < ctx_len, gate_ref[0:1, :], gate_ref[1:2, :])
    o_ref[...] = h_ref[...] + gate * ft_ref[...].T


def transpose_gated_res(h, ft, gates, *, ctx_len, tm):
    T, D = h.shape
    return pl.pallas_call(
        functools.partial(_transpose_gated_res_kernel, ctx_len=ctx_len, tm=tm),
        grid=(T // tm,),
        in_specs=[
            pl.BlockSpec((tm, D), lambda i: (i, 0)),
            pl.BlockSpec((D, tm), lambda i: (0, i)),
            pl.BlockSpec((2, D), lambda i: (0, 0)),
        ],
        out_specs=pl.BlockSpec((tm, D), lambda i: (i, 0)),
        out_shape=jax.ShapeDtypeStruct((T, D), F32),
        compiler_params=_cparams(("arbitrary",)),
        name="transpose_gated_res",
    )(h, ft, gates)


def _rmsnorm_kernel(x_ref, g_ref, o_ref):
    x = x_ref[...]
    ms = jnp.mean(x * x, axis=-1, keepdims=True)
    o_ref[...] = x * lax.rsqrt(ms + EPS) * g_ref[...]


def rmsnorm_rows(x, g, *, tm):
    T, D = x.shape
    return pl.pallas_call(
        _rmsnorm_kernel,
        grid=(T // tm,),
        in_specs=[pl.BlockSpec((tm, D), lambda i: (i, 0)), pl.BlockSpec((1, D), lambda i: (0, 0))],
        out_specs=pl.BlockSpec((tm, D), lambda i: (i, 0)),
        out_shape=jax.ShapeDtypeStruct((T, D), F32),
        compiler_params=_cparams(("arbitrary",)),
        name="rmsnorm_rows",
    )(x, g.reshape(1, D))


def grid_sincos(n_tokens, dim):
    rows = n_tokens // GRID_W
    row = jnp.repeat(jnp.arange(rows), GRID_W).astype(F32)
    col = jnp.tile(jnp.arange(GRID_W), rows).astype(F32)
    n_freq = dim // 4
    omega = 1.0 / (POS_BASE ** (jnp.arange(n_freq, dtype=F32) / n_freq))

    def enc(pp):
        a = pp[:, None] * omega[None, :]
        return jnp.concatenate([jnp.sin(a), jnp.cos(a)], axis=-1)

    return jnp.concatenate([enc(row), enc(col)], axis=-1)


def _split_bf16(a):
    hi = a.astype(BF16)
    return hi, (a - hi.astype(F32)).astype(BF16)


def _dot3s(a, b_hi, b_lo):
    a_hi, a_lo = _split_bf16(a)
    return _dot(a_hi, b_hi) + (_dot(a_hi, b_lo) + _dot(a_lo, b_hi))


def _dot3(a, b):
    return _dot3s(a, *_split_bf16(b))


def _adaln_kernel(c_ref, w_ref, b_ref, o_ref):
    c = c_ref[...]
    o_ref[...] = _dot3(c * _sigmoid(c), w_ref[...]) + b_ref[...]


def adaln(cond, w, b, *, tn=1536):
    R, D = cond.shape
    N = w.shape[1]
    return pl.pallas_call(
        _adaln_kernel,
        grid=(N // tn,),
        in_specs=[pl.BlockSpec((R, D), lambda j: (0, 0)),
                  pl.BlockSpec((D, tn), lambda j: (0, j)),
                  pl.BlockSpec((1, tn), lambda j: (0, j))],
        out_specs=pl.BlockSpec((R, tn), lambda j: (0, j)),
        out_shape=jax.ShapeDtypeStruct((R, N), F32),
        compiler_params=_cparams(("arbitrary",)),
        name="adaln",
    )(cond, w, b.reshape(1, N))


def _hyena_pre_kernel(x0_ref, x1_ref, v_ref, w_ref, b_ref, x0o_ref, s_ref, *, ctx_len):
    T = x0_ref.shape[0]
    row = lax.broadcasted_iota(jnp.int32, (T, 1), 0)
    first = jnp.logical_or(row == 0, row == ctx_len)
    last = jnp.logical_or(row == ctx_len - 1, row == T - 1)

    def conv(ref, part):
        x = ref[...]
        prev = jnp.where(first, 0.0, pltpu.roll(x, 1, 0))
        nxt = jnp.where(last, 0.0, pltpu.roll(x, T - 1, 0))
        w = lambda tap: w_ref[tap, part:part + 1, :]
        return w(0) * prev + w(1) * x + w(2) * nxt + b_ref[part:part + 1, :]

    x0o_ref[...] = conv(x0_ref, 0)
    s_ref[...] = conv(x1_ref, 1) * conv(v_ref, 2)


def hyena_pre(p, conv_w, conv_b, *, ctx_len):
    T = p.shape[0]
    nb = D_HY // LANE
    col = lambda part: pl.BlockSpec((T, LANE), lambda c: (0, OFF_HY // LANE + part * nb + c))
    out = jax.ShapeDtypeStruct((T, D_HY), F32)
    return pl.pallas_call(
        functools.partial(_hyena_pre_kernel, ctx_len=ctx_len),
        grid=(nb,),
        in_specs=[col(0), col(1), col(2),
                  pl.BlockSpec((HY_SHORT, 3, LANE), lambda c: (0, 0, c)),
                  pl.BlockSpec((3, LANE), lambda c: (0, c))],
        out_specs=[pl.BlockSpec((T, LANE), lambda c: (0, c))] * 2,
        out_shape=[out, out],
        compiler_params=_cparams(("arbitrary",)),
        name="hyena_pre",
    )(p, p, p, conv_w.reshape(HY_SHORT, 3, D_HY), conv_b.reshape(3, D_HY))


def _hyena_filter_kernel(band_ref, w1_ref, b1_ref, w2_ref, b2_ref, w3_ref, fr_ref, dec_ref, o_ref, *, n, tr):
    i = pl.program_id(0)
    ri = (i * tr + lax.broadcasted_iota(jnp.int32, (tr, 1), 0)).astype(F32)
    t = ri * (1.0 / (n - 1))
    w = ri * (2.0 * math.pi / n)
    lane = lax.broadcasted_iota(jnp.int32, (1, LANE), 1)
    arg = w * band_ref[...]
    z = jnp.where(lane == 0, t,
                  jnp.where(lane <= HY_BANDS, jnp.cos(arg),
                            jnp.where(lane <= 2 * HY_BANDS, -jnp.sin(arg), 0.0)))
    h = jnp.sin(fr_ref[0:1, :] * (_dot3(z, w1_ref[...]) + b1_ref[...]))
    h = jnp.sin(fr_ref[1:2, :] * (_dot3(h, w2_ref[...]) + b2_ref[...]))
    h = _dot3(h, w3_ref[...]) * jnp.exp(-t * dec_ref[...])
    col = lax.broadcasted_iota(jnp.int32, (1, 2 * D_HY), 1)
    o_ref[...] = jnp.where(jnp.logical_and(ri == 0.0, col >= D_HY), 0.0, h)


def hyena_filter(n, w1, b1, w2, b2, w3, freq, decay, *, tr=256):
    bands = jnp.linspace(1e-4, HY_BANDS - 1, HY_BANDS, dtype=F32)
    band_row = jnp.zeros((1, LANE), F32).at[0, 1:1 + HY_BANDS].set(bands).at[0, 1 + HY_BANDS:1 + 2 * HY_BANDS].set(bands)
    w1p = jnp.zeros((LANE, HY_HIDDEN), F32).at[:HY_FEAT].set(w1)
    const = lambda a: pl.BlockSpec(a.shape, lambda i: (0,) * a.ndim)
    args = (band_row, w1p, b1.reshape(1, -1), w2, b2.reshape(1, -1), w3, freq,
            jnp.abs(decay).reshape(1, 2 * D_HY))
    return pl.pallas_call(
        functools.partial(_hyena_filter_kernel, n=n, tr=tr),
        grid=(n // tr,),
        in_specs=[const(a) for a in args],
        out_specs=pl.BlockSpec((tr, 2 * D_HY), lambda i: (i, 0)),
        out_shape=jax.ShapeDtypeStruct((n, 2 * D_HY), F32),
        compiler_params=_cparams(("arbitrary",)),
        name="hyena_filter",
    )(*args)


FFT_R = 128
FFT_N = FFT_R * FFT_R
FFT_CH = 32


def _dft_tables():
    r = np.arange(FFT_R)
    ang = 2.0 * np.pi * np.outer(r, r) / FFT_R
    c, s = np.cos(ang), np.sin(ang)
    angt = 2.0 * np.pi * np.outer(r, r) / FFT_N
    ct, st = np.cos(angt), np.sin(angt)

    def split(m):
        m = jnp.asarray(m, F32)
        hi = m.astype(BF16)
        return jnp.stack([hi, (m - hi.astype(F32)).astype(BF16)])

    fwd_a = split(np.concatenate([c, -s], axis=1))
    fwd_b = split(np.block([[c, -s], [s, c]]))
    inv_b = split(np.block([[c, s], [-s, c]]) / FFT_N)
    inv_a = split(np.concatenate([c, -s], axis=0))
    tw = (jnp.asarray(ct, F32), jnp.asarray(-st, F32))
    twc = (jnp.asarray(ct, F32), jnp.asarray(st, F32))
    return fwd_a, fwd_b, inv_b, inv_a, tw, twc


def _twiddle_transpose(y, twr, twi, dst_ref, nc):
    y = y.reshape(nc, FFT_R, 2 * FFT_R)
    yr, yi = y[:, :, :FFT_R], y[:, :, FFT_R:]
    zr = yr * twr - yi * twi
    zi = yr * twi + yi * twr
    for c in range(nc):
        dst_ref[c * FFT_R:(c + 1) * FFT_R, 0:FFT_R] = zr[c].T
        dst_ref[c * FFT_R:(c + 1) * FFT_R, FFT_R:2 * FFT_R] = zi[c].T


def _hyena_fwd_kernel(x_ref, fa_ref, twr_ref, twi_ref, fb_ref, o_ref, l_ref, yt_ref, *, nc):
    j = pl.program_id(1)
    half = FFT_R // 2

    @pl.when(j == 0)
    def _():
        zeros = jnp.zeros((half, LANE), F32)

        def body(b, carry):
            t = jnp.concatenate([x_ref[pl.ds(b, half, stride=FFT_R), :], zeros], axis=0)
            l_ref[pl.ds(b, LANE, stride=FFT_R), :] = t.T
            return carry

        lax.fori_loop(0, FFT_R, body, 0)

    rows = nc * FFT_R
    lc = l_ref[pl.ds(pl.multiple_of(j * rows, rows), rows), :]
    y = _dot3s(lc, fa_ref[0], fa_ref[1])
    _twiddle_transpose(y, twr_ref[...], twi_ref[...], yt_ref, nc)
    o_ref[...] = _dot3s(yt_ref[...], fb_ref[0], fb_ref[1])


def hyena_fwd_dft(x, tables):
    n, C = x.shape
    assert n * 2 == FFT_N and C % LANE == 0
    fwd_a, fwd_b, _, _, (twr, twi), _ = tables
    nc = FFT_CH
    steps = LANE // nc
    const = lambda a: pl.BlockSpec(a.shape, lambda cb, j: (0,) * a.ndim)
    return pl.pallas_call(
        functools.partial(_hyena_fwd_kernel, nc=nc),
        grid=(C // LANE, steps),
        in_specs=[pl.BlockSpec((n, LANE), lambda cb, j: (0, cb)), const(fwd_a), const(twr), const(twi), const(fwd_b)],
        out_specs=pl.BlockSpec((nc * FFT_R, 2 * FFT_R), lambda cb, j: (cb * steps + j, 0)),
        out_shape=jax.ShapeDtypeStruct((C * FFT_R, 2 * FFT_R), F32),
        scratch_shapes=[pltpu.VMEM((LANE * FFT_R, LANE), F32), pltpu.VMEM((nc * FFT_R, 2 * FFT_R), F32)],
        compiler_params=_cparams(("arbitrary", "arbitrary")),
        name="hyena_fwd_dft",
    )(x, fwd_a, twr, twi, fwd_b)


def _hyena_inv_kernel(s_ref, kf_ref, kb_ref, gb_ref, twr_ref, twi_ref, ga_ref, o_ref, lr_ref, at_ref, *, nc, steps):
    j = pl.program_id(1)
    R = FFT_R
    s, kf, kb = s_ref[...], kf_ref[...], kb_ref[...]
    sr, si = s[:, :R], s[:, R:]
    kr = kf[:, :R] + kb[:, :R]
    ki = kf[:, R:] - kb[:, R:]
    p = jnp.concatenate([sr * kr - si * ki, sr * ki + si * kr], axis=1)
    a = _dot3s(p, gb_ref[0], gb_ref[1])
    _twiddle_transpose(a, twr_ref[...], twi_ref[...], at_ref, nc)
    rows = nc * R
    lr_ref[pl.ds(pl.multiple_of(j * rows, rows), rows), :] = _dot3s(at_ref[...], ga_ref[0], ga_ref[1])

    @pl.when(j == steps - 1)
    def _():
        def body(b, carry):
            t = lr_ref[pl.ds(b, LANE, stride=R), :]
            o_ref[pl.ds(b, R // 2, stride=R), :] = t.T[0:R // 2, :]
            return carry

        lax.fori_loop(0, R, body, 0)


def hyena_inv_dft(s_f, k_f, tables):
    C = s_f.shape[0] // FFT_R
    _, _, inv_b, inv_a, _, (twr, twi) = tables
    nc = FFT_CH
    steps = LANE // nc
    kb_off = C // nc
    const = lambda a: pl.BlockSpec(a.shape, lambda cb, j: (0,) * a.ndim)
    blk = lambda off: pl.BlockSpec((nc * FFT_R, 2 * FFT_R), lambda cb, j: (off + cb * steps + j, 0))
    return pl.pallas_call(
        functools.partial(_hyena_inv_kernel, nc=nc, steps=steps),
        grid=(C // LANE, steps),
        in_specs=[blk(0), blk(0), blk(kb_off), const(inv_b), const(twr), const(twi), const(inv_a)],
        out_specs=pl.BlockSpec((FFT_N // 2, LANE), lambda cb, j: (0, cb)),
        out_shape=jax.ShapeDtypeStruct((FFT_N // 2, C), F32),
        scratch_shapes=[pltpu.VMEM((LANE * FFT_R, LANE), F32), pltpu.VMEM((nc * FFT_R, 2 * FFT_R), F32)],
        compiler_params=_cparams(("arbitrary", "arbitrary")),
        name="hyena_inv_dft",
    )(s_f, k_f, k_f, inv_b, twr, twi, inv_a)


def _hyena_ctx_kernel(s_ref, k_ref, fw_ref, iv_ref, o_ref):
    n = s_ref.shape[0]
    N = 2 * n
    xs = _dot3(fw_ref[...], s_ref[...])
    xk = _dot3(fw_ref[...], k_ref[...])
    sr, si = xs[:N], xs[N:]
    kr = xk[:N, :D_HY] + xk[:N, D_HY:]
    ki = xk[N:, :D_HY] - xk[N:, D_HY:]
    p = jnp.concatenate([sr * kr - si * ki, sr * ki + si * kr], axis=0)
    o_ref[...] = _dot3(iv_ref[...], p)


def hyena_ctx_conv(s_c, k_c):
    n = s_c.shape[0]
    N = 2 * n
    ang = 2.0 * np.pi * np.outer(np.arange(N), np.arange(N)) / N
    c, s = np.cos(ang), np.sin(ang)
    fw = jnp.asarray(np.concatenate([c[:, :n], -s[:, :n]], axis=0), F32)
    iv = jnp.asarray(np.concatenate([c[:n, :], -s[:n, :]], axis=1) / N, F32)
    full = lambda a: pl.BlockSpec(a.shape, lambda i: (0,) * a.ndim)
    return pl.pallas_call(
        _hyena_ctx_kernel,
        grid=(1,),
        in_specs=[full(s_c), full(k_c), full(fw), full(iv)],
        out_specs=pl.BlockSpec((n, D_HY), lambda i: (0, 0)),
        out_shape=jax.ShapeDtypeStruct((n, D_HY), F32),
        compiler_params=_cparams(("arbitrary",)),
        name="hyena_ctx_conv",
    )(s_c, k_c, fw, iv)


def hyena_mixer_parts(p, conv_w, conv_b, filt, tables, *, ctx_len):
    T = p.shape[0]
    x0, s = hyena_pre(p, conv_w, conv_b, ctx_len=ctx_len)
    k_lat = hyena_filter(T - ctx_len, *filt)
    k_ctx = hyena_filter(ctx_len, *filt)
    conv_l = hyena_inv_dft(hyena_fwd_dft(s[ctx_len:], tables), hyena_fwd_dft(k_lat, tables), tables)
    conv_c = hyena_ctx_conv(s[:ctx_len], k_ctx)
    return x0, jnp.concatenate([conv_c, conv_l], axis=0), s


def _pack_w_in(w_in_l):
    widths = (D_S5, D_GLA_K, D_GLA_K, D_GLA_V, D_GLA_V, 2 * GLA_RANK, 3 * D_HY)
    offs = [0]
    for wd in widths:
        offs.append(offs[-1] + wd)
    s5, q, k, v, gate, alpha, hy = (w_in_l[:, offs[i]:offs[i + 1]] for i in range(7))
    pad = jnp.zeros((w_in_l.shape[0], IN_PACKED - OFF_ALPHA - 2 * GLA_RANK), w_in_l.dtype)
    return jnp.concatenate([v, gate, s5, q, k, hy, alpha, pad], axis=1).astype(BF16)


def kernel(x, c, ctx, c_ctx, w_ada, b_ada, g_norm1, g_norm2, w_in, s5_a_re, s5_a_im, s5_log_step, s5_b_re, s5_b_im, s5_c_re, s5_c_im, s5_d, s5_w_glu, s5_b_glu, gla_w_alpha, gla_b_alpha, gla_g_norm, hy_conv_w, hy_conv_b, hy_f_w1, hy_f_b1, hy_f_w2, hy_f_b2, hy_f_w3, hy_f_freq, hy_decay, hy_bias, w_out, peer_w_q, peer_keys, peer_u, peer_v, g_final):
    L = x.shape[1]
    Lc = ctx.shape[1]
    T = L + Lc
    TM = 768
    h = jnp.concatenate([ctx[0], x[0] + grid_sincos(L, D_MODEL)], axis=0)
    cond = jnp.zeros((SUBLANE, D_MODEL), F32).at[0].set(c_ctx).at[1].set(c[0])
    tables = _dft_tables()

    for l in range(DEPTH):
        m = adaln(cond, w_ada[l], b_ada[l])[0:2]
        sh1, sc1, gt1, sh2, sc2, gt2 = jnp.split(m, 6, axis=-1)
        mod1 = jnp.stack([sh1[0], sc1[0], sh1[1], sc1[1]], axis=0)
        mod2 = jnp.stack([sh2[0], sc2[0], sh2[1], sc2[1]], axis=0)

        p = norm_mod_matmul(h, g_norm1[l], mod1, _pack_w_in(w_in[l]), ctx_len=Lc, tm=TM, tn=768)

        bm, cm, pw_f, pw_b = s5_prepare(s5_a_re[l], s5_a_im[l], s5_log_step[l], s5_b_re[l], s5_b_im[l],
                                        s5_c_re[l], s5_c_im[l])
        ys_f = s5_scan(p, bm[0], cm[0], pw_f[0], ctx_len=Lc, rev=False)
        ys_b = s5_scan(p, bm[1], cm[1], pw_b[1], ctx_len=Lc, rev=True)

        wa = jnp.zeros((2, LANE, D_GLA_K), F32)
        wa = wa.at[0, 0:GLA_RANK].set(gla_w_alpha[l, 0]).at[1, GLA_RANK:2 * GLA_RANK].set(gla_w_alpha[l, 1])
        wa = wa.astype(BF16)
        og_f = gla_scan(p, wa[0], gla_b_alpha[l, 0].reshape(1, D_GLA_K), ctx_len=Lc, rev=False)
        og_b = gla_scan(p, wa[1], gla_b_alpha[l, 1].reshape(1, D_GLA_K), ctx_len=Lc, rev=True)

        filt = (hy_f_w1[l], hy_f_b1[l], hy_f_w2[l], hy_f_b2[l], hy_f_w3[l], hy_f_freq[l], hy_decay[l])
        x0, conv, s = hyena_mixer_parts(p, hy_conv_w[l], hy_conv_b[l], filt, tables, ctx_len=Lc)

        mix = mixer_finish(p, ys_f, ys_b, s5_d[l], s5_w_glu[l].astype(BF16), s5_b_glu[l],
                           og_f, og_b, gla_g_norm[l], x0, conv, s, hy_bias[l], tm=256)
        h = matmul_gated_res(mix, w_out[l].astype(BF16), h, gt1, ctx_len=Lc, tm=TM, tn=1024)

        q, xn = norm_mod_matmul(h, g_norm2[l], mod2, peer_w_q[l].astype(BF16),
                                ctx_len=Lc, tm=TM, tn=1024, emit_xn=True)
        s1t, e1t, s2t, e2t, taut = peer_prep(q, peer_keys[l], tm=TM)
        ft = peer_dense(xn, peer_u[l].astype(BF16), peer_v[l].T.astype(BF16),
                        s1t, e1t, s2t, e2t, taut, tm=TM, te=512)
        h = transpose_gated_res(h, ft, gt2, ctx_len=Lc, tm=256)

    out = rmsnorm_rows(h[Lc:], g_final, tm=512)
    return out[None]
```

```python
import functools
import math

import jax
import jax.numpy as jnp
import numpy as np
from jax import lax
from jax.experimental import pallas as pl
from jax.experimental.pallas import tpu as pltpu

F32 = jnp.float32
BF16 = jnp.bfloat16

D_MODEL = 2048
DEPTH = 4
GRID_W = 64
EPS = 1e-6
POS_BASE = 10000.0

D_S5 = D_MODEL // 4
S5_GROUP = 16
S5_GROUPS = D_S5 // S5_GROUP
S5_STATE = 64
S5_BLK_GROUPS = 8
S5_BLK_CH = S5_BLK_GROUPS * S5_GROUP
S5_BLK_ST = S5_BLK_GROUPS * S5_STATE
S5_NBLK = S5_GROUPS // S5_BLK_GROUPS

GLA_HEADS = 4
D_GLA_K = D_MODEL // 4
D_GLA_V = D_MODEL // 2
GLA_DK = D_GLA_K // GLA_HEADS
GLA_DV = D_GLA_V // GLA_HEADS
GLA_RANK = 16
GLA_TAU = 16.0
GLA_CHUNK = 64

D_HY = D_MODEL // 4
HY_SHORT = 3
HY_BANDS = 16
HY_FEAT = 1 + 2 * HY_BANDS
HY_HIDDEN = 64
HY_TARGET = 1e-2
HY_MIN_DECAY = -math.log(HY_TARGET) / 1.5
HY_MAX_DECAY = -math.log(HY_TARGET) / 0.3

PEER_HEADS = 8
PEER_KEYS = 128
PEER_EXPERTS = PEER_KEYS * PEER_KEYS
PEER_DQ = 256
PEER_TOPK = 16

OFF_V = 0
OFF_GATE = OFF_V + D_GLA_V
OFF_S5 = OFF_GATE + D_GLA_V
OFF_Q = OFF_S5 + D_S5
OFF_K = OFF_Q + D_GLA_K
OFF_HY = OFF_K + D_GLA_K
OFF_ALPHA = OFF_HY + 3 * D_HY
LANE = 128
SUBLANE = 8
IN_PACKED = OFF_ALPHA + 2 * LANE

SEQ_TILE = 256
VMEM_LIMIT = 56 * 1024 * 1024

NT_DIMS = (((1,), (1,)), ((), ()))
TN_DIMS = (((0,), (0,)), ((), ()))


def _cparams(sem):
    return pltpu.CompilerParams(dimension_semantics=sem, vmem_limit_bytes=VMEM_LIMIT)


def _dot(a, b):
    return jnp.dot(a, b, preferred_element_type=F32)


def _gelu(x):
    k1 = -2.0 * math.sqrt(2.0 / math.pi) * math.log2(math.e)
    k2 = 0.044715 * k1
    return x / (1.0 + jnp.exp2(x * (k1 + k2 * (x * x))))


def _sigmoid(x):
    return 1.0 / (1.0 + jnp.exp(-x))


def _norm_mod_matmul_kernel(x_ref, g_ref, mod_ref, w_ref, o_ref, *rest, ctx_len, tm, emit_xn):
    if emit_xn:
        xo_ref, xn_ref = rest
    else:
        (xn_ref,) = rest
    i = pl.program_id(0)
    j = pl.program_id(1)

    @pl.when(j == 0)
    def _():
        x = x_ref[...]
        ms = jnp.mean(x * x, axis=-1, keepdims=True)
        y = x * lax.rsqrt(ms + EPS) * g_ref[...]
        row = i * tm + lax.broadcasted_iota(jnp.int32, (tm, 1), 0)
        is_ctx = row < ctx_len
        shift = jnp.where(is_ctx, mod_ref[0:1, :], mod_ref[2:3, :])
        scale = jnp.where(is_ctx, mod_ref[1:2, :], mod_ref[3:4, :])
        xn = (y * (1.0 + scale) + shift).astype(BF16)
        xn_ref[...] = xn
        if emit_xn:
            xo_ref[...] = xn

    o_ref[...] = _dot(xn_ref[...], w_ref[...])


def norm_mod_matmul(x, g, mod, w, *, ctx_len, tm, tn, emit_xn=False):
    T, D = x.shape
    N = w.shape[1]
    out_shape = [jax.ShapeDtypeStruct((T, N), F32)]
    out_specs = [pl.BlockSpec((tm, tn), lambda i, j: (i, j))]
    if emit_xn:
        out_shape.append(jax.ShapeDtypeStruct((T, D), BF16))
        out_specs.append(pl.BlockSpec((tm, D), lambda i, j: (i, 0)))
    res = pl.pallas_call(
        functools.partial(_norm_mod_matmul_kernel, ctx_len=ctx_len, tm=tm, emit_xn=emit_xn),
        grid=(T // tm, N // tn),
        in_specs=[
            pl.BlockSpec((tm, D), lambda i, j: (i, 0)),
            pl.BlockSpec((1, D), lambda i, j: (0, 0)),
            pl.BlockSpec((4, D), lambda i, j: (0, 0)),
            pl.BlockSpec((D, tn), lambda i, j: (0, j)),
        ],
        out_specs=out_specs,
        out_shape=out_shape,
        scratch_shapes=[pltpu.VMEM((tm, D), BF16)],
        compiler_params=_cparams(("arbitrary", "arbitrary")),
        name="norm_mod_matmul",
    )(x, g.reshape(1, D), mod, w)
    return res if emit_xn else res[0]


def _matmul_gated_res_kernel(a_ref, w_ref, r_ref, gate_ref, o_ref, *, ctx_len, tm):
    i = pl.program_id(0)
    row = i * tm + lax.broadcasted_iota(jnp.int32, (tm, 1), 0)
    gate = jnp.where(row < ctx_len, gate_ref[0:1, :], gate_ref[1:2, :])
    o_ref[...] = r_ref[...] + gate * _dot(a_ref[...], w_ref[...])


def matmul_gated_res(a, w, res, gates, *, ctx_len, tm, tn):
    T, K = a.shape
    N = w.shape[1]
    return pl.pallas_call(
        functools.partial(_matmul_gated_res_kernel, ctx_len=ctx_len, tm=tm),
        grid=(T // tm, N // tn),
        in_specs=[
            pl.BlockSpec((tm, K), lambda i, j: (i, 0)),
            pl.BlockSpec((K, tn), lambda i, j: (0, j)),
            pl.BlockSpec((tm, tn), lambda i, j: (i, j)),
            pl.BlockSpec((2, tn), lambda i, j: (0, j)),
        ],
        out_specs=pl.BlockSpec((tm, tn), lambda i, j: (i, j)),
        out_shape=jax.ShapeDtypeStruct((T, N), F32),
        compiler_params=_cparams(("arbitrary", "arbitrary")),
        name="matmul_gated_res",
    )(a, w, res, gates)


def _time_tile(t, n_tiles, n_ctx_tiles, rev):
    if not rev:
        return t
    return jnp.where(t < n_ctx_tiles, n_ctx_tiles - 1 - t, n_tiles - 1 - (t - n_ctx_tiles))


def _s5_kernel(uf_ref, ub_ref, bm_ref, cm_ref, pw_ref, yf_ref, yb_ref, h_ref, c_ref, *, lt):
    t = pl.program_id(1)
    ns = S5_BLK_ST
    u_refs = (uf_ref, ub_ref)
    y_refs = (yf_ref, yb_ref)

    @pl.when(t == 0)
    def _():
        c_ref[...] = jnp.zeros_like(c_ref)

    for d in range(2):
        bu = _dot(u_refs[d][...].astype(BF16), bm_ref[d])
        h_ref[d, 0] = bu[:, :ns]
        h_ref[d, 1] = bu[:, ns:]

    n_grp = lt // 8

    def group_update(d, g, c_re, c_im):
        rev = d == 1
        gi = (n_grp - 1 - g) if rev else g
        rows = pl.ds(pl.multiple_of(gi * 8, 8), 8)
        a_re = h_ref[d, 0, rows, :]
        a_im = h_ref[d, 1, rows, :]
        for k, s in enumerate((1, 2, 4)):
            l_re = pw_ref[d, 8 + 8 * k:16 + 8 * k, :ns]
            l_im = pw_ref[d, 8 + 8 * k:16 + 8 * k, ns:]
            shift = (8 - s) if rev else s
            s_re = pltpu.roll(a_re, shift, 0)
            s_im = pltpu.roll(a_im, shift, 0)
            a_re = a_re + (l_re * s_re - l_im * s_im)
            a_im = a_im + (l_re * s_im + l_im * s_re)
        p_re = pw_ref[d, 0:8, :ns]
        p_im = pw_ref[d, 0:8, ns:]
        a_re = a_re + (p_re * c_re - p_im * c_im)
        a_im = a_im + (p_re * c_im + p_im * c_re)
        h_ref[d, 0, rows, :] = a_re
        h_ref[d, 1, rows, :] = a_im
        edge = slice(0, 1) if rev else slice(7, 8)
        return a_re[edge, :], a_im[edge, :]

    def body(g, carry):
        f_re, f_im, b_re, b_im = carry
        f_re, f_im = group_update(0, g, f_re, f_im)
        b_re, b_im = group_update(1, g, b_re, b_im)
        return f_re, f_im, b_re, b_im

    carry = lax.fori_loop(0, n_grp, body, (c_ref[0, 0], c_ref[0, 1], c_ref[1, 0], c_ref[1, 1]), unroll=2)
    c_ref[0, 0], c_ref[0, 1], c_ref[1, 0], c_ref[1, 1] = carry

    for d in range(2):
        y_refs[d][...] = (_dot(h_ref[d, 0].astype(BF16), cm_ref[d, :ns, :])
                          + _dot(h_ref[d, 1].astype(BF16), cm_ref[d, ns:, :]))


def s5_scan(p, bm, cm, pw, *, ctx_len, lt=SEQ_TILE):
    T = p.shape[0]
    n_tiles = T // lt
    n_ctx = ctx_len // lt
    tf = functools.partial(_time_tile, n_tiles=n_tiles, n_ctx_tiles=n_ctx, rev=False)
    tb = functools.partial(_time_tile, n_tiles=n_tiles, n_ctx_tiles=n_ctx, rev=True)
    ucol = OFF_S5 // S5_BLK_CH
    out = jax.ShapeDtypeStruct((T, D_S5), F32)
    return pl.pallas_call(
        functools.partial(_s5_kernel, lt=lt),
        grid=(S5_NBLK, n_tiles),
        in_specs=[
            pl.BlockSpec((lt, S5_BLK_CH), lambda b, t: (tf(t), ucol + b)),
            pl.BlockSpec((lt, S5_BLK_CH), lambda b, t: (tb(t), ucol + b)),
            pl.BlockSpec((2, None, S5_BLK_CH, 2 * S5_BLK_ST), lambda b, t: (0, b, 0, 0)),
            pl.BlockSpec((2, None, 2 * S5_BLK_ST, S5_BLK_CH), lambda b, t: (0, b, 0, 0)),
            pl.BlockSpec((2, None, 32, 2 * S5_BLK_ST), lambda b, t: (0, b, 0, 0)),
        ],
        out_specs=[pl.BlockSpec((lt, S5_BLK_CH), lambda b, t: (tf(t), b)),
                   pl.BlockSpec((lt, S5_BLK_CH), lambda b, t: (tb(t), b))],
        out_shape=[out, out],
        scratch_shapes=[pltpu.VMEM((2, 2, lt, S5_BLK_ST), F32), pltpu.VMEM((2, 2, 1, S5_BLK_ST), F32)],
        compiler_params=_cparams(("arbitrary", "arbitrary")),
        name="s5_scan",
    )(p, p, bm, cm, pw)


def _cmul(a, b):
    return a[0] * b[0] - a[1] * b[1], a[0] * b[1] + a[1] * b[0]


def s5_prepare(a_re, a_im, log_step, b_re, b_im, c_re, c_im):
    G, P, Cg = S5_GROUPS, S5_STATE, S5_GROUP
    dt = jnp.exp(log_step)[..., None]
    er = jnp.exp(a_re * dt)
    lam1 = (er * jnp.cos(a_im * dt), er * jnp.sin(a_im * dt))
    den = a_re * a_re + a_im * a_im
    xr, xi = lam1[0] - 1.0, lam1[1]
    coef = ((xr * a_re + xi * a_im) / den, (xi * a_re - xr * a_im) / den)
    bb_re = coef[0][..., None] * b_re - coef[1][..., None] * b_im
    bb_im = coef[0][..., None] * b_im + coef[1][..., None] * b_re
    pows = [lam1]
    for _ in range(7):
        pows.append(_cmul(pows[-1], lam1))

    eye = jnp.eye(S5_BLK_GROUPS, dtype=F32)

    def blockdiag_in(m):
        m = m.reshape(2, S5_NBLK, S5_BLK_GROUPS, P, Cg)
        return jnp.einsum('dbgpc,gh->dbgchp', m, eye).reshape(2, S5_NBLK, S5_BLK_CH, S5_BLK_ST)

    def blockdiag_out(m):
        m = m.reshape(2, S5_NBLK, S5_BLK_GROUPS, Cg, P)
        return jnp.einsum('dbgcp,gh->dbgphc', m, eye).reshape(2, S5_NBLK, S5_BLK_ST, S5_BLK_CH)

    bm = jnp.concatenate([blockdiag_in(bb_re), blockdiag_in(bb_im)], axis=-1).astype(BF16)
    cm = jnp.concatenate([blockdiag_out(c_re), blockdiag_out(-c_im)], axis=-2).astype(BF16)

    def lay(v):
        return v.reshape(2, S5_NBLK, S5_BLK_ST)

    def table(rev):
        power = lambda k: jnp.concatenate([lay(pows[k][0]), lay(pows[k][1])], axis=-1)
        rows = [power(7 - r if rev else r) for r in range(8)]
        zero = jnp.zeros_like(rows[0])
        for s in (1, 2, 4):
            for r in range(8):
                inside = (r < 8 - s) if rev else (r >= s)
                rows.append(power(s - 1) if inside else zero)
        return jnp.stack(rows, axis=2)

    return bm, cm, jnp.stack([table(False)[0], table(True)[1]])


def _log_sigmoid(z):
    return jnp.minimum(z, 0.0) - jnp.log(1.0 + jnp.exp(-jnp.abs(z)))


def _gla_kernel(qf_ref, kf_ref, vf_ref, af_ref, qb_ref, kb_ref, vb_ref, ab_ref, wa_ref, ba_ref,
                of_ref, ob_ref, st_ref, *, lt):
    t = pl.program_id(1)
    C = GLA_CHUNK

    @pl.when(t == 0)
    def _():
        st_ref[...] = jnp.zeros_like(st_ref)

    ri = lax.broadcasted_iota(jnp.int32, (C, C), 0)
    ci = lax.broadcasted_iota(jnp.int32, (C, C), 1)
    refs = ((qf_ref, kf_ref, vf_ref, af_ref, of_ref), (qb_ref, kb_ref, vb_ref, ab_ref, ob_ref))
    tris, tri_bs, g_alls = [], [], []
    for d in range(2):
        z = _dot(refs[d][3][...].astype(BF16), wa_ref[d]) + ba_ref[d:d + 1, :]
        g_alls.append(_log_sigmoid(z) * (1.0 / GLA_TAU))
        tri = (ci >= ri) if d == 1 else (ci <= ri)
        tris.append(tri)
        tri_bs.append(jnp.where(tri, 1.0, 0.0).astype(BF16))

    def chunk(d, c):
        q_ref, k_ref, v_ref, _, o_ref = refs[d]
        rev = d == 1
        rows = slice(c * C, (c + 1) * C)
        g = g_alls[d][rows, :]
        g_hi = g.astype(BF16)
        g_lo = (g - g_hi.astype(F32)).astype(BF16)
        b = _dot(tri_bs[d], g_hi) + _dot(tri_bs[d], g_lo)
        b_tot = b[0:1, :] if rev else b[C - 1:C, :]
        q = q_ref[rows, :] * (GLA_DK ** -0.5)
        k = k_ref[rows, :]
        v = v_ref[rows, :].astype(BF16)
        q_d = (q * jnp.exp(b)).astype(BF16)
        k_d = (k * jnp.exp(-b)).astype(BF16)
        k_s = (k * jnp.exp(b_tot - b)).astype(BF16)
        att = lax.dot_general(q_d, k_d, NT_DIMS, preferred_element_type=F32)
        att = jnp.where(tris[d], att, 0.0).astype(BF16)
        s_t = st_ref[d]
        o = _dot(att, v) + lax.dot_general(q_d, s_t.astype(BF16), NT_DIMS, preferred_element_type=F32)
        o_ref[rows, :] = o
        st_ref[d] = s_t * jnp.exp(b_tot) + lax.dot_general(v, k_s, TN_DIMS, preferred_element_type=F32)

    n_chunks = lt // C
    for c in range(n_chunks):
        chunk(0, c)
        chunk(1, n_chunks - 1 - c)


def gla_scan(p, wa, ba, *, ctx_len, lt=SEQ_TILE):
    T = p.shape[0]
    n_tiles = T // lt
    n_ctx = ctx_len // lt
    tf = functools.partial(_time_tile, n_tiles=n_tiles, n_ctx_tiles=n_ctx, rev=False)
    tb = functools.partial(_time_tile, n_tiles=n_tiles, n_ctx_tiles=n_ctx, rev=True)

    def stream(tt):
        return [
            pl.BlockSpec((lt, GLA_DK), lambda h, t: (tt(t), OFF_Q // GLA_DK + h)),
            pl.BlockSpec((lt, GLA_DK), lambda h, t: (tt(t), OFF_K // GLA_DK + h)),
            pl.BlockSpec((lt, GLA_DV), lambda h, t: (tt(t), OFF_V // GLA_DV + h)),
            pl.BlockSpec((lt, LANE), lambda h, t: (tt(t), OFF_ALPHA // LANE)),
        ]

    out = jax.ShapeDtypeStruct((T, D_GLA_V), F32)
    return pl.pallas_call(
        functools.partial(_gla_kernel, lt=lt),
        grid=(GLA_HEADS, n_tiles),
        in_specs=stream(tf) + stream(tb) + [
            pl.BlockSpec((2, LANE, GLA_DK), lambda h, t: (0, 0, h)),
            pl.BlockSpec((2, GLA_DK), lambda h, t: (0, h)),
        ],
        out_specs=[pl.BlockSpec((lt, GLA_DV), lambda h, t: (tf(t), h)),
                   pl.BlockSpec((lt, GLA_DV), lambda h, t: (tb(t), h))],
        out_shape=[out, out],
        scratch_shapes=[pltpu.VMEM((2, GLA_DV, GLA_DK), F32)],
        compiler_params=_cparams(("arbitrary", "arbitrary")),
        name="gla_scan",
    )(p, p, p, p, p, p, p, p, wa, ba)


def _finish_kernel(u_ref, ys_f_ref, ys_b_ref, d_ref, wglu_ref, bglu_ref,
                   og_f_ref, og_b_ref, gate_ref, gn_ref,
                   x0_ref, conv_ref, s_ref, hb_ref, o_ref):
    u = u_ref[...]
    y = ys_f_ref[...] + ys_b_ref[...] + d_ref[...] * u
    zz = _gelu(y)
    s5 = zz * _sigmoid(_dot(zz.astype(BF16), wglu_ref[...]) + bglu_ref[...])
    o_ref[:, 0:D_S5] = s5.astype(BF16)

    gate = gate_ref[...]
    for h in range(GLA_HEADS):
        cols = slice(h * GLA_DV, (h + 1) * GLA_DV)
        o = og_f_ref[:, cols] + og_b_ref[:, cols]
        ms = jnp.mean(o * o, axis=-1, keepdims=True)
        on = o * lax.rsqrt(ms + EPS) * gn_ref[...]
        gt = gate[:, cols]
        o_ref[:, D_S5 + h * GLA_DV:D_S5 + (h + 1) * GLA_DV] = (on * (gt * _sigmoid(gt))).astype(BF16)

    s = s_ref[...]
    hy = x0_ref[...] * (conv_ref[...] + hb_ref[...] * s)
    o_ref[:, D_S5 + D_GLA_V:] = hy.astype(BF16)


def mixer_finish(p, ys_f, ys_b, s5_d, w_glu, b_glu, og_f, og_b, gn, x0, conv, s, hy_bias, *, tm):
    T = p.shape[0]
    row = lambda w, off=0: pl.BlockSpec((tm, w), lambda i: (i, off))
    const = lambda r, w: pl.BlockSpec((r, w), lambda i: (0, 0))
    return pl.pallas_call(
        _finish_kernel,
        grid=(T // tm,),
        in_specs=[
            row(D_S5, OFF_S5 // D_S5), row(D_S5), row(D_S5), const(1, D_S5), const(D_S5, D_S5), const(1, D_S5),
            row(D_GLA_V), row(D_GLA_V), row(D_GLA_V, OFF_GATE // D_GLA_V), const(1, GLA_DV),
            row(D_HY), row(D_HY), row(D_HY), const(1, D_HY),
        ],
        out_specs=pl.BlockSpec((tm, D_MODEL), lambda i: (i, 0)),
        out_shape=jax.ShapeDtypeStruct((T, D_MODEL), BF16),
        compiler_params=_cparams(("arbitrary",)),
        name="mixer_finish",
    )(p, ys_f, ys_b, s5_d.reshape(1, D_S5), w_glu, b_glu.reshape(1, D_S5),
      og_f, og_b, p, gn.reshape(1, GLA_DV), x0, conv, s, hy_bias.reshape(1, D_HY))


PEER_TE = 512
PEER_I1 = PEER_TE // PEER_KEYS


def _peer_kernel(x_ref, ua_ref, ub_ref, vp_ref, va_ref, s2_ref, e2_ref, s1p_ref, e1p_ref, s1a_ref, e1a_ref,
                 tau_ref, o_ref, actb_ref, acta_ref, pb_ref, pa_ref, *, tm, n_pairs):
    g = pl.program_id(1)
    K = PEER_KEYS

    @pl.when(g == 0)
    def _():
        o_ref[...] = jnp.zeros_like(o_ref)
        actb_ref[...] = jnp.zeros_like(actb_ref)

    def first_matmul(u_ref, act_ref):
        act_ref[...] = lax.dot_general(u_ref[...], x_ref[...], NT_DIMS, preferred_element_type=F32)

    def gate(act_ref, p_ref, s1_ref, e1_ref, base, valid):
        hs = K // 2
        tile3 = (hs // SUBLANE, SUBLANE, LANE)
        for cb in range(tm // LANE):
            cols = slice(cb * LANE, (cb + 1) * LANE)
            for half in range(2):
                rows = slice(half * hs, (half + 1) * hs)
                w = [None] * PEER_I1
                for h in range(PEER_HEADS):
                    tau = tau_ref[h, cb]
                    if valid is not None:
                        tau = jnp.where(valid, tau, -NEG_BIG)
                    s2 = s2_ref[h, rows, cols].reshape(tile3)
                    e2 = e2_ref[h, rows, cols].reshape(tile3)
                    for a in range(PEER_I1):
                        bcast = pl.ds(base + a, SUBLANE, stride=0)
                        sel = jnp.where(s1_ref[h, cb, bcast, :] + s2 >= tau, e1_ref[h, cb, bcast, :] * e2, 0.0)
                        w[a] = sel if w[a] is None else w[a] + sel
                for a in range(PEER_I1):
                    r = slice(a * K + half * hs, a * K + (half + 1) * hs)
                    act = act_ref[r, cols].reshape(tile3)
                    p_ref[r, cols] = (w[a] * _gelu(act)).reshape(hs, LANE).astype(BF16)

    gate(actb_ref, pb_ref, s1p_ref, e1p_ref, PEER_I1, None)
    first_matmul(ua_ref, acta_ref)
    o_ref[...] += _dot(vp_ref[...], pb_ref[...])
    gate(acta_ref, pa_ref, s1a_ref, e1a_ref, 0, g < n_pairs)
    first_matmul(ub_ref, actb_ref)
    o_ref[...] += _dot(va_ref[...], pa_ref[...])


def peer_dense(xn, u_tab, vt_tab, s1t, e1t, s2t, e2t, taut, *, tm):
    T, D = xn.shape
    E = u_tab.shape[0]
    H, K = PEER_HEADS, PEER_KEYS
    te = PEER_TE
    assert 2 * PEER_I1 == SUBLANE
    n_e = E // te
    n_pairs = n_e // 2
    last = n_e - 1
    once = pl.Buffered(1)
    tile_a = lambda g: jnp.minimum(2 * g, last)
    tile_b = lambda g: jnp.minimum(2 * g + 1, last)
    tile_p = lambda g: jnp.maximum(2 * g - 1, 0)
    i1_spec = lambda f: pl.BlockSpec((H, tm // LANE, SUBLANE, LANE), lambda i, g: (0, i, f(g) // 2, 0))
    tab_spec = pl.BlockSpec((H, K, tm), lambda i, g: (0, 0, i), pipeline_mode=once)
    return pl.pallas_call(
        functools.partial(_peer_kernel, tm=tm, n_pairs=n_pairs),
        grid=(T // tm, n_pairs + 1),
        in_specs=[
            pl.BlockSpec((tm, D), lambda i, g: (i, 0), pipeline_mode=once),
            pl.BlockSpec((te, D), lambda i, g: (tile_a(g), 0)),
            pl.BlockSpec((te, D), lambda i, g: (tile_b(g), 0)),
            pl.BlockSpec((D, te), lambda i, g: (0, tile_p(g))),
            pl.BlockSpec((D, te), lambda i, g: (0, tile_a(g))),
            tab_spec, tab_spec,
            i1_spec(tile_p), i1_spec(tile_p), i1_spec(tile_a), i1_spec(tile_a),
            pl.BlockSpec((H, tm // LANE, SUBLANE, LANE), lambda i, g: (0, i, 0, 0), pipeline_mode=once),
        ],
        out_specs=pl.BlockSpec((D, tm), lambda i, g: (0, i)),
        out_shape=jax.ShapeDtypeStruct((D, T), F32),
        scratch_shapes=[pltpu.VMEM((te, tm), F32), pltpu.VMEM((te, tm), F32),
                        pltpu.VMEM((te, tm), BF16), pltpu.VMEM((te, tm), BF16)],
        compiler_params=_cparams(("arbitrary", "arbitrary")),
        name="peer_dense",
    )(xn, u_tab, u_tab, vt_tab, vt_tab, s2t, e2t, s1t, e1t, s1t, e1t, taut)


NEG_BIG = -3.0e38
N_CAND = PEER_TOPK + 8 * 7 + 8


def _peer_prep_kernel(q_ref, k_ref, s1_ref, e1_ref, s2_ref, e2_ref, tau_ref, top_ref, cand_ref):
    R = PEER_TOPK
    half_w = PEER_DQ // 2

    def nt(a, b):
        return lax.dot_general(a, b, NT_DIMS, preferred_element_type=F32)

    def scores(half):
        qh = q_ref[:, half * half_w:(half + 1) * half_w]
        q_hi = qh.astype(BF16)
        q_lo = (qh - q_hi.astype(F32)).astype(BF16)
        kk = k_ref[half]
        k_hi = kk.astype(BF16)
        k_lo = (kk - k_hi.astype(F32)).astype(BF16)
        return nt(k_hi, q_hi) + (nt(k_hi, q_lo) + nt(k_lo, q_hi))

    def sorted_top(s, slot):
        work = s
        for r in range(R):
            m = jnp.max(work, axis=0, keepdims=True)
            top_ref[slot, r:r + 1, :] = m
            work = jnp.where(work == m, NEG_BIG, work)

    s1 = scores(0)
    s2 = scores(1)
    sorted_top(s1, 0)
    sorted_top(s2, 1)
    a = top_ref[0]
    b = top_ref[1]
    cand_ref[0:R, :] = a[0:1, :] + b
    for i in range(1, 8):
        cand_ref[R + 8 * (i - 1):R + 8 * i, :] = a[i:i + 1, :] + b[0:8, :]
    cand_ref[R + 56:R + 64, :] = a[8:16, :] + b[0:1, :]
    work = cand_ref[...]
    m0 = jnp.max(work, axis=0, keepdims=True)
    m = m0
    z = jnp.ones_like(m0)
    for r in range(1, R):
        work = jnp.where(work == m, NEG_BIG, work)
        m = jnp.max(work, axis=0, keepdims=True)
        z = z + jnp.exp(m - m0)
    e1 = jnp.exp(s1 - a[0:1, :]) * (1.0 / z)
    s2_ref[0] = s2
    e2_ref[0] = jnp.exp(s2 - b[0:1, :])
    for lt in range(s1.shape[1] // LANE):
        cols = slice(lt * LANE, (lt + 1) * LANE)
        s1_ref[0, lt] = s1[:, cols]
        e1_ref[0, lt] = e1[:, cols]
        tau_ref[0, lt] = jnp.broadcast_to(m[:, cols], (SUBLANE, LANE))


def peer_prep(q, keys, *, tm):
    T = q.shape[0]
    H, K = PEER_HEADS, PEER_KEYS
    tab = jax.ShapeDtypeStruct((H, K, T), F32)
    tab_spec = pl.BlockSpec((1, K, tm), lambda i, h: (h, 0, i))
    slab = jax.ShapeDtypeStruct((H, T // LANE, K, LANE), F32)
    slab_spec = pl.BlockSpec((1, tm // LANE, K, LANE), lambda i, h: (h, i, 0, 0))
    return pl.pallas_call(
        _peer_prep_kernel,
        grid=(T // tm, H),
        in_specs=[
            pl.BlockSpec((tm, PEER_DQ), lambda i, h: (i, h)),
            pl.BlockSpec((None, 2, K, PEER_DQ // 2), lambda i, h: (h, 0, 0, 0)),
        ],
        out_specs=[slab_spec, slab_spec, tab_spec, tab_spec,
                   pl.BlockSpec((1, tm // LANE, SUBLANE, LANE), lambda i, h: (h, i, 0, 0))],
        out_shape=[slab, slab, tab, tab, jax.ShapeDtypeStruct((H, T // LANE, SUBLANE, LANE), F32)],
        scratch_shapes=[pltpu.VMEM((2, PEER_TOPK, tm), F32), pltpu.VMEM((N_CAND, tm), F32)],
        compiler_params=_cparams(("arbitrary", "arbitrary")),
        name="peer_prep",
    )(q, keys)


def _transpose_gated_res_kernel(h_ref, ft_ref, gate_ref, o_ref, *, ctx_len, tm):
    i = pl.program_id(0)
    row = i * tm + lax.broadcasted_iota(jnp.int32, (tm, 1), 0)
    gate = jnp.where(row < ctx_len, gate_ref[0:1, :], gate_ref[1:2, :])
    o_ref[...] = h_ref[...] + gate * ft_ref[...].T


def transpose_gated_res(h, ft, gates, *, ctx_len, tm):
    T, D = h.shape
    return pl.pallas_call(
        functools.partial(_transpose_gated_res_kernel, ctx_len=ctx_len, tm=tm),
        grid=(T // tm,),
        in_specs=[
            pl.BlockSpec((tm, D), lambda i: (i, 0)),
            pl.BlockSpec((D, tm), lambda i: (0, i)),
            pl.BlockSpec((2, D), lambda i: (0, 0)),
        ],
        out_specs=pl.BlockSpec((tm, D), lambda i: (i, 0)),
        out_shape=jax.ShapeDtypeStruct((T, D), F32),
        compiler_params=_cparams(("arbitrary",)),
        name="transpose_gated_res",
    )(h, ft, gates)


def _rmsnorm_kernel(x_ref, g_ref, o_ref):
    x = x_ref[...]
    ms = jnp.mean(x * x, axis=-1, keepdims=True)
    o_ref[...] = x * lax.rsqrt(ms + EPS) * g_ref[...]


def rmsnorm_rows(x, g, *, tm):
    T, D = x.shape
    return pl.pallas_call(
        _rmsnorm_kernel,
        grid=(T // tm,),
        in_specs=[pl.BlockSpec((tm, D), lambda i: (i, 0)), pl.BlockSpec((1, D), lambda i: (0, 0))],
        out_specs=pl.BlockSpec((tm, D), lambda i: (i, 0)),
        out_shape=jax.ShapeDtypeStruct((T, D), F32),
        compiler_params=_cparams(("arbitrary",)),
        name="rmsnorm_rows",
    )(x, g.reshape(1, D))


def grid_sincos(n_tokens, dim):
    rows = n_tokens // GRID_W
    row = jnp.repeat(jnp.arange(rows), GRID_W).astype(F32)
    col = jnp.tile(jnp.arange(GRID_W), rows).astype(F32)
    n_freq = dim // 4
    omega = 1.0 / (POS_BASE ** (jnp.arange(n_freq, dtype=F32) / n_freq))

    def enc(pp):
        a = pp[:, None] * omega[None, :]
        return jnp.concatenate([jnp.sin(a), jnp.cos(a)], axis=-1)

    return jnp.concatenate([enc(row), enc(col)], axis=-1)


def _split_bf16(a):
    hi = a.astype(BF16)
    return hi, (a - hi.astype(F32)).astype(BF16)


def _dot3s(a, b_hi, b_lo):
    a_hi, a_lo = _split_bf16(a)
    return _dot(a_hi, b_hi) + (_dot(a_hi, b_lo) + _dot(a_lo, b_hi))


def _dot3(a, b):
    return _dot3s(a, *_split_bf16(b))


def _adaln_kernel(c_ref, w_ref, b_ref, o_ref):
    c = c_ref[...]
    o_ref[...] = _dot3(c * _sigmoid(c), w_ref[...]) + b_ref[...]


def adaln(cond, w, b, *, tn=1536):
    R, D = cond.shape
    N = w.shape[1]
    return pl.pallas_call(
        _adaln_kernel,
        grid=(N // tn,),
        in_specs=[pl.BlockSpec((R, D), lambda j: (0, 0)),
                  pl.BlockSpec((D, tn), lambda j: (0, j)),
                  pl.BlockSpec((1, tn), lambda j: (0, j))],
        out_specs=pl.BlockSpec((R, tn), lambda j: (0, j)),
        out_shape=jax.ShapeDtypeStruct((R, N), F32),
        compiler_params=_cparams(("arbitrary",)),
        name="adaln",
    )(cond, w, b.reshape(1, N))


def _hyena_pre_kernel(x0_ref, x1_ref, v_ref, w_ref, b_ref, x0o_ref, s_ref, *, ctx_len):
    T = x0_ref.shape[0]
    row = lax.broadcasted_iota(jnp.int32, (T, 1), 0)
    first = jnp.logical_or(row == 0, row == ctx_len)
    last = jnp.logical_or(row == ctx_len - 1, row == T - 1)

    def conv(ref, part):
        x = ref[...]
        prev = jnp.where(first, 0.0, pltpu.roll(x, 1, 0))
        nxt = jnp.where(last, 0.0, pltpu.roll(x, T - 1, 0))
        w = lambda tap: w_ref[tap, part:part + 1, :]
        return w(0) * prev + w(1) * x + w(2) * nxt + b_ref[part:part + 1, :]

    x0o_ref[...] = conv(x0_ref, 0)
    s_ref[...] = conv(x1_ref, 1) * conv(v_ref, 2)


def hyena_pre(p, conv_w, conv_b, *, ctx_len):
    T = p.shape[0]
    nb = D_HY // LANE
    col = lambda part: pl.BlockSpec((T, LANE), lambda c: (0, OFF_HY // LANE + part * nb + c))
    out = jax.ShapeDtypeStruct((T, D_HY), F32)
    return pl.pallas_call(
        functools.partial(_hyena_pre_kernel, ctx_len=ctx_len),
        grid=(nb,),
        in_specs=[col(0), col(1), col(2),
                  pl.BlockSpec((HY_SHORT, 3, LANE), lambda c: (0, 0, c)),
                  pl.BlockSpec((3, LANE), lambda c: (0, c))],
        out_specs=[pl.BlockSpec((T, LANE), lambda c: (0, c))] * 2,
        out_shape=[out, out],
        compiler_params=_cparams(("arbitrary",)),
        name="hyena_pre",
    )(p, p, p, conv_w.reshape(HY_SHORT, 3, D_HY), conv_b.reshape(3, D_HY))


def _hyena_filter_kernel(band_ref, w1_ref, b1_ref, w2_ref, b2_ref, w3_ref, fr_ref, dec_ref, o_ref, *, n, tr):
    i = pl.program_id(0)
    ri = (i * tr + lax.broadcasted_iota(jnp.int32, (tr, 1), 0)).astype(F32)
    t = ri * (1.0 / (n - 1))
    w = ri * (2.0 * math.pi / n)
    lane = lax.broadcasted_iota(jnp.int32, (1, LANE), 1)
    arg = w * band_ref[...]
    z = jnp.where(lane == 0, t,
                  jnp.where(lane <= HY_BANDS, jnp.cos(arg),
                            jnp.where(lane <= 2 * HY_BANDS, -jnp.sin(arg), 0.0)))
    h = jnp.sin(fr_ref[0:1, :] * (_dot3(z, w1_ref[...]) + b1_ref[...]))
    h = jnp.sin(fr_ref[1:2, :] * (_dot3(h, w2_ref[...]) + b2_ref[...]))
    h = _dot3(h, w3_ref[...]) * jnp.exp(-t * dec_ref[...])
    col = lax.broadcasted_iota(jnp.int32, (1, 2 * D_HY), 1)
    o_ref[...] = jnp.where(jnp.logical_and(ri == 0.0, col >= D_HY), 0.0, h)


def hyena_filter(n, w1, b1, w2, b2, w3, freq, decay, *, tr=256):
    bands = jnp.linspace(1e-4, HY_BANDS - 1, HY_BANDS, dtype=F32)
    band_row = jnp.zeros((1, LANE), F32).at[0, 1:1 + HY_BANDS].set(bands).at[0, 1 + HY_BANDS:1 + 2 * HY_BANDS].set(bands)
    w1p = jnp.zeros((LANE, HY_HIDDEN), F32).at[:HY_FEAT].set(w1)
    const = lambda a: pl.BlockSpec(a.shape, lambda i: (0,) * a.ndim)
    args = (band_row, w1p, b1.reshape(1, -1), w2, b2.reshape(1, -1), w3, freq,
            jnp.abs(decay).reshape(1, 2 * D_HY))
    return pl.pallas_call(
        functools.partial(_hyena_filter_kernel, n=n, tr=tr),
        grid=(n // tr,),
        in_specs=[const(a) for a in args],
        out_specs=pl.BlockSpec((tr, 2 * D_HY), lambda i: (i, 0)),
        out_shape=jax.ShapeDtypeStruct((n, 2 * D_HY), F32),
        compiler_params=_cparams(("arbitrary",)),
        name="hyena_filter",
    )(*args)


FFT_R = 128
FFT_N = FFT_R * FFT_R
FFT_CH = 32


def _dft_tables():
    r = np.arange(FFT_R)
    ang = 2.0 * np.pi * np.outer(r, r) / FFT_R
    c, s = np.cos(ang), np.sin(ang)
    angt = 2.0 * np.pi * np.outer(r, r) / FFT_N
    ct, st = np.cos(angt), np.sin(angt)

    def split(m):
        m = jnp.asarray(m, F32)
        hi = m.astype(BF16)
        return jnp.stack([hi, (m - hi.astype(F32)).astype(BF16)])

    fwd_a = split(np.concatenate([c, -s], axis=1))
    fwd_b = split(np.block([[c, -s], [s, c]]))
    inv_b = split(np.block([[c, s], [-s, c]]) / FFT_N)
    inv_a = split(np.concatenate([c, -s], axis=0))
    tw = (jnp.asarray(ct, F32), jnp.asarray(-st, F32))
    twc = (jnp.asarray(ct, F32), jnp.asarray(st, F32))
    return fwd_a, fwd_b, inv_b, inv_a, tw, twc


def _twiddle_transpose(y, twr, twi, dst_ref, nc):
    y = y.reshape(nc, FFT_R, 2 * FFT_R)
    yr, yi = y[:, :, :FFT_R], y[:, :, FFT_R:]
    zr = yr * twr - yi * twi
    zi = yr * twi + yi * twr
    for c in range(nc):
        dst_ref[c * FFT_R:(c + 1) * FFT_R, 0:FFT_R] = zr[c].T
        dst_ref[c * FFT_R:(c + 1) * FFT_R, FFT_R:2 * FFT_R] = zi[c].T


def _hyena_fwd_kernel(x_ref, fa_ref, twr_ref, twi_ref, fb_ref, o_ref, l_ref, yt_ref, *, nc):
    j = pl.program_id(1)
    half = FFT_R // 2

    @pl.when(j == 0)
    def _():
        zeros = jnp.zeros((half, LANE), F32)

        def body(b, carry):
            t = jnp.concatenate([x_ref[pl.ds(b, half, stride=FFT_R), :], zeros], axis=0)
            l_ref[pl.ds(b, LANE, stride=FFT_R), :] = t.T
            return carry

        lax.fori_loop(0, FFT_R, body, 0, unroll=8)

    rows = nc * FFT_R
    lc = l_ref[pl.ds(pl.multiple_of(j * rows, rows), rows), :]
    y = _dot3s(lc, fa_ref[0], fa_ref[1])
    _twiddle_transpose(y, twr_ref[...], twi_ref[...], yt_ref, nc)
    o_ref[...] = _dot3s(yt_ref[...], fb_ref[0], fb_ref[1])


def hyena_fwd_dft(x, tables):
    n, C = x.shape
    assert n * 2 == FFT_N and C % LANE == 0
    fwd_a, fwd_b, _, _, (twr, twi), _ = tables
    nc = FFT_CH
    steps = LANE // nc
    const = lambda a: pl.BlockSpec(a.shape, lambda cb, j: (0,) * a.ndim)
    return pl.pallas_call(
        functools.partial(_hyena_fwd_kernel, nc=nc),
        grid=(C // LANE, steps),
        in_specs=[pl.BlockSpec((n, LANE), lambda cb, j: (0, cb)), const(fwd_a), const(twr), const(twi), const(fwd_b)],
        out_specs=pl.BlockSpec((nc * FFT_R, 2 * FFT_R), lambda cb, j: (cb * steps + j, 0)),
        out_shape=jax.ShapeDtypeStruct((C * FFT_R, 2 * FFT_R), F32),
        scratch_shapes=[pltpu.VMEM((LANE * FFT_R, LANE), F32), pltpu.VMEM((nc * FFT_R, 2 * FFT_R), F32)],
        compiler_params=_cparams(("arbitrary", "arbitrary")),
        name="hyena_fwd_dft",
    )(x, fwd_a, twr, twi, fwd_b)


def _hyena_inv_kernel(s_ref, kf_ref, kb_ref, gb_ref, twr_ref, twi_ref, ga_ref, o_ref, lr_ref, at_ref, *, nc, steps):
    j = pl.program_id(1)
    R = FFT_R
    s, kf, kb = s_ref[...], kf_ref[...], kb_ref[...]
    sr, si = s[:, :R], s[:, R:]
    kr = kf[:, :R] + kb[:, :R]
    ki = kf[:, R:] - kb[:, R:]
    p = jnp.concatenate([sr * kr - si * ki, sr * ki + si * kr], axis=1)
    a = _dot3s(p, gb_ref[0], gb_ref[1])
    _twiddle_transpose(a, twr_ref[...], twi_ref[...], at_ref, nc)
    rows = nc * R
    lr_ref[pl.ds(pl.multiple_of(j * rows, rows), rows), :] = _dot3s(at_ref[...], ga_ref[0], ga_ref[1])

    @pl.when(j == steps - 1)
    def _():
        def body(b, carry):
            t = lr_ref[pl.ds(b, LANE, stride=R), :]
            o_ref[pl.ds(b, R // 2, stride=R), :] = t.T[0:R // 2, :]
            return carry

        lax.fori_loop(0, R, body, 0, unroll=8)


def hyena_inv_dft(s_f, k_f, tables):
    C = s_f.shape[0] // FFT_R
    _, _, inv_b, inv_a, _, (twr, twi) = tables
    nc = FFT_CH
    steps = LANE // nc
    kb_off = C // nc
    const = lambda a: pl.BlockSpec(a.shape, lambda cb, j: (0,) * a.ndim)
    blk = lambda off: pl.BlockSpec((nc * FFT_R, 2 * FFT_R), lambda cb, j: (off + cb * steps + j, 0))
    return pl.pallas_call(
        functools.partial(_hyena_inv_kernel, nc=nc, steps=steps),
        grid=(C // LANE, steps),
        in_specs=[blk(0), blk(0), blk(kb_off), const(inv_b), const(twr), const(twi), const(inv_a)],
        out_specs=pl.BlockSpec((FFT_N // 2, LANE), lambda cb, j: (0, cb)),
        out_shape=jax.ShapeDtypeStruct((FFT_N // 2, C), F32),
        scratch_shapes=[pltpu.VMEM((LANE * FFT_R, LANE), F32), pltpu.VMEM((nc * FFT_R, 2 * FFT_R), F32)],
        compiler_params=_cparams(("arbitrary", "arbitrary")),
        name="hyena_inv_dft",
    )(s_f, k_f, k_f, inv_b, twr, twi, inv_a)


def _hyena_ctx_kernel(s_ref, k_ref, fw_ref, iv_ref, o_ref):
    n = s_ref.shape[0]
    N = 2 * n
    xs = _dot3(fw_ref[...], s_ref[...])
    xk = _dot3(fw_ref[...], k_ref[...])
    sr, si = xs[:N], xs[N:]
    kr = xk[:N, :D_HY] + xk[:N, D_HY:]
    ki = xk[N:, :D_HY] - xk[N:, D_HY:]
    p = jnp.concatenate([sr * kr - si * ki, sr * ki + si * kr], axis=0)
    o_ref[...] = _dot3(iv_ref[...], p)


def hyena_ctx_conv(s_c, k_c):
    n = s_c.shape[0]
    N = 2 * n
    ang = 2.0 * np.pi * np.outer(np.arange(N), np.arange(N)) / N
    c, s = np.cos(ang), np.sin(ang)
    fw = jnp.asarray(np.concatenate([c[:, :n], -s[:, :n]], axis=0), F32)
    iv = jnp.asarray(np.concatenate([c[:n, :], -s[:n, :]], axis=1) / N, F32)
    full = lambda a: pl.BlockSpec(a.shape, lambda i: (0,) * a.ndim)
    return pl.pallas_call(
        _hyena_ctx_kernel,
        grid=(1,),
        in_specs=[full(s_c), full(k_c), full(fw), full(iv)],
        out_specs=pl.BlockSpec((n, D_HY), lambda i: (0, 0)),
        out_shape=jax.ShapeDtypeStruct((n, D_HY), F32),
        compiler_params=_cparams(("arbitrary",)),
        name="hyena_ctx_conv",
    )(s_c, k_c, fw, iv)


def hyena_mixer_parts(p, conv_w, conv_b, filt, tables, *, ctx_len):
    T = p.shape[0]
    x0, s = hyena_pre(p, conv_w, conv_b, ctx_len=ctx_len)
    k_lat = hyena_filter(T - ctx_len, *filt)
    k_ctx = hyena_filter(ctx_len, *filt)
    conv_l = hyena_inv_dft(hyena_fwd_dft(s[ctx_len:], tables), hyena_fwd_dft(k_lat, tables), tables)
    conv_c = hyena_ctx_conv(s[:ctx_len], k_ctx)
    return x0, jnp.concatenate([conv_c, conv_l], axis=0), s


def _pack_w_in(w_in_l):
    widths = (D_S5, D_GLA_K, D_GLA_K, D_GLA_V, D_GLA_V, 2 * GLA_RANK, 3 * D_HY)
    offs = [0]
    for wd in widths:
        offs.append(offs[-1] + wd)
    s5, q, k, v, gate, alpha, hy = (w_in_l[:, offs[i]:offs[i + 1]] for i in range(7))
    pad = jnp.zeros((w_in_l.shape[0], IN_PACKED - OFF_ALPHA - 2 * GLA_RANK), w_in_l.dtype)
    return jnp.concatenate([v, gate, s5, q, k, hy, alpha, pad], axis=1).astype(BF16)


def kernel(x, c, ctx, c_ctx, w_ada, b_ada, g_norm1, g_norm2, w_in, s5_a_re, s5_a_im, s5_log_step, s5_b_re, s5_b_im, s5_c_re, s5_c_im, s5_d, s5_w_glu, s5_b_glu, gla_w_alpha, gla_b_alpha, gla_g_norm, hy_conv_w, hy_conv_b, hy_f_w1, hy_f_b1, hy_f_w2, hy_f_b2, hy_f_w3, hy_f_freq, hy_decay, hy_bias, w_out, peer_w_q, peer_keys, peer_u, peer_v, g_final):
    L = x.shape[1]
    Lc = ctx.shape[1]
    T = L + Lc
    TM = 768
    h = jnp.concatenate([ctx[0], x[0] + grid_sincos(L, D_MODEL)], axis=0)
    cond = jnp.zeros((SUBLANE, D_MODEL), F32).at[0].set(c_ctx).at[1].set(c[0])
    tables = _dft_tables()

    for l in range(DEPTH):
        m = adaln(cond, w_ada[l], b_ada[l])[0:2]
        sh1, sc1, gt1, sh2, sc2, gt2 = jnp.split(m, 6, axis=-1)
        mod1 = jnp.stack([sh1[0], sc1[0], sh1[1], sc1[1]], axis=0)
        mod2 = jnp.stack([sh2[0], sc2[0], sh2[1], sc2[1]], axis=0)

        p = norm_mod_matmul(h, g_norm1[l], mod1, _pack_w_in(w_in[l]), ctx_len=Lc, tm=TM, tn=768)

        bm, cm, pw = s5_prepare(s5_a_re[l], s5_a_im[l], s5_log_step[l], s5_b_re[l], s5_b_im[l],
                                s5_c_re[l], s5_c_im[l])
        ys_f, ys_b = s5_scan(p, bm, cm, pw, ctx_len=Lc)

        wa = jnp.zeros((2, LANE, D_GLA_K), F32)
        wa = wa.at[0, 0:GLA_RANK].set(gla_w_alpha[l, 0]).at[1, GLA_RANK:2 * GLA_RANK].set(gla_w_alpha[l, 1])
        wa = wa.astype(BF16)
        og_f, og_b = gla_scan(p, wa, gla_b_alpha[l], ctx_len=Lc)

        filt = (hy_f_w1[l], hy_f_b1[l], hy_f_w2[l], hy_f_b2[l], hy_f_w3[l], hy_f_freq[l], hy_decay[l])
        x0, conv, s = hyena_mixer_parts(p, hy_conv_w[l], hy_conv_b[l], filt, tables, ctx_len=Lc)

        mix = mixer_finish(p, ys_f, ys_b, s5_d[l], s5_w_glu[l].astype(BF16), s5_b_glu[l],
                           og_f, og_b, gla_g_norm[l], x0, conv, s, hy_bias[l], tm=256)
        h = matmul_gated_res(mix, w_out[l].astype(BF16), h, gt1, ctx_len=Lc, tm=TM, tn=1024)

        q, xn = norm_mod_matmul(h, g_norm2[l], mod2, peer_w_q[l].astype(BF16),
                                ctx_len=Lc, tm=TM, tn=1024, emit_xn=True)
        s1t, e1t, s2t, e2t, taut = peer_prep(q, peer_keys[l], tm=TM)
        ft = peer_dense(xn, peer_u[l].astype(BF16), peer_v[l].T.astype(BF16),
                        s1t, e1t, s2t, e2t, taut, tm=TM)
        h = transpose_gated_res(h, ft, gt2, ctx_len=Lc, tm=256)

    out = rmsnorm_rows(h[Lc:], g_final, tm=512)
    return out[None]
```

```python
import functools
import math

import jax
import jax.numpy as jnp
import numpy as np
from jax import lax
from jax.experimental import pallas as pl
from jax.experimental.pallas import tpu as pltpu

F32 = jnp.float32
BF16 = jnp.bfloat16

D_MODEL = 2048
DEPTH = 4
GRID_W = 64
EPS = 1e-6
POS_BASE = 10000.0

D_S5 = D_MODEL // 4
S5_GROUP = 16
S5_GROUPS = D_S5 // S5_GROUP
S5_STATE = 64
S5_BLK_GROUPS = 8
S5_BLK_CH = S5_BLK_GROUPS * S5_GROUP
S5_BLK_ST = S5_BLK_GROUPS * S5_STATE
S5_NBLK = S5_GROUPS // S5_BLK_GROUPS

GLA_HEADS = 4
D_GLA_K = D_MODEL // 4
D_GLA_V = D_MODEL // 2
GLA_DK = D_GLA_K // GLA_HEADS
GLA_DV = D_GLA_V // GLA_HEADS
GLA_RANK = 16
GLA_TAU = 16.0
GLA_CHUNK = 64

D_HY = D_MODEL // 4
HY_SHORT = 3
HY_BANDS = 16
HY_FEAT = 1 + 2 * HY_BANDS
HY_HIDDEN = 64
HY_TARGET = 1e-2
HY_MIN_DECAY = -math.log(HY_TARGET) / 1.5
HY_MAX_DECAY = -math.log(HY_TARGET) / 0.3

PEER_HEADS = 8
PEER_KEYS = 128
PEER_EXPERTS = PEER_KEYS * PEER_KEYS
PEER_DQ = 256
PEER_TOPK = 16

OFF_V = 0
OFF_GATE = OFF_V + D_GLA_V
OFF_S5 = OFF_GATE + D_GLA_V
OFF_Q = OFF_S5 + D_S5
OFF_K = OFF_Q + D_GLA_K
OFF_HY = OFF_K + D_GLA_K
OFF_ALPHA = OFF_HY + 3 * D_HY
LANE = 128
SUBLANE = 8
IN_PACKED = OFF_ALPHA + 2 * LANE

SEQ_TILE = 256
VMEM_LIMIT = 56 * 1024 * 1024

NT_DIMS = (((1,), (1,)), ((), ()))
TN_DIMS = (((0,), (0,)), ((), ()))


def _cparams(sem):
    return pltpu.CompilerParams(dimension_semantics=sem, vmem_limit_bytes=VMEM_LIMIT)


def _dot(a, b):
    return jnp.dot(a, b, preferred_element_type=F32)


def _gelu(x):
    k1 = -2.0 * math.sqrt(2.0 / math.pi) * math.log2(math.e)
    k2 = 0.044715 * k1
    return x / (1.0 + jnp.exp2(x * (k1 + k2 * (x * x))))


def _sigmoid(x):
    return 1.0 / (1.0 + jnp.exp(-x))


def _norm_mod_matmul_kernel(x_ref, g_ref, mod_ref, w_ref, o_ref, *rest, ctx_len, tm, emit_xn):
    if emit_xn:
        xo_ref, xn_ref = rest
    else:
        (xn_ref,) = rest
    i = pl.program_id(0)
    j = pl.program_id(1)

    @pl.when(j == 0)
    def _():
        x = x_ref[...]
        ms = jnp.mean(x * x, axis=-1, keepdims=True)
        y = x * lax.rsqrt(ms + EPS) * g_ref[...]
        row = i * tm + lax.broadcasted_iota(jnp.int32, (tm, 1), 0)
        is_ctx = row < ctx_len
        shift = jnp.where(is_ctx, mod_ref[0:1, :], mod_ref[2:3, :])
        scale = jnp.where(is_ctx, mod_ref[1:2, :], mod_ref[3:4, :])
        xn = (y * (1.0 + scale) + shift).astype(BF16)
        xn_ref[...] = xn
        if emit_xn:
            xo_ref[...] = xn

    o_ref[...] = _dot(xn_ref[...], w_ref[...])


def norm_mod_matmul(x, g, mod, w, *, ctx_len, tm, tn, emit_xn=False):
    T, D = x.shape
    N = w.shape[1]
    out_shape = [jax.ShapeDtypeStruct((T, N), F32)]
    out_specs = [pl.BlockSpec((tm, tn), lambda i, j: (i, j))]
    if emit_xn:
        out_shape.append(jax.ShapeDtypeStruct((T, D), BF16))
        out_specs.append(pl.BlockSpec((tm, D), lambda i, j: (i, 0)))
    res = pl.pallas_call(
        functools.partial(_norm_mod_matmul_kernel, ctx_len=ctx_len, tm=tm, emit_xn=emit_xn),
        grid=(T // tm, N // tn),
        in_specs=[
            pl.BlockSpec((tm, D), lambda i, j: (i, 0)),
            pl.BlockSpec((1, D), lambda i, j: (0, 0)),
            pl.BlockSpec((4, D), lambda i, j: (0, 0)),
            pl.BlockSpec((D, tn), lambda i, j: (0, j)),
        ],
        out_specs=out_specs,
        out_shape=out_shape,
        scratch_shapes=[pltpu.VMEM((tm, D), BF16)],
        compiler_params=_cparams(("arbitrary", "arbitrary")),
        name="norm_mod_matmul",
    )(x, g.reshape(1, D), mod, w)
    return res if emit_xn else res[0]


def _matmul_gated_res_kernel(a_ref, w_ref, r_ref, gate_ref, o_ref, *, ctx_len, tm):
    i = pl.program_id(0)
    row = i * tm + lax.broadcasted_iota(jnp.int32, (tm, 1), 0)
    gate = jnp.where(row < ctx_len, gate_ref[0:1, :], gate_ref[1:2, :])
    o_ref[...] = r_ref[...] + gate * _dot(a_ref[...], w_ref[...])


def matmul_gated_res(a, w, res, gates, *, ctx_len, tm, tn):
    T, K = a.shape
    N = w.shape[1]
    return pl.pallas_call(
        functools.partial(_matmul_gated_res_kernel, ctx_len=ctx_len, tm=tm),
        grid=(T // tm, N // tn),
        in_specs=[
            pl.BlockSpec((tm, K), lambda i, j: (i, 0)),
            pl.BlockSpec((K, tn), lambda i, j: (0, j)),
            pl.BlockSpec((tm, tn), lambda i, j: (i, j)),
            pl.BlockSpec((2, tn), lambda i, j: (0, j)),
        ],
        out_specs=pl.BlockSpec((tm, tn), lambda i, j: (i, j)),
        out_shape=jax.ShapeDtypeStruct((T, N), F32),
        compiler_params=_cparams(("arbitrary", "arbitrary")),
        name="matmul_gated_res",
    )(a, w, res, gates)


def _time_tile(t, n_tiles, n_ctx_tiles, rev):
    if not rev:
        return t
    return jnp.where(t < n_ctx_tiles, n_ctx_tiles - 1 - t, n_tiles - 1 - (t - n_ctx_tiles))


def _s5_kernel(uf_ref, ub_ref, bm_ref, cm_ref, pw_ref, yf_ref, yb_ref, h_ref, c_ref, *, lt):
    t = pl.program_id(1)
    ns = S5_BLK_ST
    u_refs = (uf_ref, ub_ref)
    y_refs = (yf_ref, yb_ref)

    @pl.when(t == 0)
    def _():
        c_ref[...] = jnp.zeros_like(c_ref)

    for d in range(2):
        bu = _dot(u_refs[d][...].astype(BF16), bm_ref[d])
        h_ref[d, 0] = bu[:, :ns]
        h_ref[d, 1] = bu[:, ns:]

    n_grp = lt // 8

    def group_update(d, g, c_re, c_im):
        rev = d == 1
        gi = (n_grp - 1 - g) if rev else g
        rows = pl.ds(pl.multiple_of(gi * 8, 8), 8)
        a_re = h_ref[d, 0, rows, :]
        a_im = h_ref[d, 1, rows, :]
        for k, s in enumerate((1, 2, 4)):
            l_re = pw_ref[d, 8 + 8 * k:16 + 8 * k, :ns]
            l_im = pw_ref[d, 8 + 8 * k:16 + 8 * k, ns:]
            shift = (8 - s) if rev else s
            s_re = pltpu.roll(a_re, shift, 0)
            s_im = pltpu.roll(a_im, shift, 0)
            a_re = a_re + (l_re * s_re - l_im * s_im)
            a_im = a_im + (l_re * s_im + l_im * s_re)
        p_re = pw_ref[d, 0:8, :ns]
        p_im = pw_ref[d, 0:8, ns:]
        a_re = a_re + (p_re * c_re - p_im * c_im)
        a_im = a_im + (p_re * c_im + p_im * c_re)
        h_ref[d, 0, rows, :] = a_re
        h_ref[d, 1, rows, :] = a_im
        edge = slice(0, 1) if rev else slice(7, 8)
        return a_re[edge, :], a_im[edge, :]

    def body(g, carry):
        f_re, f_im, b_re, b_im = carry
        f_re, f_im = group_update(0, g, f_re, f_im)
        b_re, b_im = group_update(1, g, b_re, b_im)
        return f_re, f_im, b_re, b_im

    carry = lax.fori_loop(0, n_grp, body, (c_ref[0, 0], c_ref[0, 1], c_ref[1, 0], c_ref[1, 1]), unroll=2)
    c_ref[0, 0], c_ref[0, 1], c_ref[1, 0], c_ref[1, 1] = carry

    for d in range(2):
        y_refs[d][...] = (_dot(h_ref[d, 0].astype(BF16), cm_ref[d, :ns, :])
                          + _dot(h_ref[d, 1].astype(BF16), cm_ref[d, ns:, :]))


def s5_scan(p, bm, cm, pw, *, ctx_len, lt=SEQ_TILE):
    T = p.shape[0]
    n_tiles = T // lt
    n_ctx = ctx_len // lt
    tf = functools.partial(_time_tile, n_tiles=n_tiles, n_ctx_tiles=n_ctx, rev=False)
    tb = functools.partial(_time_tile, n_tiles=n_tiles, n_ctx_tiles=n_ctx, rev=True)
    ucol = OFF_S5 // S5_BLK_CH
    out = jax.ShapeDtypeStruct((T, D_S5), F32)
    return pl.pallas_call(
        functools.partial(_s5_kernel, lt=lt),
        grid=(S5_NBLK, n_tiles),
        in_specs=[
            pl.BlockSpec((lt, S5_BLK_CH), lambda b, t: (tf(t), ucol + b)),
            pl.BlockSpec((lt, S5_BLK_CH), lambda b, t: (tb(t), ucol + b)),
            pl.BlockSpec((2, None, S5_BLK_CH, 2 * S5_BLK_ST), lambda b, t: (0, b, 0, 0)),
            pl.BlockSpec((2, None, 2 * S5_BLK_ST, S5_BLK_CH), lambda b, t: (0, b, 0, 0)),
            pl.BlockSpec((2, None, 32, 2 * S5_BLK_ST), lambda b, t: (0, b, 0, 0)),
        ],
        out_specs=[pl.BlockSpec((lt, S5_BLK_CH), lambda b, t: (tf(t), b)),
                   pl.BlockSpec((lt, S5_BLK_CH), lambda b, t: (tb(t), b))],
        out_shape=[out, out],
        scratch_shapes=[pltpu.VMEM((2, 2, lt, S5_BLK_ST), F32), pltpu.VMEM((2, 2, 1, S5_BLK_ST), F32)],
        compiler_params=_cparams(("arbitrary", "arbitrary")),
        name="s5_scan",
    )(p, p, bm, cm, pw)


def _cmul(a, b):
    return a[0] * b[0] - a[1] * b[1], a[0] * b[1] + a[1] * b[0]


def s5_prepare(a_re, a_im, log_step, b_re, b_im, c_re, c_im):
    G, P, Cg = S5_GROUPS, S5_STATE, S5_GROUP
    dt = jnp.exp(log_step)[..., None]
    er = jnp.exp(a_re * dt)
    lam1 = (er * jnp.cos(a_im * dt), er * jnp.sin(a_im * dt))
    den = a_re * a_re + a_im * a_im
    xr, xi = lam1[0] - 1.0, lam1[1]
    coef = ((xr * a_re + xi * a_im) / den, (xi * a_re - xr * a_im) / den)
    bb_re = coef[0][..., None] * b_re - coef[1][..., None] * b_im
    bb_im = coef[0][..., None] * b_im + coef[1][..., None] * b_re
    pows = [lam1]
    for _ in range(7):
        pows.append(_cmul(pows[-1], lam1))

    eye = jnp.eye(S5_BLK_GROUPS, dtype=F32)

    def blockdiag_in(m):
        m = m.reshape(2, S5_NBLK, S5_BLK_GROUPS, P, Cg)
        return jnp.einsum('dbgpc,gh->dbgchp', m, eye).reshape(2, S5_NBLK, S5_BLK_CH, S5_BLK_ST)

    def blockdiag_out(m):
        m = m.reshape(2, S5_NBLK, S5_BLK_GROUPS, Cg, P)
        return jnp.einsum('dbgcp,gh->dbgphc', m, eye).reshape(2, S5_NBLK, S5_BLK_ST, S5_BLK_CH)

    bm = jnp.concatenate([blockdiag_in(bb_re), blockdiag_in(bb_im)], axis=-1).astype(BF16)
    cm = jnp.concatenate([blockdiag_out(c_re), blockdiag_out(-c_im)], axis=-2).astype(BF16)

    def lay(v):
        return v.reshape(2, S5_NBLK, S5_BLK_ST)

    def table(rev):
        power = lambda k: jnp.concatenate([lay(pows[k][0]), lay(pows[k][1])], axis=-1)
        rows = [power(7 - r if rev else r) for r in range(8)]
        zero = jnp.zeros_like(rows[0])
        for s in (1, 2, 4):
            for r in range(8):
                inside = (r < 8 - s) if rev else (r >= s)
                rows.append(power(s - 1) if inside else zero)
        return jnp.stack(rows, axis=2)

    return bm, cm, jnp.stack([table(False)[0], table(True)[1]])


def _log_sigmoid(z):
    return jnp.minimum(z, 0.0) - jnp.log(1.0 + jnp.exp(-jnp.abs(z)))


def _gla_kernel(qf_ref, kf_ref, vf_ref, af_ref, qb_ref, kb_ref, vb_ref, ab_ref, wa_ref, ba_ref,
                of_ref, ob_ref, st_ref, *, lt):
    t = pl.program_id(1)
    C = GLA_CHUNK

    @pl.when(t == 0)
    def _():
        st_ref[...] = jnp.zeros_like(st_ref)

    ri = lax.broadcasted_iota(jnp.int32, (C, C), 0)
    ci = lax.broadcasted_iota(jnp.int32, (C, C), 1)
    refs = ((qf_ref, kf_ref, vf_ref, af_ref, of_ref), (qb_ref, kb_ref, vb_ref, ab_ref, ob_ref))
    tris, tri_bs, g_alls = [], [], []
    for d in range(2):
        z = _dot(refs[d][3][...].astype(BF16), wa_ref[d]) + ba_ref[d:d + 1, :]
        g_alls.append(_log_sigmoid(z) * (1.0 / GLA_TAU))
        tri = (ci >= ri) if d == 1 else (ci <= ri)
        tris.append(tri)
        tri_bs.append(jnp.where(tri, 1.0, 0.0).astype(BF16))

    def chunk(d, c):
        q_ref, k_ref, v_ref, _, o_ref = refs[d]
        rev = d == 1
        rows = slice(c * C, (c + 1) * C)
        g = g_alls[d][rows, :]
        g_hi = g.astype(BF16)
        g_lo = (g - g_hi.astype(F32)).astype(BF16)
        b = _dot(tri_bs[d], g_hi) + _dot(tri_bs[d], g_lo)
        b_tot = b[0:1, :] if rev else b[C - 1:C, :]
        q = q_ref[rows, :] * (GLA_DK ** -0.5)
        k = k_ref[rows, :]
        v = v_ref[rows, :].astype(BF16)
        q_d = (q * jnp.exp(b)).astype(BF16)
        k_d = (k * jnp.exp(-b)).astype(BF16)
        k_s = (k * jnp.exp(b_tot - b)).astype(BF16)
        att = lax.dot_general(q_d, k_d, NT_DIMS, preferred_element_type=F32)
        att = jnp.where(tris[d], att, 0.0).astype(BF16)
        s_t = st_ref[d]
        o = _dot(att, v) + lax.dot_general(q_d, s_t.astype(BF16), NT_DIMS, preferred_element_type=F32)
        o_ref[rows, :] = o
        st_ref[d] = s_t * jnp.exp(b_tot) + lax.dot_general(v, k_s, TN_DIMS, preferred_element_type=F32)

    n_chunks = lt // C
    for c in range(n_chunks):
        chunk(0, c)
        chunk(1, n_chunks - 1 - c)


def gla_scan(p, wa, ba, *, ctx_len, lt=SEQ_TILE):
    T = p.shape[0]
    n_tiles = T // lt
    n_ctx = ctx_len // lt
    tf = functools.partial(_time_tile, n_tiles=n_tiles, n_ctx_tiles=n_ctx, rev=False)
    tb = functools.partial(_time_tile, n_tiles=n_tiles, n_ctx_tiles=n_ctx, rev=True)

    def stream(tt):
        return [
            pl.BlockSpec((lt, GLA_DK), lambda h, t: (tt(t), OFF_Q // GLA_DK + h)),
            pl.BlockSpec((lt, GLA_DK), lambda h, t: (tt(t), OFF_K // GLA_DK + h)),
            pl.BlockSpec((lt, GLA_DV), lambda h, t: (tt(t), OFF_V // GLA_DV + h)),
            pl.BlockSpec((lt, LANE), lambda h, t: (tt(t), OFF_ALPHA // LANE)),
        ]

    out = jax.ShapeDtypeStruct((T, D_GLA_V), F32)
    return pl.pallas_call(
        functools.partial(_gla_kernel, lt=lt),
        grid=(GLA_HEADS, n_tiles),
        in_specs=stream(tf) + stream(tb) + [
            pl.BlockSpec((2, LANE, GLA_DK), lambda h, t: (0, 0, h)),
            pl.BlockSpec((2, GLA_DK), lambda h, t: (0, h)),
        ],
        out_specs=[pl.BlockSpec((lt, GLA_DV), lambda h, t: (tf(t), h)),
                   pl.BlockSpec((lt, GLA_DV), lambda h, t: (tb(t), h))],
        out_shape=[out, out],
        scratch_shapes=[pltpu.VMEM((2, GLA_DV, GLA_DK), F32)],
        compiler_params=_cparams(("arbitrary", "arbitrary")),
        name="gla_scan",
    )(p, p, p, p, p, p, p, p, wa, ba)


def _finish_kernel(u_ref, ys_f_ref, ys_b_ref, d_ref, wglu_ref, bglu_ref,
                   og_f_ref, og_b_ref, gate_ref, gn_ref,
                   x0_ref, conv_ref, s_ref, hb_ref, o_ref):
    u = u_ref[...]
    y = ys_f_ref[...] + ys_b_ref[...] + d_ref[...] * u
    zz = _gelu(y)
    s5 = zz * _sigmoid(_dot(zz.astype(BF16), wglu_ref[...]) + bglu_ref[...])
    o_ref[:, 0:D_S5] = s5.astype(BF16)

    gate = gate_ref[...]
    for h in range(GLA_HEADS):
        cols = slice(h * GLA_DV, (h + 1) * GLA_DV)
        o = og_f_ref[:, cols] + og_b_ref[:, cols]
        ms = jnp.mean(o * o, axis=-1, keepdims=True)
        on = o * lax.rsqrt(ms + EPS) * gn_ref[...]
        gt = gate[:, cols]
        o_ref[:, D_S5 + h * GLA_DV:D_S5 + (h + 1) * GLA_DV] = (on * (gt * _sigmoid(gt))).astype(BF16)

    s = s_ref[...]
    hy = x0_ref[...] * (conv_ref[...] + hb_ref[...] * s)
    o_ref[:, D_S5 + D_GLA_V:] = hy.astype(BF16)


def mixer_finish(p, ys_f, ys_b, s5_d, w_glu, b_glu, og_f, og_b, gn, x0, conv, s, hy_bias, *, tm):
    T = p.shape[0]
    row = lambda w, off=0: pl.BlockSpec((tm, w), lambda i: (i, off))
    const = lambda r, w: pl.BlockSpec((r, w), lambda i: (0, 0))
    return pl.pallas_call(
        _finish_kernel,
        grid=(T // tm,),
        in_specs=[
            row(D_S5, OFF_S5 // D_S5), row(D_S5), row(D_S5), const(1, D_S5), const(D_S5, D_S5), const(1, D_S5),
            row(D_GLA_V), row(D_GLA_V), row(D_GLA_V, OFF_GATE // D_GLA_V), const(1, GLA_DV),
            row(D_HY), row(D_HY), row(D_HY), const(1, D_HY),
        ],
        out_specs=pl.BlockSpec((tm, D_MODEL), lambda i: (i, 0)),
        out_shape=jax.ShapeDtypeStruct((T, D_MODEL), BF16),
        compiler_params=_cparams(("arbitrary",)),
        name="mixer_finish",
    )(p, ys_f, ys_b, s5_d.reshape(1, D_S5), w_glu, b_glu.reshape(1, D_S5),
      og_f, og_b, p, gn.reshape(1, GLA_DV), x0, conv, s, hy_bias.reshape(1, D_HY))


PEER_TE = 512
PEER_I1 = PEER_TE // PEER_KEYS


def _peer_kernel(x_ref, ua_ref, ub_ref, vp_ref, va_ref, s2_ref, e2_ref, s1_ref, e1_ref,
                 tau_ref, o_ref, actb_ref, acta_ref, pb_ref, pa_ref, *, tm, n_pairs):
    g = pl.program_id(1)
    K = PEER_KEYS

    @pl.when(g == 0)
    def _():
        o_ref[...] = jnp.zeros_like(o_ref)
        actb_ref[...] = jnp.zeros_like(actb_ref)

    def first_matmul(u_ref, act_ref):
        act_ref[...] = lax.dot_general(u_ref[...], x_ref[...], NT_DIMS, preferred_element_type=F32)

    def gate(act_ref, p_ref, tile, valid):
        base = tile * PEER_I1
        hs = K // 2
        tile3 = (hs // SUBLANE, SUBLANE, LANE)
        for cb in range(tm // LANE):
            cols = slice(cb * LANE, (cb + 1) * LANE)
            for half in range(2):
                rows = slice(half * hs, (half + 1) * hs)
                w = [None] * PEER_I1
                for h in range(PEER_HEADS):
                    tau = tau_ref[h, cb]
                    if valid is not None:
                        tau = jnp.where(valid, tau, -NEG_BIG)
                    s2 = s2_ref[h, rows, cols].reshape(tile3)
                    e2 = e2_ref[h, rows, cols].reshape(tile3)
                    for a in range(PEER_I1):
                        bcast = pl.ds(base + a, SUBLANE, stride=0)
                        sel = jnp.where(s1_ref[h, cb, bcast, :] + s2 >= tau, e1_ref[h, cb, bcast, :] * e2, 0.0)
                        w[a] = sel if w[a] is None else w[a] + sel
                for a in range(PEER_I1):
                    r = slice(a * K + half * hs, a * K + (half + 1) * hs)
                    act = act_ref[r, cols].reshape(tile3)
                    p_ref[r, cols] = (w[a] * _gelu(act)).reshape(hs, LANE).astype(BF16)

    last = 2 * n_pairs - 1
    gate(actb_ref, pb_ref, jnp.maximum(2 * g - 1, 0), None)
    first_matmul(ua_ref, acta_ref)
    o_ref[...] += _dot(vp_ref[...], pb_ref[...])
    gate(acta_ref, pa_ref, jnp.minimum(2 * g, last), g < n_pairs)
    first_matmul(ub_ref, actb_ref)
    o_ref[...] += _dot(va_ref[...], pa_ref[...])


def peer_dense(xn, u_tab, vt_tab, s1t, e1t, s2t, e2t, taut, *, tm):
    T, D = xn.shape
    E = u_tab.shape[0]
    H, K = PEER_HEADS, PEER_KEYS
    te = PEER_TE
    n_e = E // te
    n_pairs = n_e // 2
    last = n_e - 1
    once = pl.Buffered(1)
    tile_a = lambda g: jnp.minimum(2 * g, last)
    tile_b = lambda g: jnp.minimum(2 * g + 1, last)
    tile_p = lambda g: jnp.maximum(2 * g - 1, 0)
    slab_spec = pl.BlockSpec((H, tm // LANE, K, LANE), lambda i, g: (0, i, 0, 0), pipeline_mode=once)
    tab_spec = pl.BlockSpec((H, K, tm), lambda i, g: (0, 0, i), pipeline_mode=once)
    return pl.pallas_call(
        functools.partial(_peer_kernel, tm=tm, n_pairs=n_pairs),
        grid=(T // tm, n_pairs + 1),
        in_specs=[
            pl.BlockSpec((tm, D), lambda i, g: (i, 0), pipeline_mode=once),
            pl.BlockSpec((te, D), lambda i, g: (tile_a(g), 0)),
            pl.BlockSpec((te, D), lambda i, g: (tile_b(g), 0)),
            pl.BlockSpec((D, te), lambda i, g: (0, tile_p(g))),
            pl.BlockSpec((D, te), lambda i, g: (0, tile_a(g))),
            tab_spec, tab_spec,
            slab_spec, slab_spec,
            pl.BlockSpec((H, tm // LANE, SUBLANE, LANE), lambda i, g: (0, i, 0, 0), pipeline_mode=once),
        ],
        out_specs=pl.BlockSpec((D, tm), lambda i, g: (0, i)),
        out_shape=jax.ShapeDtypeStruct((D, T), F32),
        scratch_shapes=[pltpu.VMEM((te, tm), F32), pltpu.VMEM((te, tm), F32),
                        pltpu.VMEM((te, tm), BF16), pltpu.VMEM((te, tm), BF16)],
        compiler_params=_cparams(("arbitrary", "arbitrary")),
        name="peer_dense",
    )(xn, u_tab, u_tab, vt_tab, vt_tab, s2t, e2t, s1t, e1t, taut)


NEG_BIG = -3.0e38
N_CAND = PEER_TOPK + 8 * 7 + 8


def _peer_prep_kernel(q_ref, k_ref, s1_ref, e1_ref, s2_ref, e2_ref, tau_ref, top_ref, cand_ref):
    R = PEER_TOPK
    half_w = PEER_DQ // 2

    def nt(a, b):
        return lax.dot_general(a, b, NT_DIMS, preferred_element_type=F32)

    def scores(half):
        qh = q_ref[:, half * half_w:(half + 1) * half_w]
        q_hi = qh.astype(BF16)
        q_lo = (qh - q_hi.astype(F32)).astype(BF16)
        kk = k_ref[half]
        k_hi = kk.astype(BF16)
        k_lo = (kk - k_hi.astype(F32)).astype(BF16)
        return nt(k_hi, q_hi) + (nt(k_hi, q_lo) + nt(k_lo, q_hi))

    def sorted_top(s, slot):
        work = s
        for r in range(R):
            m = jnp.max(work, axis=0, keepdims=True)
            top_ref[slot, r:r + 1, :] = m
            work = jnp.where(work == m, NEG_BIG, work)

    s1 = scores(0)
    s2 = scores(1)
    sorted_top(s1, 0)
    sorted_top(s2, 1)
    a = top_ref[0]
    b = top_ref[1]
    cand_ref[0:R, :] = a[0:1, :] + b
    for i in range(1, 8):
        cand_ref[R + 8 * (i - 1):R + 8 * i, :] = a[i:i + 1, :] + b[0:8, :]
    cand_ref[R + 56:R + 64, :] = a[8:16, :] + b[0:1, :]
    work = cand_ref[...]
    m0 = jnp.max(work, axis=0, keepdims=True)
    m = m0
    z = jnp.ones_like(m0)
    for r in range(1, R):
        work = jnp.where(work == m, NEG_BIG, work)
        m = jnp.max(work, axis=0, keepdims=True)
        z = z + jnp.exp(m - m0)
    e1 = jnp.exp(s1 - a[0:1, :]) * (1.0 / z)
    s2_ref[0] = s2
    e2_ref[0] = jnp.exp(s2 - b[0:1, :])
    for lt in range(s1.shape[1] // LANE):
        cols = slice(lt * LANE, (lt + 1) * LANE)
        s1_ref[0, lt] = s1[:, cols]
        e1_ref[0, lt] = e1[:, cols]
        tau_ref[0, lt] = jnp.broadcast_to(m[:, cols], (SUBLANE, LANE))


def peer_prep(q, keys, *, tm):
    T = q.shape[0]
    H, K = PEER_HEADS, PEER_KEYS
    tab = jax.ShapeDtypeStruct((H, K, T), F32)
    tab_spec = pl.BlockSpec((1, K, tm), lambda i, h: (h, 0, i))
    slab = jax.ShapeDtypeStruct((H, T // LANE, K, LANE), F32)
    slab_spec = pl.BlockSpec((1, tm // LANE, K, LANE), lambda i, h: (h, i, 0, 0))
    return pl.pallas_call(
        _peer_prep_kernel,
        grid=(T // tm, H),
        in_specs=[
            pl.BlockSpec((tm, PEER_DQ), lambda i, h: (i, h)),
            pl.BlockSpec((None, 2, K, PEER_DQ // 2), lambda i, h: (h, 0, 0, 0)),
        ],
        out_specs=[slab_spec, slab_spec, tab_spec, tab_spec,
                   pl.BlockSpec((1, tm // LANE, SUBLANE, LANE), lambda i, h: (h, i, 0, 0))],
        out_shape=[slab, slab, tab, tab, jax.ShapeDtypeStruct((H, T // LANE, SUBLANE, LANE), F32)],
        scratch_shapes=[pltpu.VMEM((2, PEER_TOPK, tm), F32), pltpu.VMEM((N_CAND, tm), F32)],
        compiler_params=_cparams(("arbitrary", "arbitrary")),
        name="peer_prep",
    )(q, keys)


def _transpose_gated_res_kernel(h_ref, ft_ref, gate_ref, o_ref, *, ctx_len, tm):
    i = pl.program_id(0)
    row = i * tm + lax.broadcasted_iota(jnp.int32, (tm, 1), 0)
    gate = jnp.where(row < ctx_len, gate_ref[0:1, :], gate_ref[1:2, :])
    o_ref[...] = h_ref[...] + gate * ft_ref[...].T


def transpose_gated_res(h, ft, gates, *, ctx_len, tm):
    T, D = h.shape
    return pl.pallas_call(
        functools.partial(_transpose_gated_res_kernel, ctx_len=ctx_len, tm=tm),
        grid=(T // tm,),
        in_specs=[
            pl.BlockSpec((tm, D), lambda i: (i, 0)),
            pl.BlockSpec((D, tm), lambda i: (0, i)),
            pl.BlockSpec((2, D), lambda i: (0, 0)),
        ],
        out_specs=pl.BlockSpec((tm, D), lambda i: (i, 0)),
        out_shape=jax.ShapeDtypeStruct((T, D), F32),
        compiler_params=_cparams(("arbitrary",)),
        name="transpose_gated_res",
    )(h, ft, gates)


def _rmsnorm_kernel(x_ref, g_ref, o_ref):
    x = x_ref[...]
    ms = jnp.mean(x * x, axis=-1, keepdims=True)
    o_ref[...] = x * lax.rsqrt(ms + EPS) * g_ref[...]


def rmsnorm_rows(x, g, *, tm):
    T, D = x.shape
    return pl.pallas_call(
        _rmsnorm_kernel,
        grid=(T // tm,),
        in_specs=[pl.BlockSpec((tm, D), lambda i: (i, 0)), pl.BlockSpec((1, D), lambda i: (0, 0))],
        out_specs=pl.BlockSpec((tm, D), lambda i: (i, 0)),
        out_shape=jax.ShapeDtypeStruct((T, D), F32),
        compiler_params=_cparams(("arbitrary",)),
        name="rmsnorm_rows",
    )(x, g.reshape(1, D))


def grid_sincos(n_tokens, dim):
    rows = n_tokens // GRID_W
    row = jnp.repeat(jnp.arange(rows), GRID_W).astype(F32)
    col = jnp.tile(jnp.arange(GRID_W), rows).astype(F32)
    n_freq = dim // 4
    omega = 1.0 / (POS_BASE ** (jnp.arange(n_freq, dtype=F32) / n_freq))

    def enc(pp):
        a = pp[:, None] * omega[None, :]
        return jnp.concatenate([jnp.sin(a), jnp.cos(a)], axis=-1)

    return jnp.concatenate([enc(row), enc(col)], axis=-1)


def _split_bf16(a):
    hi = a.astype(BF16)
    return hi, (a - hi.astype(F32)).astype(BF16)


def _dot3s(a, b_hi, b_lo):
    a_hi, a_lo = _split_bf16(a)
    return _dot(a_hi, b_hi) + (_dot(a_hi, b_lo) + _dot(a_lo, b_hi))


def _dot3(a, b):
    return _dot3s(a, *_split_bf16(b))


def _adaln_kernel(c_ref, w_ref, b_ref, o_ref):
    c = c_ref[...]
    o_ref[...] = _dot3(c * _sigmoid(c), w_ref[...]) + b_ref[...]


def adaln(cond, w, b, *, tn=1536):
    R, D = cond.shape
    N = w.shape[1]
    return pl.pallas_call(
        _adaln_kernel,
        grid=(N // tn,),
        in_specs=[pl.BlockSpec((R, D), lambda j: (0, 0)),
                  pl.BlockSpec((D, tn), lambda j: (0, j)),
                  pl.BlockSpec((1, tn), lambda j: (0, j))],
        out_specs=pl.BlockSpec((R, tn), lambda j: (0, j)),
        out_shape=jax.ShapeDtypeStruct((R, N), F32),
        compiler_params=_cparams(("arbitrary",)),
        name="adaln",
    )(cond, w, b.reshape(1, N))


def _hyena_pre_kernel(x0_ref, x1_ref, v_ref, w_ref, b_ref, x0o_ref, s_ref, *, ctx_len):
    T = x0_ref.shape[0]
    row = lax.broadcasted_iota(jnp.int32, (T, 1), 0)
    first = jnp.logical_or(row == 0, row == ctx_len)
    last = jnp.logical_or(row == ctx_len - 1, row == T - 1)

    def conv(ref, part):
        x = ref[...]
        prev = jnp.where(first, 0.0, pltpu.roll(x, 1, 0))
        nxt = jnp.where(last, 0.0, pltpu.roll(x, T - 1, 0))
        w = lambda tap: w_ref[tap, part:part + 1, :]
        return w(0) * prev + w(1) * x + w(2) * nxt + b_ref[part:part + 1, :]

    x0o_ref[...] = conv(x0_ref, 0)
    s_ref[...] = conv(x1_ref, 1) * conv(v_ref, 2)


def hyena_pre(p, conv_w, conv_b, *, ctx_len):
    T = p.shape[0]
    nb = D_HY // LANE
    col = lambda part: pl.BlockSpec((T, LANE), lambda c: (0, OFF_HY // LANE + part * nb + c))
    out = jax.ShapeDtypeStruct((T, D_HY), F32)
    return pl.pallas_call(
        functools.partial(_hyena_pre_kernel, ctx_len=ctx_len),
        grid=(nb,),
        in_specs=[col(0), col(1), col(2),
                  pl.BlockSpec((HY_SHORT, 3, LANE), lambda c: (0, 0, c)),
                  pl.BlockSpec((3, LANE), lambda c: (0, c))],
        out_specs=[pl.BlockSpec((T, LANE), lambda c: (0, c))] * 2,
        out_shape=[out, out],
        compiler_params=_cparams(("arbitrary",)),
        name="hyena_pre",
    )(p, p, p, conv_w.reshape(HY_SHORT, 3, D_HY), conv_b.reshape(3, D_HY))


def _hyena_filter_kernel(band_ref, w1_ref, b1_ref, w2_ref, b2_ref, w3_ref, fr_ref, dec_ref, o_ref, *, n, tr):
    i = pl.program_id(0)
    ri = (i * tr + lax.broadcasted_iota(jnp.int32, (tr, 1), 0)).astype(F32)
    t = ri * (1.0 / (n - 1))
    w = ri * (2.0 * math.pi / n)
    lane = lax.broadcasted_iota(jnp.int32, (1, LANE), 1)
    arg = w * band_ref[...]
    z = jnp.where(lane == 0, t,
                  jnp.where(lane <= HY_BANDS, jnp.cos(arg),
                            jnp.where(lane <= 2 * HY_BANDS, -jnp.sin(arg), 0.0)))
    h = jnp.sin(fr_ref[0:1, :] * (_dot3(z, w1_ref[...]) + b1_ref[...]))
    h = jnp.sin(fr_ref[1:2, :] * (_dot3(h, w2_ref[...]) + b2_ref[...]))
    h = _dot3(h, w3_ref[...]) * jnp.exp(-t * dec_ref[...])
    col = lax.broadcasted_iota(jnp.int32, (1, 2 * D_HY), 1)
    o_ref[...] = jnp.where(jnp.logical_and(ri == 0.0, col >= D_HY), 0.0, h)


def hyena_filter(n, w1, b1, w2, b2, w3, freq, decay, *, tr=256):
    bands = jnp.linspace(1e-4, HY_BANDS - 1, HY_BANDS, dtype=F32)
    band_row = jnp.zeros((1, LANE), F32).at[0, 1:1 + HY_BANDS].set(bands).at[0, 1 + HY_BANDS:1 + 2 * HY_BANDS].set(bands)
    w1p = jnp.zeros((LANE, HY_HIDDEN), F32).at[:HY_FEAT].set(w1)
    const = lambda a: pl.BlockSpec(a.shape, lambda i: (0,) * a.ndim)
    args = (band_row, w1p, b1.reshape(1, -1), w2, b2.reshape(1, -1), w3, freq,
            jnp.abs(decay).reshape(1, 2 * D_HY))
    return pl.pallas_call(
        functools.partial(_hyena_filter_kernel, n=n, tr=tr),
        grid=(n // tr,),
        in_specs=[const(a) for a in args],
        out_specs=pl.BlockSpec((tr, 2 * D_HY), lambda i: (i, 0)),
        out_shape=jax.ShapeDtypeStruct((n, 2 * D_HY), F32),
        compiler_params=_cparams(("arbitrary",)),
        name="hyena_filter",
    )(*args)


FFT_R = 128
FFT_N = FFT_R * FFT_R
FFT_CH = 32


def _dft_tables():
    r = np.arange(FFT_R)
    ang = 2.0 * np.pi * np.outer(r, r) / FFT_R
    c, s = np.cos(ang), np.sin(ang)
    angt = 2.0 * np.pi * np.outer(r, r) / FFT_N
    ct, st = np.cos(angt), np.sin(angt)

    def split(m):
        m = jnp.asarray(m, F32)
        hi = m.astype(BF16)
        return jnp.stack([hi, (m - hi.astype(F32)).astype(BF16)])

    fwd_a = split(np.concatenate([c, -s], axis=1))
    fwd_b = split(np.block([[c, -s], [s, c]]))
    inv_b = split(np.block([[c, s], [-s, c]]) / FFT_N)
    inv_a = split(np.concatenate([c, -s], axis=0))
    tw = (jnp.asarray(ct, F32), jnp.asarray(-st, F32))
    twc = (jnp.asarray(ct, F32), jnp.asarray(st, F32))
    return fwd_a, fwd_b, inv_b, inv_a, tw, twc


def _twiddle_transpose(y, twr, twi, dst_ref, nc):
    y = y.reshape(nc, FFT_R, 2 * FFT_R)
    yr, yi = y[:, :, :FFT_R], y[:, :, FFT_R:]
    zr = yr * twr - yi * twi
    zi = yr * twi + yi * twr
    for c in range(nc):
        dst_ref[c * FFT_R:(c + 1) * FFT_R, 0:FFT_R] = zr[c].T
        dst_ref[c * FFT_R:(c + 1) * FFT_R, FFT_R:2 * FFT_R] = zi[c].T


def _hyena_fwd_kernel(x_ref, fa_ref, twr_ref, twi_ref, fb_ref, o_ref, l_ref, yt_ref, *, nc):
    j = pl.program_id(1)
    half = FFT_R // 2

    @pl.when(j == 0)
    def _():
        zeros = jnp.zeros((half, LANE), F32)

        def body(b, carry):
            t = jnp.concatenate([x_ref[pl.ds(b, half, stride=FFT_R), :], zeros], axis=0)
            l_ref[pl.ds(b, LANE, stride=FFT_R), :] = t.T
            return carry

        lax.fori_loop(0, FFT_R, body, 0, unroll=8)

    rows = nc * FFT_R
    lc = l_ref[pl.ds(pl.multiple_of(j * rows, rows), rows), :]
    y = _dot3s(lc, fa_ref[0], fa_ref[1])
    _twiddle_transpose(y, twr_ref[...], twi_ref[...], yt_ref, nc)
    o_ref[...] = _dot3s(yt_ref[...], fb_ref[0], fb_ref[1])


def hyena_fwd_dft(x, tables):
    n, C = x.shape
    assert n * 2 == FFT_N and C % LANE == 0
    fwd_a, fwd_b, _, _, (twr, twi), _ = tables
    nc = FFT_CH
    steps = LANE // nc
    const = lambda a: pl.BlockSpec(a.shape, lambda cb, j: (0,) * a.ndim)
    return pl.pallas_call(
        functools.partial(_hyena_fwd_kernel, nc=nc),
        grid=(C // LANE, steps),
        in_specs=[pl.BlockSpec((n, LANE), lambda cb, j: (0, cb)), const(fwd_a), const(twr), const(twi), const(fwd_b)],
        out_specs=pl.BlockSpec((nc * FFT_R, 2 * FFT_R), lambda cb, j: (cb * steps + j, 0)),
        out_shape=jax.ShapeDtypeStruct((C * FFT_R, 2 * FFT_R), F32),
        scratch_shapes=[pltpu.VMEM((LANE * FFT_R, LANE), F32), pltpu.VMEM((nc * FFT_R, 2 * FFT_R), F32)],
        compiler_params=_cparams(("arbitrary", "arbitrary")),
        name="hyena_fwd_dft",
    )(x, fwd_a, twr, twi, fwd_b)


def _hyena_inv_kernel(s_ref, kf_ref, kb_ref, gb_ref, twr_ref, twi_ref, ga_ref, o_ref, lr_ref, at_ref, *, nc, steps):
    j = pl.program_id(1)
    R = FFT_R
    s, kf, kb = s_ref[...], kf_ref[...], kb_ref[...]
    sr, si = s[:, :R], s[:, R:]
    kr = kf[:, :R] + kb[:, :R]
    ki = kf[:, R:] - kb[:, R:]
    p = jnp.concatenate([sr * kr - si * ki, sr * ki + si * kr], axis=1)
    a = _dot3s(p, gb_ref[0], gb_ref[1])
    _twiddle_transpose(a, twr_ref[...], twi_ref[...], at_ref, nc)
    rows = nc * R
    lr_ref[pl.ds(pl.multiple_of(j * rows, rows), rows), :] = _dot3s(at_ref[...], ga_ref[0], ga_ref[1])

    @pl.when(j == steps - 1)
    def _():
        def body(b, carry):
            t = lr_ref[pl.ds(b, LANE, stride=R), :]
            o_ref[pl.ds(b, R // 2, stride=R), :] = t.T[0:R // 2, :]
            return carry

        lax.fori_loop(0, R, body, 0, unroll=8)


def hyena_inv_dft(s_f, k_f, tables):
    C = s_f.shape[0] // FFT_R
    _, _, inv_b, inv_a, _, (twr, twi) = tables
    nc = FFT_CH
    steps = LANE // nc
    kb_off = C // nc
    const = lambda a: pl.BlockSpec(a.shape, lambda cb, j: (0,) * a.ndim)
    blk = lambda off: pl.BlockSpec((nc * FFT_R, 2 * FFT_R), lambda cb, j: (off + cb * steps + j, 0))
    return pl.pallas_call(
        functools.partial(_hyena_inv_kernel, nc=nc, steps=steps),
        grid=(C // LANE, steps),
        in_specs=[blk(0), blk(0), blk(kb_off), const(inv_b), const(twr), const(twi), const(inv_a)],
        out_specs=pl.BlockSpec((FFT_N // 2, LANE), lambda cb, j: (0, cb)),
        out_shape=jax.ShapeDtypeStruct((FFT_N // 2, C), F32),
        scratch_shapes=[pltpu.VMEM((LANE * FFT_R, LANE), F32), pltpu.VMEM((nc * FFT_R, 2 * FFT_R), F32)],
        compiler_params=_cparams(("arbitrary", "arbitrary")),
        name="hyena_inv_dft",
    )(s_f, k_f, k_f, inv_b, twr, twi, inv_a)


def _hyena_ctx_kernel(s_ref, k_ref, fw_ref, iv_ref, o_ref):
    n = s_ref.shape[0]
    N = 2 * n
    xs = _dot3(fw_ref[...], s_ref[...])
    xk = _dot3(fw_ref[...], k_ref[...])
    sr, si = xs[:N], xs[N:]
    kr = xk[:N, :D_HY] + xk[:N, D_HY:]
    ki = xk[N:, :D_HY] - xk[N:, D_HY:]
    p = jnp.concatenate([sr * kr - si * ki, sr * ki + si * kr], axis=0)
    o_ref[...] = _dot3(iv_ref[...], p)


def hyena_ctx_conv(s_c, k_c):
    n = s_c.shape[0]
    N = 2 * n
    ang = 2.0 * np.pi * np.outer(np.arange(N), np.arange(N)) / N
    c, s = np.cos(ang), np.sin(ang)
    fw = jnp.asarray(np.concatenate([c[:, :n], -s[:, :n]], axis=0), F32)
    iv = jnp.asarray(np.concatenate([c[:n, :], -s[:n, :]], axis=1) / N, F32)
    full = lambda a: pl.BlockSpec(a.shape, lambda i: (0,) * a.ndim)
    return pl.pallas_call(
        _hyena_ctx_kernel,
        grid=(1,),
        in_specs=[full(s_c), full(k_c), full(fw), full(iv)],
        out_specs=pl.BlockSpec((n, D_HY), lambda i: (0, 0)),
        out_shape=jax.ShapeDtypeStruct((n, D_HY), F32),
        compiler_params=_cparams(("arbitrary",)),
        name="hyena_ctx_conv",
    )(s_c, k_c, fw, iv)


def hyena_mixer_parts(p, conv_w, conv_b, filt, tables, *, ctx_len):
    T = p.shape[0]
    x0, s = hyena_pre(p, conv_w, conv_b, ctx_len=ctx_len)
    k_lat = hyena_filter(T - ctx_len, *filt)
    k_ctx = hyena_filter(ctx_len, *filt)
    conv_l = hyena_inv_dft(hyena_fwd_dft(s[ctx_len:], tables), hyena_fwd_dft(k_lat, tables), tables)
    conv_c = hyena_ctx_conv(s[:ctx_len], k_ctx)
    return x0, jnp.concatenate([conv_c, conv_l], axis=0), s


def _pack_w_in(w_in_l):
    widths = (D_S5, D_GLA_K, D_GLA_K, D_GLA_V, D_GLA_V, 2 * GLA_RANK, 3 * D_HY)
    offs = [0]
    for wd in widths:
        offs.append(offs[-1] + wd)
    s5, q, k, v, gate, alpha, hy = (w_in_l[:, offs[i]:offs[i + 1]] for i in range(7))
    pad = jnp.zeros((w_in_l.shape[0], IN_PACKED - OFF_ALPHA - 2 * GLA_RANK), w_in_l.dtype)
    return jnp.concatenate([v, gate, s5, q, k, hy, alpha, pad], axis=1).astype(BF16)


def kernel(x, c, ctx, c_ctx, w_ada, b_ada, g_norm1, g_norm2, w_in, s5_a_re, s5_a_im, s5_log_step, s5_b_re, s5_b_im, s5_c_re, s5_c_im, s5_d, s5_w_glu, s5_b_glu, gla_w_alpha, gla_b_alpha, gla_g_norm, hy_conv_w, hy_conv_b, hy_f_w1, hy_f_b1, hy_f_w2, hy_f_b2, hy_f_w3, hy_f_freq, hy_decay, hy_bias, w_out, peer_w_q, peer_keys, peer_u, peer_v, g_final):
    L = x.shape[1]
    Lc = ctx.shape[1]
    T = L + Lc
    TM = 768
    h = jnp.concatenate([ctx[0], x[0] + grid_sincos(L, D_MODEL)], axis=0)
    cond = jnp.zeros((SUBLANE, D_MODEL), F32).at[0].set(c_ctx).at[1].set(c[0])
    tables = _dft_tables()

    for l in range(DEPTH):
        m = adaln(cond, w_ada[l], b_ada[l])[0:2]
        sh1, sc1, gt1, sh2, sc2, gt2 = jnp.split(m, 6, axis=-1)
        mod1 = jnp.stack([sh1[0], sc1[0], sh1[1], sc1[1]], axis=0)
        mod2 = jnp.stack([sh2[0], sc2[0], sh2[1], sc2[1]], axis=0)

        p = norm_mod_matmul(h, g_norm1[l], mod1, _pack_w_in(w_in[l]), ctx_len=Lc, tm=TM, tn=768)

        bm, cm, pw = s5_prepare(s5_a_re[l], s5_a_im[l], s5_log_step[l], s5_b_re[l], s5_b_im[l],
                                s5_c_re[l], s5_c_im[l])
        ys_f, ys_b = s5_scan(p, bm, cm, pw, ctx_len=Lc)

        wa = jnp.zeros((2, LANE, D_GLA_K), F32)
        wa = wa.at[0, 0:GLA_RANK].set(gla_w_alpha[l, 0]).at[1, GLA_RANK:2 * GLA_RANK].set(gla_w_alpha[l, 1])
        wa = wa.astype(BF16)
        og_f, og_b = gla_scan(p, wa, gla_b_alpha[l], ctx_len=Lc)

        filt = (hy_f_w1[l], hy_f_b1[l], hy_f_w2[l], hy_f_b2[l], hy_f_w3[l], hy_f_freq[l], hy_decay[l])
        x0, conv, s = hyena_mixer_parts(p, hy_conv_w[l], hy_conv_b[l], filt, tables, ctx_len=Lc)

        mix = mixer_finish(p, ys_f, ys_b, s5_d[l], s5_w_glu[l].astype(BF16), s5_b_glu[l],
                           og_f, og_b, gla_g_norm[l], x0, conv, s, hy_bias[l], tm=256)
        h = matmul_gated_res(mix, w_out[l].astype(BF16), h, gt1, ctx_len=Lc, tm=TM, tn=1024)

        q, xn = norm_mod_matmul(h, g_norm2[l], mod2, peer_w_q[l].astype(BF16),
                                ctx_len=Lc, tm=TM, tn=1024, emit_xn=True)
        s1t, e1t, s2t, e2t, taut = peer_prep(q, peer_keys[l], tm=TM)
        ft = peer_dense(xn, peer_u[l].astype(BF16), peer_v[l].T.astype(BF16),
                        s1t, e1t, s2t, e2t, taut, tm=TM)
        h = transpose_gated_res(h, ft, gt2, ctx_len=Lc, tm=256)

    out = rmsnorm_rows(h[Lc:], g_final, tm=512)
    return out[None]
```

```python
import functools
import math

import jax
import jax.numpy as jnp
import numpy as np
from jax import lax
from jax.experimental import pallas as pl
from jax.experimental.pallas import tpu as pltpu

F32 = jnp.float32
BF16 = jnp.bfloat16

D_MODEL = 2048
DEPTH = 4
GRID_W = 64
EPS = 1e-6
POS_BASE = 10000.0

D_S5 = D_MODEL // 4
S5_GROUP = 16
S5_GROUPS = D_S5 // S5_GROUP
S5_STATE = 64
S5_BLK_GROUPS = 8
S5_BLK_CH = S5_BLK_GROUPS * S5_GROUP
S5_BLK_ST = S5_BLK_GROUPS * S5_STATE
S5_NBLK = S5_GROUPS // S5_BLK_GROUPS

GLA_HEADS = 4
D_GLA_K = D_MODEL // 4
D_GLA_V = D_MODEL // 2
GLA_DK = D_GLA_K // GLA_HEADS
GLA_DV = D_GLA_V // GLA_HEADS
GLA_RANK = 16
GLA_TAU = 16.0
GLA_CHUNK = 64

D_HY = D_MODEL // 4
HY_SHORT = 3
HY_BANDS = 16
HY_FEAT = 1 + 2 * HY_BANDS
HY_HIDDEN = 64
HY_TARGET = 1e-2
HY_MIN_DECAY = -math.log(HY_TARGET) / 1.5
HY_MAX_DECAY = -math.log(HY_TARGET) / 0.3

PEER_HEADS = 8
PEER_KEYS = 128
PEER_EXPERTS = PEER_KEYS * PEER_KEYS
PEER_DQ = 256
PEER_TOPK = 16

OFF_V = 0
OFF_GATE = OFF_V + D_GLA_V
OFF_S5 = OFF_GATE + D_GLA_V
OFF_Q = OFF_S5 + D_S5
OFF_K = OFF_Q + D_GLA_K
OFF_HY = OFF_K + D_GLA_K
OFF_ALPHA = OFF_HY + 3 * D_HY
LANE = 128
SUBLANE = 8
IN_PACKED = OFF_ALPHA + 2 * LANE

SEQ_TILE = 256
VMEM_LIMIT = 56 * 1024 * 1024

NT_DIMS = (((1,), (1,)), ((), ()))
TN_DIMS = (((0,), (0,)), ((), ()))


def _cparams(sem):
    return pltpu.CompilerParams(dimension_semantics=sem, vmem_limit_bytes=VMEM_LIMIT)


def _dot(a, b):
    return jnp.dot(a, b, preferred_element_type=F32)


def _gelu(x):
    k1 = -2.0 * math.sqrt(2.0 / math.pi) * math.log2(math.e)
    k2 = 0.044715 * k1
    return x / (1.0 + jnp.exp2(x * (k1 + k2 * (x * x))))


def _sigmoid(x):
    return 1.0 / (1.0 + jnp.exp(-x))


def _norm_mod_matmul_kernel(x_ref, g_ref, mod_ref, w_ref, o_ref, *rest, ctx_len, tm, emit_xn):
    if emit_xn:
        xo_ref, xn_ref = rest
    else:
        (xn_ref,) = rest
    i = pl.program_id(0)
    j = pl.program_id(1)

    @pl.when(j == 0)
    def _():
        x = x_ref[...]
        ms = jnp.mean(x * x, axis=-1, keepdims=True)
        y = x * lax.rsqrt(ms + EPS) * g_ref[...]
        row = i * tm + lax.broadcasted_iota(jnp.int32, (tm, 1), 0)
        is_ctx = row < ctx_len
        shift = jnp.where(is_ctx, mod_ref[0:1, :], mod_ref[2:3, :])
        scale = jnp.where(is_ctx, mod_ref[1:2, :], mod_ref[3:4, :])
        xn = (y * (1.0 + scale) + shift).astype(BF16)
        xn_ref[...] = xn
        if emit_xn:
            xo_ref[...] = xn

    o_ref[...] = _dot(xn_ref[...], w_ref[...])


def norm_mod_matmul(x, g, mod, w, *, ctx_len, tm, tn, emit_xn=False):
    T, D = x.shape
    N = w.shape[1]
    out_shape = [jax.ShapeDtypeStruct((T, N), F32)]
    out_specs = [pl.BlockSpec((tm, tn), lambda i, j: (i, j))]
    if emit_xn:
        out_shape.append(jax.ShapeDtypeStruct((T, D), BF16))
        out_specs.append(pl.BlockSpec((tm, D), lambda i, j: (i, 0)))
    res = pl.pallas_call(
        functools.partial(_norm_mod_matmul_kernel, ctx_len=ctx_len, tm=tm, emit_xn=emit_xn),
        grid=(T // tm, N // tn),
        in_specs=[
            pl.BlockSpec((tm, D), lambda i, j: (i, 0)),
            pl.BlockSpec((1, D), lambda i, j: (0, 0)),
            pl.BlockSpec((4, D), lambda i, j: (0, 0)),
            pl.BlockSpec((D, tn), lambda i, j: (0, j)),
        ],
        out_specs=out_specs,
        out_shape=out_shape,
        scratch_shapes=[pltpu.VMEM((tm, D), BF16)],
        compiler_params=_cparams(("arbitrary", "arbitrary")),
        name="norm_mod_matmul",
    )(x, g.reshape(1, D), mod, w)
    return res if emit_xn else res[0]


def _matmul_gated_res_kernel(a_ref, w_ref, r_ref, gate_ref, o_ref, *, ctx_len, tm):
    i = pl.program_id(0)
    row = i * tm + lax.broadcasted_iota(jnp.int32, (tm, 1), 0)
    gate = jnp.where(row < ctx_len, gate_ref[0:1, :], gate_ref[1:2, :])
    o_ref[...] = r_ref[...] + gate * _dot(a_ref[...], w_ref[...])


def matmul_gated_res(a, w, res, gates, *, ctx_len, tm, tn):
    T, K = a.shape
    N = w.shape[1]
    return pl.pallas_call(
        functools.partial(_matmul_gated_res_kernel, ctx_len=ctx_len, tm=tm),
        grid=(T // tm, N // tn),
        in_specs=[
            pl.BlockSpec((tm, K), lambda i, j: (i, 0)),
            pl.BlockSpec((K, tn), lambda i, j: (0, j)),
            pl.BlockSpec((tm, tn), lambda i, j: (i, j)),
            pl.BlockSpec((2, tn), lambda i, j: (0, j)),
        ],
        out_specs=pl.BlockSpec((tm, tn), lambda i, j: (i, j)),
        out_shape=jax.ShapeDtypeStruct((T, N), F32),
        compiler_params=_cparams(("arbitrary", "arbitrary")),
        name="matmul_gated_res",
    )(a, w, res, gates)


def _time_tile(t, n_tiles, n_ctx_tiles, rev):
    if not rev:
        return t
    return jnp.where(t < n_ctx_tiles, n_ctx_tiles - 1 - t, n_tiles - 1 - (t - n_ctx_tiles))


def _s5_kernel(uf_ref, ub_ref, bm_ref, cm_ref, pw_ref, yf_ref, yb_ref, h_ref, c_ref, *, lt):
    t = pl.program_id(1)
    ns = S5_BLK_ST
    u_refs = (uf_ref, ub_ref)
    y_refs = (yf_ref, yb_ref)

    @pl.when(t == 0)
    def _():
        c_ref[...] = jnp.zeros_like(c_ref)

    for d in range(2):
        bu = _dot(u_refs[d][...].astype(BF16), bm_ref[d])
        h_ref[d, 0] = bu[:, :ns]
        h_ref[d, 1] = bu[:, ns:]

    n_grp = lt // 8

    def group_update(d, g, c_re, c_im):
        rev = d == 1
        gi = (n_grp - 1 - g) if rev else g
        rows = pl.ds(pl.multiple_of(gi * 8, 8), 8)
        a_re = h_ref[d, 0, rows, :]
        a_im = h_ref[d, 1, rows, :]
        for k, s in enumerate((1, 2, 4)):
            l_re = pw_ref[d, 8 + 8 * k:16 + 8 * k, :ns]
            l_im = pw_ref[d, 8 + 8 * k:16 + 8 * k, ns:]
            shift = (8 - s) if rev else s
            s_re = pltpu.roll(a_re, shift, 0)
            s_im = pltpu.roll(a_im, shift, 0)
            a_re = a_re + (l_re * s_re - l_im * s_im)
            a_im = a_im + (l_re * s_im + l_im * s_re)
        p_re = pw_ref[d, 0:8, :ns]
        p_im = pw_ref[d, 0:8, ns:]
        a_re = a_re + (p_re * c_re - p_im * c_im)
        a_im = a_im + (p_re * c_im + p_im * c_re)
        h_ref[d, 0, rows, :] = a_re
        h_ref[d, 1, rows, :] = a_im
        edge = slice(0, 1) if rev else slice(7, 8)
        return a_re[edge, :], a_im[edge, :]

    def body(g, carry):
        f_re, f_im, b_re, b_im = carry
        f_re, f_im = group_update(0, g, f_re, f_im)
        b_re, b_im = group_update(1, g, b_re, b_im)
        return f_re, f_im, b_re, b_im

    carry = lax.fori_loop(0, n_grp, body, (c_ref[0, 0], c_ref[0, 1], c_ref[1, 0], c_ref[1, 1]), unroll=2)
    c_ref[0, 0], c_ref[0, 1], c_ref[1, 0], c_ref[1, 1] = carry

    for d in range(2):
        y_refs[d][...] = (_dot(h_ref[d, 0].astype(BF16), cm_ref[d, :ns, :])
                          + _dot(h_ref[d, 1].astype(BF16), cm_ref[d, ns:, :]))


def s5_scan(p, bm, cm, pw, *, ctx_len, lt=SEQ_TILE):
    T = p.shape[0]
    n_tiles = T // lt
    n_ctx = ctx_len // lt
    tf = functools.partial(_time_tile, n_tiles=n_tiles, n_ctx_tiles=n_ctx, rev=False)
    tb = functools.partial(_time_tile, n_tiles=n_tiles, n_ctx_tiles=n_ctx, rev=True)
    ucol = OFF_S5 // S5_BLK_CH
    out = jax.ShapeDtypeStruct((T, D_S5), F32)
    return pl.pallas_call(
        functools.partial(_s5_kernel, lt=lt),
        grid=(S5_NBLK, n_tiles),
        in_specs=[
            pl.BlockSpec((lt, S5_BLK_CH), lambda b, t: (tf(t), ucol + b)),
            pl.BlockSpec((lt, S5_BLK_CH), lambda b, t: (tb(t), ucol + b)),
            pl.BlockSpec((2, None, S5_BLK_CH, 2 * S5_BLK_ST), lambda b, t: (0, b, 0, 0)),
            pl.BlockSpec((2, None, 2 * S5_BLK_ST, S5_BLK_CH), lambda b, t: (0, b, 0, 0)),
            pl.BlockSpec((2, None, 32, 2 * S5_BLK_ST), lambda b, t: (0, b, 0, 0)),
        ],
        out_specs=[pl.BlockSpec((lt, S5_BLK_CH), lambda b, t: (tf(t), b)),
                   pl.BlockSpec((lt, S5_BLK_CH), lambda b, t: (tb(t), b))],
        out_shape=[out, out],
        scratch_shapes=[pltpu.VMEM((2, 2, lt, S5_BLK_ST), F32), pltpu.VMEM((2, 2, 1, S5_BLK_ST), F32)],
        compiler_params=_cparams(("arbitrary", "arbitrary")),
        name="s5_scan",
    )(p, p, bm, cm, pw)


def _cmul(a, b):
    return a[0] * b[0] - a[1] * b[1], a[0] * b[1] + a[1] * b[0]


def s5_prepare(a_re, a_im, log_step, b_re, b_im, c_re, c_im):
    G, P, Cg = S5_GROUPS, S5_STATE, S5_GROUP
    dt = jnp.exp(log_step)[..., None]
    er = jnp.exp(a_re * dt)
    lam1 = (er * jnp.cos(a_im * dt), er * jnp.sin(a_im * dt))
    den = a_re * a_re + a_im * a_im
    xr, xi = lam1[0] - 1.0, lam1[1]
    coef = ((xr * a_re + xi * a_im) / den, (xi * a_re - xr * a_im) / den)
    bb_re = coef[0][..., None] * b_re - coef[1][..., None] * b_im
    bb_im = coef[0][..., None] * b_im + coef[1][..., None] * b_re
    pows = [lam1]
    for _ in range(7):
        pows.append(_cmul(pows[-1], lam1))

    eye = jnp.eye(S5_BLK_GROUPS, dtype=F32)

    def blockdiag_in(m):
        m = m.reshape(2, S5_NBLK, S5_BLK_GROUPS, P, Cg)
        return jnp.einsum('dbgpc,gh->dbgchp', m, eye).reshape(2, S5_NBLK, S5_BLK_CH, S5_BLK_ST)

    def blockdiag_out(m):
        m = m.reshape(2, S5_NBLK, S5_BLK_GROUPS, Cg, P)
        return jnp.einsum('dbgcp,gh->dbgphc', m, eye).reshape(2, S5_NBLK, S5_BLK_ST, S5_BLK_CH)

    bm = jnp.concatenate([blockdiag_in(bb_re), blockdiag_in(bb_im)], axis=-1).astype(BF16)
    cm = jnp.concatenate([blockdiag_out(c_re), blockdiag_out(-c_im)], axis=-2).astype(BF16)

    def lay(v):
        return v.reshape(2, S5_NBLK, S5_BLK_ST)

    def table(rev):
        power = lambda k: jnp.concatenate([lay(pows[k][0]), lay(pows[k][1])], axis=-1)
        rows = [power(7 - r if rev else r) for r in range(8)]
        zero = jnp.zeros_like(rows[0])
        for s in (1, 2, 4):
            for r in range(8):
                inside = (r < 8 - s) if rev else (r >= s)
                rows.append(power(s - 1) if inside else zero)
        return jnp.stack(rows, axis=2)

    return bm, cm, jnp.stack([table(False)[0], table(True)[1]])


def _log_sigmoid(z):
    return jnp.minimum(z, 0.0) - jnp.log(1.0 + jnp.exp(-jnp.abs(z)))


def _gla_kernel(qf_ref, kf_ref, vf_ref, af_ref, qb_ref, kb_ref, vb_ref, ab_ref, wa_ref, ba_ref,
                of_ref, ob_ref, st_ref, *, lt):
    t = pl.program_id(1)
    C = GLA_CHUNK

    @pl.when(t == 0)
    def _():
        st_ref[...] = jnp.zeros_like(st_ref)

    ri = lax.broadcasted_iota(jnp.int32, (C, C), 0)
    ci = lax.broadcasted_iota(jnp.int32, (C, C), 1)
    refs = ((qf_ref, kf_ref, vf_ref, af_ref, of_ref), (qb_ref, kb_ref, vb_ref, ab_ref, ob_ref))
    tris, tri_bs, g_alls = [], [], []
    for d in range(2):
        z = _dot(refs[d][3][...].astype(BF16), wa_ref[d]) + ba_ref[d:d + 1, :]
        g_alls.append(_log_sigmoid(z) * (1.0 / GLA_TAU))
        tri = (ci >= ri) if d == 1 else (ci <= ri)
        tris.append(tri)
        tri_bs.append(jnp.where(tri, 1.0, 0.0).astype(BF16))

    def chunk(d, c):
        q_ref, k_ref, v_ref, _, o_ref = refs[d]
        rev = d == 1
        rows = slice(c * C, (c + 1) * C)
        g = g_alls[d][rows, :]
        g_hi = g.astype(BF16)
        g_lo = (g - g_hi.astype(F32)).astype(BF16)
        b = _dot(tri_bs[d], g_hi) + _dot(tri_bs[d], g_lo)
        b_tot = b[0:1, :] if rev else b[C - 1:C, :]
        q = q_ref[rows, :] * (GLA_DK ** -0.5)
        k = k_ref[rows, :]
        v = v_ref[rows, :].astype(BF16)
        q_d = (q * jnp.exp(b)).astype(BF16)
        k_d = (k * jnp.exp(-b)).astype(BF16)
        k_s = (k * jnp.exp(b_tot - b)).astype(BF16)
        att = lax.dot_general(q_d, k_d, NT_DIMS, preferred_element_type=F32)
        att = jnp.where(tris[d], att, 0.0).astype(BF16)
        s_t = st_ref[d]
        o = _dot(att, v) + lax.dot_general(q_d, s_t.astype(BF16), NT_DIMS, preferred_element_type=F32)
        o_ref[rows, :] = o
        st_ref[d] = s_t * jnp.exp(b_tot) + lax.dot_general(v, k_s, TN_DIMS, preferred_element_type=F32)

    n_chunks = lt // C
    for c in range(n_chunks):
        chunk(0, c)
        chunk(1, n_chunks - 1 - c)


def gla_scan(p, wa, ba, *, ctx_len, lt=SEQ_TILE):
    T = p.shape[0]
    n_tiles = T // lt
    n_ctx = ctx_len // lt
    tf = functools.partial(_time_tile, n_tiles=n_tiles, n_ctx_tiles=n_ctx, rev=False)
    tb = functools.partial(_time_tile, n_tiles=n_tiles, n_ctx_tiles=n_ctx, rev=True)

    def stream(tt):
        return [
            pl.BlockSpec((lt, GLA_DK), lambda h, t: (tt(t), OFF_Q // GLA_DK + h)),
            pl.BlockSpec((lt, GLA_DK), lambda h, t: (tt(t), OFF_K // GLA_DK + h)),
            pl.BlockSpec((lt, GLA_DV), lambda h, t: (tt(t), OFF_V // GLA_DV + h)),
            pl.BlockSpec((lt, LANE), lambda h, t: (tt(t), OFF_ALPHA // LANE)),
        ]

    out = jax.ShapeDtypeStruct((T, D_GLA_V), F32)
    return pl.pallas_call(
        functools.partial(_gla_kernel, lt=lt),
        grid=(GLA_HEADS, n_tiles),
        in_specs=stream(tf) + stream(tb) + [
            pl.BlockSpec((2, LANE, GLA_DK), lambda h, t: (0, 0, h)),
            pl.BlockSpec((2, GLA_DK), lambda h, t: (0, h)),
        ],
        out_specs=[pl.BlockSpec((lt, GLA_DV), lambda h, t: (tf(t), h)),
                   pl.BlockSpec((lt, GLA_DV), lambda h, t: (tb(t), h))],
        out_shape=[out, out],
        scratch_shapes=[pltpu.VMEM((2, GLA_DV, GLA_DK), F32)],
        compiler_params=_cparams(("arbitrary", "arbitrary")),
        name="gla_scan",
    )(p, p, p, p, p, p, p, p, wa, ba)


def _finish_kernel(u_ref, ys_f_ref, ys_b_ref, d_ref, wglu_ref, bglu_ref,
                   og_f_ref, og_b_ref, gate_ref, gn_ref,
                   x0_ref, conv_ref, s_ref, hb_ref, o_ref):
    u = u_ref[...]
    y = ys_f_ref[...] + ys_b_ref[...] + d_ref[...] * u
    zz = _gelu(y)
    s5 = zz * _sigmoid(_dot(zz.astype(BF16), wglu_ref[...]) + bglu_ref[...])
    o_ref[:, 0:D_S5] = s5.astype(BF16)

    gate = gate_ref[...]
    for h in range(GLA_HEADS):
        cols = slice(h * GLA_DV, (h + 1) * GLA_DV)
        o = og_f_ref[:, cols] + og_b_ref[:, cols]
        ms = jnp.mean(o * o, axis=-1, keepdims=True)
        on = o * lax.rsqrt(ms + EPS) * gn_ref[...]
        gt = gate[:, cols]
        o_ref[:, D_S5 + h * GLA_DV:D_S5 + (h + 1) * GLA_DV] = (on * (gt * _sigmoid(gt))).astype(BF16)

    s = s_ref[...]
    hy = x0_ref[...] * (conv_ref[...] + hb_ref[...] * s)
    o_ref[:, D_S5 + D_GLA_V:] = hy.astype(BF16)


def mixer_finish(p, ys_f, ys_b, s5_d, w_glu, b_glu, og_f, og_b, gn, x0, conv, s, hy_bias, *, tm):
    T = p.shape[0]
    row = lambda w, off=0: pl.BlockSpec((tm, w), lambda i: (i, off))
    const = lambda r, w: pl.BlockSpec((r, w), lambda i: (0, 0))
    return pl.pallas_call(
        _finish_kernel,
        grid=(T // tm,),
        in_specs=[
            row(D_S5, OFF_S5 // D_S5), row(D_S5), row(D_S5), const(1, D_S5), const(D_S5, D_S5), const(1, D_S5),
            row(D_GLA_V), row(D_GLA_V), row(D_GLA_V, OFF_GATE // D_GLA_V), const(1, GLA_DV),
            row(D_HY), row(D_HY), row(D_HY), const(1, D_HY),
        ],
        out_specs=pl.BlockSpec((tm, D_MODEL), lambda i: (i, 0)),
        out_shape=jax.ShapeDtypeStruct((T, D_MODEL), BF16),
        compiler_params=_cparams(("arbitrary",)),
        name="mixer_finish",
    )(p, ys_f, ys_b, s5_d.reshape(1, D_S5), w_glu, b_glu.reshape(1, D_S5),
      og_f, og_b, p, gn.reshape(1, GLA_DV), x0, conv, s, hy_bias.reshape(1, D_HY))


PEER_TE = 512
PEER_I1 = PEER_TE // PEER_KEYS


def _peer_kernel(x_ref, ua_ref, ub_ref, vp_ref, va_ref, s2_ref, e2_ref, s1_ref, e1_ref,
                 tau_ref, o_ref, actb_ref, acta_ref, pb_ref, pa_ref, bc_ref, *, tm, n_pairs):
    g = pl.program_id(1)
    K = PEER_KEYS

    @pl.when(g == 0)
    def _():
        o_ref[...] = jnp.zeros_like(o_ref)
        actb_ref[...] = jnp.zeros_like(actb_ref)

    def first_matmul(u_ref, act_ref):
        act_ref[...] = lax.dot_general(u_ref[...], x_ref[...], NT_DIMS, preferred_element_type=F32)

    def gate(act_ref, p_ref, tile, valid):
        base = tile * PEER_I1
        n_lt = tm // LANE
        for h in range(PEER_HEADS):
            for a in range(PEER_I1):
                row = pl.ds(base + a, 1)
                for cb in range(n_lt):
                    bc_ref[0, h, a, cb] = jnp.broadcast_to(s1_ref[h, cb, row, :], (SUBLANE, LANE))
                    bc_ref[1, h, a, cb] = jnp.broadcast_to(e1_ref[h, cb, row, :], (SUBLANE, LANE))
        qs = K // 4
        tile3 = (qs // SUBLANE, SUBLANE, LANE)
        for cb in range(n_lt):
            cols = slice(cb * LANE, (cb + 1) * LANE)
            for q in range(K // qs):
                rows = slice(q * qs, (q + 1) * qs)
                w = [None] * PEER_I1
                for h in range(PEER_HEADS):
                    tau = tau_ref[h, cb]
                    if valid is not None:
                        tau = jnp.where(valid, tau, -NEG_BIG)
                    s2 = s2_ref[h, rows, cols].reshape(tile3)
                    e2 = e2_ref[h, rows, cols].reshape(tile3)
                    for a in range(PEER_I1):
                        sel = jnp.where(bc_ref[0, h, a, cb] + s2 >= tau, bc_ref[1, h, a, cb] * e2, 0.0)
                        w[a] = sel if w[a] is None else w[a] + sel
                for a in range(PEER_I1):
                    r = slice(a * K + q * qs, a * K + (q + 1) * qs)
                    act = act_ref[r, cols].reshape(tile3)
                    p_ref[r, cols] = (w[a] * _gelu(act)).reshape(qs, LANE).astype(BF16)

    last = 2 * n_pairs - 1
    gate(actb_ref, pb_ref, jnp.maximum(2 * g - 1, 0), None)
    first_matmul(ua_ref, acta_ref)
    o_ref[...] += _dot(vp_ref[...], pb_ref[...])
    gate(acta_ref, pa_ref, jnp.minimum(2 * g, last), g < n_pairs)
    first_matmul(ub_ref, actb_ref)
    o_ref[...] += _dot(va_ref[...], pa_ref[...])


def peer_dense(xn, u_tab, vt_tab, s1t, e1t, s2t, e2t, taut, *, tm):
    T, D = xn.shape
    E = u_tab.shape[0]
    H, K = PEER_HEADS, PEER_KEYS
    te = PEER_TE
    n_e = E // te
    n_pairs = n_e // 2
    last = n_e - 1
    once = pl.Buffered(1)
    tile_a = lambda g: jnp.minimum(2 * g, last)
    tile_b = lambda g: jnp.minimum(2 * g + 1, last)
    tile_p = lambda g: jnp.maximum(2 * g - 1, 0)
    slab_spec = pl.BlockSpec((H, tm // LANE, K, LANE), lambda i, g: (0, i, 0, 0), pipeline_mode=once)
    tab_spec = pl.BlockSpec((H, K, tm), lambda i, g: (0, 0, i), pipeline_mode=once)
    return pl.pallas_call(
        functools.partial(_peer_kernel, tm=tm, n_pairs=n_pairs),
        grid=(T // tm, n_pairs + 1),
        in_specs=[
            pl.BlockSpec((tm, D), lambda i, g: (i, 0), pipeline_mode=once),
            pl.BlockSpec((te, D), lambda i, g: (tile_a(g), 0)),
            pl.BlockSpec((te, D), lambda i, g: (tile_b(g), 0)),
            pl.BlockSpec((D, te), lambda i, g: (0, tile_p(g))),
            pl.BlockSpec((D, te), lambda i, g: (0, tile_a(g))),
            tab_spec, tab_spec,
            slab_spec, slab_spec,
            pl.BlockSpec((H, tm // LANE, SUBLANE, LANE), lambda i, g: (0, i, 0, 0), pipeline_mode=once),
        ],
        out_specs=pl.BlockSpec((D, tm), lambda i, g: (0, i)),
        out_shape=jax.ShapeDtypeStruct((D, T), F32),
        scratch_shapes=[pltpu.VMEM((te, tm), F32), pltpu.VMEM((te, tm), F32),
                        pltpu.VMEM((te, tm), BF16), pltpu.VMEM((te, tm), BF16),
                        pltpu.VMEM((2, H, PEER_I1, tm // LANE, SUBLANE, LANE), F32)],
        compiler_params=_cparams(("arbitrary", "arbitrary")),
        name="peer_dense",
    )(xn, u_tab, u_tab, vt_tab, vt_tab, s2t, e2t, s1t, e1t, taut)


NEG_BIG = -3.0e38
N_CAND = PEER_TOPK + 8 * 7 + 8


def _peer_prep_kernel(q_ref, k_ref, s1_ref, e1_ref, s2_ref, e2_ref, tau_ref, top_ref, cand_ref):
    R = PEER_TOPK
    half_w = PEER_DQ // 2

    def nt(a, b):
        return lax.dot_general(a, b, NT_DIMS, preferred_element_type=F32)

    def scores(half):
        qh = q_ref[:, half * half_w:(half + 1) * half_w]
        q_hi = qh.astype(BF16)
        q_lo = (qh - q_hi.astype(F32)).astype(BF16)
        kk = k_ref[half]
        k_hi = kk.astype(BF16)
        k_lo = (kk - k_hi.astype(F32)).astype(BF16)
        return nt(k_hi, q_hi) + (nt(k_hi, q_lo) + nt(k_lo, q_hi))

    def sorted_top(s, slot):
        work = s
        for r in range(R):
            m = jnp.max(work, axis=0, keepdims=True)
            top_ref[slot, r:r + 1, :] = m
            work = jnp.where(work == m, NEG_BIG, work)

    s1 = scores(0)
    s2 = scores(1)
    sorted_top(s1, 0)
    sorted_top(s2, 1)
    a = top_ref[0]
    b = top_ref[1]
    cand_ref[0:R, :] = a[0:1, :] + b
    for i in range(1, 8):
        cand_ref[R + 8 * (i - 1):R + 8 * i, :] = a[i:i + 1, :] + b[0:8, :]
    cand_ref[R + 56:R + 64, :] = a[8:16, :] + b[0:1, :]
    work = cand_ref[...]
    m0 = jnp.max(work, axis=0, keepdims=True)
    m = m0
    z = jnp.ones_like(m0)
    for r in range(1, R):
        work = jnp.where(work == m, NEG_BIG, work)
        m = jnp.max(work, axis=0, keepdims=True)
        z = z + jnp.exp(m - m0)
    e1 = jnp.exp(s1 - a[0:1, :]) * (1.0 / z)
    s2_ref[0] = s2
    e2_ref[0] = jnp.exp(s2 - b[0:1, :])
    for lt in range(s1.shape[1] // LANE):
        cols = slice(lt * LANE, (lt + 1) * LANE)
        s1_ref[0, lt] = s1[:, cols]
        e1_ref[0, lt] = e1[:, cols]
        tau_ref[0, lt] = jnp.broadcast_to(m[:, cols], (SUBLANE, LANE))


def peer_prep(q, keys, *, tm):
    T = q.shape[0]
    H, K = PEER_HEADS, PEER_KEYS
    tab = jax.ShapeDtypeStruct((H, K, T), F32)
    tab_spec = pl.BlockSpec((1, K, tm), lambda i, h: (h, 0, i))
    slab = jax.ShapeDtypeStruct((H, T // LANE, K, LANE), F32)
    slab_spec = pl.BlockSpec((1, tm // LANE, K, LANE), lambda i, h: (h, i, 0, 0))
    return pl.pallas_call(
        _peer_prep_kernel,
        grid=(T // tm, H),
        in_specs=[
            pl.BlockSpec((tm, PEER_DQ), lambda i, h: (i, h)),
            pl.BlockSpec((None, 2, K, PEER_DQ // 2), lambda i, h: (h, 0, 0, 0)),
        ],
        out_specs=[slab_spec, slab_spec, tab_spec, tab_spec,
                   pl.BlockSpec((1, tm // LANE, SUBLANE, LANE), lambda i, h: (h, i, 0, 0))],
        out_shape=[slab, slab, tab, tab, jax.ShapeDtypeStruct((H, T // LANE, SUBLANE, LANE), F32)],
        scratch_shapes=[pltpu.VMEM((2, PEER_TOPK, tm), F32), pltpu.VMEM((N_CAND, tm), F32)],
        compiler_params=_cparams(("arbitrary", "arbitrary")),
        name="peer_prep",
    )(q, keys)


def _transpose_gated_res_kernel(h_ref, ft_ref, gate_ref, o_ref, *, ctx_len, tm):
    i = pl.program_id(0)
    row = i * tm + lax.broadcasted_iota(jnp.int32, (tm, 1), 0)
    gate = jnp.where(row < ctx_len, gate_ref[0:1, :], gate_ref[1:2, :])
    o_ref[...] = h_ref[...] + gate * ft_ref[...].T


def transpose_gated_res(h, ft, gates, *, ctx_len, tm):
    T, D = h.shape
    return pl.pallas_call(
        functools.partial(_transpose_gated_res_kernel, ctx_len=ctx_len, tm=tm),
        grid=(T // tm,),
        in_specs=[
            pl.BlockSpec((tm, D), lambda i: (i, 0)),
            pl.BlockSpec((D, tm), lambda i: (0, i)),
            pl.BlockSpec((2, D), lambda i: (0, 0)),
        ],
        out_specs=pl.BlockSpec((tm, D), lambda i: (i, 0)),
        out_shape=jax.ShapeDtypeStruct((T, D), F32),
        compiler_params=_cparams(("arbitrary",)),
        name="transpose_gated_res",
    )(h, ft, gates)


def _rmsnorm_kernel(x_ref, g_ref, o_ref):
    x = x_ref[...]
    ms = jnp.mean(x * x, axis=-1, keepdims=True)
    o_ref[...] = x * lax.rsqrt(ms + EPS) * g_ref[...]


def rmsnorm_rows(x, g, *, tm):
    T, D = x.shape
    return pl.pallas_call(
        _rmsnorm_kernel,
        grid=(T // tm,),
        in_specs=[pl.BlockSpec((tm, D), lambda i: (i, 0)), pl.BlockSpec((1, D), lambda i: (0, 0))],
        out_specs=pl.BlockSpec((tm, D), lambda i: (i, 0)),
        out_shape=jax.ShapeDtypeStruct((T, D), F32),
        compiler_params=_cparams(("arbitrary",)),
        name="rmsnorm_rows",
    )(x, g.reshape(1, D))


def grid_sincos(n_tokens, dim):
    rows = n_tokens // GRID_W
    row = jnp.repeat(jnp.arange(rows), GRID_W).astype(F32)
    col = jnp.tile(jnp.arange(GRID_W), rows).astype(F32)
    n_freq = dim // 4
    omega = 1.0 / (POS_BASE ** (jnp.arange(n_freq, dtype=F32) / n_freq))

    def enc(pp):
        a = pp[:, None] * omega[None, :]
        return jnp.concatenate([jnp.sin(a), jnp.cos(a)], axis=-1)

    return jnp.concatenate([enc(row), enc(col)], axis=-1)


def _split_bf16(a):
    hi = a.astype(BF16)
    return hi, (a - hi.astype(F32)).astype(BF16)


def _dot3s(a, b_hi, b_lo):
    a_hi, a_lo = _split_bf16(a)
    return _dot(a_hi, b_hi) + (_dot(a_hi, b_lo) + _dot(a_lo, b_hi))


def _dot3(a, b):
    return _dot3s(a, *_split_bf16(b))


def _adaln_kernel(c_ref, w_ref, b_ref, o_ref):
    c = c_ref[...]
    o_ref[...] = _dot3(c * _sigmoid(c), w_ref[...]) + b_ref[...]


def adaln(cond, w, b, *, tn=1536):
    R, D = cond.shape
    N = w.shape[1]
    return pl.pallas_call(
        _adaln_kernel,
        grid=(N // tn,),
        in_specs=[pl.BlockSpec((R, D), lambda j: (0, 0)),
                  pl.BlockSpec((D, tn), lambda j: (0, j)),
                  pl.BlockSpec((1, tn), lambda j: (0, j))],
        out_specs=pl.BlockSpec((R, tn), lambda j: (0, j)),
        out_shape=jax.ShapeDtypeStruct((R, N), F32),
        compiler_params=_cparams(("arbitrary",)),
        name="adaln",
    )(cond, w, b.reshape(1, N))


def _hyena_pre_kernel(x0_ref, x1_ref, v_ref, w_ref, b_ref, x0o_ref, s_ref, *, ctx_len):
    T = x0_ref.shape[0]
    row = lax.broadcasted_iota(jnp.int32, (T, 1), 0)
    first = jnp.logical_or(row == 0, row == ctx_len)
    last = jnp.logical_or(row == ctx_len - 1, row == T - 1)

    def conv(ref, part):
        x = ref[...]
        prev = jnp.where(first, 0.0, pltpu.roll(x, 1, 0))
        nxt = jnp.where(last, 0.0, pltpu.roll(x, T - 1, 0))
        w = lambda tap: w_ref[tap, part:part + 1, :]
        return w(0) * prev + w(1) * x + w(2) * nxt + b_ref[part:part + 1, :]

    x0o_ref[...] = conv(x0_ref, 0)
    s_ref[...] = conv(x1_ref, 1) * conv(v_ref, 2)


def hyena_pre(p, conv_w, conv_b, *, ctx_len):
    T = p.shape[0]
    nb = D_HY // LANE
    col = lambda part: pl.BlockSpec((T, LANE), lambda c: (0, OFF_HY // LANE + part * nb + c))
    out = jax.ShapeDtypeStruct((T, D_HY), F32)
    return pl.pallas_call(
        functools.partial(_hyena_pre_kernel, ctx_len=ctx_len),
        grid=(nb,),
        in_specs=[col(0), col(1), col(2),
                  pl.BlockSpec((HY_SHORT, 3, LANE), lambda c: (0, 0, c)),
                  pl.BlockSpec((3, LANE), lambda c: (0, c))],
        out_specs=[pl.BlockSpec((T, LANE), lambda c: (0, c))] * 2,
        out_shape=[out, out],
        compiler_params=_cparams(("arbitrary",)),
        name="hyena_pre",
    )(p, p, p, conv_w.reshape(HY_SHORT, 3, D_HY), conv_b.reshape(3, D_HY))


def _hyena_filter_kernel(band_ref, w1_ref, b1_ref, w2_ref, b2_ref, w3_ref, fr_ref, dec_ref, o_ref, *, n, tr):
    i = pl.program_id(0)
    ri = (i * tr + lax.broadcasted_iota(jnp.int32, (tr, 1), 0)).astype(F32)
    t = ri * (1.0 / (n - 1))
    w = ri * (2.0 * math.pi / n)
    lane = lax.broadcasted_iota(jnp.int32, (1, LANE), 1)
    arg = w * band_ref[...]
    z = jnp.where(lane == 0, t,
                  jnp.where(lane <= HY_BANDS, jnp.cos(arg),
                            jnp.where(lane <= 2 * HY_BANDS, -jnp.sin(arg), 0.0)))
    h = jnp.sin(fr_ref[0:1, :] * (_dot3(z, w1_ref[...]) + b1_ref[...]))
    h = jnp.sin(fr_ref[1:2, :] * (_dot3(h, w2_ref[...]) + b2_ref[...]))
    h = _dot3(h, w3_ref[...]) * jnp.exp(-t * dec_ref[...])
    col = lax.broadcasted_iota(jnp.int32, (1, 2 * D_HY), 1)
    o_ref[...] = jnp.where(jnp.logical_and(ri == 0.0, col >= D_HY), 0.0, h)


def hyena_filter(n, w1, b1, w2, b2, w3, freq, decay, *, tr=256):
    bands = jnp.linspace(1e-4, HY_BANDS - 1, HY_BANDS, dtype=F32)
    band_row = jnp.zeros((1, LANE), F32).at[0, 1:1 + HY_BANDS].set(bands).at[0, 1 + HY_BANDS:1 + 2 * HY_BANDS].set(bands)
    w1p = jnp.zeros((LANE, HY_HIDDEN), F32).at[:HY_FEAT].set(w1)
    const = lambda a: pl.BlockSpec(a.shape, lambda i: (0,) * a.ndim)
    args = (band_row, w1p, b1.reshape(1, -1), w2, b2.reshape(1, -1), w3, freq,
            jnp.abs(decay).reshape(1, 2 * D_HY))
    return pl.pallas_call(
        functools.partial(_hyena_filter_kernel, n=n, tr=tr),
        grid=(n // tr,),
        in_specs=[const(a) for a in args],
        out_specs=pl.BlockSpec((tr, 2 * D_HY), lambda i: (i, 0)),
        out_shape=jax.ShapeDtypeStruct((n, 2 * D_HY), F32),
        compiler_params=_cparams(("arbitrary",)),
        name="hyena_filter",
    )(*args)


FFT_R = 128
FFT_N = FFT_R * FFT_R
FFT_CH = 32


def _dft_tables():
    r = np.arange(FFT_R)
    ang = 2.0 * np.pi * np.outer(r, r) / FFT_R
    c, s = np.cos(ang), np.sin(ang)
    angt = 2.0 * np.pi * np.outer(r, r) / FFT_N
    ct, st = np.cos(angt), np.sin(angt)

    def split(m):
        m = jnp.asarray(m, F32)
        hi = m.astype(BF16)
        return jnp.stack([hi, (m - hi.astype(F32)).astype(BF16)])

    fwd_a = split(np.concatenate([c, -s], axis=1))
    fwd_b = split(np.block([[c, -s], [s, c]]))
    inv_b = split(np.block([[c, s], [-s, c]]) / FFT_N)
    inv_a = split(np.concatenate([c, -s], axis=0))
    tw = (jnp.asarray(ct, F32), jnp.asarray(-st, F32))
    twc = (jnp.asarray(ct, F32), jnp.asarray(st, F32))
    return fwd_a, fwd_b, inv_b, inv_a, tw, twc


def _twiddle_transpose(y, twr, twi, dst_ref, nc):
    y = y.reshape(nc, FFT_R, 2 * FFT_R)
    yr, yi = y[:, :, :FFT_R], y[:, :, FFT_R:]
    zr = yr * twr - yi * twi
    zi = yr * twi + yi * twr
    for c in range(nc):
        dst_ref[c * FFT_R:(c + 1) * FFT_R, 0:FFT_R] = zr[c].T
        dst_ref[c * FFT_R:(c + 1) * FFT_R, FFT_R:2 * FFT_R] = zi[c].T


def _hyena_fwd_kernel(x_ref, fa_ref, twr_ref, twi_ref, fb_ref, o_ref, l_ref, yt_ref, *, nc):
    j = pl.program_id(1)
    half = FFT_R // 2

    @pl.when(j == 0)
    def _():
        zeros = jnp.zeros((half, LANE), F32)

        def body(b, carry):
            t = jnp.concatenate([x_ref[pl.ds(b, half, stride=FFT_R), :], zeros], axis=0)
            l_ref[pl.ds(b, LANE, stride=FFT_R), :] = t.T
            return carry

        lax.fori_loop(0, FFT_R, body, 0, unroll=8)

    rows = nc * FFT_R
    lc = l_ref[pl.ds(pl.multiple_of(j * rows, rows), rows), :]
    y = _dot3s(lc, fa_ref[0], fa_ref[1])
    _twiddle_transpose(y, twr_ref[...], twi_ref[...], yt_ref, nc)
    o_ref[...] = _dot3s(yt_ref[...], fb_ref[0], fb_ref[1])


def hyena_fwd_dft(x, tables):
    n, C = x.shape
    assert n * 2 == FFT_N and C % LANE == 0
    fwd_a, fwd_b, _, _, (twr, twi), _ = tables
    nc = FFT_CH
    steps = LANE // nc
    const = lambda a: pl.BlockSpec(a.shape, lambda cb, j: (0,) * a.ndim)
    return pl.pallas_call(
        functools.partial(_hyena_fwd_kernel, nc=nc),
        grid=(C // LANE, steps),
        in_specs=[pl.BlockSpec((n, LANE), lambda cb, j: (0, cb)), const(fwd_a), const(twr), const(twi), const(fwd_b)],
        out_specs=pl.BlockSpec((nc * FFT_R, 2 * FFT_R), lambda cb, j: (cb * steps + j, 0)),
        out_shape=jax.ShapeDtypeStruct((C * FFT_R, 2 * FFT_R), F32),
        scratch_shapes=[pltpu.VMEM((LANE * FFT_R, LANE), F32), pltpu.VMEM((nc * FFT_R, 2 * FFT_R), F32)],
        compiler_params=_cparams(("arbitrary", "arbitrary")),
        name="hyena_fwd_dft",
    )(x, fwd_a, twr, twi, fwd_b)


def _hyena_inv_kernel(s_ref, kf_ref, kb_ref, gb_ref, twr_ref, twi_ref, ga_ref, o_ref, lr_ref, at_ref, *, nc, steps):
    j = pl.program_id(1)
    R = FFT_R
    s, kf, kb = s_ref[...], kf_ref[...], kb_ref[...]
    sr, si = s[:, :R], s[:, R:]
    kr = kf[:, :R] + kb[:, :R]
    ki = kf[:, R:] - kb[:, R:]
    p = jnp.concatenate([sr * kr - si * ki, sr * ki + si * kr], axis=1)
    a = _dot3s(p, gb_ref[0], gb_ref[1])
    _twiddle_transpose(a, twr_ref[...], twi_ref[...], at_ref, nc)
    rows = nc * R
    lr_ref[pl.ds(pl.multiple_of(j * rows, rows), rows), :] = _dot3s(at_ref[...], ga_ref[0], ga_ref[1])

    @pl.when(j == steps - 1)
    def _():
        def body(b, carry):
            t = lr_ref[pl.ds(b, LANE, stride=R), :]
            o_ref[pl.ds(b, R // 2, stride=R), :] = t.T[0:R // 2, :]
            return carry

        lax.fori_loop(0, R, body, 0, unroll=8)


def hyena_inv_dft(s_f, k_f, tables):
    C = s_f.shape[0] // FFT_R
    _, _, inv_b, inv_a, _, (twr, twi) = tables
    nc = FFT_CH
    steps = LANE // nc
    kb_off = C // nc
    const = lambda a: pl.BlockSpec(a.shape, lambda cb, j: (0,) * a.ndim)
    blk = lambda off: pl.BlockSpec((nc * FFT_R, 2 * FFT_R), lambda cb, j: (off + cb * steps + j, 0))
    return pl.pallas_call(
        functools.partial(_hyena_inv_kernel, nc=nc, steps=steps),
        grid=(C // LANE, steps),
        in_specs=[blk(0), blk(0), blk(kb_off), const(inv_b), const(twr), const(twi), const(inv_a)],
        out_specs=pl.BlockSpec((FFT_N // 2, LANE), lambda cb, j: (0, cb)),
        out_shape=jax.ShapeDtypeStruct((FFT_N // 2, C), F32),
        scratch_shapes=[pltpu.VMEM((LANE * FFT_R, LANE), F32), pltpu.VMEM((nc * FFT_R, 2 * FFT_R), F32)],
        compiler_params=_cparams(("arbitrary", "arbitrary")),
        name="hyena_inv_dft",
    )(s_f, k_f, k_f, inv_b, twr, twi, inv_a)


def _hyena_ctx_kernel(s_ref, k_ref, fw_ref, iv_ref, o_ref):
    n = s_ref.shape[0]
    N = 2 * n
    xs = _dot3(fw_ref[...], s_ref[...])
    xk = _dot3(fw_ref[...], k_ref[...])
    sr, si = xs[:N], xs[N:]
    kr = xk[:N, :D_HY] + xk[:N, D_HY:]
    ki = xk[N:, :D_HY] - xk[N:, D_HY:]
    p = jnp.concatenate([sr * kr - si * ki, sr * ki + si * kr], axis=0)
    o_ref[...] = _dot3(iv_ref[...], p)


def hyena_ctx_conv(s_c, k_c):
    n = s_c.shape[0]
    N = 2 * n
    ang = 2.0 * np.pi * np.outer(np.arange(N), np.arange(N)) / N
    c, s = np.cos(ang), np.sin(ang)
    fw = jnp.asarray(np.concatenate([c[:, :n], -s[:, :n]], axis=0), F32)
    iv = jnp.asarray(np.concatenate([c[:n, :], -s[:n, :]], axis=1) / N, F32)
    full = lambda a: pl.BlockSpec(a.shape, lambda i: (0,) * a.ndim)
    return pl.pallas_call(
        _hyena_ctx_kernel,
        grid=(1,),
        in_specs=[full(s_c), full(k_c), full(fw), full(iv)],
        out_specs=pl.BlockSpec((n, D_HY), lambda i: (0, 0)),
        out_shape=jax.ShapeDtypeStruct((n, D_HY), F32),
        compiler_params=_cparams(("arbitrary",)),
        name="hyena_ctx_conv",
    )(s_c, k_c, fw, iv)


def hyena_mixer_parts(p, conv_w, conv_b, filt, tables, *, ctx_len):
    T = p.shape[0]
    x0, s = hyena_pre(p, conv_w, conv_b, ctx_len=ctx_len)
    k_lat = hyena_filter(T - ctx_len, *filt)
    k_ctx = hyena_filter(ctx_len, *filt)
    conv_l = hyena_inv_dft(hyena_fwd_dft(s[ctx_len:], tables), hyena_fwd_dft(k_lat, tables), tables)
    conv_c = hyena_ctx_conv(s[:ctx_len], k_ctx)
    return x0, jnp.concatenate([conv_c, conv_l], axis=0), s


def _pack_w_in(w_in_l):
    widths = (D_S5, D_GLA_K, D_GLA_K, D_GLA_V, D_GLA_V, 2 * GLA_RANK, 3 * D_HY)
    offs = [0]
    for wd in widths:
        offs.append(offs[-1] + wd)
    s5, q, k, v, gate, alpha, hy = (w_in_l[:, offs[i]:offs[i + 1]] for i in range(7))
    pad = jnp.zeros((w_in_l.shape[0], IN_PACKED - OFF_ALPHA - 2 * GLA_RANK), w_in_l.dtype)
    return jnp.concatenate([v, gate, s5, q, k, hy, alpha, pad], axis=1).astype(BF16)


def kernel(x, c, ctx, c_ctx, w_ada, b_ada, g_norm1, g_norm2, w_in, s5_a_re, s5_a_im, s5_log_step, s5_b_re, s5_b_im, s5_c_re, s5_c_im, s5_d, s5_w_glu, s5_b_glu, gla_w_alpha, gla_b_alpha, gla_g_norm, hy_conv_w, hy_conv_b, hy_f_w1, hy_f_b1, hy_f_w2, hy_f_b2, hy_f_w3, hy_f_freq, hy_decay, hy_bias, w_out, peer_w_q, peer_keys, peer_u, peer_v, g_final):
    L = x.shape[1]
    Lc = ctx.shape[1]
    T = L + Lc
    TM = 768
    h = jnp.concatenate([ctx[0], x[0] + grid_sincos(L, D_MODEL)], axis=0)
    cond = jnp.zeros((SUBLANE, D_MODEL), F32).at[0].set(c_ctx).at[1].set(c[0])
    tables = _dft_tables()

    for l in range(DEPTH):
        m = adaln(cond, w_ada[l], b_ada[l])[0:2]
        sh1, sc1, gt1, sh2, sc2, gt2 = jnp.split(m, 6, axis=-1)
        mod1 = jnp.stack([sh1[0], sc1[0], sh1[1], sc1[1]], axis=0)
        mod2 = jnp.stack([sh2[0], sc2[0], sh2[1], sc2[1]], axis=0)

        p = norm_mod_matmul(h, g_norm1[l], mod1, _pack_w_in(w_in[l]), ctx_len=Lc, tm=TM, tn=768)

        bm, cm, pw = s5_prepare(s5_a_re[l], s5_a_im[l], s5_log_step[l], s5_b_re[l], s5_b_im[l],
                                s5_c_re[l], s5_c_im[l])
        ys_f, ys_b = s5_scan(p, bm, cm, pw, ctx_len=Lc)

        wa = jnp.zeros((2, LANE, D_GLA_K), F32)
        wa = wa.at[0, 0:GLA_RANK].set(gla_w_alpha[l, 0]).at[1, GLA_RANK:2 * GLA_RANK].set(gla_w_alpha[l, 1])
        wa = wa.astype(BF16)
        og_f, og_b = gla_scan(p, wa, gla_b_alpha[l], ctx_len=Lc)

        filt = (hy_f_w1[l], hy_f_b1[l], hy_f_w2[l], hy_f_b2[l], hy_f_w3[l], hy_f_freq[l], hy_decay[l])
        x0, conv, s = hyena_mixer_parts(p, hy_conv_w[l], hy_conv_b[l], filt, tables, ctx_len=Lc)

        mix = mixer_finish(p, ys_f, ys_b, s5_d[l], s5_w_glu[l].astype(BF16), s5_b_glu[l],
                           og_f, og_b, gla_g_norm[l], x0, conv, s, hy_bias[l], tm=256)
        h = matmul_gated_res(mix, w_out[l].astype(BF16), h, gt1, ctx_len=Lc, tm=TM, tn=1024)

        q, xn = norm_mod_matmul(h, g_norm2[l], mod2, peer_w_q[l].astype(BF16),
                                ctx_len=Lc, tm=TM, tn=1024, emit_xn=True)
        s1t, e1t, s2t, e2t, taut = peer_prep(q, peer_keys[l], tm=TM)
        ft = peer_dense(xn, peer_u[l].astype(BF16), peer_v[l].T.astype(BF16),
                        s1t, e1t, s2t, e2t, taut, tm=TM)
        h = transpose_gated_res(h, ft, gt2, ctx_len=Lc, tm=256)

    out = rmsnorm_rows(h[Lc:], g_final, tm=512)
    return out[None]
```

```python
import functools
import math

import jax
import jax.numpy as jnp
import numpy as np
from jax import lax
from jax.experimental import pallas as pl
from jax.experimental.pallas import tpu as pltpu

F32 = jnp.float32
BF16 = jnp.bfloat16

D_MODEL = 2048
DEPTH = 4
GRID_W = 64
EPS = 1e-6
POS_BASE = 10000.0

D_S5 = D_MODEL // 4
S5_GROUP = 16
S5_GROUPS = D_S5 // S5_GROUP
S5_STATE = 64
S5_BLK_GROUPS = 8
S5_BLK_CH = S5_BLK_GROUPS * S5_GROUP
S5_BLK_ST = S5_BLK_GROUPS * S5_STATE
S5_NBLK = S5_GROUPS // S5_BLK_GROUPS

GLA_HEADS = 4
D_GLA_K = D_MODEL // 4
D_GLA_V = D_MODEL // 2
GLA_DK = D_GLA_K // GLA_HEADS
GLA_DV = D_GLA_V // GLA_HEADS
GLA_RANK = 16
GLA_TAU = 16.0
GLA_CHUNK = 64

D_HY = D_MODEL // 4
HY_SHORT = 3
HY_BANDS = 16
HY_FEAT = 1 + 2 * HY_BANDS
HY_HIDDEN = 64
HY_TARGET = 1e-2
HY_MIN_DECAY = -math.log(HY_TARGET) / 1.5
HY_MAX_DECAY = -math.log(HY_TARGET) / 0.3

PEER_HEADS = 8
PEER_KEYS = 128
PEER_EXPERTS = PEER_KEYS * PEER_KEYS
PEER_DQ = 256
PEER_TOPK = 16

OFF_V = 0
OFF_GATE = OFF_V + D_GLA_V
OFF_S5 = OFF_GATE + D_GLA_V
OFF_Q = OFF_S5 + D_S5
OFF_K = OFF_Q + D_GLA_K
OFF_HY = OFF_K + D_GLA_K
OFF_ALPHA = OFF_HY + 3 * D_HY
LANE = 128
SUBLANE = 8
IN_PACKED = OFF_ALPHA + 2 * LANE

SEQ_TILE = 256
VMEM_LIMIT = 56 * 1024 * 1024

NT_DIMS = (((1,), (1,)), ((), ()))
TN_DIMS = (((0,), (0,)), ((), ()))


def _cparams(sem):
    return pltpu.CompilerParams(dimension_semantics=sem, vmem_limit_bytes=VMEM_LIMIT)


def _dot(a, b):
    return jnp.dot(a, b, preferred_element_type=F32)


def _gelu(x):
    k1 = -2.0 * math.sqrt(2.0 / math.pi) * math.log2(math.e)
    k2 = 0.044715 * k1
    return x / (1.0 + jnp.exp2(x * (k1 + k2 * (x * x))))


def _sigmoid(x):
    return 1.0 / (1.0 + jnp.exp(-x))


def _norm_mod_matmul_kernel(x_ref, g_ref, mod_ref, w_ref, o_ref, *rest, ctx_len, tm, emit_xn):
    if emit_xn:
        xo_ref, xn_ref = rest
    else:
        (xn_ref,) = rest
    i = pl.program_id(0)
    j = pl.program_id(1)

    @pl.when(j == 0)
    def _():
        x = x_ref[...]
        ms = jnp.mean(x * x, axis=-1, keepdims=True)
        y = x * lax.rsqrt(ms + EPS) * g_ref[...]
        row = i * tm + lax.broadcasted_iota(jnp.int32, (tm, 1), 0)
        is_ctx = row < ctx_len
        shift = jnp.where(is_ctx, mod_ref[0:1, :], mod_ref[2:3, :])
        scale = jnp.where(is_ctx, mod_ref[1:2, :], mod_ref[3:4, :])
        xn = (y * (1.0 + scale) + shift).astype(BF16)
        xn_ref[...] = xn
        if emit_xn:
            xo_ref[...] = xn

    o_ref[...] = _dot(xn_ref[...], w_ref[...])


def norm_mod_matmul(x, g, mod, w, *, ctx_len, tm, tn, emit_xn=False):
    T, D = x.shape
    N = w.shape[1]
    out_shape = [jax.ShapeDtypeStruct((T, N), F32)]
    out_specs = [pl.BlockSpec((tm, tn), lambda i, j: (i, j))]
    if emit_xn:
        out_shape.append(jax.ShapeDtypeStruct((T, D), BF16))
        out_specs.append(pl.BlockSpec((tm, D), lambda i, j: (i, 0)))
    res = pl.pallas_call(
        functools.partial(_norm_mod_matmul_kernel, ctx_len=ctx_len, tm=tm, emit_xn=emit_xn),
        grid=(T // tm, N // tn),
        in_specs=[
            pl.BlockSpec((tm, D), lambda i, j: (i, 0)),
            pl.BlockSpec((1, D), lambda i, j: (0, 0)),
            pl.BlockSpec((4, D), lambda i, j: (0, 0)),
            pl.BlockSpec((D, tn), lambda i, j: (0, j)),
        ],
        out_specs=out_specs,
        out_shape=out_shape,
        scratch_shapes=[pltpu.VMEM((tm, D), BF16)],
        compiler_params=_cparams(("arbitrary", "arbitrary")),
        name="norm_mod_matmul",
    )(x, g.reshape(1, D), mod, w)
    return res if emit_xn else res[0]


def _matmul_gated_res_kernel(a_ref, w_ref, r_ref, gate_ref, o_ref, *, ctx_len, tm):
    i = pl.program_id(0)
    row = i * tm + lax.broadcasted_iota(jnp.int32, (tm, 1), 0)
    gate = jnp.where(row < ctx_len, gate_ref[0:1, :], gate_ref[1:2, :])
    o_ref[...] = r_ref[...] + gate * _dot(a_ref[...], w_ref[...])


def matmul_gated_res(a, w, res, gates, *, ctx_len, tm, tn):
    T, K = a.shape
    N = w.shape[1]
    return pl.pallas_call(
        functools.partial(_matmul_gated_res_kernel, ctx_len=ctx_len, tm=tm),
        grid=(T // tm, N // tn),
        in_specs=[
            pl.BlockSpec((tm, K), lambda i, j: (i, 0)),
            pl.BlockSpec((K, tn), lambda i, j: (0, j)),
            pl.BlockSpec((tm, tn), lambda i, j: (i, j)),
            pl.BlockSpec((2, tn), lambda i, j: (0, j)),
        ],
        out_specs=pl.BlockSpec((tm, tn), lambda i, j: (i, j)),
        out_shape=jax.ShapeDtypeStruct((T, N), F32),
        compiler_params=_cparams(("arbitrary", "arbitrary")),
        name="matmul_gated_res",
    )(a, w, res, gates)


def _time_tile(t, n_tiles, n_ctx_tiles, rev):
    if not rev:
        return t
    return jnp.where(t < n_ctx_tiles, n_ctx_tiles - 1 - t, n_tiles - 1 - (t - n_ctx_tiles))


def _s5_kernel(uf_ref, ub_ref, bm_ref, cm_ref, pw_ref, yf_ref, yb_ref, h_ref, c_ref, *, lt):
    t = pl.program_id(1)
    ns = S5_BLK_ST
    u_refs = (uf_ref, ub_ref)
    y_refs = (yf_ref, yb_ref)

    @pl.when(t == 0)
    def _():
        c_ref[...] = jnp.zeros_like(c_ref)

    for d in range(2):
        bu = _dot(u_refs[d][...].astype(BF16), bm_ref[d])
        h_ref[d, 0] = bu[:, :ns]
        h_ref[d, 1] = bu[:, ns:]

    n_grp = lt // 8

    def group_update(d, g, c_re, c_im):
        rev = d == 1
        gi = (n_grp - 1 - g) if rev else g
        rows = pl.ds(pl.multiple_of(gi * 8, 8), 8)
        a_re = h_ref[d, 0, rows, :]
        a_im = h_ref[d, 1, rows, :]
        for k, s in enumerate((1, 2, 4)):
            l_re = pw_ref[d, 8 + 8 * k:16 + 8 * k, :ns]
            l_im = pw_ref[d, 8 + 8 * k:16 + 8 * k, ns:]
            shift = (8 - s) if rev else s
            s_re = pltpu.roll(a_re, shift, 0)
            s_im = pltpu.roll(a_im, shift, 0)
            a_re = a_re + (l_re * s_re - l_im * s_im)
            a_im = a_im + (l_re * s_im + l_im * s_re)
        p_re = pw_ref[d, 0:8, :ns]
        p_im = pw_ref[d, 0:8, ns:]
        a_re = a_re + (p_re * c_re - p_im * c_im)
        a_im = a_im + (p_re * c_im + p_im * c_re)
        h_ref[d, 0, rows, :] = a_re
        h_ref[d, 1, rows, :] = a_im
        edge = slice(0, 1) if rev else slice(7, 8)
        return a_re[edge, :], a_im[edge, :]

    def body(g, carry):
        f_re, f_im, b_re, b_im = carry
        f_re, f_im = group_update(0, g, f_re, f_im)
        b_re, b_im = group_update(1, g, b_re, b_im)
        return f_re, f_im, b_re, b_im

    carry = lax.fori_loop(0, n_grp, body, (c_ref[0, 0], c_ref[0, 1], c_ref[1, 0], c_ref[1, 1]), unroll=2)
    c_ref[0, 0], c_ref[0, 1], c_ref[1, 0], c_ref[1, 1] = carry

    for d in range(2):
        y_refs[d][...] = (_dot(h_ref[d, 0].astype(BF16), cm_ref[d, :ns, :])
                          + _dot(h_ref[d, 1].astype(BF16), cm_ref[d, ns:, :]))


def s5_scan(p, bm, cm, pw, *, ctx_len, lt=SEQ_TILE):
    T = p.shape[0]
    n_tiles = T // lt
    n_ctx = ctx_len // lt
    tf = functools.partial(_time_tile, n_tiles=n_tiles, n_ctx_tiles=n_ctx, rev=False)
    tb = functools.partial(_time_tile, n_tiles=n_tiles, n_ctx_tiles=n_ctx, rev=True)
    ucol = OFF_S5 // S5_BLK_CH
    out = jax.ShapeDtypeStruct((T, D_S5), F32)
    return pl.pallas_call(
        functools.partial(_s5_kernel, lt=lt),
        grid=(S5_NBLK, n_tiles),
        in_specs=[
            pl.BlockSpec((lt, S5_BLK_CH), lambda b, t: (tf(t), ucol + b)),
            pl.BlockSpec((lt, S5_BLK_CH), lambda b, t: (tb(t), ucol + b)),
            pl.BlockSpec((2, None, S5_BLK_CH, 2 * S5_BLK_ST), lambda b, t: (0, b, 0, 0)),
            pl.BlockSpec((2, None, 2 * S5_BLK_ST, S5_BLK_CH), lambda b, t: (0, b, 0, 0)),
            pl.BlockSpec((2, None, 32, 2 * S5_BLK_ST), lambda b, t: (0, b, 0, 0)),
        ],
        out_specs=[pl.BlockSpec((lt, S5_BLK_CH), lambda b, t: (tf(t), b)),
                   pl.BlockSpec((lt, S5_BLK_CH), lambda b, t: (tb(t), b))],
        out_shape=[out, out],
        scratch_shapes=[pltpu.VMEM((2, 2, lt, S5_BLK_ST), F32), pltpu.VMEM((2, 2, 1, S5_BLK_ST), F32)],
        compiler_params=_cparams(("arbitrary", "arbitrary")),
        name="s5_scan",
    )(p, p, bm, cm, pw)


def _cmul(a, b):
    return a[0] * b[0] - a[1] * b[1], a[0] * b[1] + a[1] * b[0]


def s5_prepare(a_re, a_im, log_step, b_re, b_im, c_re, c_im):
    G, P, Cg = S5_GROUPS, S5_STATE, S5_GROUP
    dt = jnp.exp(log_step)[..., None]
    er = jnp.exp(a_re * dt)
    lam1 = (er * jnp.cos(a_im * dt), er * jnp.sin(a_im * dt))
    den = a_re * a_re + a_im * a_im
    xr, xi = lam1[0] - 1.0, lam1[1]
    coef = ((xr * a_re + xi * a_im) / den, (xi * a_re - xr * a_im) / den)
    bb_re = coef[0][..., None] * b_re - coef[1][..., None] * b_im
    bb_im = coef[0][..., None] * b_im + coef[1][..., None] * b_re
    pows = [lam1]
    for _ in range(7):
        pows.append(_cmul(pows[-1], lam1))

    eye = jnp.eye(S5_BLK_GROUPS, dtype=F32)

    def blockdiag_in(m):
        m = m.reshape(2, S5_NBLK, S5_BLK_GROUPS, P, Cg)
        return jnp.einsum('dbgpc,gh->dbgchp', m, eye).reshape(2, S5_NBLK, S5_BLK_CH, S5_BLK_ST)

    def blockdiag_out(m):
        m = m.reshape(2, S5_NBLK, S5_BLK_GROUPS, Cg, P)
        return jnp.einsum('dbgcp,gh->dbgphc', m, eye).reshape(2, S5_NBLK, S5_BLK_ST, S5_BLK_CH)

    bm = jnp.concatenate([blockdiag_in(bb_re), blockdiag_in(bb_im)], axis=-1).astype(BF16)
    cm = jnp.concatenate([blockdiag_out(c_re), blockdiag_out(-c_im)], axis=-2).astype(BF16)

    def lay(v):
        return v.reshape(2, S5_NBLK, S5_BLK_ST)

    def table(rev):
        power = lambda k: jnp.concatenate([lay(pows[k][0]), lay(pows[k][1])], axis=-1)
        rows = [power(7 - r if rev else r) for r in range(8)]
        zero = jnp.zeros_like(rows[0])
        for s in (1, 2, 4):
            for r in range(8):
                inside = (r < 8 - s) if rev else (r >= s)
                rows.append(power(s - 1) if inside else zero)
        return jnp.stack(rows, axis=2)

    return bm, cm, jnp.stack([table(False)[0], table(True)[1]])


def _log_sigmoid(z):
    return jnp.minimum(z, 0.0) - jnp.log(1.0 + jnp.exp(-jnp.abs(z)))


def _gla_kernel(qf_ref, kf_ref, vf_ref, af_ref, qb_ref, kb_ref, vb_ref, ab_ref, wa_ref, ba_ref,
                of_ref, ob_ref, st_ref, *, lt):
    t = pl.program_id(1)
    C = GLA_CHUNK

    @pl.when(t == 0)
    def _():
        st_ref[...] = jnp.zeros_like(st_ref)

    ri = lax.broadcasted_iota(jnp.int32, (C, C), 0)
    ci = lax.broadcasted_iota(jnp.int32, (C, C), 1)
    refs = ((qf_ref, kf_ref, vf_ref, af_ref, of_ref), (qb_ref, kb_ref, vb_ref, ab_ref, ob_ref))
    tris, tri_bs, g_alls = [], [], []
    for d in range(2):
        z = _dot(refs[d][3][...].astype(BF16), wa_ref[d]) + ba_ref[d:d + 1, :]
        g_alls.append(_log_sigmoid(z) * (1.0 / GLA_TAU))
        tri = (ci >= ri) if d == 1 else (ci <= ri)
        tris.append(tri)
        tri_bs.append(jnp.where(tri, 1.0, 0.0).astype(BF16))

    def chunk(d, c):
        q_ref, k_ref, v_ref, _, o_ref = refs[d]
        rev = d == 1
        rows = slice(c * C, (c + 1) * C)
        g = g_alls[d][rows, :]
        g_hi = g.astype(BF16)
        g_lo = (g - g_hi.astype(F32)).astype(BF16)
        b = _dot(tri_bs[d], g_hi) + _dot(tri_bs[d], g_lo)
        b_tot = b[0:1, :] if rev else b[C - 1:C, :]
        q = q_ref[rows, :] * (GLA_DK ** -0.5)
        k = k_ref[rows, :]
        v = v_ref[rows, :].astype(BF16)
        q_d = (q * jnp.exp(b)).astype(BF16)
        k_d = (k * jnp.exp(-b)).astype(BF16)
        k_s = (k * jnp.exp(b_tot - b)).astype(BF16)
        att = lax.dot_general(q_d, k_d, NT_DIMS, preferred_element_type=F32)
        att = jnp.where(tris[d], att, 0.0).astype(BF16)
        s_t = st_ref[d]
        o = _dot(att, v) + lax.dot_general(q_d, s_t.astype(BF16), NT_DIMS, preferred_element_type=F32)
        o_ref[rows, :] = o
        st_ref[d] = s_t * jnp.exp(b_tot) + lax.dot_general(v, k_s, TN_DIMS, preferred_element_type=F32)

    n_chunks = lt // C
    for c in range(n_chunks):
        chunk(0, c)
        chunk(1, n_chunks - 1 - c)


def gla_scan(p, wa, ba, *, ctx_len, lt=SEQ_TILE):
    T = p.shape[0]
    n_tiles = T // lt
    n_ctx = ctx_len // lt
    tf = functools.partial(_time_tile, n_tiles=n_tiles, n_ctx_tiles=n_ctx, rev=False)
    tb = functools.partial(_time_tile, n_tiles=n_tiles, n_ctx_tiles=n_ctx, rev=True)

    def stream(tt):
        return [
            pl.BlockSpec((lt, GLA_DK), lambda h, t: (tt(t), OFF_Q // GLA_DK + h)),
            pl.BlockSpec((lt, GLA_DK), lambda h, t: (tt(t), OFF_K // GLA_DK + h)),
            pl.BlockSpec((lt, GLA_DV), lambda h, t: (tt(t), OFF_V // GLA_DV + h)),
            pl.BlockSpec((lt, LANE), lambda h, t: (tt(t), OFF_ALPHA // LANE)),
        ]

    out = jax.ShapeDtypeStruct((T, D_GLA_V), F32)
    return pl.pallas_call(
        functools.partial(_gla_kernel, lt=lt),
        grid=(GLA_HEADS, n_tiles),
        in_specs=stream(tf) + stream(tb) + [
            pl.BlockSpec((2, LANE, GLA_DK), lambda h, t: (0, 0, h)),
            pl.BlockSpec((2, GLA_DK), lambda h, t: (0, h)),
        ],
        out_specs=[pl.BlockSpec((lt, GLA_DV), lambda h, t: (tf(t), h)),
                   pl.BlockSpec((lt, GLA_DV), lambda h, t: (tb(t), h))],
        out_shape=[out, out],
        scratch_shapes=[pltpu.VMEM((2, GLA_DV, GLA_DK), F32)],
        compiler_params=_cparams(("arbitrary", "arbitrary")),
        name="gla_scan",
    )(p, p, p, p, p, p, p, p, wa, ba)


def _finish_kernel(u_ref, ys_f_ref, ys_b_ref, d_ref, wglu_ref, bglu_ref,
                   og_f_ref, og_b_ref, gate_ref, gn_ref,
                   x0_ref, conv_ref, s_ref, hb_ref, o_ref):
    u = u_ref[...]
    y = ys_f_ref[...] + ys_b_ref[...] + d_ref[...] * u
    zz = _gelu(y)
    s5 = zz * _sigmoid(_dot(zz.astype(BF16), wglu_ref[...]) + bglu_ref[...])
    o_ref[:, 0:D_S5] = s5.astype(BF16)

    gate = gate_ref[...]
    for h in range(GLA_HEADS):
        cols = slice(h * GLA_DV, (h + 1) * GLA_DV)
        o = og_f_ref[:, cols] + og_b_ref[:, cols]
        ms = jnp.mean(o * o, axis=-1, keepdims=True)
        on = o * lax.rsqrt(ms + EPS) * gn_ref[...]
        gt = gate[:, cols]
        o_ref[:, D_S5 + h * GLA_DV:D_S5 + (h + 1) * GLA_DV] = (on * (gt * _sigmoid(gt))).astype(BF16)

    s = s_ref[...]
    hy = x0_ref[...] * (conv_ref[...] + hb_ref[...] * s)
    o_ref[:, D_S5 + D_GLA_V:] = hy.astype(BF16)


def mixer_finish(p, ys_f, ys_b, s5_d, w_glu, b_glu, og_f, og_b, gn, x0, conv, s, hy_bias, *, tm):
    T = p.shape[0]
    row = lambda w, off=0: pl.BlockSpec((tm, w), lambda i: (i, off))
    const = lambda r, w: pl.BlockSpec((r, w), lambda i: (0, 0))
    return pl.pallas_call(
        _finish_kernel,
        grid=(T // tm,),
        in_specs=[
            row(D_S5, OFF_S5 // D_S5), row(D_S5), row(D_S5), const(1, D_S5), const(D_S5, D_S5), const(1, D_S5),
            row(D_GLA_V), row(D_GLA_V), row(D_GLA_V, OFF_GATE // D_GLA_V), const(1, GLA_DV),
            row(D_HY), row(D_HY), row(D_HY), const(1, D_HY),
        ],
        out_specs=pl.BlockSpec((tm, D_MODEL), lambda i: (i, 0)),
        out_shape=jax.ShapeDtypeStruct((T, D_MODEL), BF16),
        compiler_params=_cparams(("arbitrary",)),
        name="mixer_finish",
    )(p, ys_f, ys_b, s5_d.reshape(1, D_S5), w_glu, b_glu.reshape(1, D_S5),
      og_f, og_b, p, gn.reshape(1, GLA_DV), x0, conv, s, hy_bias.reshape(1, D_HY))


def _peer_kernel(x_ref, u_ref, vt_ref, s2_ref, e2_ref, s1_ref, e1_ref, tau_ref, o_ref,
                 act_ref, p_ref, *, n_i1, tm, cw):
    j = pl.program_id(1)

    @pl.when(j == 0)
    def _():
        o_ref[...] = jnp.zeros_like(o_ref)

    act_ref[...] = lax.dot_general(u_ref[...], x_ref[...], NT_DIMS, preferred_element_type=F32)
    K = PEER_KEYS
    sub = SUBLANE // n_i1
    base = (j % sub) * n_i1 if sub > 1 else 0
    for a in range(n_i1):
        r1 = pl.ds(base + a, 1)
        for cb in range(tm // cw):
            cols = slice(cb * cw, (cb + 1) * cw)
            w = jnp.zeros((K, cw), F32)
            for h in range(PEER_HEADS):
                tot = s1_ref[h, r1, cols] + s2_ref[h, :, cols]
                val = e1_ref[h, r1, cols] * e2_ref[h, :, cols]
                w = w + jnp.where(tot >= tau_ref[h, :, cols], val, 0.0)
            act = act_ref[a * K:(a + 1) * K, cols]
            p_ref[a * K:(a + 1) * K, cols] = (w * _gelu(act)).astype(BF16)
    o_ref[...] += _dot(vt_ref[...], p_ref[...])


def peer_dense(xn, u_tab, vt_tab, s1t, e1t, s2t, e2t, taut, *, tm, te, cw=256):
    T, D = xn.shape
    E = u_tab.shape[0]
    H, K = PEER_HEADS, PEER_KEYS
    n_i1 = te // K
    sub = SUBLANE // n_i1
    return pl.pallas_call(
        functools.partial(_peer_kernel, n_i1=n_i1, tm=tm, cw=cw),
        grid=(T // tm, E // te),
        in_specs=[
            pl.BlockSpec((tm, D), lambda i, j: (i, 0)),
            pl.BlockSpec((te, D), lambda i, j: (j, 0)),
            pl.BlockSpec((D, te), lambda i, j: (0, j)),
            pl.BlockSpec((H, K, tm), lambda i, j: (0, 0, i)),
            pl.BlockSpec((H, K, tm), lambda i, j: (0, 0, i)),
            pl.BlockSpec((H, SUBLANE, tm), lambda i, j: (0, j // sub, i)),
            pl.BlockSpec((H, SUBLANE, tm), lambda i, j: (0, j // sub, i)),
            pl.BlockSpec((H, 1, tm), lambda i, j: (0, 0, i)),
        ],
        out_specs=pl.BlockSpec((D, tm), lambda i, j: (0, i)),
        out_shape=jax.ShapeDtypeStruct((D, T), F32),
        scratch_shapes=[pltpu.VMEM((te, tm), F32), pltpu.VMEM((te, tm), BF16)],
        compiler_params=_cparams(("arbitrary", "arbitrary")),
        name="peer_dense",
    )(xn, u_tab, vt_tab, s2t, e2t, s1t, e1t, taut)


NEG_BIG = -3.0e38
N_CAND = PEER_TOPK + 8 * 7 + 8


def _peer_prep_kernel(q_ref, k_ref, s1_ref, e1_ref, s2_ref, e2_ref, tau_ref, top_ref, cand_ref):
    R = PEER_TOPK
    half_w = PEER_DQ // 2

    def nt(a, b):
        return lax.dot_general(a, b, NT_DIMS, preferred_element_type=F32)

    def scores(half):
        qh = q_ref[:, half * half_w:(half + 1) * half_w]
        q_hi = qh.astype(BF16)
        q_lo = (qh - q_hi.astype(F32)).astype(BF16)
        kk = k_ref[half]
        k_hi = kk.astype(BF16)
        k_lo = (kk - k_hi.astype(F32)).astype(BF16)
        return nt(k_hi, q_hi) + (nt(k_hi, q_lo) + nt(k_lo, q_hi))

    def sorted_top(s, slot):
        work = s
        for r in range(R):
            m = jnp.max(work, axis=0, keepdims=True)
            top_ref[slot, r:r + 1, :] = m
            work = jnp.where(work == m, NEG_BIG, work)

    s1 = scores(0)
    s2 = scores(1)
    sorted_top(s1, 0)
    sorted_top(s2, 1)
    a = top_ref[0]
    b = top_ref[1]
    cand_ref[0:R, :] = a[0:1, :] + b
    for i in range(1, 8):
        cand_ref[R + 8 * (i - 1):R + 8 * i, :] = a[i:i + 1, :] + b[0:8, :]
    cand_ref[R + 56:R + 64, :] = a[8:16, :] + b[0:1, :]
    work = cand_ref[...]
    m0 = jnp.max(work, axis=0, keepdims=True)
    m = m0
    z = jnp.ones_like(m0)
    for r in range(1, R):
        work = jnp.where(work == m, NEG_BIG, work)
        m = jnp.max(work, axis=0, keepdims=True)
        z = z + jnp.exp(m - m0)
    tau_ref[0] = m
    s1_ref[0] = s1
    s2_ref[0] = s2
    e1_ref[0] = jnp.exp(s1 - a[0:1, :]) * (1.0 / z)
    e2_ref[0] = jnp.exp(s2 - b[0:1, :])


def peer_prep(q, keys, *, tm):
    T = q.shape[0]
    H, K = PEER_HEADS, PEER_KEYS
    tab = jax.ShapeDtypeStruct((H, K, T), F32)
    tab_spec = pl.BlockSpec((1, K, tm), lambda i, h: (h, 0, i))
    return pl.pallas_call(
        _peer_prep_kernel,
        grid=(T // tm, H),
        in_specs=[
            pl.BlockSpec((tm, PEER_DQ), lambda i, h: (i, h)),
            pl.BlockSpec((None, 2, K, PEER_DQ // 2), lambda i, h: (h, 0, 0, 0)),
        ],
        out_specs=[tab_spec, tab_spec, tab_spec, tab_spec, pl.BlockSpec((1, 1, tm), lambda i, h: (h, 0, i))],
        out_shape=[tab, tab, tab, tab, jax.ShapeDtypeStruct((H, 1, T), F32)],
        scratch_shapes=[pltpu.VMEM((2, PEER_TOPK, tm), F32), pltpu.VMEM((N_CAND, tm), F32)],
        compiler_params=_cparams(("arbitrary", "arbitrary")),
        name="peer_prep",
    )(q, keys)


def _transpose_gated_res_kernel(h_ref, ft_ref, gate_ref, o_ref, *, ctx_len, tm):
    i = pl.program_id(0)
    row = i * tm + lax.broadcasted_iota(jnp.int32, (tm, 1), 0)
    gate = jnp.where(row < ctx_len, gate_ref[0:1, :], gate_ref[1:2, :])
    o_ref[...] = h_ref[...] + gate * ft_ref[...].T


def transpose_gated_res(h, ft, gates, *, ctx_len, tm):
    T, D = h.shape
    return pl.pallas_call(
        functools.partial(_transpose_gated_res_kernel, ctx_len=ctx_len, tm=tm),
        grid=(T // tm,),
        in_specs=[
            pl.BlockSpec((tm, D), lambda i: (i, 0)),
            pl.BlockSpec((D, tm), lambda i: (0, i)),
            pl.BlockSpec((2, D), lambda i: (0, 0)),
        ],
        out_specs=pl.BlockSpec((tm, D), lambda i: (i, 0)),
        out_shape=jax.ShapeDtypeStruct((T, D), F32),
        compiler_params=_cparams(("arbitrary",)),
        name="transpose_gated_res",
    )(h, ft, gates)


def _rmsnorm_kernel(x_ref, g_ref, o_ref):
    x = x_ref[...]
    ms = jnp.mean(x * x, axis=-1, keepdims=True)
    o_ref[...] = x * lax.rsqrt(ms + EPS) * g_ref[...]


def rmsnorm_rows(x, g, *, tm):
    T, D = x.shape
    return pl.pallas_call(
        _rmsnorm_kernel,
        grid=(T // tm,),
        in_specs=[pl.BlockSpec((tm, D), lambda i: (i, 0)), pl.BlockSpec((1, D), lambda i: (0, 0))],
        out_specs=pl.BlockSpec((tm, D), lambda i: (i, 0)),
        out_shape=jax.ShapeDtypeStruct((T, D), F32),
        compiler_params=_cparams(("arbitrary",)),
        name="rmsnorm_rows",
    )(x, g.reshape(1, D))


def grid_sincos(n_tokens, dim):
    rows = n_tokens // GRID_W
    row = jnp.repeat(jnp.arange(rows), GRID_W).astype(F32)
    col = jnp.tile(jnp.arange(GRID_W), rows).astype(F32)
    n_freq = dim // 4
    omega = 1.0 / (POS_BASE ** (jnp.arange(n_freq, dtype=F32) / n_freq))

    def enc(pp):
        a = pp[:, None] * omega[None, :]
        return jnp.concatenate([jnp.sin(a), jnp.cos(a)], axis=-1)

    return jnp.concatenate([enc(row), enc(col)], axis=-1)


def _split_bf16(a):
    hi = a.astype(BF16)
    return hi, (a - hi.astype(F32)).astype(BF16)


def _dot3s(a, b_hi, b_lo):
    a_hi, a_lo = _split_bf16(a)
    return _dot(a_hi, b_hi) + (_dot(a_hi, b_lo) + _dot(a_lo, b_hi))


def _dot3(a, b):
    return _dot3s(a, *_split_bf16(b))


def _adaln_kernel(c_ref, w_ref, b_ref, o_ref):
    c = c_ref[...]
    o_ref[...] = _dot3(c * _sigmoid(c), w_ref[...]) + b_ref[...]


def adaln(cond, w, b, *, tn=1536):
    R, D = cond.shape
    N = w.shape[1]
    return pl.pallas_call(
        _adaln_kernel,
        grid=(N // tn,),
        in_specs=[pl.BlockSpec((R, D), lambda j: (0, 0)),
                  pl.BlockSpec((D, tn), lambda j: (0, j)),
                  pl.BlockSpec((1, tn), lambda j: (0, j))],
        out_specs=pl.BlockSpec((R, tn), lambda j: (0, j)),
        out_shape=jax.ShapeDtypeStruct((R, N), F32),
        compiler_params=_cparams(("arbitrary",)),
        name="adaln",
    )(cond, w, b.reshape(1, N))


def _hyena_pre_kernel(x0_ref, x1_ref, v_ref, w_ref, b_ref, x0o_ref, s_ref, *, ctx_len):
    T = x0_ref.shape[0]
    row = lax.broadcasted_iota(jnp.int32, (T, 1), 0)
    first = jnp.logical_or(row == 0, row == ctx_len)
    last = jnp.logical_or(row == ctx_len - 1, row == T - 1)

    def conv(ref, part):
        x = ref[...]
        prev = jnp.where(first, 0.0, pltpu.roll(x, 1, 0))
        nxt = jnp.where(last, 0.0, pltpu.roll(x, T - 1, 0))
        w = lambda tap: w_ref[tap, part:part + 1, :]
        return w(0) * prev + w(1) * x + w(2) * nxt + b_ref[part:part + 1, :]

    x0o_ref[...] = conv(x0_ref, 0)
    s_ref[...] = conv(x1_ref, 1) * conv(v_ref, 2)


def hyena_pre(p, conv_w, conv_b, *, ctx_len):
    T = p.shape[0]
    nb = D_HY // LANE
    col = lambda part: pl.BlockSpec((T, LANE), lambda c: (0, OFF_HY // LANE + part * nb + c))
    out = jax.ShapeDtypeStruct((T, D_HY), F32)
    return pl.pallas_call(
        functools.partial(_hyena_pre_kernel, ctx_len=ctx_len),
        grid=(nb,),
        in_specs=[col(0), col(1), col(2),
                  pl.BlockSpec((HY_SHORT, 3, LANE), lambda c: (0, 0, c)),
                  pl.BlockSpec((3, LANE), lambda c: (0, c))],
        out_specs=[pl.BlockSpec((T, LANE), lambda c: (0, c))] * 2,
        out_shape=[out, out],
        compiler_params=_cparams(("arbitrary",)),
        name="hyena_pre",
    )(p, p, p, conv_w.reshape(HY_SHORT, 3, D_HY), conv_b.reshape(3, D_HY))


def _hyena_filter_kernel(band_ref, w1_ref, b1_ref, w2_ref, b2_ref, w3_ref, fr_ref, dec_ref, o_ref, *, n, tr):
    i = pl.program_id(0)
    ri = (i * tr + lax.broadcasted_iota(jnp.int32, (tr, 1), 0)).astype(F32)
    t = ri * (1.0 / (n - 1))
    w = ri * (2.0 * math.pi / n)
    lane = lax.broadcasted_iota(jnp.int32, (1, LANE), 1)
    arg = w * band_ref[...]
    z = jnp.where(lane == 0, t,
                  jnp.where(lane <= HY_BANDS, jnp.cos(arg),
                            jnp.where(lane <= 2 * HY_BANDS, -jnp.sin(arg), 0.0)))
    h = jnp.sin(fr_ref[0:1, :] * (_dot3(z, w1_ref[...]) + b1_ref[...]))
    h = jnp.sin(fr_ref[1:2, :] * (_dot3(h, w2_ref[...]) + b2_ref[...]))
    h = _dot3(h, w3_ref[...]) * jnp.exp(-t * dec_ref[...])
    col = lax.broadcasted_iota(jnp.int32, (1, 2 * D_HY), 1)
    o_ref[...] = jnp.where(jnp.logical_and(ri == 0.0, col >= D_HY), 0.0, h)


def hyena_filter(n, w1, b1, w2, b2, w3, freq, decay, *, tr=256):
    bands = jnp.linspace(1e-4, HY_BANDS - 1, HY_BANDS, dtype=F32)
    band_row = jnp.zeros((1, LANE), F32).at[0, 1:1 + HY_BANDS].set(bands).at[0, 1 + HY_BANDS:1 + 2 * HY_BANDS].set(bands)
    w1p = jnp.zeros((LANE, HY_HIDDEN), F32).at[:HY_FEAT].set(w1)
    const = lambda a: pl.BlockSpec(a.shape, lambda i: (0,) * a.ndim)
    args = (band_row, w1p, b1.reshape(1, -1), w2, b2.reshape(1, -1), w3, freq,
            jnp.abs(decay).reshape(1, 2 * D_HY))
    return pl.pallas_call(
        functools.partial(_hyena_filter_kernel, n=n, tr=tr),
        grid=(n // tr,),
        in_specs=[const(a) for a in args],
        out_specs=pl.BlockSpec((tr, 2 * D_HY), lambda i: (i, 0)),
        out_shape=jax.ShapeDtypeStruct((n, 2 * D_HY), F32),
        compiler_params=_cparams(("arbitrary",)),
        name="hyena_filter",
    )(*args)


FFT_R = 128
FFT_N = FFT_R * FFT_R
FFT_CH = 32


def _dft_tables():
    r = np.arange(FFT_R)
    ang = 2.0 * np.pi * np.outer(r, r) / FFT_R
    c, s = np.cos(ang), np.sin(ang)
    angt = 2.0 * np.pi * np.outer(r, r) / FFT_N
    ct, st = np.cos(angt), np.sin(angt)

    def split(m):
        m = jnp.asarray(m, F32)
        hi = m.astype(BF16)
        return jnp.stack([hi, (m - hi.astype(F32)).astype(BF16)])

    fwd_a = split(np.concatenate([c, -s], axis=1))
    fwd_b = split(np.block([[c, -s], [s, c]]))
    inv_b = split(np.block([[c, s], [-s, c]]) / FFT_N)
    inv_a = split(np.concatenate([c, -s], axis=0))
    tw = (jnp.asarray(ct, F32), jnp.asarray(-st, F32))
    twc = (jnp.asarray(ct, F32), jnp.asarray(st, F32))
    return fwd_a, fwd_b, inv_b, inv_a, tw, twc


def _twiddle_transpose(y, twr, twi, dst_ref, nc):
    y = y.reshape(nc, FFT_R, 2 * FFT_R)
    yr, yi = y[:, :, :FFT_R], y[:, :, FFT_R:]
    zr = yr * twr - yi * twi
    zi = yr * twi + yi * twr
    for c in range(nc):
        dst_ref[c * FFT_R:(c + 1) * FFT_R, 0:FFT_R] = zr[c].T
        dst_ref[c * FFT_R:(c + 1) * FFT_R, FFT_R:2 * FFT_R] = zi[c].T


def _hyena_fwd_kernel(x_ref, fa_ref, twr_ref, twi_ref, fb_ref, o_ref, l_ref, yt_ref, *, nc):
    j = pl.program_id(1)
    half = FFT_R // 2

    @pl.when(j == 0)
    def _():
        zeros = jnp.zeros((half, LANE), F32)

        def body(b, carry):
            t = jnp.concatenate([x_ref[pl.ds(b, half, stride=FFT_R), :], zeros], axis=0)
            l_ref[pl.ds(b, LANE, stride=FFT_R), :] = t.T
            return carry

        lax.fori_loop(0, FFT_R, body, 0, unroll=8)

    rows = nc * FFT_R
    lc = l_ref[pl.ds(pl.multiple_of(j * rows, rows), rows), :]
    y = _dot3s(lc, fa_ref[0], fa_ref[1])
    _twiddle_transpose(y, twr_ref[...], twi_ref[...], yt_ref, nc)
    o_ref[...] = _dot3s(yt_ref[...], fb_ref[0], fb_ref[1])


def hyena_fwd_dft(x, tables):
    n, C = x.shape
    assert n * 2 == FFT_N and C % LANE == 0
    fwd_a, fwd_b, _, _, (twr, twi), _ = tables
    nc = FFT_CH
    steps = LANE // nc
    const = lambda a: pl.BlockSpec(a.shape, lambda cb, j: (0,) * a.ndim)
    return pl.pallas_call(
        functools.partial(_hyena_fwd_kernel, nc=nc),
        grid=(C // LANE, steps),
        in_specs=[pl.BlockSpec((n, LANE), lambda cb, j: (0, cb)), const(fwd_a), const(twr), const(twi), const(fwd_b)],
        out_specs=pl.BlockSpec((nc * FFT_R, 2 * FFT_R), lambda cb, j: (cb * steps + j, 0)),
        out_shape=jax.ShapeDtypeStruct((C * FFT_R, 2 * FFT_R), F32),
        scratch_shapes=[pltpu.VMEM((LANE * FFT_R, LANE), F32), pltpu.VMEM((nc * FFT_R, 2 * FFT_R), F32)],
        compiler_params=_cparams(("arbitrary", "arbitrary")),
        name="hyena_fwd_dft",
    )(x, fwd_a, twr, twi, fwd_b)


def _hyena_inv_kernel(s_ref, kf_ref, kb_ref, gb_ref, twr_ref, twi_ref, ga_ref, o_ref, lr_ref, at_ref, *, nc, steps):
    j = pl.program_id(1)
    R = FFT_R
    s, kf, kb = s_ref[...], kf_ref[...], kb_ref[...]
    sr, si = s[:, :R], s[:, R:]
    kr = kf[:, :R] + kb[:, :R]
    ki = kf[:, R:] - kb[:, R:]
    p = jnp.concatenate([sr * kr - si * ki, sr * ki + si * kr], axis=1)
    a = _dot3s(p, gb_ref[0], gb_ref[1])
    _twiddle_transpose(a, twr_ref[...], twi_ref[...], at_ref, nc)
    rows = nc * R
    lr_ref[pl.ds(pl.multiple_of(j * rows, rows), rows), :] = _dot3s(at_ref[...], ga_ref[0], ga_ref[1])

    @pl.when(j == steps - 1)
    def _():
        def body(b, carry):
            t = lr_ref[pl.ds(b, LANE, stride=R), :]
            o_ref[pl.ds(b, R // 2, stride=R), :] = t.T[0:R // 2, :]
            return carry

        lax.fori_loop(0, R, body, 0, unroll=8)


def hyena_inv_dft(s_f, k_f, tables):
    C = s_f.shape[0] // FFT_R
    _, _, inv_b, inv_a, _, (twr, twi) = tables
    nc = FFT_CH
    steps = LANE // nc
    kb_off = C // nc
    const = lambda a: pl.BlockSpec(a.shape, lambda cb, j: (0,) * a.ndim)
    blk = lambda off: pl.BlockSpec((nc * FFT_R, 2 * FFT_R), lambda cb, j: (off + cb * steps + j, 0))
    return pl.pallas_call(
        functools.partial(_hyena_inv_kernel, nc=nc, steps=steps),
        grid=(C // LANE, steps),
        in_specs=[blk(0), blk(0), blk(kb_off), const(inv_b), const(twr), const(twi), const(inv_a)],
        out_specs=pl.BlockSpec((FFT_N // 2, LANE), lambda cb, j: (0, cb)),
        out_shape=jax.ShapeDtypeStruct((FFT_N // 2, C), F32),
        scratch_shapes=[pltpu.VMEM((LANE * FFT_R, LANE), F32), pltpu.VMEM((nc * FFT_R, 2 * FFT_R), F32)],
        compiler_params=_cparams(("arbitrary", "arbitrary")),
        name="hyena_inv_dft",
    )(s_f, k_f, k_f, inv_b, twr, twi, inv_a)


def _hyena_ctx_kernel(s_ref, k_ref, fw_ref, iv_ref, o_ref):
    n = s_ref.shape[0]
    N = 2 * n
    xs = _dot3(fw_ref[...], s_ref[...])
    xk = _dot3(fw_ref[...], k_ref[...])
    sr, si = xs[:N], xs[N:]
    kr = xk[:N, :D_HY] + xk[:N, D_HY:]
    ki = xk[N:, :D_HY] - xk[N:, D_HY:]
    p = jnp.concatenate([sr * kr - si * ki, sr * ki + si * kr], axis=0)
    o_ref[...] = _dot3(iv_ref[...], p)


def hyena_ctx_conv(s_c, k_c):
    n = s_c.shape[0]
    N = 2 * n
    ang = 2.0 * np.pi * np.outer(np.arange(N), np.arange(N)) / N
    c, s = np.cos(ang), np.sin(ang)
    fw = jnp.asarray(np.concatenate([c[:, :n], -s[:, :n]], axis=0), F32)
    iv = jnp.asarray(np.concatenate([c[:n, :], -s[:n, :]], axis=1) / N, F32)
    full = lambda a: pl.BlockSpec(a.shape, lambda i: (0,) * a.ndim)
    return pl.pallas_call(
        _hyena_ctx_kernel,
        grid=(1,),
        in_specs=[full(s_c), full(k_c), full(fw), full(iv)],
        out_specs=pl.BlockSpec((n, D_HY), lambda i: (0, 0)),
        out_shape=jax.ShapeDtypeStruct((n, D_HY), F32),
        compiler_params=_cparams(("arbitrary",)),
        name="hyena_ctx_conv",
    )(s_c, k_c, fw, iv)


def hyena_mixer_parts(p, conv_w, conv_b, filt, tables, *, ctx_len):
    T = p.shape[0]
    x0, s = hyena_pre(p, conv_w, conv_b, ctx_len=ctx_len)
    k_lat = hyena_filter(T - ctx_len, *filt)
    k_ctx = hyena_filter(ctx_len, *filt)
    conv_l = hyena_inv_dft(hyena_fwd_dft(s[ctx_len:], tables), hyena_fwd_dft(k_lat, tables), tables)
    conv_c = hyena_ctx_conv(s[:ctx_len], k_ctx)
    return x0, jnp.concatenate([conv_c, conv_l], axis=0), s


def _pack_w_in(w_in_l):
    widths = (D_S5, D_GLA_K, D_GLA_K, D_GLA_V, D_GLA_V, 2 * GLA_RANK, 3 * D_HY)
    offs = [0]
    for wd in widths:
        offs.append(offs[-1] + wd)
    s5, q, k, v, gate, alpha, hy = (w_in_l[:, offs[i]:offs[i + 1]] for i in range(7))
    pad = jnp.zeros((w_in_l.shape[0], IN_PACKED - OFF_ALPHA - 2 * GLA_RANK), w_in_l.dtype)
    return jnp.concatenate([v, gate, s5, q, k, hy, alpha, pad], axis=1).astype(BF16)


def kernel(x, c, ctx, c_ctx, w_ada, b_ada, g_norm1, g_norm2, w_in, s5_a_re, s5_a_im, s5_log_step, s5_b_re, s5_b_im, s5_c_re, s5_c_im, s5_d, s5_w_glu, s5_b_glu, gla_w_alpha, gla_b_alpha, gla_g_norm, hy_conv_w, hy_conv_b, hy_f_w1, hy_f_b1, hy_f_w2, hy_f_b2, hy_f_w3, hy_f_freq, hy_decay, hy_bias, w_out, peer_w_q, peer_keys, peer_u, peer_v, g_final):
    L = x.shape[1]
    Lc = ctx.shape[1]
    T = L + Lc
    TM = 768
    h = jnp.concatenate([ctx[0], x[0] + grid_sincos(L, D_MODEL)], axis=0)
    cond = jnp.zeros((SUBLANE, D_MODEL), F32).at[0].set(c_ctx).at[1].set(c[0])
    tables = _dft_tables()

    for l in range(DEPTH):
        m = adaln(cond, w_ada[l], b_ada[l])[0:2]
        sh1, sc1, gt1, sh2, sc2, gt2 = jnp.split(m, 6, axis=-1)
        mod1 = jnp.stack([sh1[0], sc1[0], sh1[1], sc1[1]], axis=0)
        mod2 = jnp.stack([sh2[0], sc2[0], sh2[1], sc2[1]], axis=0)

        p = norm_mod_matmul(h, g_norm1[l], mod1, _pack_w_in(w_in[l]), ctx_len=Lc, tm=TM, tn=768)

        bm, cm, pw = s5_prepare(s5_a_re[l], s5_a_im[l], s5_log_step[l], s5_b_re[l], s5_b_im[l],
                                s5_c_re[l], s5_c_im[l])
        ys_f, ys_b = s5_scan(p, bm, cm, pw, ctx_len=Lc)

        wa = jnp.zeros((2, LANE, D_GLA_K), F32)
        wa = wa.at[0, 0:GLA_RANK].set(gla_w_alpha[l, 0]).at[1, GLA_RANK:2 * GLA_RANK].set(gla_w_alpha[l, 1])
        wa = wa.astype(BF16)
        og_f, og_b = gla_scan(p, wa, gla_b_alpha[l], ctx_len=Lc)

        filt = (hy_f_w1[l], hy_f_b1[l], hy_f_w2[l], hy_f_b2[l], hy_f_w3[l], hy_f_freq[l], hy_decay[l])
        x0, conv, s = hyena_mixer_parts(p, hy_conv_w[l], hy_conv_b[l], filt, tables, ctx_len=Lc)

        mix = mixer_finish(p, ys_f, ys_b, s5_d[l], s5_w_glu[l].astype(BF16), s5_b_glu[l],
                           og_f, og_b, gla_g_norm[l], x0, conv, s, hy_bias[l], tm=256)
        h = matmul_gated_res(mix, w_out[l].astype(BF16), h, gt1, ctx_len=Lc, tm=TM, tn=1024)

        q, xn = norm_mod_matmul(h, g_norm2[l], mod2, peer_w_q[l].astype(BF16),
                                ctx_len=Lc, tm=TM, tn=1024, emit_xn=True)
        s1t, e1t, s2t, e2t, taut = peer_prep(q, peer_keys[l], tm=TM)
        ft = peer_dense(xn, peer_u[l].astype(BF16), peer_v[l].T.astype(BF16),
                        s1t, e1t, s2t, e2t, taut, tm=TM, te=512)
        h = transpose_gated_res(h, ft, gt2, ctx_len=Lc, tm=256)

    out = rmsnorm_rows(h[Lc:], g_final, tm=512)
    return out[None]
```

```python
import functools
import math

import jax
import jax.numpy as jnp
import numpy as np
from jax import lax
from jax.experimental import pallas as pl
from jax.experimental.pallas import tpu as pltpu

F32 = jnp.float32
BF16 = jnp.bfloat16

D_MODEL = 2048
DEPTH = 4
GRID_W = 64
EPS = 1e-6
POS_BASE = 10000.0

D_S5 = D_MODEL // 4
S5_GROUP = 16
S5_GROUPS = D_S5 // S5_GROUP
S5_STATE = 64
S5_BLK_GROUPS = 8
S5_BLK_CH = S5_BLK_GROUPS * S5_GROUP
S5_BLK_ST = S5_BLK_GROUPS * S5_STATE
S5_NBLK = S5_GROUPS // S5_BLK_GROUPS

GLA_HEADS = 4
D_GLA_K = D_MODEL // 4
D_GLA_V = D_MODEL // 2
GLA_DK = D_GLA_K // GLA_HEADS
GLA_DV = D_GLA_V // GLA_HEADS
GLA_RANK = 16
GLA_TAU = 16.0
GLA_CHUNK = 64

D_HY = D_MODEL // 4
HY_SHORT = 3
HY_BANDS = 16
HY_FEAT = 1 + 2 * HY_BANDS
HY_HIDDEN = 64
HY_TARGET = 1e-2
HY_MIN_DECAY = -math.log(HY_TARGET) / 1.5
HY_MAX_DECAY = -math.log(HY_TARGET) / 0.3

PEER_HEADS = 8
PEER_KEYS = 128
PEER_EXPERTS = PEER_KEYS * PEER_KEYS
PEER_DQ = 256
PEER_TOPK = 16

OFF_V = 0
OFF_GATE = OFF_V + D_GLA_V
OFF_S5 = OFF_GATE + D_GLA_V
OFF_Q = OFF_S5 + D_S5
OFF_K = OFF_Q + D_GLA_K
OFF_HY = OFF_K + D_GLA_K
OFF_ALPHA = OFF_HY + 3 * D_HY
LANE = 128
SUBLANE = 8
IN_PACKED = OFF_ALPHA + 2 * LANE

SEQ_TILE = 256
VMEM_LIMIT = 56 * 1024 * 1024

NT_DIMS = (((1,), (1,)), ((), ()))
TN_DIMS = (((0,), (0,)), ((), ()))


def _cparams(sem):
    return pltpu.CompilerParams(dimension_semantics=sem, vmem_limit_bytes=VMEM_LIMIT)


def _dot(a, b):
    return jnp.dot(a, b, preferred_element_type=F32)


def _gelu(x):
    k1 = -2.0 * math.sqrt(2.0 / math.pi) * math.log2(math.e)
    k2 = 0.044715 * k1
    return x / (1.0 + jnp.exp2(x * (k1 + k2 * (x * x))))


def _sigmoid(x):
    return 1.0 / (1.0 + jnp.exp(-x))


def _norm_mod_matmul_kernel(x_ref, g_ref, mod_ref, w_ref, o_ref, *rest, ctx_len, tm, emit_xn):
    if emit_xn:
        xo_ref, xn_ref = rest
    else:
        (xn_ref,) = rest
    i = pl.program_id(0)
    j = pl.program_id(1)

    @pl.when(j == 0)
    def _():
        x = x_ref[...]
        ms = jnp.mean(x * x, axis=-1, keepdims=True)
        y = x * lax.rsqrt(ms + EPS) * g_ref[...]
        row = i * tm + lax.broadcasted_iota(jnp.int32, (tm, 1), 0)
        is_ctx = row < ctx_len
        shift = jnp.where(is_ctx, mod_ref[0:1, :], mod_ref[2:3, :])
        scale = jnp.where(is_ctx, mod_ref[1:2, :], mod_ref[3:4, :])
        xn = (y * (1.0 + scale) + shift).astype(BF16)
        xn_ref[...] = xn
        if emit_xn:
            xo_ref[...] = xn

    o_ref[...] = _dot(xn_ref[...], w_ref[...])


def norm_mod_matmul(x, g, mod, w, *, ctx_len, tm, tn, emit_xn=False):
    T, D = x.shape
    N = w.shape[1]
    out_shape = [jax.ShapeDtypeStruct((T, N), F32)]
    out_specs = [pl.BlockSpec((tm, tn), lambda i, j: (i, j))]
    if emit_xn:
        out_shape.append(jax.ShapeDtypeStruct((T, D), BF16))
        out_specs.append(pl.BlockSpec((tm, D), lambda i, j: (i, 0)))
    res = pl.pallas_call(
        functools.partial(_norm_mod_matmul_kernel, ctx_len=ctx_len, tm=tm, emit_xn=emit_xn),
        grid=(T // tm, N // tn),
        in_specs=[
            pl.BlockSpec((tm, D), lambda i, j: (i, 0)),
            pl.BlockSpec((1, D), lambda i, j: (0, 0)),
            pl.BlockSpec((4, D), lambda i, j: (0, 0)),
            pl.BlockSpec((D, tn), lambda i, j: (0, j)),
        ],
        out_specs=out_specs,
        out_shape=out_shape,
        scratch_shapes=[pltpu.VMEM((tm, D), BF16)],
        compiler_params=_cparams(("arbitrary", "arbitrary")),
        name="norm_mod_matmul",
    )(x, g.reshape(1, D), mod, w)
    return res if emit_xn else res[0]


def _matmul_gated_res_kernel(a_ref, w_ref, r_ref, gate_ref, o_ref, *, ctx_len, tm):
    i = pl.program_id(0)
    row = i * tm + lax.broadcasted_iota(jnp.int32, (tm, 1), 0)
    gate = jnp.where(row < ctx_len, gate_ref[0:1, :], gate_ref[1:2, :])
    o_ref[...] = r_ref[...] + gate * _dot(a_ref[...], w_ref[...])


def matmul_gated_res(a, w, res, gates, *, ctx_len, tm, tn):
    T, K = a.shape
    N = w.shape[1]
    return pl.pallas_call(
        functools.partial(_matmul_gated_res_kernel, ctx_len=ctx_len, tm=tm),
        grid=(T // tm, N // tn),
        in_specs=[
            pl.BlockSpec((tm, K), lambda i, j: (i, 0)),
            pl.BlockSpec((K, tn), lambda i, j: (0, j)),
            pl.BlockSpec((tm, tn), lambda i, j: (i, j)),
            pl.BlockSpec((2, tn), lambda i, j: (0, j)),
        ],
        out_specs=pl.BlockSpec((tm, tn), lambda i, j: (i, j)),
        out_shape=jax.ShapeDtypeStruct((T, N), F32),
        compiler_params=_cparams(("arbitrary", "arbitrary")),
        name="matmul_gated_res",
    )(a, w, res, gates)


def _time_tile(t, n_tiles, n_ctx_tiles, rev):
    if not rev:
        return t
    return jnp.where(t < n_ctx_tiles, n_ctx_tiles - 1 - t, n_tiles - 1 - (t - n_ctx_tiles))


def _s5_kernel(uf_ref, ub_ref, bm_ref, cm_ref, pw_ref, yf_ref, yb_ref, h_ref, c_ref, *, lt):
    t = pl.program_id(1)
    ns = S5_BLK_ST
    u_refs = (uf_ref, ub_ref)
    y_refs = (yf_ref, yb_ref)

    @pl.when(t == 0)
    def _():
        c_ref[...] = jnp.zeros_like(c_ref)

    for d in range(2):
        bu = _dot(u_refs[d][...].astype(BF16), bm_ref[d])
        h_ref[d, 0] = bu[:, :ns]
        h_ref[d, 1] = bu[:, ns:]

    n_grp = lt // 8

    def group_update(d, g, c_re, c_im):
        rev = d == 1
        gi = (n_grp - 1 - g) if rev else g
        rows = pl.ds(pl.multiple_of(gi * 8, 8), 8)
        a_re = h_ref[d, 0, rows, :]
        a_im = h_ref[d, 1, rows, :]
        for k, s in enumerate((1, 2, 4)):
            l_re = pw_ref[d, 8 + 8 * k:16 + 8 * k, :ns]
            l_im = pw_ref[d, 8 + 8 * k:16 + 8 * k, ns:]
            shift = (8 - s) if rev else s
            s_re = pltpu.roll(a_re, shift, 0)
            s_im = pltpu.roll(a_im, shift, 0)
            a_re = a_re + (l_re * s_re - l_im * s_im)
            a_im = a_im + (l_re * s_im + l_im * s_re)
        p_re = pw_ref[d, 0:8, :ns]
        p_im = pw_ref[d, 0:8, ns:]
        a_re = a_re + (p_re * c_re - p_im * c_im)
        a_im = a_im + (p_re * c_im + p_im * c_re)
        h_ref[d, 0, rows, :] = a_re
        h_ref[d, 1, rows, :] = a_im
        edge = slice(0, 1) if rev else slice(7, 8)
        return a_re[edge, :], a_im[edge, :]

    def body(g, carry):
        f_re, f_im, b_re, b_im = carry
        f_re, f_im = group_update(0, g, f_re, f_im)
        b_re, b_im = group_update(1, g, b_re, b_im)
        return f_re, f_im, b_re, b_im

    carry = lax.fori_loop(0, n_grp, body, (c_ref[0, 0], c_ref[0, 1], c_ref[1, 0], c_ref[1, 1]), unroll=2)
    c_ref[0, 0], c_ref[0, 1], c_ref[1, 0], c_ref[1, 1] = carry

    for d in range(2):
        y_refs[d][...] = (_dot(h_ref[d, 0].astype(BF16), cm_ref[d, :ns, :])
                          + _dot(h_ref[d, 1].astype(BF16), cm_ref[d, ns:, :]))


def s5_scan(p, bm, cm, pw, *, ctx_len, lt=SEQ_TILE):
    T = p.shape[0]
    n_tiles = T // lt
    n_ctx = ctx_len // lt
    tf = functools.partial(_time_tile, n_tiles=n_tiles, n_ctx_tiles=n_ctx, rev=False)
    tb = functools.partial(_time_tile, n_tiles=n_tiles, n_ctx_tiles=n_ctx, rev=True)
    ucol = OFF_S5 // S5_BLK_CH
    out = jax.ShapeDtypeStruct((T, D_S5), F32)
    return pl.pallas_call(
        functools.partial(_s5_kernel, lt=lt),
        grid=(S5_NBLK, n_tiles),
        in_specs=[
            pl.BlockSpec((lt, S5_BLK_CH), lambda b, t: (tf(t), ucol + b)),
            pl.BlockSpec((lt, S5_BLK_CH), lambda b, t: (tb(t), ucol + b)),
            pl.BlockSpec((2, None, S5_BLK_CH, 2 * S5_BLK_ST), lambda b, t: (0, b, 0, 0)),
            pl.BlockSpec((2, None, 2 * S5_BLK_ST, S5_BLK_CH), lambda b, t: (0, b, 0, 0)),
            pl.BlockSpec((2, None, 32, 2 * S5_BLK_ST), lambda b, t: (0, b, 0, 0)),
        ],
        out_specs=[pl.BlockSpec((lt, S5_BLK_CH), lambda b, t: (tf(t), b)),
                   pl.BlockSpec((lt, S5_BLK_CH), lambda b, t: (tb(t), b))],
        out_shape=[out, out],
        scratch_shapes=[pltpu.VMEM((2, 2, lt, S5_BLK_ST), F32), pltpu.VMEM((2, 2, 1, S5_BLK_ST), F32)],
        compiler_params=_cparams(("arbitrary", "arbitrary")),
        name="s5_scan",
    )(p, p, bm, cm, pw)


def _cmul(a, b):
    return a[0] * b[0] - a[1] * b[1], a[0] * b[1] + a[1] * b[0]


def s5_prepare(a_re, a_im, log_step, b_re, b_im, c_re, c_im):
    G, P, Cg = S5_GROUPS, S5_STATE, S5_GROUP
    dt = jnp.exp(log_step)[..., None]
    er = jnp.exp(a_re * dt)
    lam1 = (er * jnp.cos(a_im * dt), er * jnp.sin(a_im * dt))
    den = a_re * a_re + a_im * a_im
    xr, xi = lam1[0] - 1.0, lam1[1]
    coef = ((xr * a_re + xi * a_im) / den, (xi * a_re - xr * a_im) / den)
    bb_re = coef[0][..., None] * b_re - coef[1][..., None] * b_im
    bb_im = coef[0][..., None] * b_im + coef[1][..., None] * b_re
    pows = [lam1]
    for _ in range(7):
        pows.append(_cmul(pows[-1], lam1))

    eye = jnp.eye(S5_BLK_GROUPS, dtype=F32)

    def blockdiag_in(m):
        m = m.reshape(2, S5_NBLK, S5_BLK_GROUPS, P, Cg)
        return jnp.einsum('dbgpc,gh->dbgchp', m, eye).reshape(2, S5_NBLK, S5_BLK_CH, S5_BLK_ST)

    def blockdiag_out(m):
        m = m.reshape(2, S5_NBLK, S5_BLK_GROUPS, Cg, P)
        return jnp.einsum('dbgcp,gh->dbgphc', m, eye).reshape(2, S5_NBLK, S5_BLK_ST, S5_BLK_CH)

    bm = jnp.concatenate([blockdiag_in(bb_re), blockdiag_in(bb_im)], axis=-1).astype(BF16)
    cm = jnp.concatenate([blockdiag_out(c_re), blockdiag_out(-c_im)], axis=-2).astype(BF16)

    def lay(v):
        return v.reshape(2, S5_NBLK, S5_BLK_ST)

    def table(rev):
        power = lambda k: jnp.concatenate([lay(pows[k][0]), lay(pows[k][1])], axis=-1)
        rows = [power(7 - r if rev else r) for r in range(8)]
        zero = jnp.zeros_like(rows[0])
        for s in (1, 2, 4):
            for r in range(8):
                inside = (r < 8 - s) if rev else (r >= s)
                rows.append(power(s - 1) if inside else zero)
        return jnp.stack(rows, axis=2)

    return bm, cm, jnp.stack([table(False)[0], table(True)[1]])


def _log_sigmoid(z):
    return jnp.minimum(z, 0.0) - jnp.log(1.0 + jnp.exp(-jnp.abs(z)))


def _gla_kernel(qf_ref, kf_ref, vf_ref, af_ref, qb_ref, kb_ref, vb_ref, ab_ref, wa_ref, ba_ref,
                of_ref, ob_ref, st_ref, *, lt):
    t = pl.program_id(1)
    C = GLA_CHUNK

    @pl.when(t == 0)
    def _():
        st_ref[...] = jnp.zeros_like(st_ref)

    ri = lax.broadcasted_iota(jnp.int32, (C, C), 0)
    ci = lax.broadcasted_iota(jnp.int32, (C, C), 1)
    refs = ((qf_ref, kf_ref, vf_ref, af_ref, of_ref), (qb_ref, kb_ref, vb_ref, ab_ref, ob_ref))
    tris, tri_bs, g_alls = [], [], []
    for d in range(2):
        z = _dot(refs[d][3][...].astype(BF16), wa_ref[d]) + ba_ref[d:d + 1, :]
        g_alls.append(_log_sigmoid(z) * (1.0 / GLA_TAU))
        tri = (ci >= ri) if d == 1 else (ci <= ri)
        tris.append(tri)
        tri_bs.append(jnp.where(tri, 1.0, 0.0).astype(BF16))

    def chunk(d, c):
        q_ref, k_ref, v_ref, _, o_ref = refs[d]
        rev = d == 1
        rows = slice(c * C, (c + 1) * C)
        g = g_alls[d][rows, :]
        g_hi = g.astype(BF16)
        g_lo = (g - g_hi.astype(F32)).astype(BF16)
        b = _dot(tri_bs[d], g_hi) + _dot(tri_bs[d], g_lo)
        b_tot = b[0:1, :] if rev else b[C - 1:C, :]
        q = q_ref[rows, :] * (GLA_DK ** -0.5)
        k = k_ref[rows, :]
        v = v_ref[rows, :].astype(BF16)
        q_d = (q * jnp.exp(b)).astype(BF16)
        k_d = (k * jnp.exp(-b)).astype(BF16)
        k_s = (k * jnp.exp(b_tot - b)).astype(BF16)
        att = lax.dot_general(q_d, k_d, NT_DIMS, preferred_element_type=F32)
        att = jnp.where(tris[d], att, 0.0).astype(BF16)
        s_t = st_ref[d]
        o = _dot(att, v) + lax.dot_general(q_d, s_t.astype(BF16), NT_DIMS, preferred_element_type=F32)
        o_ref[rows, :] = o
        st_ref[d] = s_t * jnp.exp(b_tot) + lax.dot_general(v, k_s, TN_DIMS, preferred_element_type=F32)

    n_chunks = lt // C
    for c in range(n_chunks):
        chunk(0, c)
        chunk(1, n_chunks - 1 - c)


def gla_scan(p, wa, ba, *, ctx_len, lt=SEQ_TILE):
    T = p.shape[0]
    n_tiles = T // lt
    n_ctx = ctx_len // lt
    tf = functools.partial(_time_tile, n_tiles=n_tiles, n_ctx_tiles=n_ctx, rev=False)
    tb = functools.partial(_time_tile, n_tiles=n_tiles, n_ctx_tiles=n_ctx, rev=True)

    def stream(tt):
        return [
            pl.BlockSpec((lt, GLA_DK), lambda h, t: (tt(t), OFF_Q // GLA_DK + h)),
            pl.BlockSpec((lt, GLA_DK), lambda h, t: (tt(t), OFF_K // GLA_DK + h)),
            pl.BlockSpec((lt, GLA_DV), lambda h, t: (tt(t), OFF_V // GLA_DV + h)),
            pl.BlockSpec((lt, LANE), lambda h, t: (tt(t), OFF_ALPHA // LANE)),
        ]

    out = jax.ShapeDtypeStruct((T, D_GLA_V), F32)
    return pl.pallas_call(
        functools.partial(_gla_kernel, lt=lt),
        grid=(GLA_HEADS, n_tiles),
        in_specs=stream(tf) + stream(tb) + [
            pl.BlockSpec((2, LANE, GLA_DK), lambda h, t: (0, 0, h)),
            pl.BlockSpec((2, GLA_DK), lambda h, t: (0, h)),
        ],
        out_specs=[pl.BlockSpec((lt, GLA_DV), lambda h, t: (tf(t), h)),
                   pl.BlockSpec((lt, GLA_DV), lambda h, t: (tb(t), h))],
        out_shape=[out, out],
        scratch_shapes=[pltpu.VMEM((2, GLA_DV, GLA_DK), F32)],
        compiler_params=_cparams(("arbitrary", "arbitrary")),
        name="gla_scan",
    )(p, p, p, p, p, p, p, p, wa, ba)


def _finish_kernel(u_ref, ys_f_ref, ys_b_ref, d_ref, wglu_ref, bglu_ref,
                   og_f_ref, og_b_ref, gate_ref, gn_ref,
                   x0_ref, conv_ref, s_ref, hb_ref, o_ref):
    u = u_ref[...]
    y = ys_f_ref[...] + ys_b_ref[...] + d_ref[...] * u
    zz = _gelu(y)
    s5 = zz * _sigmoid(_dot(zz.astype(BF16), wglu_ref[...]) + bglu_ref[...])
    o_ref[:, 0:D_S5] = s5.astype(BF16)

    gate = gate_ref[...]
    for h in range(GLA_HEADS):
        cols = slice(h * GLA_DV, (h + 1) * GLA_DV)
        o = og_f_ref[:, cols] + og_b_ref[:, cols]
        ms = jnp.mean(o * o, axis=-1, keepdims=True)
        on = o * lax.rsqrt(ms + EPS) * gn_ref[...]
        gt = gate[:, cols]
        o_ref[:, D_S5 + h * GLA_DV:D_S5 + (h + 1) * GLA_DV] = (on * (gt * _sigmoid(gt))).astype(BF16)

    s = s_ref[...]
    hy = x0_ref[...] * (conv_ref[...] + hb_ref[...] * s)
    o_ref[:, D_S5 + D_GLA_V:] = hy.astype(BF16)


def mixer_finish(p, ys_f, ys_b, s5_d, w_glu, b_glu, og_f, og_b, gn, x0, conv, s, hy_bias, *, tm):
    T = p.shape[0]
    row = lambda w, off=0: pl.BlockSpec((tm, w), lambda i: (i, off))
    const = lambda r, w: pl.BlockSpec((r, w), lambda i: (0, 0))
    return pl.pallas_call(
        _finish_kernel,
        grid=(T // tm,),
        in_specs=[
            row(D_S5, OFF_S5 // D_S5), row(D_S5), row(D_S5), const(1, D_S5), const(D_S5, D_S5), const(1, D_S5),
            row(D_GLA_V), row(D_GLA_V), row(D_GLA_V, OFF_GATE // D_GLA_V), const(1, GLA_DV),
            row(D_HY), row(D_HY), row(D_HY), const(1, D_HY),
        ],
        out_specs=pl.BlockSpec((tm, D_MODEL), lambda i: (i, 0)),
        out_shape=jax.ShapeDtypeStruct((T, D_MODEL), BF16),
        compiler_params=_cparams(("arbitrary",)),
        name="mixer_finish",
    )(p, ys_f, ys_b, s5_d.reshape(1, D_S5), w_glu, b_glu.reshape(1, D_S5),
      og_f, og_b, p, gn.reshape(1, GLA_DV), x0, conv, s, hy_bias.reshape(1, D_HY))


def _peer_kernel(x_ref, u_ref, vt_ref, s2_ref, e2_ref, s1_ref, e1_ref, tau_ref, o_ref,
                 act_ref, p_ref, *, n_i1, tm, cw):
    j = pl.program_id(1)

    @pl.when(j == 0)
    def _():
        o_ref[...] = jnp.zeros_like(o_ref)

    act_ref[...] = lax.dot_general(u_ref[...], x_ref[...], NT_DIMS, preferred_element_type=F32)
    K = PEER_KEYS
    sub = SUBLANE // n_i1
    base = (j % sub) * n_i1 if sub > 1 else 0
    for a in range(n_i1):
        r1 = pl.ds(base + a, 1)
        for cb in range(tm // cw):
            cols = slice(cb * cw, (cb + 1) * cw)
            w = jnp.zeros((K, cw), F32)
            for h in range(PEER_HEADS):
                tot = s1_ref[h, r1, cols] + s2_ref[h, :, cols]
                val = e1_ref[h, r1, cols] * e2_ref[h, :, cols]
                w = w + jnp.where(tot >= tau_ref[h, :, cols], val, 0.0)
            act = act_ref[a * K:(a + 1) * K, cols]
            p_ref[a * K:(a + 1) * K, cols] = (w * _gelu(act)).astype(BF16)
    o_ref[...] += _dot(vt_ref[...], p_ref[...])


def peer_dense(xn, u_tab, vt_tab, s1t, e1t, s2t, e2t, taut, *, tm, te, cw=256):
    T, D = xn.shape
    E = u_tab.shape[0]
    H, K = PEER_HEADS, PEER_KEYS
    n_i1 = te // K
    sub = SUBLANE // n_i1
    i1_spec = pl.BlockSpec((H, SUBLANE, tm), lambda i, j: (0, j // sub, i))
    once = pl.Buffered(1)
    tab_spec = pl.BlockSpec((H, K, tm), lambda i, j: (0, 0, i), pipeline_mode=once)
    return pl.pallas_call(
        functools.partial(_peer_kernel, n_i1=n_i1, tm=tm, cw=cw),
        grid=(T // tm, E // te),
        in_specs=[
            pl.BlockSpec((tm, D), lambda i, j: (i, 0), pipeline_mode=once),
            pl.BlockSpec((te, D), lambda i, j: (j, 0)),
            pl.BlockSpec((D, te), lambda i, j: (0, j)),
            tab_spec, tab_spec,
            i1_spec, i1_spec,
            pl.BlockSpec((H, 1, tm), lambda i, j: (0, 0, i), pipeline_mode=once),
        ],
        out_specs=pl.BlockSpec((D, tm), lambda i, j: (0, i)),
        out_shape=jax.ShapeDtypeStruct((D, T), F32),
        scratch_shapes=[pltpu.VMEM((te, tm), F32), pltpu.VMEM((te, tm), BF16)],
        compiler_params=_cparams(("arbitrary", "arbitrary")),
        name="peer_dense",
    )(xn, u_tab, vt_tab, s2t, e2t, s1t, e1t, taut)


NEG_BIG = -3.0e38
N_CAND = PEER_TOPK + 8 * 7 + 8


def _peer_prep_kernel(q_ref, k_ref, s1_ref, e1_ref, s2_ref, e2_ref, tau_ref, top_ref, cand_ref):
    R = PEER_TOPK
    half_w = PEER_DQ // 2

    def nt(a, b):
        return lax.dot_general(a, b, NT_DIMS, preferred_element_type=F32)

    def scores(half):
        qh = q_ref[:, half * half_w:(half + 1) * half_w]
        q_hi = qh.astype(BF16)
        q_lo = (qh - q_hi.astype(F32)).astype(BF16)
        kk = k_ref[half]
        k_hi = kk.astype(BF16)
        k_lo = (kk - k_hi.astype(F32)).astype(BF16)
        return nt(k_hi, q_hi) + (nt(k_hi, q_lo) + nt(k_lo, q_hi))

    def sorted_top(s, slot):
        work = s
        for r in range(R):
            m = jnp.max(work, axis=0, keepdims=True)
            top_ref[slot, r:r + 1, :] = m
            work = jnp.where(work == m, NEG_BIG, work)

    s1 = scores(0)
    s2 = scores(1)
    sorted_top(s1, 0)
    sorted_top(s2, 1)
    a = top_ref[0]
    b = top_ref[1]
    cand_ref[0:R, :] = a[0:1, :] + b
    for i in range(1, 8):
        cand_ref[R + 8 * (i - 1):R + 8 * i, :] = a[i:i + 1, :] + b[0:8, :]
    cand_ref[R + 56:R + 64, :] = a[8:16, :] + b[0:1, :]
    work = cand_ref[...]
    m0 = jnp.max(work, axis=0, keepdims=True)
    m = m0
    z = jnp.ones_like(m0)
    for r in range(1, R):
        work = jnp.where(work == m, NEG_BIG, work)
        m = jnp.max(work, axis=0, keepdims=True)
        z = z + jnp.exp(m - m0)
    tau_ref[0] = m
    s1_ref[0] = s1
    s2_ref[0] = s2
    e1_ref[0] = jnp.exp(s1 - a[0:1, :]) * (1.0 / z)
    e2_ref[0] = jnp.exp(s2 - b[0:1, :])


def peer_prep(q, keys, *, tm):
    T = q.shape[0]
    H, K = PEER_HEADS, PEER_KEYS
    tab = jax.ShapeDtypeStruct((H, K, T), F32)
    tab_spec = pl.BlockSpec((1, K, tm), lambda i, h: (h, 0, i))
    return pl.pallas_call(
        _peer_prep_kernel,
        grid=(T // tm, H),
        in_specs=[
            pl.BlockSpec((tm, PEER_DQ), lambda i, h: (i, h)),
            pl.BlockSpec((None, 2, K, PEER_DQ // 2), lambda i, h: (h, 0, 0, 0)),
        ],
        out_specs=[tab_spec, tab_spec, tab_spec, tab_spec, pl.BlockSpec((1, 1, tm), lambda i, h: (h, 0, i))],
        out_shape=[tab, tab, tab, tab, jax.ShapeDtypeStruct((H, 1, T), F32)],
        scratch_shapes=[pltpu.VMEM((2, PEER_TOPK, tm), F32), pltpu.VMEM((N_CAND, tm), F32)],
        compiler_params=_cparams(("arbitrary", "arbitrary")),
        name="peer_prep",
    )(q, keys)


def _transpose_gated_res_kernel(h_ref, ft_ref, gate_ref, o_ref, *, ctx_len, tm):
    i = pl.program_id(0)
    row = i * tm + lax.broadcasted_iota(jnp.int32, (tm, 1), 0)
    gate = jnp.where(row < ctx_len, gate_ref[0:1, :], gate_ref[1:2, :])
    o_ref[...] = h_ref[...] + gate * ft_ref[...].T


def transpose_gated_res(h, ft, gates, *, ctx_len, tm):
    T, D = h.shape
    return pl.pallas_call(
        functools.partial(_transpose_gated_res_kernel, ctx_len=ctx_len, tm=tm),
        grid=(T // tm,),
        in_specs=[
            pl.BlockSpec((tm, D), lambda i: (i, 0)),
            pl.BlockSpec((D, tm), lambda i: (0, i)),
            pl.BlockSpec((2, D), lambda i: (0, 0)),
        ],
        out_specs=pl.BlockSpec((tm, D), lambda i: (i, 0)),
        out_shape=jax.ShapeDtypeStruct((T, D), F32),
        compiler_params=_cparams(("arbitrary",)),
        name="transpose_gated_res",
    )(h, ft, gates)


def _rmsnorm_kernel(x_ref, g_ref, o_ref):
    x = x_ref[...]
    ms = jnp.mean(x * x, axis=-1, keepdims=True)
    o_ref[...] = x * lax.rsqrt(ms + EPS) * g_ref[...]


def rmsnorm_rows(x, g, *, tm):
    T, D = x.shape
    return pl.pallas_call(
        _rmsnorm_kernel,
        grid=(T // tm,),
        in_specs=[pl.BlockSpec((tm, D), lambda i: (i, 0)), pl.BlockSpec((1, D), lambda i: (0, 0))],
        out_specs=pl.BlockSpec((tm, D), lambda i: (i, 0)),
        out_shape=jax.ShapeDtypeStruct((T, D), F32),
        compiler_params=_cparams(("arbitrary",)),
        name="rmsnorm_rows",
    )(x, g.reshape(1, D))


def grid_sincos(n_tokens, dim):
    rows = n_tokens // GRID_W
    row = jnp.repeat(jnp.arange(rows), GRID_W).astype(F32)
    col = jnp.tile(jnp.arange(GRID_W), rows).astype(F32)
    n_freq = dim // 4
    omega = 1.0 / (POS_BASE ** (jnp.arange(n_freq, dtype=F32) / n_freq))

    def enc(pp):
        a = pp[:, None] * omega[None, :]
        return jnp.concatenate([jnp.sin(a), jnp.cos(a)], axis=-1)

    return jnp.concatenate([enc(row), enc(col)], axis=-1)


def _split_bf16(a):
    hi = a.astype(BF16)
    return hi, (a - hi.astype(F32)).astype(BF16)


def _dot3s(a, b_hi, b_lo):
    a_hi, a_lo = _split_bf16(a)
    return _dot(a_hi, b_hi) + (_dot(a_hi, b_lo) + _dot(a_lo, b_hi))


def _dot3(a, b):
    return _dot3s(a, *_split_bf16(b))


def _adaln_kernel(c_ref, w_ref, b_ref, o_ref):
    c = c_ref[...]
    o_ref[...] = _dot3(c * _sigmoid(c), w_ref[...]) + b_ref[...]


def adaln(cond, w, b, *, tn=1536):
    R, D = cond.shape
    N = w.shape[1]
    return pl.pallas_call(
        _adaln_kernel,
        grid=(N // tn,),
        in_specs=[pl.BlockSpec((R, D), lambda j: (0, 0)),
                  pl.BlockSpec((D, tn), lambda j: (0, j)),
                  pl.BlockSpec((1, tn), lambda j: (0, j))],
        out_specs=pl.BlockSpec((R, tn), lambda j: (0, j)),
        out_shape=jax.ShapeDtypeStruct((R, N), F32),
        compiler_params=_cparams(("arbitrary",)),
        name="adaln",
    )(cond, w, b.reshape(1, N))


def _hyena_pre_kernel(x0_ref, x1_ref, v_ref, w_ref, b_ref, x0o_ref, s_ref, *, ctx_len):
    T = x0_ref.shape[0]
    row = lax.broadcasted_iota(jnp.int32, (T, 1), 0)
    first = jnp.logical_or(row == 0, row == ctx_len)
    last = jnp.logical_or(row == ctx_len - 1, row == T - 1)

    def conv(ref, part):
        x = ref[...]
        prev = jnp.where(first, 0.0, pltpu.roll(x, 1, 0))
        nxt = jnp.where(last, 0.0, pltpu.roll(x, T - 1, 0))
        w = lambda tap: w_ref[tap, part:part + 1, :]
        return w(0) * prev + w(1) * x + w(2) * nxt + b_ref[part:part + 1, :]

    x0o_ref[...] = conv(x0_ref, 0)
    s_ref[...] = conv(x1_ref, 1) * conv(v_ref, 2)


def hyena_pre(p, conv_w, conv_b, *, ctx_len):
    T = p.shape[0]
    nb = D_HY // LANE
    col = lambda part: pl.BlockSpec((T, LANE), lambda c: (0, OFF_HY // LANE + part * nb + c))
    out = jax.ShapeDtypeStruct((T, D_HY), F32)
    return pl.pallas_call(
        functools.partial(_hyena_pre_kernel, ctx_len=ctx_len),
        grid=(nb,),
        in_specs=[col(0), col(1), col(2),
                  pl.BlockSpec((HY_SHORT, 3, LANE), lambda c: (0, 0, c)),
                  pl.BlockSpec((3, LANE), lambda c: (0, c))],
        out_specs=[pl.BlockSpec((T, LANE), lambda c: (0, c))] * 2,
        out_shape=[out, out],
        compiler_params=_cparams(("arbitrary",)),
        name="hyena_pre",
    )(p, p, p, conv_w.reshape(HY_SHORT, 3, D_HY), conv_b.reshape(3, D_HY))


def _hyena_filter_kernel(band_ref, w1_ref, b1_ref, w2_ref, b2_ref, w3_ref, fr_ref, dec_ref, o_ref, *, n, tr):
    i = pl.program_id(0)
    ri = (i * tr + lax.broadcasted_iota(jnp.int32, (tr, 1), 0)).astype(F32)
    t = ri * (1.0 / (n - 1))
    w = ri * (2.0 * math.pi / n)
    lane = lax.broadcasted_iota(jnp.int32, (1, LANE), 1)
    arg = w * band_ref[...]
    z = jnp.where(lane == 0, t,
                  jnp.where(lane <= HY_BANDS, jnp.cos(arg),
                            jnp.where(lane <= 2 * HY_BANDS, -jnp.sin(arg), 0.0)))
    h = jnp.sin(fr_ref[0:1, :] * (_dot3(z, w1_ref[...]) + b1_ref[...]))
    h = jnp.sin(fr_ref[1:2, :] * (_dot3(h, w2_ref[...]) + b2_ref[...]))
    h = _dot3(h, w3_ref[...]) * jnp.exp(-t * dec_ref[...])
    col = lax.broadcasted_iota(jnp.int32, (1, 2 * D_HY), 1)
    o_ref[...] = jnp.where(jnp.logical_and(ri == 0.0, col >= D_HY), 0.0, h)


def hyena_filter(n, w1, b1, w2, b2, w3, freq, decay, *, tr=256):
    bands = jnp.linspace(1e-4, HY_BANDS - 1, HY_BANDS, dtype=F32)
    band_row = jnp.zeros((1, LANE), F32).at[0, 1:1 + HY_BANDS].set(bands).at[0, 1 + HY_BANDS:1 + 2 * HY_BANDS].set(bands)
    w1p = jnp.zeros((LANE, HY_HIDDEN), F32).at[:HY_FEAT].set(w1)
    const = lambda a: pl.BlockSpec(a.shape, lambda i: (0,) * a.ndim)
    args = (band_row, w1p, b1.reshape(1, -1), w2, b2.reshape(1, -1), w3, freq,
            jnp.abs(decay).reshape(1, 2 * D_HY))
    return pl.pallas_call(
        functools.partial(_hyena_filter_kernel, n=n, tr=tr),
        grid=(n // tr,),
        in_specs=[const(a) for a in args],
        out_specs=pl.BlockSpec((tr, 2 * D_HY), lambda i: (i, 0)),
        out_shape=jax.ShapeDtypeStruct((n, 2 * D_HY), F32),
        compiler_params=_cparams(("arbitrary",)),
        name="hyena_filter",
    )(*args)


FFT_R = 128
FFT_N = FFT_R * FFT_R
FFT_CH = 32


def _dft_tables():
    r = np.arange(FFT_R)
    ang = 2.0 * np.pi * np.outer(r, r) / FFT_R
    c, s = np.cos(ang), np.sin(ang)
    angt = 2.0 * np.pi * np.outer(r, r) / FFT_N
    ct, st = np.cos(angt), np.sin(angt)

    def split(m):
        m = jnp.asarray(m, F32)
        hi = m.astype(BF16)
        return jnp.stack([hi, (m - hi.astype(F32)).astype(BF16)])

    fwd_a = split(np.concatenate([c, -s], axis=1))
    fwd_b = split(np.block([[c, -s], [s, c]]))
    inv_b = split(np.block([[c, s], [-s, c]]) / FFT_N)
    inv_a = split(np.concatenate([c, -s], axis=0))
    tw = (jnp.asarray(ct, F32), jnp.asarray(-st, F32))
    twc = (jnp.asarray(ct, F32), jnp.asarray(st, F32))
    return fwd_a, fwd_b, inv_b, inv_a, tw, twc


def _twiddle_transpose(y, twr, twi, dst_ref, nc):
    y = y.reshape(nc, FFT_R, 2 * FFT_R)
    yr, yi = y[:, :, :FFT_R], y[:, :, FFT_R:]
    zr = yr * twr - yi * twi
    zi = yr * twi + yi * twr
    for c in range(nc):
        dst_ref[c * FFT_R:(c + 1) * FFT_R, 0:FFT_R] = zr[c].T
        dst_ref[c * FFT_R:(c + 1) * FFT_R, FFT_R:2 * FFT_R] = zi[c].T


def _hyena_fwd_kernel(x_ref, fa_ref, twr_ref, twi_ref, fb_ref, o_ref, l_ref, yt_ref, *, nc):
    j = pl.program_id(1)
    half = FFT_R // 2

    @pl.when(j == 0)
    def _():
        zeros = jnp.zeros((half, LANE), F32)

        def body(b, carry):
            t = jnp.concatenate([x_ref[pl.ds(b, half, stride=FFT_R), :], zeros], axis=0)
            l_ref[pl.ds(b, LANE, stride=FFT_R), :] = t.T
            return carry

        lax.fori_loop(0, FFT_R, body, 0, unroll=8)

    rows = nc * FFT_R
    lc = l_ref[pl.ds(pl.multiple_of(j * rows, rows), rows), :]
    y = _dot3s(lc, fa_ref[0], fa_ref[1])
    _twiddle_transpose(y, twr_ref[...], twi_ref[...], yt_ref, nc)
    o_ref[...] = _dot3s(yt_ref[...], fb_ref[0], fb_ref[1])


def hyena_fwd_dft(x, tables):
    n, C = x.shape
    assert n * 2 == FFT_N and C % LANE == 0
    fwd_a, fwd_b, _, _, (twr, twi), _ = tables
    nc = FFT_CH
    steps = LANE // nc
    const = lambda a: pl.BlockSpec(a.shape, lambda cb, j: (0,) * a.ndim)
    return pl.pallas_call(
        functools.partial(_hyena_fwd_kernel, nc=nc),
        grid=(C // LANE, steps),
        in_specs=[pl.BlockSpec((n, LANE), lambda cb, j: (0, cb)), const(fwd_a), const(twr), const(twi), const(fwd_b)],
        out_specs=pl.BlockSpec((nc * FFT_R, 2 * FFT_R), lambda cb, j: (cb * steps + j, 0)),
        out_shape=jax.ShapeDtypeStruct((C * FFT_R, 2 * FFT_R), F32),
        scratch_shapes=[pltpu.VMEM((LANE * FFT_R, LANE), F32), pltpu.VMEM((nc * FFT_R, 2 * FFT_R), F32)],
        compiler_params=_cparams(("arbitrary", "arbitrary")),
        name="hyena_fwd_dft",
    )(x, fwd_a, twr, twi, fwd_b)


def _hyena_inv_kernel(s_ref, kf_ref, kb_ref, gb_ref, twr_ref, twi_ref, ga_ref, o_ref, lr_ref, at_ref, *, nc, steps):
    j = pl.program_id(1)
    R = FFT_R
    s, kf, kb = s_ref[...], kf_ref[...], kb_ref[...]
    sr, si = s[:, :R], s[:, R:]
    kr = kf[:, :R] + kb[:, :R]
    ki = kf[:, R:] - kb[:, R:]
    p = jnp.concatenate([sr * kr - si * ki, sr * ki + si * kr], axis=1)
    a = _dot3s(p, gb_ref[0], gb_ref[1])
    _twiddle_transpose(a, twr_ref[...], twi_ref[...], at_ref, nc)
    rows = nc * R
    lr_ref[pl.ds(pl.multiple_of(j * rows, rows), rows), :] = _dot3s(at_ref[...], ga_ref[0], ga_ref[1])

    @pl.when(j == steps - 1)
    def _():
        def body(b, carry):
            t = lr_ref[pl.ds(b, LANE, stride=R), :]
            o_ref[pl.ds(b, R // 2, stride=R), :] = t.T[0:R // 2, :]
            return carry

        lax.fori_loop(0, R, body, 0, unroll=8)


def hyena_inv_dft(s_f, k_f, tables):
    C = s_f.shape[0] // FFT_R
    _, _, inv_b, inv_a, _, (twr, twi) = tables
    nc = FFT_CH
    steps = LANE // nc
    kb_off = C // nc
    const = lambda a: pl.BlockSpec(a.shape, lambda cb, j: (0,) * a.ndim)
    blk = lambda off: pl.BlockSpec((nc * FFT_R, 2 * FFT_R), lambda cb, j: (off + cb * steps + j, 0))
    return pl.pallas_call(
        functools.partial(_hyena_inv_kernel, nc=nc, steps=steps),
        grid=(C // LANE, steps),
        in_specs=[blk(0), blk(0), blk(kb_off), const(inv_b), const(twr), const(twi), const(inv_a)],
        out_specs=pl.BlockSpec((FFT_N // 2, LANE), lambda cb, j: (0, cb)),
        out_shape=jax.ShapeDtypeStruct((FFT_N // 2, C), F32),
        scratch_shapes=[pltpu.VMEM((LANE * FFT_R, LANE), F32), pltpu.VMEM((nc * FFT_R, 2 * FFT_R), F32)],
        compiler_params=_cparams(("arbitrary", "arbitrary")),
        name="hyena_inv_dft",
    )(s_f, k_f, k_f, inv_b, twr, twi, inv_a)


def _hyena_ctx_kernel(s_ref, k_ref, fw_ref, iv_ref, o_ref):
    n = s_ref.shape[0]
    N = 2 * n
    xs = _dot3(fw_ref[...], s_ref[...])
    xk = _dot3(fw_ref[...], k_ref[...])
    sr, si = xs[:N], xs[N:]
    kr = xk[:N, :D_HY] + xk[:N, D_HY:]
    ki = xk[N:, :D_HY] - xk[N:, D_HY:]
    p = jnp.concatenate([sr * kr - si * ki, sr * ki + si * kr], axis=0)
    o_ref[...] = _dot3(iv_ref[...], p)


def hyena_ctx_conv(s_c, k_c):
    n = s_c.shape[0]
    N = 2 * n
    ang = 2.0 * np.pi * np.outer(np.arange(N), np.arange(N)) / N
    c, s = np.cos(ang), np.sin(ang)
    fw = jnp.asarray(np.concatenate([c[:, :n], -s[:, :n]], axis=0), F32)
    iv = jnp.asarray(np.concatenate([c[:n, :], -s[:n, :]], axis=1) / N, F32)
    full = lambda a: pl.BlockSpec(a.shape, lambda i: (0,) * a.ndim)
    return pl.pallas_call(
        _hyena_ctx_kernel,
        grid=(1,),
        in_specs=[full(s_c), full(k_c), full(fw), full(iv)],
        out_specs=pl.BlockSpec((n, D_HY), lambda i: (0, 0)),
        out_shape=jax.ShapeDtypeStruct((n, D_HY), F32),
        compiler_params=_cparams(("arbitrary",)),
        name="hyena_ctx_conv",
    )(s_c, k_c, fw, iv)


def hyena_mixer_parts(p, conv_w, conv_b, filt, tables, *, ctx_len):
    T = p.shape[0]
    x0, s = hyena_pre(p, conv_w, conv_b, ctx_len=ctx_len)
    k_lat = hyena_filter(T - ctx_len, *filt)
    k_ctx = hyena_filter(ctx_len, *filt)
    conv_l = hyena_inv_dft(hyena_fwd_dft(s[ctx_len:], tables), hyena_fwd_dft(k_lat, tables), tables)
    conv_c = hyena_ctx_conv(s[:ctx_len], k_ctx)
    return x0, jnp.concatenate([conv_c, conv_l], axis=0), s


def _pack_w_in(w_in_l):
    widths = (D_S5, D_GLA_K, D_GLA_K, D_GLA_V, D_GLA_V, 2 * GLA_RANK, 3 * D_HY)
    offs = [0]
    for wd in widths:
        offs.append(offs[-1] + wd)
    s5, q, k, v, gate, alpha, hy = (w_in_l[:, offs[i]:offs[i + 1]] for i in range(7))
    pad = jnp.zeros((w_in_l.shape[0], IN_PACKED - OFF_ALPHA - 2 * GLA_RANK), w_in_l.dtype)
    return jnp.concatenate([v, gate, s5, q, k, hy, alpha, pad], axis=1).astype(BF16)


def kernel(x, c, ctx, c_ctx, w_ada, b_ada, g_norm1, g_norm2, w_in, s5_a_re, s5_a_im, s5_log_step, s5_b_re, s5_b_im, s5_c_re, s5_c_im, s5_d, s5_w_glu, s5_b_glu, gla_w_alpha, gla_b_alpha, gla_g_norm, hy_conv_w, hy_conv_b, hy_f_w1, hy_f_b1, hy_f_w2, hy_f_b2, hy_f_w3, hy_f_freq, hy_decay, hy_bias, w_out, peer_w_q, peer_keys, peer_u, peer_v, g_final):
    L = x.shape[1]
    Lc = ctx.shape[1]
    T = L + Lc
    TM = 768
    h = jnp.concatenate([ctx[0], x[0] + grid_sincos(L, D_MODEL)], axis=0)
    cond = jnp.zeros((SUBLANE, D_MODEL), F32).at[0].set(c_ctx).at[1].set(c[0])
    tables = _dft_tables()

    for l in range(DEPTH):
        m = adaln(cond, w_ada[l], b_ada[l])[0:2]
        sh1, sc1, gt1, sh2, sc2, gt2 = jnp.split(m, 6, axis=-1)
        mod1 = jnp.stack([sh1[0], sc1[0], sh1[1], sc1[1]], axis=0)
        mod2 = jnp.stack([sh2[0], sc2[0], sh2[1], sc2[1]], axis=0)

        p = norm_mod_matmul(h, g_norm1[l], mod1, _pack_w_in(w_in[l]), ctx_len=Lc, tm=TM, tn=768)

        bm, cm, pw = s5_prepare(s5_a_re[l], s5_a_im[l], s5_log_step[l], s5_b_re[l], s5_b_im[l],
                                s5_c_re[l], s5_c_im[l])
        ys_f, ys_b = s5_scan(p, bm, cm, pw, ctx_len=Lc)

        wa = jnp.zeros((2, LANE, D_GLA_K), F32)
        wa = wa.at[0, 0:GLA_RANK].set(gla_w_alpha[l, 0]).at[1, GLA_RANK:2 * GLA_RANK].set(gla_w_alpha[l, 1])
        wa = wa.astype(BF16)
        og_f, og_b = gla_scan(p, wa, gla_b_alpha[l], ctx_len=Lc)

        filt = (hy_f_w1[l], hy_f_b1[l], hy_f_w2[l], hy_f_b2[l], hy_f_w3[l], hy_f_freq[l], hy_decay[l])
        x0, conv, s = hyena_mixer_parts(p, hy_conv_w[l], hy_conv_b[l], filt, tables, ctx_len=Lc)

        mix = mixer_finish(p, ys_f, ys_b, s5_d[l], s5_w_glu[l].astype(BF16), s5_b_glu[l],
                           og_f, og_b, gla_g_norm[l], x0, conv, s, hy_bias[l], tm=256)
        h = matmul_gated_res(mix, w_out[l].astype(BF16), h, gt1, ctx_len=Lc, tm=TM, tn=1024)

        q, xn = norm_mod_matmul(h, g_norm2[l], mod2, peer_w_q[l].astype(BF16),
                                ctx_len=Lc, tm=TM, tn=1024, emit_xn=True)
        s1t, e1t, s2t, e2t, taut = peer_prep(q, peer_keys[l], tm=TM)
        ft = peer_dense(xn, peer_u[l].astype(BF16), peer_v[l].T.astype(BF16),
                        s1t, e1t, s2t, e2t, taut, tm=TM, te=1024)
        h = transpose_gated_res(h, ft, gt2, ctx_len=Lc, tm=256)

    out = rmsnorm_rows(h[Lc:], g_final, tm=512)
    return out[None]
```

```python
import functools
import math

import jax
import jax.numpy as jnp
import numpy as np
from jax import lax
from jax.experimental import pallas as pl
from jax.experimental.pallas import tpu as pltpu

F32 = jnp.float32
BF16 = jnp.bfloat16

D_MODEL = 2048
DEPTH = 4
GRID_W = 64
EPS = 1e-6
POS_BASE = 10000.0

D_S5 = D_MODEL // 4
S5_GROUP = 16
S5_GROUPS = D_S5 // S5_GROUP
S5_STATE = 64
S5_BLK_GROUPS = 8
S5_BLK_CH = S5_BLK_GROUPS * S5_GROUP
S5_BLK_ST = S5_BLK_GROUPS * S5_STATE
S5_NBLK = S5_GROUPS // S5_BLK_GROUPS

GLA_HEADS = 4
D_GLA_K = D_MODEL // 4
D_GLA_V = D_MODEL // 2
GLA_DK = D_GLA_K // GLA_HEADS
GLA_DV = D_GLA_V // GLA_HEADS
GLA_RANK = 16
GLA_TAU = 16.0
GLA_CHUNK = 64

D_HY = D_MODEL // 4
HY_SHORT = 3
HY_BANDS = 16
HY_FEAT = 1 + 2 * HY_BANDS
HY_HIDDEN = 64
HY_TARGET = 1e-2
HY_MIN_DECAY = -math.log(HY_TARGET) / 1.5
HY_MAX_DECAY = -math.log(HY_TARGET) / 0.3

PEER_HEADS = 8
PEER_KEYS = 128
PEER_EXPERTS = PEER_KEYS * PEER_KEYS
PEER_DQ = 256
PEER_TOPK = 16

OFF_V = 0
OFF_GATE = OFF_V + D_GLA_V
OFF_S5 = OFF_GATE + D_GLA_V
OFF_Q = OFF_S5 + D_S5
OFF_K = OFF_Q + D_GLA_K
OFF_HY = OFF_K + D_GLA_K
OFF_ALPHA = OFF_HY + 3 * D_HY
LANE = 128
SUBLANE = 8
IN_PACKED = OFF_ALPHA + 2 * LANE

SEQ_TILE = 256
VMEM_LIMIT = 56 * 1024 * 1024

NT_DIMS = (((1,), (1,)), ((), ()))
TN_DIMS = (((0,), (0,)), ((), ()))


def _cparams(sem):
    return pltpu.CompilerParams(dimension_semantics=sem, vmem_limit_bytes=VMEM_LIMIT)


def _dot(a, b):
    return jnp.dot(a, b, preferred_element_type=F32)


def _gelu(x):
    k1 = -2.0 * math.sqrt(2.0 / math.pi) * math.log2(math.e)
    k2 = 0.044715 * k1
    return x / (1.0 + jnp.exp2(x * (k1 + k2 * (x * x))))


def _sigmoid(x):
    return 1.0 / (1.0 + jnp.exp(-x))


def _norm_mod_matmul_kernel(x_ref, g_ref, mod_ref, w_ref, o_ref, *rest, ctx_len, tm, emit_xn):
    if emit_xn:
        xo_ref, xn_ref = rest
    else:
        (xn_ref,) = rest
    i = pl.program_id(0)
    j = pl.program_id(1)

    @pl.when(j == 0)
    def _():
        x = x_ref[...]
        ms = jnp.mean(x * x, axis=-1, keepdims=True)
        y = x * lax.rsqrt(ms + EPS) * g_ref[...]
        row = i * tm + lax.broadcasted_iota(jnp.int32, (tm, 1), 0)
        is_ctx = row < ctx_len
        shift = jnp.where(is_ctx, mod_ref[0:1, :], mod_ref[2:3, :])
        scale = jnp.where(is_ctx, mod_ref[1:2, :], mod_ref[3:4, :])
        xn = (y * (1.0 + scale) + shift).astype(BF16)
        xn_ref[...] = xn
        if emit_xn:
            xo_ref[...] = xn

    o_ref[...] = _dot(xn_ref[...], w_ref[...])


def norm_mod_matmul(x, g, mod, w, *, ctx_len, tm, tn, emit_xn=False):
    T, D = x.shape
    N = w.shape[1]
    out_shape = [jax.ShapeDtypeStruct((T, N), F32)]
    out_specs = [pl.BlockSpec((tm, tn), lambda i, j: (i, j))]
    if emit_xn:
        out_shape.append(jax.ShapeDtypeStruct((T, D), BF16))
        out_specs.append(pl.BlockSpec((tm, D), lambda i, j: (i, 0)))
    res = pl.pallas_call(
        functools.partial(_norm_mod_matmul_kernel, ctx_len=ctx_len, tm=tm, emit_xn=emit_xn),
        grid=(T // tm, N // tn),
        in_specs=[
            pl.BlockSpec((tm, D), lambda i, j: (i, 0)),
            pl.BlockSpec((1, D), lambda i, j: (0, 0)),
            pl.BlockSpec((4, D), lambda i, j: (0, 0)),
            pl.BlockSpec((D, tn), lambda i, j: (0, j)),
        ],
        out_specs=out_specs,
        out_shape=out_shape,
        scratch_shapes=[pltpu.VMEM((tm, D), BF16)],
        compiler_params=_cparams(("arbitrary", "arbitrary")),
        name="norm_mod_matmul",
    )(x, g.reshape(1, D), mod, w)
    return res if emit_xn else res[0]


def _matmul_gated_res_kernel(a_ref, w_ref, r_ref, gate_ref, o_ref, *, ctx_len, tm):
    i = pl.program_id(0)
    row = i * tm + lax.broadcasted_iota(jnp.int32, (tm, 1), 0)
    gate = jnp.where(row < ctx_len, gate_ref[0:1, :], gate_ref[1:2, :])
    o_ref[...] = r_ref[...] + gate * _dot(a_ref[...], w_ref[...])


def matmul_gated_res(a, w, res, gates, *, ctx_len, tm, tn):
    T, K = a.shape
    N = w.shape[1]
    return pl.pallas_call(
        functools.partial(_matmul_gated_res_kernel, ctx_len=ctx_len, tm=tm),
        grid=(T // tm, N // tn),
        in_specs=[
            pl.BlockSpec((tm, K), lambda i, j: (i, 0)),
            pl.BlockSpec((K, tn), lambda i, j: (0, j)),
            pl.BlockSpec((tm, tn), lambda i, j: (i, j)),
            pl.BlockSpec((2, tn), lambda i, j: (0, j)),
        ],
        out_specs=pl.BlockSpec((tm, tn), lambda i, j: (i, j)),
        out_shape=jax.ShapeDtypeStruct((T, N), F32),
        compiler_params=_cparams(("arbitrary", "arbitrary")),
        name="matmul_gated_res",
    )(a, w, res, gates)


def _time_tile(t, n_tiles, n_ctx_tiles, rev):
    if not rev:
        return t
    return jnp.where(t < n_ctx_tiles, n_ctx_tiles - 1 - t, n_tiles - 1 - (t - n_ctx_tiles))


def _s5_kernel(uf_ref, ub_ref, bm_ref, cm_ref, pw_ref, yf_ref, yb_ref, h_ref, c_ref, *, lt):
    t = pl.program_id(1)
    ns = S5_BLK_ST
    u_refs = (uf_ref, ub_ref)
    y_refs = (yf_ref, yb_ref)

    @pl.when(t == 0)
    def _():
        c_ref[...] = jnp.zeros_like(c_ref)

    for d in range(2):
        bu = _dot(u_refs[d][...].astype(BF16), bm_ref[d])
        h_ref[d, 0] = bu[:, :ns]
        h_ref[d, 1] = bu[:, ns:]

    n_grp = lt // 8

    def group_update(d, g, c_re, c_im):
        rev = d == 1
        gi = (n_grp - 1 - g) if rev else g
        rows = pl.ds(pl.multiple_of(gi * 8, 8), 8)
        a_re = h_ref[d, 0, rows, :]
        a_im = h_ref[d, 1, rows, :]
        for k, s in enumerate((1, 2, 4)):
            l_re = pw_ref[d, 8 + 8 * k:16 + 8 * k, :ns]
            l_im = pw_ref[d, 8 + 8 * k:16 + 8 * k, ns:]
            shift = (8 - s) if rev else s
            s_re = pltpu.roll(a_re, shift, 0)
            s_im = pltpu.roll(a_im, shift, 0)
            a_re = a_re + (l_re * s_re - l_im * s_im)
            a_im = a_im + (l_re * s_im + l_im * s_re)
        p_re = pw_ref[d, 0:8, :ns]
        p_im = pw_ref[d, 0:8, ns:]
        a_re = a_re + (p_re * c_re - p_im * c_im)
        a_im = a_im + (p_re * c_im + p_im * c_re)
        h_ref[d, 0, rows, :] = a_re
        h_ref[d, 1, rows, :] = a_im
        edge = slice(0, 1) if rev else slice(7, 8)
        return a_re[edge, :], a_im[edge, :]

    def body(g, carry):
        f_re, f_im, b_re, b_im = carry
        f_re, f_im = group_update(0, g, f_re, f_im)
        b_re, b_im = group_update(1, g, b_re, b_im)
        return f_re, f_im, b_re, b_im

    carry = lax.fori_loop(0, n_grp, body, (c_ref[0, 0], c_ref[0, 1], c_ref[1, 0], c_ref[1, 1]), unroll=2)
    c_ref[0, 0], c_ref[0, 1], c_ref[1, 0], c_ref[1, 1] = carry

    for d in range(2):
        y_refs[d][...] = (_dot(h_ref[d, 0].astype(BF16), cm_ref[d, :ns, :])
                          + _dot(h_ref[d, 1].astype(BF16), cm_ref[d, ns:, :]))


def s5_scan(p, bm, cm, pw, *, ctx_len, lt=SEQ_TILE):
    T = p.shape[0]
    n_tiles = T // lt
    n_ctx = ctx_len // lt
    tf = functools.partial(_time_tile, n_tiles=n_tiles, n_ctx_tiles=n_ctx, rev=False)
    tb = functools.partial(_time_tile, n_tiles=n_tiles, n_ctx_tiles=n_ctx, rev=True)
    ucol = OFF_S5 // S5_BLK_CH
    out = jax.ShapeDtypeStruct((T, D_S5), F32)
    return pl.pallas_call(
        functools.partial(_s5_kernel, lt=lt),
        grid=(S5_NBLK, n_tiles),
        in_specs=[
            pl.BlockSpec((lt, S5_BLK_CH), lambda b, t: (tf(t), ucol + b)),
            pl.BlockSpec((lt, S5_BLK_CH), lambda b, t: (tb(t), ucol + b)),
            pl.BlockSpec((2, None, S5_BLK_CH, 2 * S5_BLK_ST), lambda b, t: (0, b, 0, 0)),
            pl.BlockSpec((2, None, 2 * S5_BLK_ST, S5_BLK_CH), lambda b, t: (0, b, 0, 0)),
            pl.BlockSpec((2, None, 32, 2 * S5_BLK_ST), lambda b, t: (0, b, 0, 0)),
        ],
        out_specs=[pl.BlockSpec((lt, S5_BLK_CH), lambda b, t: (tf(t), b)),
                   pl.BlockSpec((lt, S5_BLK_CH), lambda b, t: (tb(t), b))],
        out_shape=[out, out],
        scratch_shapes=[pltpu.VMEM((2, 2, lt, S5_BLK_ST), F32), pltpu.VMEM((2, 2, 1, S5_BLK_ST), F32)],
        compiler_params=_cparams(("arbitrary", "arbitrary")),
        name="s5_scan",
    )(p, p, bm, cm, pw)


def _cmul(a, b):
    return a[0] * b[0] - a[1] * b[1], a[0] * b[1] + a[1] * b[0]


def s5_prepare(a_re, a_im, log_step, b_re, b_im, c_re, c_im):
    G, P, Cg = S5_GROUPS, S5_STATE, S5_GROUP
    dt = jnp.exp(log_step)[..., None]
    er = jnp.exp(a_re * dt)
    lam1 = (er * jnp.cos(a_im * dt), er * jnp.sin(a_im * dt))
    den = a_re * a_re + a_im * a_im
    xr, xi = lam1[0] - 1.0, lam1[1]
    coef = ((xr * a_re + xi * a_im) / den, (xi * a_re - xr * a_im) / den)
    bb_re = coef[0][..., None] * b_re - coef[1][..., None] * b_im
    bb_im = coef[0][..., None] * b_im + coef[1][..., None] * b_re
    pows = [lam1]
    for _ in range(7):
        pows.append(_cmul(pows[-1], lam1))

    eye = jnp.eye(S5_BLK_GROUPS, dtype=F32)

    def blockdiag_in(m):
        m = m.reshape(2, S5_NBLK, S5_BLK_GROUPS, P, Cg)
        return jnp.einsum('dbgpc,gh->dbgchp', m, eye).reshape(2, S5_NBLK, S5_BLK_CH, S5_BLK_ST)

    def blockdiag_out(m):
        m = m.reshape(2, S5_NBLK, S5_BLK_GROUPS, Cg, P)
        return jnp.einsum('dbgcp,gh->dbgphc', m, eye).reshape(2, S5_NBLK, S5_BLK_ST, S5_BLK_CH)

    bm = jnp.concatenate([blockdiag_in(bb_re), blockdiag_in(bb_im)], axis=-1).astype(BF16)
    cm = jnp.concatenate([blockdiag_out(c_re), blockdiag_out(-c_im)], axis=-2).astype(BF16)

    def lay(v):
        return v.reshape(2, S5_NBLK, S5_BLK_ST)

    def table(rev):
        power = lambda k: jnp.concatenate([lay(pows[k][0]), lay(pows[k][1])], axis=-1)
        rows = [power(7 - r if rev else r) for r in range(8)]
        zero = jnp.zeros_like(rows[0])
        for s in (1, 2, 4):
            for r in range(8):
                inside = (r < 8 - s) if rev else (r >= s)
                rows.append(power(s - 1) if inside else zero)
        return jnp.stack(rows, axis=2)

    return bm, cm, jnp.stack([table(False)[0], table(True)[1]])


def _log_sigmoid(z):
    return jnp.minimum(z, 0.0) - jnp.log(1.0 + jnp.exp(-jnp.abs(z)))


def _gla_kernel(qf_ref, kf_ref, vf_ref, af_ref, qb_ref, kb_ref, vb_ref, ab_ref, wa_ref, ba_ref,
                of_ref, ob_ref, st_ref, *, lt):
    t = pl.program_id(1)
    C = GLA_CHUNK

    @pl.when(t == 0)
    def _():
        st_ref[...] = jnp.zeros_like(st_ref)

    ri = lax.broadcasted_iota(jnp.int32, (C, C), 0)
    ci = lax.broadcasted_iota(jnp.int32, (C, C), 1)
    refs = ((qf_ref, kf_ref, vf_ref, af_ref, of_ref), (qb_ref, kb_ref, vb_ref, ab_ref, ob_ref))
    tris, tri_bs, g_alls = [], [], []
    for d in range(2):
        z = _dot(refs[d][3][...].astype(BF16), wa_ref[d]) + ba_ref[d:d + 1, :]
        g_alls.append(_log_sigmoid(z) * (1.0 / GLA_TAU))
        tri = (ci >= ri) if d == 1 else (ci <= ri)
        tris.append(tri)
        tri_bs.append(jnp.where(tri, 1.0, 0.0).astype(BF16))

    def chunk(d, c):
        q_ref, k_ref, v_ref, _, o_ref = refs[d]
        rev = d == 1
        rows = slice(c * C, (c + 1) * C)
        g = g_alls[d][rows, :]
        g_hi = g.astype(BF16)
        g_lo = (g - g_hi.astype(F32)).astype(BF16)
        b = _dot(tri_bs[d], g_hi) + _dot(tri_bs[d], g_lo)
        b_tot = b[0:1, :] if rev else b[C - 1:C, :]
        q = q_ref[rows, :] * (GLA_DK ** -0.5)
        k = k_ref[rows, :]
        v = v_ref[rows, :].astype(BF16)
        q_d = (q * jnp.exp(b)).astype(BF16)
        k_d = (k * jnp.exp(-b)).astype(BF16)
        k_s = (k * jnp.exp(b_tot - b)).astype(BF16)
        att = lax.dot_general(q_d, k_d, NT_DIMS, preferred_element_type=F32)
        att = jnp.where(tris[d], att, 0.0).astype(BF16)
        s_t = st_ref[d]
        o = _dot(att, v) + lax.dot_general(q_d, s_t.astype(BF16), NT_DIMS, preferred_element_type=F32)
        o_ref[rows, :] = o
        st_ref[d] = s_t * jnp.exp(b_tot) + lax.dot_general(v, k_s, TN_DIMS, preferred_element_type=F32)

    n_chunks = lt // C
    for c in range(n_chunks):
        chunk(0, c)
        chunk(1, n_chunks - 1 - c)


def gla_scan(p, wa, ba, *, ctx_len, lt=SEQ_TILE):
    T = p.shape[0]
    n_tiles = T // lt
    n_ctx = ctx_len // lt
    tf = functools.partial(_time_tile, n_tiles=n_tiles, n_ctx_tiles=n_ctx, rev=False)
    tb = functools.partial(_time_tile, n_tiles=n_tiles, n_ctx_tiles=n_ctx, rev=True)

    def stream(tt):
        return [
            pl.BlockSpec((lt, GLA_DK), lambda h, t: (tt(t), OFF_Q // GLA_DK + h)),
            pl.BlockSpec((lt, GLA_DK), lambda h, t: (tt(t), OFF_K // GLA_DK + h)),
            pl.BlockSpec((lt, GLA_DV), lambda h, t: (tt(t), OFF_V // GLA_DV + h)),
            pl.BlockSpec((lt, LANE), lambda h, t: (tt(t), OFF_ALPHA // LANE)),
        ]

    out = jax.ShapeDtypeStruct((T, D_GLA_V), F32)
    return pl.pallas_call(
        functools.partial(_gla_kernel, lt=lt),
        grid=(GLA_HEADS, n_tiles),
        in_specs=stream(tf) + stream(tb) + [
            pl.BlockSpec((2, LANE, GLA_DK), lambda h, t: (0, 0, h)),
            pl.BlockSpec((2, GLA_DK), lambda h, t: (0, h)),
        ],
        out_specs=[pl.BlockSpec((lt, GLA_DV), lambda h, t: (tf(t), h)),
                   pl.BlockSpec((lt, GLA_DV), lambda h, t: (tb(t), h))],
        out_shape=[out, out],
        scratch_shapes=[pltpu.VMEM((2, GLA_DV, GLA_DK), F32)],
        compiler_params=_cparams(("arbitrary", "arbitrary")),
        name="gla_scan",
    )(p, p, p, p, p, p, p, p, wa, ba)


def _finish_kernel(u_ref, ys_f_ref, ys_b_ref, d_ref, wglu_ref, bglu_ref,
                   og_f_ref, og_b_ref, gate_ref, gn_ref,
                   x0_ref, conv_ref, s_ref, hb_ref, o_ref):
    u = u_ref[...]
    y = ys_f_ref[...] + ys_b_ref[...] + d_ref[...] * u
    zz = _gelu(y)
    s5 = zz * _sigmoid(_dot(zz.astype(BF16), wglu_ref[...]) + bglu_ref[...])
    o_ref[:, 0:D_S5] = s5.astype(BF16)

    gate = gate_ref[...]
    for h in range(GLA_HEADS):
        cols = slice(h * GLA_DV, (h + 1) * GLA_DV)
        o = og_f_ref[:, cols] + og_b_ref[:, cols]
        ms = jnp.mean(o * o, axis=-1, keepdims=True)
        on = o * lax.rsqrt(ms + EPS) * gn_ref[...]
        gt = gate[:, cols]
        o_ref[:, D_S5 + h * GLA_DV:D_S5 + (h + 1) * GLA_DV] = (on * (gt * _sigmoid(gt))).astype(BF16)

    s = s_ref[...]
    hy = x0_ref[...] * (conv_ref[...] + hb_ref[...] * s)
    o_ref[:, D_S5 + D_GLA_V:] = hy.astype(BF16)


def mixer_finish(p, ys_f, ys_b, s5_d, w_glu, b_glu, og_f, og_b, gn, x0, conv, s, hy_bias, *, tm):
    T = p.shape[0]
    row = lambda w, off=0: pl.BlockSpec((tm, w), lambda i: (i, off))
    const = lambda r, w: pl.BlockSpec((r, w), lambda i: (0, 0))
    return pl.pallas_call(
        _finish_kernel,
        grid=(T // tm,),
        in_specs=[
            row(D_S5, OFF_S5 // D_S5), row(D_S5), row(D_S5), const(1, D_S5), const(D_S5, D_S5), const(1, D_S5),
            row(D_GLA_V), row(D_GLA_V), row(D_GLA_V, OFF_GATE // D_GLA_V), const(1, GLA_DV),
            row(D_HY), row(D_HY), row(D_HY), const(1, D_HY),
        ],
        out_specs=pl.BlockSpec((tm, D_MODEL), lambda i: (i, 0)),
        out_shape=jax.ShapeDtypeStruct((T, D_MODEL), BF16),
        compiler_params=_cparams(("arbitrary",)),
        name="mixer_finish",
    )(p, ys_f, ys_b, s5_d.reshape(1, D_S5), w_glu, b_glu.reshape(1, D_S5),
      og_f, og_b, p, gn.reshape(1, GLA_DV), x0, conv, s, hy_bias.reshape(1, D_HY))


def _peer_kernel(x_ref, u_ref, vt_ref, s2_ref, e2_ref, s1_ref, e1_ref, tau_ref, o_ref,
                 act_ref, p_ref, *, n_i1, tm, cw):
    j = pl.program_id(1)

    @pl.when(j == 0)
    def _():
        o_ref[...] = jnp.zeros_like(o_ref)

    act_ref[...] = lax.dot_general(u_ref[...], x_ref[...], NT_DIMS, preferred_element_type=F32)
    K = PEER_KEYS
    sub = SUBLANE // n_i1
    base = (j % sub) * n_i1 if sub > 1 else 0
    for a in range(n_i1):
        r1 = pl.ds(base + a, 1)
        for cb in range(tm // cw):
            cols = slice(cb * cw, (cb + 1) * cw)
            w = jnp.zeros((K, cw), F32)
            for h in range(PEER_HEADS):
                tot = s1_ref[h, r1, cols] + s2_ref[h, :, cols]
                val = e1_ref[h, r1, cols] * e2_ref[h, :, cols]
                w = w + jnp.where(tot >= tau_ref[h, :, cols], val, 0.0)
            act = act_ref[a * K:(a + 1) * K, cols]
            p_ref[a * K:(a + 1) * K, cols] = (w * _gelu(act)).astype(BF16)
    o_ref[...] += _dot(vt_ref[...], p_ref[...])


def peer_dense(xn, u_tab, vt_tab, s1t, e1t, s2t, e2t, taut, *, tm, te, cw=256):
    T, D = xn.shape
    E = u_tab.shape[0]
    H, K = PEER_HEADS, PEER_KEYS
    n_i1 = te // K
    sub = SUBLANE // n_i1
    i1_spec = pl.BlockSpec((H, SUBLANE, tm), lambda i, j: (0, j // sub, i))
    once = pl.Buffered(1)
    tab_spec = pl.BlockSpec((H, K, tm), lambda i, j: (0, 0, i), pipeline_mode=once)
    return pl.pallas_call(
        functools.partial(_peer_kernel, n_i1=n_i1, tm=tm, cw=cw),
        grid=(T // tm, E // te),
        in_specs=[
            pl.BlockSpec((tm, D), lambda i, j: (i, 0), pipeline_mode=once),
            pl.BlockSpec((te, D), lambda i, j: (j, 0)),
            pl.BlockSpec((D, te), lambda i, j: (0, j)),
            tab_spec, tab_spec,
            i1_spec, i1_spec,
            pl.BlockSpec((H, 1, tm), lambda i, j: (0, 0, i), pipeline_mode=once),
        ],
        out_specs=pl.BlockSpec((D, tm), lambda i, j: (0, i)),
        out_shape=jax.ShapeDtypeStruct((D, T), F32),
        scratch_shapes=[pltpu.VMEM((te, tm), F32), pltpu.VMEM((te, tm), BF16)],
        compiler_params=_cparams(("arbitrary", "arbitrary")),
        name="peer_dense",
    )(xn, u_tab, vt_tab, s2t, e2t, s1t, e1t, taut)


NEG_BIG = -3.0e38
N_CAND = PEER_TOPK + 8 * 7 + 8


def _peer_prep_kernel(q_ref, k_ref, s1_ref, e1_ref, s2_ref, e2_ref, tau_ref, top_ref, cand_ref):
    R = PEER_TOPK
    half_w = PEER_DQ // 2

    def nt(a, b):
        return lax.dot_general(a, b, NT_DIMS, preferred_element_type=F32)

    def scores(half):
        qh = q_ref[:, half * half_w:(half + 1) * half_w]
        q_hi = qh.astype(BF16)
        q_lo = (qh - q_hi.astype(F32)).astype(BF16)
        kk = k_ref[half]
        k_hi = kk.astype(BF16)
        k_lo = (kk - k_hi.astype(F32)).astype(BF16)
        return nt(k_hi, q_hi) + (nt(k_hi, q_lo) + nt(k_lo, q_hi))

    def sorted_top(s, slot):
        work = s
        for r in range(R):
            m = jnp.max(work, axis=0, keepdims=True)
            top_ref[slot, r:r + 1, :] = m
            work = jnp.where(work == m, NEG_BIG, work)

    s1 = scores(0)
    s2 = scores(1)
    sorted_top(s1, 0)
    sorted_top(s2, 1)
    a = top_ref[0]
    b = top_ref[1]
    cand_ref[0:R, :] = a[0:1, :] + b
    for i in range(1, 8):
        cand_ref[R + 8 * (i - 1):R + 8 * i, :] = a[i:i + 1, :] + b[0:8, :]
    cand_ref[R + 56:R + 64, :] = a[8:16, :] + b[0:1, :]
    work = cand_ref[...]
    m0 = jnp.max(work, axis=0, keepdims=True)
    m = m0
    z = jnp.ones_like(m0)
    for r in range(1, R):
        work = jnp.where(work == m, NEG_BIG, work)
        m = jnp.max(work, axis=0, keepdims=True)
        z = z + jnp.exp(m - m0)
    tau_ref[0] = m
    s1_ref[0] = s1
    s2_ref[0] = s2
    e1_ref[0] = jnp.exp(s1 - a[0:1, :]) * (1.0 / z)
    e2_ref[0] = jnp.exp(s2 - b[0:1, :])


def peer_prep(q, keys, *, tm):
    T = q.shape[0]
    H, K = PEER_HEADS, PEER_KEYS
    tab = jax.ShapeDtypeStruct((H, K, T), F32)
    tab_spec = pl.BlockSpec((1, K, tm), lambda i, h: (h, 0, i))
    return pl.pallas_call(
        _peer_prep_kernel,
        grid=(T // tm, H),
        in_specs=[
            pl.BlockSpec((tm, PEER_DQ), lambda i, h: (i, h)),
            pl.BlockSpec((None, 2, K, PEER_DQ // 2), lambda i, h: (h, 0, 0, 0)),
        ],
        out_specs=[tab_spec, tab_spec, tab_spec, tab_spec, pl.BlockSpec((1, 1, tm), lambda i, h: (h, 0, i))],
        out_shape=[tab, tab, tab, tab, jax.ShapeDtypeStruct((H, 1, T), F32)],
        scratch_shapes=[pltpu.VMEM((2, PEER_TOPK, tm), F32), pltpu.VMEM((N_CAND, tm), F32)],
        compiler_params=_cparams(("arbitrary", "arbitrary")),
        name="peer_prep",
    )(q, keys)


def _transpose_gated_res_kernel(h_ref, ft_ref, gate_ref, o_ref, *, ctx_len, tm):
    i = pl.program_id(0)
    row = i * tm + lax.broadcasted_iota(jnp.int32, (tm, 1), 0)
    gate = jnp.where(row < ctx_len, gate_ref[0:1, :], gate_ref[1:2, :])
    o_ref[...] = h_ref[...] + gate * ft_ref[...].T


def transpose_gated_res(h, ft, gates, *, ctx_len, tm):
    T, D = h.shape
    return pl.pallas_call(
        functools.partial(_transpose_gated_res_kernel, ctx_len=ctx_len, tm=tm),
        grid=(T // tm,),
        in_specs=[
            pl.BlockSpec((tm, D), lambda i: (i, 0)),
            pl.BlockSpec((D, tm), lambda i: (0, i)),
            pl.BlockSpec((2, D), lambda i: (0, 0)),
        ],
        out_specs=pl.BlockSpec((tm, D), lambda i: (i, 0)),
        out_shape=jax.ShapeDtypeStruct((T, D), F32),
        compiler_params=_cparams(("arbitrary",)),
        name="transpose_gated_res",
    )(h, ft, gates)


def _rmsnorm_kernel(x_ref, g_ref, o_ref):
    x = x_ref[...]
    ms = jnp.mean(x * x, axis=-1, keepdims=True)
    o_ref[...] = x * lax.rsqrt(ms + EPS) * g_ref[...]


def rmsnorm_rows(x, g, *, tm):
    T, D = x.shape
    return pl.pallas_call(
        _rmsnorm_kernel,
        grid=(T // tm,),
        in_specs=[pl.BlockSpec((tm, D), lambda i: (i, 0)), pl.BlockSpec((1, D), lambda i: (0, 0))],
        out_specs=pl.BlockSpec((tm, D), lambda i: (i, 0)),
        out_shape=jax.ShapeDtypeStruct((T, D), F32),
        compiler_params=_cparams(("arbitrary",)),
        name="rmsnorm_rows",
    )(x, g.reshape(1, D))


def grid_sincos(n_tokens, dim):
    rows = n_tokens // GRID_W
    n_freq = dim // 4
    omega = 1.0 / (POS_BASE ** (jnp.arange(n_freq, dtype=F32) / n_freq))

    def enc(count):
        a = jnp.arange(count).astype(F32)[:, None] * omega[None, :]
        return jnp.concatenate([jnp.sin(a), jnp.cos(a)], axis=-1)

    row_code = jnp.broadcast_to(enc(rows)[:, None, :], (rows, GRID_W, dim // 2))
    col_code = jnp.broadcast_to(enc(GRID_W)[None, :, :], (rows, GRID_W, dim // 2))
    return jnp.concatenate([row_code, col_code], axis=-1).reshape(n_tokens, dim)


def _split_bf16(a):
    hi = a.astype(BF16)
    return hi, (a - hi.astype(F32)).astype(BF16)


def _dot3s(a, b_hi, b_lo):
    a_hi, a_lo = _split_bf16(a)
    return _dot(a_hi, b_hi) + (_dot(a_hi, b_lo) + _dot(a_lo, b_hi))


def _dot3(a, b):
    return _dot3s(a, *_split_bf16(b))


def _adaln_kernel(c_ref, w_ref, b_ref, o_ref):
    c = c_ref[...]
    o_ref[...] = _dot3(c * _sigmoid(c), w_ref[...]) + b_ref[...]


def adaln(cond, w, b, *, tn=1536):
    R, D = cond.shape
    N = w.shape[1]
    return pl.pallas_call(
        _adaln_kernel,
        grid=(N // tn,),
        in_specs=[pl.BlockSpec((R, D), lambda j: (0, 0)),
                  pl.BlockSpec((D, tn), lambda j: (0, j)),
                  pl.BlockSpec((1, tn), lambda j: (0, j))],
        out_specs=pl.BlockSpec((R, tn), lambda j: (0, j)),
        out_shape=jax.ShapeDtypeStruct((R, N), F32),
        compiler_params=_cparams(("arbitrary",)),
        name="adaln",
    )(cond, w, b.reshape(1, N))


def _hyena_pre_kernel(x0_ref, x1_ref, v_ref, w_ref, b_ref, x0o_ref, s_ref, *, ctx_len):
    T = x0_ref.shape[0]
    row = lax.broadcasted_iota(jnp.int32, (T, 1), 0)
    first = jnp.logical_or(row == 0, row == ctx_len)
    last = jnp.logical_or(row == ctx_len - 1, row == T - 1)

    def conv(ref, part):
        x = ref[...]
        prev = jnp.where(first, 0.0, pltpu.roll(x, 1, 0))
        nxt = jnp.where(last, 0.0, pltpu.roll(x, T - 1, 0))
        w = lambda tap: w_ref[tap, part:part + 1, :]
        return w(0) * prev + w(1) * x + w(2) * nxt + b_ref[part:part + 1, :]

    x0o_ref[...] = conv(x0_ref, 0)
    s_ref[...] = conv(x1_ref, 1) * conv(v_ref, 2)


def hyena_pre(p, conv_w, conv_b, *, ctx_len):
    T = p.shape[0]
    nb = D_HY // LANE
    col = lambda part: pl.BlockSpec((T, LANE), lambda c: (0, OFF_HY // LANE + part * nb + c))
    out = jax.ShapeDtypeStruct((T, D_HY), F32)
    return pl.pallas_call(
        functools.partial(_hyena_pre_kernel, ctx_len=ctx_len),
        grid=(nb,),
        in_specs=[col(0), col(1), col(2),
                  pl.BlockSpec((HY_SHORT, 3, LANE), lambda c: (0, 0, c)),
                  pl.BlockSpec((3, LANE), lambda c: (0, c))],
        out_specs=[pl.BlockSpec((T, LANE), lambda c: (0, c))] * 2,
        out_shape=[out, out],
        compiler_params=_cparams(("arbitrary",)),
        name="hyena_pre",
    )(p, p, p, conv_w.reshape(HY_SHORT, 3, D_HY), conv_b.reshape(3, D_HY))


def _hyena_filter_kernel(band_ref, w1_ref, b1_ref, w2_ref, b2_ref, w3_ref, fr_ref, dec_ref, o_ref, *, n, tr):
    i = pl.program_id(0)
    ri = (i * tr + lax.broadcasted_iota(jnp.int32, (tr, 1), 0)).astype(F32)
    t = ri * (1.0 / (n - 1))
    w = ri * (2.0 * math.pi / n)
    lane = lax.broadcasted_iota(jnp.int32, (1, LANE), 1)
    arg = w * band_ref[...]
    z = jnp.where(lane == 0, t,
                  jnp.where(lane <= HY_BANDS, jnp.cos(arg),
                            jnp.where(lane <= 2 * HY_BANDS, -jnp.sin(arg), 0.0)))
    h = jnp.sin(fr_ref[0:1, :] * (_dot3(z, w1_ref[...]) + b1_ref[...]))
    h = jnp.sin(fr_ref[1:2, :] * (_dot3(h, w2_ref[...]) + b2_ref[...]))
    h = _dot3(h, w3_ref[...]) * jnp.exp(-t * dec_ref[...])
    col = lax.broadcasted_iota(jnp.int32, (1, 2 * D_HY), 1)
    o_ref[...] = jnp.where(jnp.logical_and(ri == 0.0, col >= D_HY), 0.0, h)


def hyena_filter(n, w1, b1, w2, b2, w3, freq, decay, *, tr=256):
    bands = jnp.linspace(1e-4, HY_BANDS - 1, HY_BANDS, dtype=F32)
    band_row = jnp.zeros((1, LANE), F32).at[0, 1:1 + HY_BANDS].set(bands).at[0, 1 + HY_BANDS:1 + 2 * HY_BANDS].set(bands)
    w1p = jnp.zeros((LANE, HY_HIDDEN), F32).at[:HY_FEAT].set(w1)
    const = lambda a: pl.BlockSpec(a.shape, lambda i: (0,) * a.ndim)
    args = (band_row, w1p, b1.reshape(1, -1), w2, b2.reshape(1, -1), w3, freq,
            jnp.abs(decay).reshape(1, 2 * D_HY))
    return pl.pallas_call(
        functools.partial(_hyena_filter_kernel, n=n, tr=tr),
        grid=(n // tr,),
        in_specs=[const(a) for a in args],
        out_specs=pl.BlockSpec((tr, 2 * D_HY), lambda i: (i, 0)),
        out_shape=jax.ShapeDtypeStruct((n, 2 * D_HY), F32),
        compiler_params=_cparams(("arbitrary",)),
        name="hyena_filter",
    )(*args)


FFT_R = 128
FFT_N = FFT_R * FFT_R
FFT_CH = 32


def _dft_tables():
    r = np.arange(FFT_R)
    ang = 2.0 * np.pi * np.outer(r, r) / FFT_R
    c, s = np.cos(ang), np.sin(ang)
    angt = 2.0 * np.pi * np.outer(r, r) / FFT_N
    ct, st = np.cos(angt), np.sin(angt)

    def split(m):
        m = jnp.asarray(m, F32)
        hi = m.astype(BF16)
        return jnp.stack([hi, (m - hi.astype(F32)).astype(BF16)])

    fwd_a = split(np.concatenate([c, -s], axis=1))
    fwd_b = split(np.block([[c, -s], [s, c]]))
    inv_b = split(np.block([[c, s], [-s, c]]) / FFT_N)
    inv_a = split(np.concatenate([c, -s], axis=0))
    tw = (jnp.asarray(ct, F32), jnp.asarray(-st, F32))
    twc = (jnp.asarray(ct, F32), jnp.asarray(st, F32))
    return fwd_a, fwd_b, inv_b, inv_a, tw, twc


def _twiddle_transpose(y, twr, twi, dst_ref, nc):
    y = y.reshape(nc, FFT_R, 2 * FFT_R)
    yr, yi = y[:, :, :FFT_R], y[:, :, FFT_R:]
    zr = yr * twr - yi * twi
    zi = yr * twi + yi * twr
    for c in range(nc):
        dst_ref[c * FFT_R:(c + 1) * FFT_R, 0:FFT_R] = zr[c].T
        dst_ref[c * FFT_R:(c + 1) * FFT_R, FFT_R:2 * FFT_R] = zi[c].T


def _hyena_fwd_kernel(x_ref, fa_ref, twr_ref, twi_ref, fb_ref, o_ref, l_ref, yt_ref, *, nc):
    j = pl.program_id(1)
    half = FFT_R // 2

    @pl.when(j == 0)
    def _():
        zeros = jnp.zeros((half, LANE), F32)

        def body(b, carry):
            t = jnp.concatenate([x_ref[pl.ds(b, half, stride=FFT_R), :], zeros], axis=0)
            l_ref[pl.ds(b, LANE, stride=FFT_R), :] = t.T
            return carry

        lax.fori_loop(0, FFT_R, body, 0, unroll=8)

    rows = nc * FFT_R
    lc = l_ref[pl.ds(pl.multiple_of(j * rows, rows), rows), :]
    y = _dot3s(lc, fa_ref[0], fa_ref[1])
    _twiddle_transpose(y, twr_ref[...], twi_ref[...], yt_ref, nc)
    o_ref[...] = _dot3s(yt_ref[...], fb_ref[0], fb_ref[1])


def hyena_fwd_dft(x, tables):
    n, C = x.shape
    assert n * 2 == FFT_N and C % LANE == 0
    fwd_a, fwd_b, _, _, (twr, twi), _ = tables
    nc = FFT_CH
    steps = LANE // nc
    const = lambda a: pl.BlockSpec(a.shape, lambda cb, j: (0,) * a.ndim)
    return pl.pallas_call(
        functools.partial(_hyena_fwd_kernel, nc=nc),
        grid=(C // LANE, steps),
        in_specs=[pl.BlockSpec((n, LANE), lambda cb, j: (0, cb)), const(fwd_a), const(twr), const(twi), const(fwd_b)],
        out_specs=pl.BlockSpec((nc * FFT_R, 2 * FFT_R), lambda cb, j: (cb * steps + j, 0)),
        out_shape=jax.ShapeDtypeStruct((C * FFT_R, 2 * FFT_R), F32),
        scratch_shapes=[pltpu.VMEM((LANE * FFT_R, LANE), F32), pltpu.VMEM((nc * FFT_R, 2 * FFT_R), F32)],
        compiler_params=_cparams(("arbitrary", "arbitrary")),
        name="hyena_fwd_dft",
    )(x, fwd_a, twr, twi, fwd_b)


def _hyena_inv_kernel(s_ref, kf_ref, kb_ref, gb_ref, twr_ref, twi_ref, ga_ref, o_ref, lr_ref, at_ref, *, nc, steps):
    j = pl.program_id(1)
    R = FFT_R
    s, kf, kb = s_ref[...], kf_ref[...], kb_ref[...]
    sr, si = s[:, :R], s[:, R:]
    kr = kf[:, :R] + kb[:, :R]
    ki = kf[:, R:] - kb[:, R:]
    p = jnp.concatenate([sr * kr - si * ki, sr * ki + si * kr], axis=1)
    a = _dot3s(p, gb_ref[0], gb_ref[1])
    _twiddle_transpose(a, twr_ref[...], twi_ref[...], at_ref, nc)
    rows = nc * R
    lr_ref[pl.ds(pl.multiple_of(j * rows, rows), rows), :] = _dot3s(at_ref[...], ga_ref[0], ga_ref[1])

    @pl.when(j == steps - 1)
    def _():
        def body(b, carry):
            t = lr_ref[pl.ds(b, LANE, stride=R), :]
            o_ref[pl.ds(b, R // 2, stride=R), :] = t.T[0:R // 2, :]
            return carry

        lax.fori_loop(0, R, body, 0, unroll=8)


def hyena_inv_dft(s_f, k_f, tables):
    C = s_f.shape[0] // FFT_R
    _, _, inv_b, inv_a, _, (twr, twi) = tables
    nc = FFT_CH
    steps = LANE // nc
    kb_off = C // nc
    const = lambda a: pl.BlockSpec(a.shape, lambda cb, j: (0,) * a.ndim)
    blk = lambda off: pl.BlockSpec((nc * FFT_R, 2 * FFT_R), lambda cb, j: (off + cb * steps + j, 0))
    return pl.pallas_call(
        functools.partial(_hyena_inv_kernel, nc=nc, steps=steps),
        grid=(C // LANE, steps),
        in_specs=[blk(0), blk(0), blk(kb_off), const(inv_b), const(twr), const(twi), const(inv_a)],
        out_specs=pl.BlockSpec((FFT_N // 2, LANE), lambda cb, j: (0, cb)),
        out_shape=jax.ShapeDtypeStruct((FFT_N // 2, C), F32),
        scratch_shapes=[pltpu.VMEM((LANE * FFT_R, LANE), F32), pltpu.VMEM((nc * FFT_R, 2 * FFT_R), F32)],
        compiler_params=_cparams(("arbitrary", "arbitrary")),
        name="hyena_inv_dft",
    )(s_f, k_f, k_f, inv_b, twr, twi, inv_a)


def _hyena_ctx_kernel(s_ref, k_ref, fw_ref, iv_ref, o_ref):
    n = s_ref.shape[0]
    N = 2 * n
    xs = _dot3(fw_ref[...], s_ref[...])
    xk = _dot3(fw_ref[...], k_ref[...])
    sr, si = xs[:N], xs[N:]
    kr = xk[:N, :D_HY] + xk[:N, D_HY:]
    ki = xk[N:, :D_HY] - xk[N:, D_HY:]
    p = jnp.concatenate([sr * kr - si * ki, sr * ki + si * kr], axis=0)
    o_ref[...] = _dot3(iv_ref[...], p)


def hyena_ctx_conv(s_c, k_c):
    n = s_c.shape[0]
    N = 2 * n
    ang = 2.0 * np.pi * np.outer(np.arange(N), np.arange(N)) / N
    c, s = np.cos(ang), np.sin(ang)
    fw = jnp.asarray(np.concatenate([c[:, :n], -s[:, :n]], axis=0), F32)
    iv = jnp.asarray(np.concatenate([c[:n, :], -s[:n, :]], axis=1) / N, F32)
    full = lambda a: pl.BlockSpec(a.shape, lambda i: (0,) * a.ndim)
    return pl.pallas_call(
        _hyena_ctx_kernel,
        grid=(1,),
        in_specs=[full(s_c), full(k_c), full(fw), full(iv)],
        out_specs=pl.BlockSpec((n, D_HY), lambda i: (0, 0)),
        out_shape=jax.ShapeDtypeStruct((n, D_HY), F32),
        compiler_params=_cparams(("arbitrary",)),
        name="hyena_ctx_conv",
    )(s_c, k_c, fw, iv)


def hyena_mixer_parts(p, conv_w, conv_b, filt, tables, *, ctx_len):
    T = p.shape[0]
    x0, s = hyena_pre(p, conv_w, conv_b, ctx_len=ctx_len)
    k_lat = hyena_filter(T - ctx_len, *filt)
    k_ctx = hyena_filter(ctx_len, *filt)
    conv_l = hyena_inv_dft(hyena_fwd_dft(s[ctx_len:], tables), hyena_fwd_dft(k_lat, tables), tables)
    conv_c = hyena_ctx_conv(s[:ctx_len], k_ctx)
    return x0, jnp.concatenate([conv_c, conv_l], axis=0), s


def _pack_w_in(w_in_l):
    widths = (D_S5, D_GLA_K, D_GLA_K, D_GLA_V, D_GLA_V, 2 * GLA_RANK, 3 * D_HY)
    offs = [0]
    for wd in widths:
        offs.append(offs[-1] + wd)
    s5, q, k, v, gate, alpha, hy = (w_in_l[:, offs[i]:offs[i + 1]] for i in range(7))
    pad = jnp.zeros((w_in_l.shape[0], IN_PACKED - OFF_ALPHA - 2 * GLA_RANK), w_in_l.dtype)
    return jnp.concatenate([v, gate, s5, q, k, hy, alpha, pad], axis=1).astype(BF16)


def kernel(x, c, ctx, c_ctx, w_ada, b_ada, g_norm1, g_norm2, w_in, s5_a_re, s5_a_im, s5_log_step, s5_b_re, s5_b_im, s5_c_re, s5_c_im, s5_d, s5_w_glu, s5_b_glu, gla_w_alpha, gla_b_alpha, gla_g_norm, hy_conv_w, hy_conv_b, hy_f_w1, hy_f_b1, hy_f_w2, hy_f_b2, hy_f_w3, hy_f_freq, hy_decay, hy_bias, w_out, peer_w_q, peer_keys, peer_u, peer_v, g_final):
    L = x.shape[1]
    Lc = ctx.shape[1]
    T = L + Lc
    TM = 768
    h = jnp.concatenate([ctx[0], x[0] + grid_sincos(L, D_MODEL)], axis=0)
    cond = jnp.zeros((SUBLANE, D_MODEL), F32).at[0].set(c_ctx).at[1].set(c[0])
    tables = _dft_tables()

    for l in range(DEPTH):
        m = adaln(cond, w_ada[l], b_ada[l])[0:2]
        sh1, sc1, gt1, sh2, sc2, gt2 = jnp.split(m, 6, axis=-1)
        mod1 = jnp.stack([sh1[0], sc1[0], sh1[1], sc1[1]], axis=0)
        mod2 = jnp.stack([sh2[0], sc2[0], sh2[1], sc2[1]], axis=0)

        p = norm_mod_matmul(h, g_norm1[l], mod1, _pack_w_in(w_in[l]), ctx_len=Lc, tm=TM, tn=768)

        bm, cm, pw = s5_prepare(s5_a_re[l], s5_a_im[l], s5_log_step[l], s5_b_re[l], s5_b_im[l],
                                s5_c_re[l], s5_c_im[l])
        ys_f, ys_b = s5_scan(p, bm, cm, pw, ctx_len=Lc)

        wa = jnp.zeros((2, LANE, D_GLA_K), F32)
        wa = wa.at[0, 0:GLA_RANK].set(gla_w_alpha[l, 0]).at[1, GLA_RANK:2 * GLA_RANK].set(gla_w_alpha[l, 1])
        wa = wa.astype(BF16)
        og_f, og_b = gla_scan(p, wa, gla_b_alpha[l], ctx_len=Lc)

        filt = (hy_f_w1[l], hy_f_b1[l], hy_f_w2[l], hy_f_b2[l], hy_f_w3[l], hy_f_freq[l], hy_decay[l])
        x0, conv, s = hyena_mixer_parts(p, hy_conv_w[l], hy_conv_b[l], filt, tables, ctx_len=Lc)

        mix = mixer_finish(p, ys_f, ys_b, s5_d[l], s5_w_glu[l].astype(BF16), s5_b_glu[l],
                           og_f, og_b, gla_g_norm[l], x0, conv, s, hy_bias[l], tm=256)
        h = matmul_gated_res(mix, w_out[l].astype(BF16), h, gt1, ctx_len=Lc, tm=TM, tn=1024)

        q, xn = norm_mod_matmul(h, g_norm2[l], mod2, peer_w_q[l].astype(BF16),
                                ctx_len=Lc, tm=TM, tn=1024, emit_xn=True)
        s1t, e1t, s2t, e2t, taut = peer_prep(q, peer_keys[l], tm=TM)
        ft = peer_dense(xn, peer_u[l].astype(BF16), peer_v[l].astype(BF16).T,
                        s1t, e1t, s2t, e2t, taut, tm=TM, te=1024)
        h = transpose_gated_res(h, ft, gt2, ctx_len=Lc, tm=256)

    out = rmsnorm_rows(h[Lc:], g_final, tm=512)
    return out[None]
```

```python
import functools
import math

import jax
import jax.numpy as jnp
import numpy as np
from jax import lax
from jax.experimental import pallas as pl
from jax.experimental.pallas import tpu as pltpu

F32 = jnp.float32
BF16 = jnp.bfloat16

D_MODEL = 2048
DEPTH = 4
GRID_W = 64
EPS = 1e-6
POS_BASE = 10000.0

D_S5 = D_MODEL // 4
S5_GROUP = 16
S5_GROUPS = D_S5 // S5_GROUP
S5_STATE = 64
S5_BLK_GROUPS = 8
S5_BLK_CH = S5_BLK_GROUPS * S5_GROUP
S5_BLK_ST = S5_BLK_GROUPS * S5_STATE
S5_NBLK = S5_GROUPS // S5_BLK_GROUPS

GLA_HEADS = 4
D_GLA_K = D_MODEL // 4
D_GLA_V = D_MODEL // 2
GLA_DK = D_GLA_K // GLA_HEADS
GLA_DV = D_GLA_V // GLA_HEADS
GLA_RANK = 16
GLA_TAU = 16.0
GLA_CHUNK = 64

D_HY = D_MODEL // 4
HY_SHORT = 3
HY_BANDS = 16
HY_FEAT = 1 + 2 * HY_BANDS
HY_HIDDEN = 64
HY_TARGET = 1e-2
HY_MIN_DECAY = -math.log(HY_TARGET) / 1.5
HY_MAX_DECAY = -math.log(HY_TARGET) / 0.3

PEER_HEADS = 8
PEER_KEYS = 128
PEER_EXPERTS = PEER_KEYS * PEER_KEYS
PEER_DQ = 256
PEER_TOPK = 16

OFF_V = 0
OFF_GATE = OFF_V + D_GLA_V
OFF_S5 = OFF_GATE + D_GLA_V
OFF_Q = OFF_S5 + D_S5
OFF_K = OFF_Q + D_GLA_K
OFF_HY = OFF_K + D_GLA_K
OFF_ALPHA = OFF_HY + 3 * D_HY
LANE = 128
SUBLANE = 8
IN_PACKED = OFF_ALPHA + 2 * LANE

SEQ_TILE = 256
VMEM_LIMIT = 56 * 1024 * 1024

NT_DIMS = (((1,), (1,)), ((), ()))
TN_DIMS = (((0,), (0,)), ((), ()))


def _cparams(sem):
    return pltpu.CompilerParams(dimension_semantics=sem, vmem_limit_bytes=VMEM_LIMIT)


def _dot(a, b):
    return jnp.dot(a, b, preferred_element_type=F32)


def _gelu(x):
    k1 = -2.0 * math.sqrt(2.0 / math.pi) * math.log2(math.e)
    k2 = 0.044715 * k1
    return x / (1.0 + jnp.exp2(x * (k1 + k2 * (x * x))))


def _sigmoid(x):
    return 1.0 / (1.0 + jnp.exp(-x))


def _norm_mod_matmul_kernel(x_ref, g_ref, mod_ref, w_ref, o_ref, *rest, ctx_len, tm, emit_xn):
    if emit_xn:
        xo_ref, xn_ref = rest
    else:
        (xn_ref,) = rest
    i = pl.program_id(0)
    j = pl.program_id(1)

    @pl.when(j == 0)
    def _():
        x = x_ref[...]
        ms = jnp.mean(x * x, axis=-1, keepdims=True)
        y = x * lax.rsqrt(ms + EPS) * g_ref[...]
        row = i * tm + lax.broadcasted_iota(jnp.int32, (tm, 1), 0)
        is_ctx = row < ctx_len
        shift = jnp.where(is_ctx, mod_ref[0:1, :], mod_ref[2:3, :])
        scale = jnp.where(is_ctx, mod_ref[1:2, :], mod_ref[3:4, :])
        xn = (y * (1.0 + scale) + shift).astype(BF16)
        xn_ref[...] = xn
        if emit_xn:
            xo_ref[...] = xn

    o_ref[...] = _dot(xn_ref[...], w_ref[...])


def norm_mod_matmul(x, g, mod, w, *, ctx_len, tm, tn, emit_xn=False):
    T, D = x.shape
    N = w.shape[1]
    out_shape = [jax.ShapeDtypeStruct((T, N), F32)]
    out_specs = [pl.BlockSpec((tm, tn), lambda i, j: (i, j))]
    if emit_xn:
        out_shape.append(jax.ShapeDtypeStruct((T, D), BF16))
        out_specs.append(pl.BlockSpec((tm, D), lambda i, j: (i, 0)))
    res = pl.pallas_call(
        functools.partial(_norm_mod_matmul_kernel, ctx_len=ctx_len, tm=tm, emit_xn=emit_xn),
        grid=(T // tm, N // tn),
        in_specs=[
            pl.BlockSpec((tm, D), lambda i, j: (i, 0)),
            pl.BlockSpec((1, D), lambda i, j: (0, 0)),
            pl.BlockSpec((4, D), lambda i, j: (0, 0)),
            pl.BlockSpec((D, tn), lambda i, j: (0, j)),
        ],
        out_specs=out_specs,
        out_shape=out_shape,
        scratch_shapes=[pltpu.VMEM((tm, D), BF16)],
        compiler_params=_cparams(("arbitrary", "arbitrary")),
        name="norm_mod_matmul",
    )(x, g.reshape(1, D), mod, w)
    return res if emit_xn else res[0]


def _matmul_gated_res_kernel(a_ref, w_ref, r_ref, gate_ref, o_ref, *, ctx_len, tm):
    i = pl.program_id(0)
    row = i * tm + lax.broadcasted_iota(jnp.int32, (tm, 1), 0)
    gate = jnp.where(row < ctx_len, gate_ref[0:1, :], gate_ref[1:2, :])
    o_ref[...] = r_ref[...] + gate * _dot(a_ref[...], w_ref[...])


def matmul_gated_res(a, w, res, gates, *, ctx_len, tm, tn):
    T, K = a.shape
    N = w.shape[1]
    return pl.pallas_call(
        functools.partial(_matmul_gated_res_kernel, ctx_len=ctx_len, tm=tm),
        grid=(T // tm, N // tn),
        in_specs=[
            pl.BlockSpec((tm, K), lambda i, j: (i, 0)),
            pl.BlockSpec((K, tn), lambda i, j: (0, j)),
            pl.BlockSpec((tm, tn), lambda i, j: (i, j)),
            pl.BlockSpec((2, tn), lambda i, j: (0, j)),
        ],
        out_specs=pl.BlockSpec((tm, tn), lambda i, j: (i, j)),
        out_shape=jax.ShapeDtypeStruct((T, N), F32),
        compiler_params=_cparams(("arbitrary", "arbitrary")),
        name="matmul_gated_res",
    )(a, w, res, gates)


def _time_tile(t, n_tiles, n_ctx_tiles, rev):
    if not rev:
        return t
    return jnp.where(t < n_ctx_tiles, n_ctx_tiles - 1 - t, n_tiles - 1 - (t - n_ctx_tiles))


def _s5_kernel(uf_ref, ub_ref, bm_ref, cm_ref, pw_ref, yf_ref, yb_ref, h_ref, c_ref, *, lt):
    t = pl.program_id(1)
    ns = S5_BLK_ST
    u_refs = (uf_ref, ub_ref)
    y_refs = (yf_ref, yb_ref)

    @pl.when(t == 0)
    def _():
        c_ref[...] = jnp.zeros_like(c_ref)

    for d in range(2):
        bu = _dot(u_refs[d][...].astype(BF16), bm_ref[d])
        h_ref[d, 0] = bu[:, :ns]
        h_ref[d, 1] = bu[:, ns:]

    n_grp = lt // 8

    def group_update(d, g, c_re, c_im):
        rev = d == 1
        gi = (n_grp - 1 - g) if rev else g
        rows = pl.ds(pl.multiple_of(gi * 8, 8), 8)
        a_re = h_ref[d, 0, rows, :]
        a_im = h_ref[d, 1, rows, :]
        for k, s in enumerate((1, 2, 4)):
            l_re = pw_ref[d, 8 + 8 * k:16 + 8 * k, :ns]
            l_im = pw_ref[d, 8 + 8 * k:16 + 8 * k, ns:]
            shift = (8 - s) if rev else s
            s_re = pltpu.roll(a_re, shift, 0)
            s_im = pltpu.roll(a_im, shift, 0)
            a_re = a_re + (l_re * s_re - l_im * s_im)
            a_im = a_im + (l_re * s_im + l_im * s_re)
        p_re = pw_ref[d, 0:8, :ns]
        p_im = pw_ref[d, 0:8, ns:]
        a_re = a_re + (p_re * c_re - p_im * c_im)
        a_im = a_im + (p_re * c_im + p_im * c_re)
        h_ref[d, 0, rows, :] = a_re
        h_ref[d, 1, rows, :] = a_im
        edge = slice(0, 1) if rev else slice(7, 8)
        return a_re[edge, :], a_im[edge, :]

    def body(g, carry):
        f_re, f_im, b_re, b_im = carry
        f_re, f_im = group_update(0, g, f_re, f_im)
        b_re, b_im = group_update(1, g, b_re, b_im)
        return f_re, f_im, b_re, b_im

    carry = lax.fori_loop(0, n_grp, body, (c_ref[0, 0], c_ref[0, 1], c_ref[1, 0], c_ref[1, 1]), unroll=2)
    c_ref[0, 0], c_ref[0, 1], c_ref[1, 0], c_ref[1, 1] = carry

    for d in range(2):
        y_refs[d][...] = (_dot(h_ref[d, 0].astype(BF16), cm_ref[d, :ns, :])
                          + _dot(h_ref[d, 1].astype(BF16), cm_ref[d, ns:, :]))


def s5_scan(p, bm, cm, pw, *, ctx_len, lt=SEQ_TILE):
    T = p.shape[0]
    n_tiles = T // lt
    n_ctx = ctx_len // lt
    tf = functools.partial(_time_tile, n_tiles=n_tiles, n_ctx_tiles=n_ctx, rev=False)
    tb = functools.partial(_time_tile, n_tiles=n_tiles, n_ctx_tiles=n_ctx, rev=True)
    ucol = OFF_S5 // S5_BLK_CH
    out = jax.ShapeDtypeStruct((T, D_S5), F32)
    return pl.pallas_call(
        functools.partial(_s5_kernel, lt=lt),
        grid=(S5_NBLK, n_tiles),
        in_specs=[
            pl.BlockSpec((lt, S5_BLK_CH), lambda b, t: (tf(t), ucol + b)),
            pl.BlockSpec((lt, S5_BLK_CH), lambda b, t: (tb(t), ucol + b)),
            pl.BlockSpec((2, None, S5_BLK_CH, 2 * S5_BLK_ST), lambda b, t: (0, b, 0, 0)),
            pl.BlockSpec((2, None, 2 * S5_BLK_ST, S5_BLK_CH), lambda b, t: (0, b, 0, 0)),
            pl.BlockSpec((2, None, 32, 2 * S5_BLK_ST), lambda b, t: (0, b, 0, 0)),
        ],
        out_specs=[pl.BlockSpec((lt, S5_BLK_CH), lambda b, t: (tf(t), b)),
                   pl.BlockSpec((lt, S5_BLK_CH), lambda b, t: (tb(t), b))],
        out_shape=[out, out],
        scratch_shapes=[pltpu.VMEM((2, 2, lt, S5_BLK_ST), F32), pltpu.VMEM((2, 2, 1, S5_BLK_ST), F32)],
        compiler_params=_cparams(("arbitrary", "arbitrary")),
        name="s5_scan",
    )(p, p, bm, cm, pw)


def _cmul(a, b):
    return a[0] * b[0] - a[1] * b[1], a[0] * b[1] + a[1] * b[0]


def s5_prepare(a_re, a_im, log_step, b_re, b_im, c_re, c_im):
    G, P, Cg = S5_GROUPS, S5_STATE, S5_GROUP
    dt = jnp.exp(log_step)[..., None]
    er = jnp.exp(a_re * dt)
    lam1 = (er * jnp.cos(a_im * dt), er * jnp.sin(a_im * dt))
    den = a_re * a_re + a_im * a_im
    xr, xi = lam1[0] - 1.0, lam1[1]
    coef = ((xr * a_re + xi * a_im) / den, (xi * a_re - xr * a_im) / den)
    bb_re = coef[0][..., None] * b_re - coef[1][..., None] * b_im
    bb_im = coef[0][..., None] * b_im + coef[1][..., None] * b_re
    pows = [lam1]
    for _ in range(7):
        pows.append(_cmul(pows[-1], lam1))

    eye = jnp.eye(S5_BLK_GROUPS, dtype=F32)

    def blockdiag_in(m):
        m = m.reshape(2, S5_NBLK, S5_BLK_GROUPS, P, Cg)
        return jnp.einsum('dbgpc,gh->dbgchp', m, eye).reshape(2, S5_NBLK, S5_BLK_CH, S5_BLK_ST)

    def blockdiag_out(m):
        m = m.reshape(2, S5_NBLK, S5_BLK_GROUPS, Cg, P)
        return jnp.einsum('dbgcp,gh->dbgphc', m, eye).reshape(2, S5_NBLK, S5_BLK_ST, S5_BLK_CH)

    bm = jnp.concatenate([blockdiag_in(bb_re), blockdiag_in(bb_im)], axis=-1).astype(BF16)
    cm = jnp.concatenate([blockdiag_out(c_re), blockdiag_out(-c_im)], axis=-2).astype(BF16)

    def lay(v):
        return v.reshape(2, S5_NBLK, S5_BLK_ST)

    def table(rev):
        power = lambda k: jnp.concatenate([lay(pows[k][0]), lay(pows[k][1])], axis=-1)
        rows = [power(7 - r if rev else r) for r in range(8)]
        zero = jnp.zeros_like(rows[0])
        for s in (1, 2, 4):
            for r in range(8):
                inside = (r < 8 - s) if rev else (r >= s)
                rows.append(power(s - 1) if inside else zero)
        return jnp.stack(rows, axis=2)

    return bm, cm, jnp.stack([table(False)[0], table(True)[1]])


def _log_sigmoid(z):
    return jnp.minimum(z, 0.0) - jnp.log(1.0 + jnp.exp(-jnp.abs(z)))


def _gla_kernel(qf_ref, kf_ref, vf_ref, af_ref, qb_ref, kb_ref, vb_ref, ab_ref, wa_ref, ba_ref,
                of_ref, ob_ref, st_ref, *, lt):
    t = pl.program_id(1)
    C = GLA_CHUNK

    @pl.when(t == 0)
    def _():
        st_ref[...] = jnp.zeros_like(st_ref)

    ri = lax.broadcasted_iota(jnp.int32, (C, C), 0)
    ci = lax.broadcasted_iota(jnp.int32, (C, C), 1)
    refs = ((qf_ref, kf_ref, vf_ref, af_ref, of_ref), (qb_ref, kb_ref, vb_ref, ab_ref, ob_ref))
    tris, tri_bs, g_alls = [], [], []
    for d in range(2):
        z = _dot(refs[d][3][...].astype(BF16), wa_ref[d]) + ba_ref[d:d + 1, :]
        g_alls.append(_log_sigmoid(z) * (1.0 / GLA_TAU))
        tri = (ci >= ri) if d == 1 else (ci <= ri)
        tris.append(tri)
        tri_bs.append(jnp.where(tri, 1.0, 0.0).astype(BF16))

    def chunk(d, c):
        q_ref, k_ref, v_ref, _, o_ref = refs[d]
        rev = d == 1
        rows = slice(c * C, (c + 1) * C)
        g = g_alls[d][rows, :]
        g_hi = g.astype(BF16)
        g_lo = (g - g_hi.astype(F32)).astype(BF16)
        b = _dot(tri_bs[d], g_hi) + _dot(tri_bs[d], g_lo)
        b_tot = b[0:1, :] if rev else b[C - 1:C, :]
        q = q_ref[rows, :] * (GLA_DK ** -0.5)
        k = k_ref[rows, :]
        v = v_ref[rows, :].astype(BF16)
        q_d = (q * jnp.exp(b)).astype(BF16)
        k_d = (k * jnp.exp(-b)).astype(BF16)
        k_s = (k * jnp.exp(b_tot - b)).astype(BF16)
        att = lax.dot_general(q_d, k_d, NT_DIMS, preferred_element_type=F32)
        att = jnp.where(tris[d], att, 0.0).astype(BF16)
        s_t = st_ref[d]
        o = _dot(att, v) + lax.dot_general(q_d, s_t.astype(BF16), NT_DIMS, preferred_element_type=F32)
        o_ref[rows, :] = o
        st_ref[d] = s_t * jnp.exp(b_tot) + lax.dot_general(v, k_s, TN_DIMS, preferred_element_type=F32)

    n_chunks = lt // C
    for c in range(n_chunks):
        chunk(0, c)
        chunk(1, n_chunks - 1 - c)


def gla_scan(p, wa, ba, *, ctx_len, lt=SEQ_TILE):
    T = p.shape[0]
    n_tiles = T // lt
    n_ctx = ctx_len // lt
    tf = functools.partial(_time_tile, n_tiles=n_tiles, n_ctx_tiles=n_ctx, rev=False)
    tb = functools.partial(_time_tile, n_tiles=n_tiles, n_ctx_tiles=n_ctx, rev=True)

    def stream(tt):
        return [
            pl.BlockSpec((lt, GLA_DK), lambda h, t: (tt(t), OFF_Q // GLA_DK + h)),
            pl.BlockSpec((lt, GLA_DK), lambda h, t: (tt(t), OFF_K // GLA_DK + h)),
            pl.BlockSpec((lt, GLA_DV), lambda h, t: (tt(t), OFF_V // GLA_DV + h)),
            pl.BlockSpec((lt, LANE), lambda h, t: (tt(t), OFF_ALPHA // LANE)),
        ]

    out = jax.ShapeDtypeStruct((T, D_GLA_V), F32)
    return pl.pallas_call(
        functools.partial(_gla_kernel, lt=lt),
        grid=(GLA_HEADS, n_tiles),
        in_specs=stream(tf) + stream(tb) + [
            pl.BlockSpec((2, LANE, GLA_DK), lambda h, t: (0, 0, h)),
            pl.BlockSpec((2, GLA_DK), lambda h, t: (0, h)),
        ],
        out_specs=[pl.BlockSpec((lt, GLA_DV), lambda h, t: (tf(t), h)),
                   pl.BlockSpec((lt, GLA_DV), lambda h, t: (tb(t), h))],
        out_shape=[out, out],
        scratch_shapes=[pltpu.VMEM((2, GLA_DV, GLA_DK), F32)],
        compiler_params=_cparams(("arbitrary", "arbitrary")),
        name="gla_scan",
    )(p, p, p, p, p, p, p, p, wa, ba)


def _finish_kernel(u_ref, ys_f_ref, ys_b_ref, d_ref, wglu_ref, bglu_ref,
                   og_f_ref, og_b_ref, gate_ref, gn_ref,
                   x0_ref, conv_ref, s_ref, hb_ref, o_ref):
    u = u_ref[...]
    y = ys_f_ref[...] + ys_b_ref[...] + d_ref[...] * u
    zz = _gelu(y)
    s5 = zz * _sigmoid(_dot(zz.astype(BF16), wglu_ref[...]) + bglu_ref[...])
    o_ref[:, 0:D_S5] = s5.astype(BF16)

    gate = gate_ref[...]
    for h in range(GLA_HEADS):
        cols = slice(h * GLA_DV, (h + 1) * GLA_DV)
        o = og_f_ref[:, cols] + og_b_ref[:, cols]
        ms = jnp.mean(o * o, axis=-1, keepdims=True)
        on = o * lax.rsqrt(ms + EPS) * gn_ref[...]
        gt = gate[:, cols]
        o_ref[:, D_S5 + h * GLA_DV:D_S5 + (h + 1) * GLA_DV] = (on * (gt * _sigmoid(gt))).astype(BF16)

    s = s_ref[...]
    hy = x0_ref[...] * (conv_ref[...] + hb_ref[...] * s)
    o_ref[:, D_S5 + D_GLA_V:] = hy.astype(BF16)


def mixer_finish(p, ys_f, ys_b, s5_d, w_glu, b_glu, og_f, og_b, gn, x0, conv, s, hy_bias, *, tm):
    T = p.shape[0]
    row = lambda w, off=0: pl.BlockSpec((tm, w), lambda i: (i, off))
    const = lambda r, w: pl.BlockSpec((r, w), lambda i: (0, 0))
    return pl.pallas_call(
        _finish_kernel,
        grid=(T // tm,),
        in_specs=[
            row(D_S5, OFF_S5 // D_S5), row(D_S5), row(D_S5), const(1, D_S5), const(D_S5, D_S5), const(1, D_S5),
            row(D_GLA_V), row(D_GLA_V), row(D_GLA_V, OFF_GATE // D_GLA_V), const(1, GLA_DV),
            row(D_HY), row(D_HY), row(D_HY), const(1, D_HY),
        ],
        out_specs=pl.BlockSpec((tm, D_MODEL), lambda i: (i, 0)),
        out_shape=jax.ShapeDtypeStruct((T, D_MODEL), BF16),
        compiler_params=_cparams(("arbitrary",)),
        name="mixer_finish",
    )(p, ys_f, ys_b, s5_d.reshape(1, D_S5), w_glu, b_glu.reshape(1, D_S5),
      og_f, og_b, p, gn.reshape(1, GLA_DV), x0, conv, s, hy_bias.reshape(1, D_HY))


def _peer_kernel(x_ref, u_ref, vt_ref, s2_ref, e2_ref, s1_ref, e1_ref, tau_ref, o_ref,
                 act_ref, p_ref, *, n_i1, tm, cw):
    j = pl.program_id(1)

    @pl.when(j == 0)
    def _():
        o_ref[...] = jnp.zeros_like(o_ref)

    act_ref[...] = lax.dot_general(u_ref[...], x_ref[...], NT_DIMS, preferred_element_type=F32)
    K = PEER_KEYS
    sub = SUBLANE // n_i1
    base = (j % sub) * n_i1 if sub > 1 else 0
    for a in range(n_i1):
        r1 = pl.ds(base + a, 1)
        for cb in range(tm // cw):
            cols = slice(cb * cw, (cb + 1) * cw)
            w = jnp.zeros((K, cw), F32)
            for h in range(PEER_HEADS):
                tot = s1_ref[h, r1, cols] + s2_ref[h, :, cols]
                val = e1_ref[h, r1, cols] * e2_ref[h, :, cols]
                w = w + jnp.where(tot >= tau_ref[h, :, cols], val, 0.0)
            act = act_ref[a * K:(a + 1) * K, cols]
            p_ref[a * K:(a + 1) * K, cols] = (w * _gelu(act)).astype(BF16)
    o_ref[...] += lax.dot_general(vt_ref[...], p_ref[...], TN_DIMS, preferred_element_type=F32)


def peer_dense(xn, u_tab, vt_tab, s1t, e1t, s2t, e2t, taut, *, tm, te, cw=256):
    T, D = xn.shape
    E = u_tab.shape[0]
    H, K = PEER_HEADS, PEER_KEYS
    n_i1 = te // K
    sub = SUBLANE // n_i1
    i1_spec = pl.BlockSpec((H, SUBLANE, tm), lambda i, j: (0, j // sub, i))
    once = pl.Buffered(1)
    tab_spec = pl.BlockSpec((H, K, tm), lambda i, j: (0, 0, i), pipeline_mode=once)
    return pl.pallas_call(
        functools.partial(_peer_kernel, n_i1=n_i1, tm=tm, cw=cw),
        grid=(T // tm, E // te),
        in_specs=[
            pl.BlockSpec((tm, D), lambda i, j: (i, 0), pipeline_mode=once),
            pl.BlockSpec((te, D), lambda i, j: (j, 0)),
            pl.BlockSpec((te, D), lambda i, j: (j, 0)),
            tab_spec, tab_spec,
            i1_spec, i1_spec,
            pl.BlockSpec((H, 1, tm), lambda i, j: (0, 0, i), pipeline_mode=once),
        ],
        out_specs=pl.BlockSpec((D, tm), lambda i, j: (0, i)),
        out_shape=jax.ShapeDtypeStruct((D, T), F32),
        scratch_shapes=[pltpu.VMEM((te, tm), F32), pltpu.VMEM((te, tm), BF16)],
        compiler_params=_cparams(("arbitrary", "arbitrary")),
        name="peer_dense",
    )(xn, u_tab, vt_tab, s2t, e2t, s1t, e1t, taut)


NEG_BIG = -3.0e38
N_CAND = PEER_TOPK + 8 * 7 + 8


def _peer_prep_kernel(q_ref, k_ref, s1_ref, e1_ref, s2_ref, e2_ref, tau_ref, top_ref, cand_ref):
    R = PEER_TOPK
    half_w = PEER_DQ // 2

    def nt(a, b):
        return lax.dot_general(a, b, NT_DIMS, preferred_element_type=F32)

    def scores(half):
        qh = q_ref[:, half * half_w:(half + 1) * half_w]
        q_hi = qh.astype(BF16)
        q_lo = (qh - q_hi.astype(F32)).astype(BF16)
        kk = k_ref[half]
        k_hi = kk.astype(BF16)
        k_lo = (kk - k_hi.astype(F32)).astype(BF16)
        return nt(k_hi, q_hi) + (nt(k_hi, q_lo) + nt(k_lo, q_hi))

    def sorted_top(s, slot):
        work = s
        for r in range(R):
            m = jnp.max(work, axis=0, keepdims=True)
            top_ref[slot, r:r + 1, :] = m
            work = jnp.where(work == m, NEG_BIG, work)

    s1 = scores(0)
    s2 = scores(1)
    sorted_top(s1, 0)
    sorted_top(s2, 1)
    a = top_ref[0]
    b = top_ref[1]
    cand_ref[0:R, :] = a[0:1, :] + b
    for i in range(1, 8):
        cand_ref[R + 8 * (i - 1):R + 8 * i, :] = a[i:i + 1, :] + b[0:8, :]
    cand_ref[R + 56:R + 64, :] = a[8:16, :] + b[0:1, :]
    work = cand_ref[...]
    m0 = jnp.max(work, axis=0, keepdims=True)
    m = m0
    z = jnp.ones_like(m0)
    for r in range(1, R):
        work = jnp.where(work == m, NEG_BIG, work)
        m = jnp.max(work, axis=0, keepdims=True)
        z = z + jnp.exp(m - m0)
    tau_ref[0] = m
    s1_ref[0] = s1
    s2_ref[0] = s2
    e1_ref[0] = jnp.exp(s1 - a[0:1, :]) * (1.0 / z)
    e2_ref[0] = jnp.exp(s2 - b[0:1, :])


def peer_prep(q, keys, *, tm):
    T = q.shape[0]
    H, K = PEER_HEADS, PEER_KEYS
    tab = jax.ShapeDtypeStruct((H, K, T), F32)
    tab_spec = pl.BlockSpec((1, K, tm), lambda i, h: (h, 0, i))
    return pl.pallas_call(
        _peer_prep_kernel,
        grid=(T // tm, H),
        in_specs=[
            pl.BlockSpec((tm, PEER_DQ), lambda i, h: (i, h)),
            pl.BlockSpec((None, 2, K, PEER_DQ // 2), lambda i, h: (h, 0, 0, 0)),
        ],
        out_specs=[tab_spec, tab_spec, tab_spec, tab_spec, pl.BlockSpec((1, 1, tm), lambda i, h: (h, 0, i))],
        out_shape=[tab, tab, tab, tab, jax.ShapeDtypeStruct((H, 1, T), F32)],
        scratch_shapes=[pltpu.VMEM((2, PEER_TOPK, tm), F32), pltpu.VMEM((N_CAND, tm), F32)],
        compiler_params=_cparams(("arbitrary", "arbitrary")),
        name="peer_prep",
    )(q, keys)


def _transpose_gated_res_kernel(h_ref, ft_ref, gate_ref, o_ref, *, ctx_len, tm):
    i = pl.program_id(0)
    row = i * tm + lax.broadcasted_iota(jnp.int32, (tm, 1), 0)
    gate = jnp.where(row < ctx_len, gate_ref[0:1, :], gate_ref[1:2, :])
    o_ref[...] = h_ref[...] + gate * ft_ref[...].T


def transpose_gated_res(h, ft, gates, *, ctx_len, tm):
    T, D = h.shape
    return pl.pallas_call(
        functools.partial(_transpose_gated_res_kernel, ctx_len=ctx_len, tm=tm),
        grid=(T // tm,),
        in_specs=[
            pl.BlockSpec((tm, D), lambda i: (i, 0)),
            pl.BlockSpec((D, tm), lambda i: (0, i)),
            pl.BlockSpec((2, D), lambda i: (0, 0)),
        ],
        out_specs=pl.BlockSpec((tm, D), lambda i: (i, 0)),
        out_shape=jax.ShapeDtypeStruct((T, D), F32),
        compiler_params=_cparams(("arbitrary",)),
        name="transpose_gated_res",
    )(h, ft, gates)


def _rmsnorm_kernel(x_ref, g_ref, o_ref):
    x = x_ref[...]
    ms = jnp.mean(x * x, axis=-1, keepdims=True)
    o_ref[...] = x * lax.rsqrt(ms + EPS) * g_ref[...]


def rmsnorm_rows(x, g, *, tm):
    T, D = x.shape
    return pl.pallas_call(
        _rmsnorm_kernel,
        grid=(T // tm,),
        in_specs=[pl.BlockSpec((tm, D), lambda i: (i, 0)), pl.BlockSpec((1, D), lambda i: (0, 0))],
        out_specs=pl.BlockSpec((tm, D), lambda i: (i, 0)),
        out_shape=jax.ShapeDtypeStruct((T, D), F32),
        compiler_params=_cparams(("arbitrary",)),
        name="rmsnorm_rows",
    )(x, g.reshape(1, D))


def grid_sincos(n_tokens, dim):
    rows = n_tokens // GRID_W
    n_freq = dim // 4
    omega = 1.0 / (POS_BASE ** (jnp.arange(n_freq, dtype=F32) / n_freq))

    def enc(count):
        a = jnp.arange(count).astype(F32)[:, None] * omega[None, :]
        return jnp.concatenate([jnp.sin(a), jnp.cos(a)], axis=-1)

    row_code = jnp.broadcast_to(enc(rows)[:, None, :], (rows, GRID_W, dim // 2))
    col_code = jnp.broadcast_to(enc(GRID_W)[None, :, :], (rows, GRID_W, dim // 2))
    return jnp.concatenate([row_code, col_code], axis=-1).reshape(n_tokens, dim)


def _split_bf16(a):
    hi = a.astype(BF16)
    return hi, (a - hi.astype(F32)).astype(BF16)


def _dot3s(a, b_hi, b_lo):
    a_hi, a_lo = _split_bf16(a)
    return _dot(a_hi, b_hi) + (_dot(a_hi, b_lo) + _dot(a_lo, b_hi))


def _dot3(a, b):
    return _dot3s(a, *_split_bf16(b))


def _adaln_kernel(c_ref, w_ref, b_ref, o_ref):
    c = c_ref[...]
    o_ref[...] = _dot3(c * _sigmoid(c), w_ref[...]) + b_ref[...]


def adaln(cond, w, b, *, tn=1536):
    R, D = cond.shape
    N = w.shape[1]
    return pl.pallas_call(
        _adaln_kernel,
        grid=(N // tn,),
        in_specs=[pl.BlockSpec((R, D), lambda j: (0, 0)),
                  pl.BlockSpec((D, tn), lambda j: (0, j)),
                  pl.BlockSpec((1, tn), lambda j: (0, j))],
        out_specs=pl.BlockSpec((R, tn), lambda j: (0, j)),
        out_shape=jax.ShapeDtypeStruct((R, N), F32),
        compiler_params=_cparams(("arbitrary",)),
        name="adaln",
    )(cond, w, b.reshape(1, N))


def _hyena_pre_kernel(x0_ref, x1_ref, v_ref, w_ref, b_ref, x0o_ref, s_ref, *, ctx_len):
    T = x0_ref.shape[0]
    row = lax.broadcasted_iota(jnp.int32, (T, 1), 0)
    first = jnp.logical_or(row == 0, row == ctx_len)
    last = jnp.logical_or(row == ctx_len - 1, row == T - 1)

    def conv(ref, part):
        x = ref[...]
        prev = jnp.where(first, 0.0, pltpu.roll(x, 1, 0))
        nxt = jnp.where(last, 0.0, pltpu.roll(x, T - 1, 0))
        w = lambda tap: w_ref[tap, part:part + 1, :]
        return w(0) * prev + w(1) * x + w(2) * nxt + b_ref[part:part + 1, :]

    x0o_ref[...] = conv(x0_ref, 0)
    s_ref[...] = conv(x1_ref, 1) * conv(v_ref, 2)


def hyena_pre(p, conv_w, conv_b, *, ctx_len):
    T = p.shape[0]
    nb = D_HY // LANE
    col = lambda part: pl.BlockSpec((T, LANE), lambda c: (0, OFF_HY // LANE + part * nb + c))
    out = jax.ShapeDtypeStruct((T, D_HY), F32)
    return pl.pallas_call(
        functools.partial(_hyena_pre_kernel, ctx_len=ctx_len),
        grid=(nb,),
        in_specs=[col(0), col(1), col(2),
                  pl.BlockSpec((HY_SHORT, 3, LANE), lambda c: (0, 0, c)),
                  pl.BlockSpec((3, LANE), lambda c: (0, c))],
        out_specs=[pl.BlockSpec((T, LANE), lambda c: (0, c))] * 2,
        out_shape=[out, out],
        compiler_params=_cparams(("arbitrary",)),
        name="hyena_pre",
    )(p, p, p, conv_w.reshape(HY_SHORT, 3, D_HY), conv_b.reshape(3, D_HY))


def _hyena_filter_kernel(band_ref, w1_ref, b1_ref, w2_ref, b2_ref, w3_ref, fr_ref, dec_ref, o_ref, *, n, tr):
    i = pl.program_id(0)
    ri = (i * tr + lax.broadcasted_iota(jnp.int32, (tr, 1), 0)).astype(F32)
    t = ri * (1.0 / (n - 1))
    w = ri * (2.0 * math.pi / n)
    lane = lax.broadcasted_iota(jnp.int32, (1, LANE), 1)
    arg = w * band_ref[...]
    z = jnp.where(lane == 0, t,
                  jnp.where(lane <= HY_BANDS, jnp.cos(arg),
                            jnp.where(lane <= 2 * HY_BANDS, -jnp.sin(arg), 0.0)))
    h = jnp.sin(fr_ref[0:1, :] * (_dot3(z, w1_ref[...]) + b1_ref[...]))
    h = jnp.sin(fr_ref[1:2, :] * (_dot3(h, w2_ref[...]) + b2_ref[...]))
    h = _dot3(h, w3_ref[...]) * jnp.exp(-t * dec_ref[...])
    col = lax.broadcasted_iota(jnp.int32, (1, 2 * D_HY), 1)
    o_ref[...] = jnp.where(jnp.logical_and(ri == 0.0, col >= D_HY), 0.0, h)


def hyena_filter(n, w1, b1, w2, b2, w3, freq, decay, *, tr=256):
    bands = jnp.linspace(1e-4, HY_BANDS - 1, HY_BANDS, dtype=F32)
    band_row = jnp.zeros((1, LANE), F32).at[0, 1:1 + HY_BANDS].set(bands).at[0, 1 + HY_BANDS:1 + 2 * HY_BANDS].set(bands)
    w1p = jnp.zeros((LANE, HY_HIDDEN), F32).at[:HY_FEAT].set(w1)
    const = lambda a: pl.BlockSpec(a.shape, lambda i: (0,) * a.ndim)
    args = (band_row, w1p, b1.reshape(1, -1), w2, b2.reshape(1, -1), w3, freq,
            jnp.abs(decay).reshape(1, 2 * D_HY))
    return pl.pallas_call(
        functools.partial(_hyena_filter_kernel, n=n, tr=tr),
        grid=(n // tr,),
        in_specs=[const(a) for a in args],
        out_specs=pl.BlockSpec((tr, 2 * D_HY), lambda i: (i, 0)),
        out_shape=jax.ShapeDtypeStruct((n, 2 * D_HY), F32),
        compiler_params=_cparams(("arbitrary",)),
        name="hyena_filter",
    )(*args)


FFT_R = 128
FFT_N = FFT_R * FFT_R
FFT_CH = 32


def _dft_tables():
    r = np.arange(FFT_R)
    ang = 2.0 * np.pi * np.outer(r, r) / FFT_R
    c, s = np.cos(ang), np.sin(ang)
    angt = 2.0 * np.pi * np.outer(r, r) / FFT_N
    ct, st = np.cos(angt), np.sin(angt)

    def split(m):
        m = jnp.asarray(m, F32)
        hi = m.astype(BF16)
        return jnp.stack([hi, (m - hi.astype(F32)).astype(BF16)])

    fwd_a = split(np.concatenate([c, -s], axis=1))
    fwd_b = split(np.block([[c, -s], [s, c]]))
    inv_b = split(np.block([[c, s], [-s, c]]) / FFT_N)
    inv_a = split(np.concatenate([c, -s], axis=0))
    tw = (jnp.asarray(ct, F32), jnp.asarray(-st, F32))
    twc = (jnp.asarray(ct, F32), jnp.asarray(st, F32))
    return fwd_a, fwd_b, inv_b, inv_a, tw, twc


def _twiddle_transpose(y, twr, twi, dst_ref, nc):
    y = y.reshape(nc, FFT_R, 2 * FFT_R)
    yr, yi = y[:, :, :FFT_R], y[:, :, FFT_R:]
    zr = yr * twr - yi * twi
    zi = yr * twi + yi * twr
    for c in range(nc):
        dst_ref[c * FFT_R:(c + 1) * FFT_R, 0:FFT_R] = zr[c].T
        dst_ref[c * FFT_R:(c + 1) * FFT_R, FFT_R:2 * FFT_R] = zi[c].T


def _hyena_fwd_kernel(x_ref, fa_ref, twr_ref, twi_ref, fb_ref, o_ref, l_ref, yt_ref, *, nc):
    j = pl.program_id(1)
    half = FFT_R // 2

    @pl.when(j == 0)
    def _():
        zeros = jnp.zeros((half, LANE), F32)

        def body(b, carry):
            t = jnp.concatenate([x_ref[pl.ds(b, half, stride=FFT_R), :], zeros], axis=0)
            l_ref[pl.ds(b, LANE, stride=FFT_R), :] = t.T
            return carry

        lax.fori_loop(0, FFT_R, body, 0, unroll=8)

    rows = nc * FFT_R
    lc = l_ref[pl.ds(pl.multiple_of(j * rows, rows), rows), :]
    y = _dot3s(lc, fa_ref[0], fa_ref[1])
    _twiddle_transpose(y, twr_ref[...], twi_ref[...], yt_ref, nc)
    o_ref[...] = _dot3s(yt_ref[...], fb_ref[0], fb_ref[1])


def hyena_fwd_dft(x, tables):
    n, C = x.shape
    assert n * 2 == FFT_N and C % LANE == 0
    fwd_a, fwd_b, _, _, (twr, twi), _ = tables
    nc = FFT_CH
    steps = LANE // nc
    const = lambda a: pl.BlockSpec(a.shape, lambda cb, j: (0,) * a.ndim)
    return pl.pallas_call(
        functools.partial(_hyena_fwd_kernel, nc=nc),
        grid=(C // LANE, steps),
        in_specs=[pl.BlockSpec((n, LANE), lambda cb, j: (0, cb)), const(fwd_a), const(twr), const(twi), const(fwd_b)],
        out_specs=pl.BlockSpec((nc * FFT_R, 2 * FFT_R), lambda cb, j: (cb * steps + j, 0)),
        out_shape=jax.ShapeDtypeStruct((C * FFT_R, 2 * FFT_R), F32),
        scratch_shapes=[pltpu.VMEM((LANE * FFT_R, LANE), F32), pltpu.VMEM((nc * FFT_R, 2 * FFT_R), F32)],
        compiler_params=_cparams(("arbitrary", "arbitrary")),
        name="hyena_fwd_dft",
    )(x, fwd_a, twr, twi, fwd_b)


def _hyena_inv_kernel(s_ref, kf_ref, kb_ref, gb_ref, twr_ref, twi_ref, ga_ref, o_ref, lr_ref, at_ref, *, nc, steps):
    j = pl.program_id(1)
    R = FFT_R
    s, kf, kb = s_ref[...], kf_ref[...], kb_ref[...]
    sr, si = s[:, :R], s[:, R:]
    kr = kf[:, :R] + kb[:, :R]
    ki = kf[:, R:] - kb[:, R:]
    p = jnp.concatenate([sr * kr - si * ki, sr * ki + si * kr], axis=1)
    a = _dot3s(p, gb_ref[0], gb_ref[1])
    _twiddle_transpose(a, twr_ref[...], twi_ref[...], at_ref, nc)
    rows = nc * R
    lr_ref[pl.ds(pl.multiple_of(j * rows, rows), rows), :] = _dot3s(at_ref[...], ga_ref[0], ga_ref[1])

    @pl.when(j == steps - 1)
    def _():
        def body(b, carry):
            t = lr_ref[pl.ds(b, LANE, stride=R), :]
            o_ref[pl.ds(b, R // 2, stride=R), :] = t.T[0:R // 2, :]
            return carry

        lax.fori_loop(0, R, body, 0, unroll=8)


def hyena_inv_dft(s_f, k_f, tables):
    C = s_f.shape[0] // FFT_R
    _, _, inv_b, inv_a, _, (twr, twi) = tables
    nc = FFT_CH
    steps = LANE // nc
    kb_off = C // nc
    const = lambda a: pl.BlockSpec(a.shape, lambda cb, j: (0,) * a.ndim)
    blk = lambda off: pl.BlockSpec((nc * FFT_R, 2 * FFT_R), lambda cb, j: (off + cb * steps + j, 0))
    return pl.pallas_call(
        functools.partial(_hyena_inv_kernel, nc=nc, steps=steps),
        grid=(C // LANE, steps),
        in_specs=[blk(0), blk(0), blk(kb_off), const(inv_b), const(twr), const(twi), const(inv_a)],
        out_specs=pl.BlockSpec((FFT_N // 2, LANE), lambda cb, j: (0, cb)),
        out_shape=jax.ShapeDtypeStruct((FFT_N // 2, C), F32),
        scratch_shapes=[pltpu.VMEM((LANE * FFT_R, LANE), F32), pltpu.VMEM((nc * FFT_R, 2 * FFT_R), F32)],
        compiler_params=_cparams(("arbitrary", "arbitrary")),
        name="hyena_inv_dft",
    )(s_f, k_f, k_f, inv_b, twr, twi, inv_a)


def _hyena_ctx_kernel(s_ref, k_ref, fw_ref, iv_ref, o_ref):
    n = s_ref.shape[0]
    N = 2 * n
    xs = _dot3(fw_ref[...], s_ref[...])
    xk = _dot3(fw_ref[...], k_ref[...])
    sr, si = xs[:N], xs[N:]
    kr = xk[:N, :D_HY] + xk[:N, D_HY:]
    ki = xk[N:, :D_HY] - xk[N:, D_HY:]
    p = jnp.concatenate([sr * kr - si * ki, sr * ki + si * kr], axis=0)
    o_ref[...] = _dot3(iv_ref[...], p)


def hyena_ctx_conv(s_c, k_c):
    n = s_c.shape[0]
    N = 2 * n
    ang = 2.0 * np.pi * np.outer(np.arange(N), np.arange(N)) / N
    c, s = np.cos(ang), np.sin(ang)
    fw = jnp.asarray(np.concatenate([c[:, :n], -s[:, :n]], axis=0), F32)
    iv = jnp.asarray(np.concatenate([c[:n, :], -s[:n, :]], axis=1) / N, F32)
    full = lambda a: pl.BlockSpec(a.shape, lambda i: (0,) * a.ndim)
    return pl.pallas_call(
        _hyena_ctx_kernel,
        grid=(1,),
        in_specs=[full(s_c), full(k_c), full(fw), full(iv)],
        out_specs=pl.BlockSpec((n, D_HY), lambda i: (0, 0)),
        out_shape=jax.ShapeDtypeStruct((n, D_HY), F32),
        compiler_params=_cparams(("arbitrary",)),
        name="hyena_ctx_conv",
    )(s_c, k_c, fw, iv)


def hyena_mixer_parts(p, conv_w, conv_b, filt, tables, *, ctx_len):
    T = p.shape[0]
    x0, s = hyena_pre(p, conv_w, conv_b, ctx_len=ctx_len)
    k_lat = hyena_filter(T - ctx_len, *filt)
    k_ctx = hyena_filter(ctx_len, *filt)
    conv_l = hyena_inv_dft(hyena_fwd_dft(s[ctx_len:], tables), hyena_fwd_dft(k_lat, tables), tables)
    conv_c = hyena_ctx_conv(s[:ctx_len], k_ctx)
    return x0, jnp.concatenate([conv_c, conv_l], axis=0), s


def _pack_w_in(w_in_l):
    widths = (D_S5, D_GLA_K, D_GLA_K, D_GLA_V, D_GLA_V, 2 * GLA_RANK, 3 * D_HY)
    offs = [0]
    for wd in widths:
        offs.append(offs[-1] + wd)
    s5, q, k, v, gate, alpha, hy = (w_in_l[:, offs[i]:offs[i + 1]] for i in range(7))
    pad = jnp.zeros((w_in_l.shape[0], IN_PACKED - OFF_ALPHA - 2 * GLA_RANK), w_in_l.dtype)
    return jnp.concatenate([v, gate, s5, q, k, hy, alpha, pad], axis=1).astype(BF16)


def kernel(x, c, ctx, c_ctx, w_ada, b_ada, g_norm1, g_norm2, w_in, s5_a_re, s5_a_im, s5_log_step, s5_b_re, s5_b_im, s5_c_re, s5_c_im, s5_d, s5_w_glu, s5_b_glu, gla_w_alpha, gla_b_alpha, gla_g_norm, hy_conv_w, hy_conv_b, hy_f_w1, hy_f_b1, hy_f_w2, hy_f_b2, hy_f_w3, hy_f_freq, hy_decay, hy_bias, w_out, peer_w_q, peer_keys, peer_u, peer_v, g_final):
    L = x.shape[1]
    Lc = ctx.shape[1]
    T = L + Lc
    TM = 768
    h = jnp.concatenate([ctx[0], x[0] + grid_sincos(L, D_MODEL)], axis=0)
    cond = jnp.zeros((SUBLANE, D_MODEL), F32).at[0].set(c_ctx).at[1].set(c[0])
    tables = _dft_tables()

    for l in range(DEPTH):
        m = adaln(cond, w_ada[l], b_ada[l])[0:2]
        sh1, sc1, gt1, sh2, sc2, gt2 = jnp.split(m, 6, axis=-1)
        mod1 = jnp.stack([sh1[0], sc1[0], sh1[1], sc1[1]], axis=0)
        mod2 = jnp.stack([sh2[0], sc2[0], sh2[1], sc2[1]], axis=0)

        p = norm_mod_matmul(h, g_norm1[l], mod1, _pack_w_in(w_in[l]), ctx_len=Lc, tm=TM, tn=768)

        bm, cm, pw = s5_prepare(s5_a_re[l], s5_a_im[l], s5_log_step[l], s5_b_re[l], s5_b_im[l],
                                s5_c_re[l], s5_c_im[l])
        ys_f, ys_b = s5_scan(p, bm, cm, pw, ctx_len=Lc)

        wa = jnp.zeros((2, LANE, D_GLA_K), F32)
        wa = wa.at[0, 0:GLA_RANK].set(gla_w_alpha[l, 0]).at[1, GLA_RANK:2 * GLA_RANK].set(gla_w_alpha[l, 1])
        wa = wa.astype(BF16)
        og_f, og_b = gla_scan(p, wa, gla_b_alpha[l], ctx_len=Lc)

        filt = (hy_f_w1[l], hy_f_b1[l], hy_f_w2[l], hy_f_b2[l], hy_f_w3[l], hy_f_freq[l], hy_decay[l])
        x0, conv, s = hyena_mixer_parts(p, hy_conv_w[l], hy_conv_b[l], filt, tables, ctx_len=Lc)

        mix = mixer_finish(p, ys_f, ys_b, s5_d[l], s5_w_glu[l].astype(BF16), s5_b_glu[l],
                           og_f, og_b, gla_g_norm[l], x0, conv, s, hy_bias[l], tm=256)
        h = matmul_gated_res(mix, w_out[l].astype(BF16), h, gt1, ctx_len=Lc, tm=TM, tn=1024)

        q, xn = norm_mod_matmul(h, g_norm2[l], mod2, peer_w_q[l].astype(BF16),
                                ctx_len=Lc, tm=TM, tn=1024, emit_xn=True)
        s1t, e1t, s2t, e2t, taut = peer_prep(q, peer_keys[l], tm=TM)
        ft = peer_dense(xn, peer_u[l].astype(BF16), peer_v[l].astype(BF16),
                        s1t, e1t, s2t, e2t, taut, tm=TM, te=1024)
        h = transpose_gated_res(h, ft, gt2, ctx_len=Lc, tm=256)

    out = rmsnorm_rows(h[Lc:], g_final, tm=512)
    return out[None]
```

```python
import functools
import math

import jax
import jax.numpy as jnp
import numpy as np
from jax import lax
from jax.experimental import pallas as pl
from jax.experimental.pallas import tpu as pltpu

F32 = jnp.float32
BF16 = jnp.bfloat16

D_MODEL = 2048
DEPTH = 4
GRID_W = 64
EPS = 1e-6
POS_BASE = 10000.0

D_S5 = D_MODEL // 4
S5_GROUP = 16
S5_GROUPS = D_S5 // S5_GROUP
S5_STATE = 64
S5_BLK_GROUPS = 8
S5_BLK_CH = S5_BLK_GROUPS * S5_GROUP
S5_BLK_ST = S5_BLK_GROUPS * S5_STATE
S5_NBLK = S5_GROUPS // S5_BLK_GROUPS

GLA_HEADS = 4
D_GLA_K = D_MODEL // 4
D_GLA_V = D_MODEL // 2
GLA_DK = D_GLA_K // GLA_HEADS
GLA_DV = D_GLA_V // GLA_HEADS
GLA_RANK = 16
GLA_TAU = 16.0
GLA_CHUNK = 64
GLA_HPS = 4

D_HY = D_MODEL // 4
HY_SHORT = 3
HY_BANDS = 16
HY_FEAT = 1 + 2 * HY_BANDS
HY_HIDDEN = 64
HY_TARGET = 1e-2
HY_MIN_DECAY = -math.log(HY_TARGET) / 1.5
HY_MAX_DECAY = -math.log(HY_TARGET) / 0.3

PEER_HEADS = 8
PEER_KEYS = 128
PEER_EXPERTS = PEER_KEYS * PEER_KEYS
PEER_DQ = 256
PEER_TOPK = 16

OFF_V = 0
OFF_GATE = OFF_V + D_GLA_V
OFF_S5 = OFF_GATE + D_GLA_V
OFF_Q = OFF_S5 + D_S5
OFF_K = OFF_Q + D_GLA_K
OFF_HY = OFF_K + D_GLA_K
OFF_ALPHA = OFF_HY + 3 * D_HY
LANE = 128
SUBLANE = 8
IN_PACKED = OFF_ALPHA + 2 * LANE

SEQ_TILE = 256
VMEM_LIMIT = 56 * 1024 * 1024

NT_DIMS = (((1,), (1,)), ((), ()))
TN_DIMS = (((0,), (0,)), ((), ()))


def _cparams(sem):
    return pltpu.CompilerParams(dimension_semantics=sem, vmem_limit_bytes=VMEM_LIMIT)


def _dot(a, b):
    return jnp.dot(a, b, preferred_element_type=F32)


def _gelu(x):
    k1 = -2.0 * math.sqrt(2.0 / math.pi) * math.log2(math.e)
    k2 = 0.044715 * k1
    return x / (1.0 + jnp.exp2(x * (k1 + k2 * (x * x))))


def _sigmoid(x):
    return 1.0 / (1.0 + jnp.exp(-x))


def _norm_mod_matmul_kernel(x_ref, g_ref, mod_ref, w_ref, o_ref, *rest, ctx_len, tm, emit_xn):
    if emit_xn:
        xo_ref, xn_ref = rest
    else:
        (xn_ref,) = rest
    i = pl.program_id(0)
    j = pl.program_id(1)

    @pl.when(j == 0)
    def _():
        x = x_ref[...]
        ms = jnp.mean(x * x, axis=-1, keepdims=True)
        y = x * lax.rsqrt(ms + EPS) * g_ref[...]
        row = i * tm + lax.broadcasted_iota(jnp.int32, (tm, 1), 0)
        is_ctx = row < ctx_len
        shift = jnp.where(is_ctx, mod_ref[0:1, :], mod_ref[2:3, :])
        scale = jnp.where(is_ctx, mod_ref[1:2, :], mod_ref[3:4, :])
        xn = (y * (1.0 + scale) + shift).astype(BF16)
        xn_ref[...] = xn
        if emit_xn:
            xo_ref[...] = xn

    o_ref[...] = _dot(xn_ref[...], w_ref[...])


def norm_mod_matmul(x, g, mod, w, *, ctx_len, tm, tn, emit_xn=False):
    T, D = x.shape
    N = w.shape[1]
    out_shape = [jax.ShapeDtypeStruct((T, N), F32)]
    out_specs = [pl.BlockSpec((tm, tn), lambda i, j: (i, j))]
    if emit_xn:
        out_shape.append(jax.ShapeDtypeStruct((T, D), BF16))
        out_specs.append(pl.BlockSpec((tm, D), lambda i, j: (i, 0)))
    res = pl.pallas_call(
        functools.partial(_norm_mod_matmul_kernel, ctx_len=ctx_len, tm=tm, emit_xn=emit_xn),
        grid=(T // tm, N // tn),
        in_specs=[
            pl.BlockSpec((tm, D), lambda i, j: (i, 0)),
            pl.BlockSpec((1, D), lambda i, j: (0, 0)),
            pl.BlockSpec((4, D), lambda i, j: (0, 0)),
            pl.BlockSpec((D, tn), lambda i, j: (0, j)),
        ],
        out_specs=out_specs,
        out_shape=out_shape,
        scratch_shapes=[pltpu.VMEM((tm, D), BF16)],
        compiler_params=_cparams(("arbitrary", "arbitrary")),
        name="norm_mod_matmul",
    )(x, g.reshape(1, D), mod, w)
    return res if emit_xn else res[0]


def _matmul_gated_res_kernel(a_ref, w_ref, r_ref, gate_ref, o_ref, *, ctx_len, tm):
    i = pl.program_id(0)
    row = i * tm + lax.broadcasted_iota(jnp.int32, (tm, 1), 0)
    gate = jnp.where(row < ctx_len, gate_ref[0:1, :], gate_ref[1:2, :])
    o_ref[...] = r_ref[...] + gate * _dot(a_ref[...], w_ref[...])


def matmul_gated_res(a, w, res, gates, *, ctx_len, tm, tn):
    T, K = a.shape
    N = w.shape[1]
    return pl.pallas_call(
        functools.partial(_matmul_gated_res_kernel, ctx_len=ctx_len, tm=tm),
        grid=(T // tm, N // tn),
        in_specs=[
            pl.BlockSpec((tm, K), lambda i, j: (i, 0)),
            pl.BlockSpec((K, tn), lambda i, j: (0, j)),
            pl.BlockSpec((tm, tn), lambda i, j: (i, j)),
            pl.BlockSpec((2, tn), lambda i, j: (0, j)),
        ],
        out_specs=pl.BlockSpec((tm, tn), lambda i, j: (i, j)),
        out_shape=jax.ShapeDtypeStruct((T, N), F32),
        compiler_params=_cparams(("arbitrary", "arbitrary")),
        name="matmul_gated_res",
    )(a, w, res, gates)


def _time_tile(t, n_tiles, n_ctx_tiles, rev):
    if not rev:
        return t
    return jnp.where(t < n_ctx_tiles, n_ctx_tiles - 1 - t, n_tiles - 1 - (t - n_ctx_tiles))


def _s5_kernel(uf_ref, ub_ref, bm_ref, cm_ref, pw_ref, yf_ref, yb_ref, h_ref, c_ref, *, lt):
    t = pl.program_id(1)
    ns = S5_BLK_ST
    u_refs = (uf_ref, ub_ref)
    y_refs = (yf_ref, yb_ref)

    @pl.when(t == 0)
    def _():
        c_ref[...] = jnp.zeros_like(c_ref)

    for d in range(2):
        bu = _dot(u_refs[d][...].astype(BF16), bm_ref[d])
        h_ref[d, 0] = bu[:, :ns]
        h_ref[d, 1] = bu[:, ns:]

    n_grp = lt // 8

    def group_update(d, g, c_re, c_im):
        rev = d == 1
        gi = (n_grp - 1 - g) if rev else g
        rows = pl.ds(pl.multiple_of(gi * 8, 8), 8)
        a_re = h_ref[d, 0, rows, :]
        a_im = h_ref[d, 1, rows, :]
        for k, s in enumerate((1, 2, 4)):
            l_re = pw_ref[d, 8 + 8 * k:16 + 8 * k, :ns]
            l_im = pw_ref[d, 8 + 8 * k:16 + 8 * k, ns:]
            shift = (8 - s) if rev else s
            s_re = pltpu.roll(a_re, shift, 0)
            s_im = pltpu.roll(a_im, shift, 0)
            a_re = a_re + (l_re * s_re - l_im * s_im)
            a_im = a_im + (l_re * s_im + l_im * s_re)
        p_re = pw_ref[d, 0:8, :ns]
        p_im = pw_ref[d, 0:8, ns:]
        a_re = a_re + (p_re * c_re - p_im * c_im)
        a_im = a_im + (p_re * c_im + p_im * c_re)
        h_ref[d, 0, rows, :] = a_re
        h_ref[d, 1, rows, :] = a_im
        edge = slice(0, 1) if rev else slice(7, 8)
        return a_re[edge, :], a_im[edge, :]

    def body(g, carry):
        f_re, f_im, b_re, b_im = carry
        f_re, f_im = group_update(0, g, f_re, f_im)
        b_re, b_im = group_update(1, g, b_re, b_im)
        return f_re, f_im, b_re, b_im

    carry = lax.fori_loop(0, n_grp, body, (c_ref[0, 0], c_ref[0, 1], c_ref[1, 0], c_ref[1, 1]), unroll=2)
    c_ref[0, 0], c_ref[0, 1], c_ref[1, 0], c_ref[1, 1] = carry

    for d in range(2):
        y_refs[d][...] = (_dot(h_ref[d, 0].astype(BF16), cm_ref[d, :ns, :])
                          + _dot(h_ref[d, 1].astype(BF16), cm_ref[d, ns:, :]))


def s5_scan(p, bm, cm, pw, *, ctx_len, lt=SEQ_TILE):
    T = p.shape[0]
    n_tiles = T // lt
    n_ctx = ctx_len // lt
    tf = functools.partial(_time_tile, n_tiles=n_tiles, n_ctx_tiles=n_ctx, rev=False)
    tb = functools.partial(_time_tile, n_tiles=n_tiles, n_ctx_tiles=n_ctx, rev=True)
    ucol = OFF_S5 // S5_BLK_CH
    out = jax.ShapeDtypeStruct((T, D_S5), F32)
    return pl.pallas_call(
        functools.partial(_s5_kernel, lt=lt),
        grid=(S5_NBLK, n_tiles),
        in_specs=[
            pl.BlockSpec((lt, S5_BLK_CH), lambda b, t: (tf(t), ucol + b)),
            pl.BlockSpec((lt, S5_BLK_CH), lambda b, t: (tb(t), ucol + b)),
            pl.BlockSpec((2, None, S5_BLK_CH, 2 * S5_BLK_ST), lambda b, t: (0, b, 0, 0)),
            pl.BlockSpec((2, None, 2 * S5_BLK_ST, S5_BLK_CH), lambda b, t: (0, b, 0, 0)),
            pl.BlockSpec((2, None, 32, 2 * S5_BLK_ST), lambda b, t: (0, b, 0, 0)),
        ],
        out_specs=[pl.BlockSpec((lt, S5_BLK_CH), lambda b, t: (tf(t), b)),
                   pl.BlockSpec((lt, S5_BLK_CH), lambda b, t: (tb(t), b))],
        out_shape=[out, out],
        scratch_shapes=[pltpu.VMEM((2, 2, lt, S5_BLK_ST), F32), pltpu.VMEM((2, 2, 1, S5_BLK_ST), F32)],
        compiler_params=_cparams(("arbitrary", "arbitrary")),
        name="s5_scan",
    )(p, p, bm, cm, pw)


def _cmul(a, b):
    return a[0] * b[0] - a[1] * b[1], a[0] * b[1] + a[1] * b[0]


def s5_prepare(a_re, a_im, log_step, b_re, b_im, c_re, c_im):
    G, P, Cg = S5_GROUPS, S5_STATE, S5_GROUP
    dt = jnp.exp(log_step)[..., None]
    er = jnp.exp(a_re * dt)
    lam1 = (er * jnp.cos(a_im * dt), er * jnp.sin(a_im * dt))
    den = a_re * a_re + a_im * a_im
    xr, xi = lam1[0] - 1.0, lam1[1]
    coef = ((xr * a_re + xi * a_im) / den, (xi * a_re - xr * a_im) / den)
    bb_re = coef[0][..., None] * b_re - coef[1][..., None] * b_im
    bb_im = coef[0][..., None] * b_im + coef[1][..., None] * b_re
    pows = [lam1]
    for _ in range(7):
        pows.append(_cmul(pows[-1], lam1))

    eye = jnp.eye(S5_BLK_GROUPS, dtype=F32)

    def blockdiag_in(m):
        m = m.reshape(2, S5_NBLK, S5_BLK_GROUPS, P, Cg)
        return jnp.einsum('dbgpc,gh->dbgchp', m, eye).reshape(2, S5_NBLK, S5_BLK_CH, S5_BLK_ST)

    def blockdiag_out(m):
        m = m.reshape(2, S5_NBLK, S5_BLK_GROUPS, Cg, P)
        return jnp.einsum('dbgcp,gh->dbgphc', m, eye).reshape(2, S5_NBLK, S5_BLK_ST, S5_BLK_CH)

    bm = jnp.concatenate([blockdiag_in(bb_re), blockdiag_in(bb_im)], axis=-1).astype(BF16)
    cm = jnp.concatenate([blockdiag_out(c_re), blockdiag_out(-c_im)], axis=-2).astype(BF16)

    def lay(v):
        return v.reshape(2, S5_NBLK, S5_BLK_ST)

    def table(rev):
        power = lambda k: jnp.concatenate([lay(pows[k][0]), lay(pows[k][1])], axis=-1)
        rows = [power(7 - r if rev else r) for r in range(8)]
        zero = jnp.zeros_like(rows[0])
        for s in (1, 2, 4):
            for r in range(8):
                inside = (r < 8 - s) if rev else (r >= s)
                rows.append(power(s - 1) if inside else zero)
        return jnp.stack(rows, axis=2)

    return bm, cm, jnp.stack([table(False)[0], table(True)[1]])


def _log_sigmoid(z):
    return jnp.minimum(z, 0.0) - jnp.log(1.0 + jnp.exp(-jnp.abs(z)))


def _gla_kernel(qf_ref, kf_ref, vf_ref, af_ref, qb_ref, kb_ref, vb_ref, ab_ref, wa_ref, ba_ref,
                of_ref, ob_ref, st_ref, *, lt):
    t = pl.program_id(1)
    C = GLA_CHUNK

    @pl.when(t == 0)
    def _():
        st_ref[...] = jnp.zeros_like(st_ref)

    ri = lax.broadcasted_iota(jnp.int32, (C, C), 0)
    ci = lax.broadcasted_iota(jnp.int32, (C, C), 1)
    refs = ((qf_ref, kf_ref, vf_ref, af_ref, of_ref), (qb_ref, kb_ref, vb_ref, ab_ref, ob_ref))
    tris, tri_bs, g_alls = [], [], []
    for d in range(2):
        z = _dot(refs[d][3][...].astype(BF16), wa_ref[d]) + ba_ref[d:d + 1, :]
        g_alls.append(_log_sigmoid(z) * (1.0 / GLA_TAU))
        tri = (ci >= ri) if d == 1 else (ci <= ri)
        tris.append(tri)
        tri_bs.append(jnp.where(tri, 1.0, 0.0).astype(BF16))

    def chunk(d, hh, c):
        q_ref, k_ref, v_ref, _, o_ref = refs[d]
        rev = d == 1
        rows = slice(c * C, (c + 1) * C)
        kc = slice(hh * GLA_DK, (hh + 1) * GLA_DK)
        vc = slice(hh * GLA_DV, (hh + 1) * GLA_DV)
        g = g_alls[d][rows, kc]
        g_hi = g.astype(BF16)
        g_lo = (g - g_hi.astype(F32)).astype(BF16)
        b = _dot(tri_bs[d], g_hi) + _dot(tri_bs[d], g_lo)
        b_tot = b[0:1, :] if rev else b[C - 1:C, :]
        q = q_ref[rows, kc] * (GLA_DK ** -0.5)
        k = k_ref[rows, kc]
        v = v_ref[rows, vc].astype(BF16)
        q_d = (q * jnp.exp(b)).astype(BF16)
        k_d = (k * jnp.exp(-b)).astype(BF16)
        k_s = (k * jnp.exp(b_tot - b)).astype(BF16)
        att = lax.dot_general(q_d, k_d, NT_DIMS, preferred_element_type=F32)
        att = jnp.where(tris[d], att, 0.0).astype(BF16)
        s_t = st_ref[d, hh]
        o = _dot(att, v) + lax.dot_general(q_d, s_t.astype(BF16), NT_DIMS, preferred_element_type=F32)
        o_ref[rows, vc] = o
        st_ref[d, hh] = s_t * jnp.exp(b_tot) + lax.dot_general(v, k_s, TN_DIMS, preferred_element_type=F32)

    n_chunks = lt // C
    for c in range(n_chunks):
        for hh in range(GLA_HPS):
            chunk(0, hh, c)
            chunk(1, hh, n_chunks - 1 - c)


def gla_scan(p, wa, ba, *, ctx_len, lt=SEQ_TILE):
    T = p.shape[0]
    n_tiles = T // lt
    n_ctx = ctx_len // lt
    tf = functools.partial(_time_tile, n_tiles=n_tiles, n_ctx_tiles=n_ctx, rev=False)
    tb = functools.partial(_time_tile, n_tiles=n_tiles, n_ctx_tiles=n_ctx, rev=True)
    dk2, dv2 = GLA_HPS * GLA_DK, GLA_HPS * GLA_DV

    def stream(tt):
        return [
            pl.BlockSpec((lt, dk2), lambda h, t: (tt(t), OFF_Q // dk2 + h)),
            pl.BlockSpec((lt, dk2), lambda h, t: (tt(t), OFF_K // dk2 + h)),
            pl.BlockSpec((lt, dv2), lambda h, t: (tt(t), OFF_V // dv2 + h)),
            pl.BlockSpec((lt, LANE), lambda h, t: (tt(t), OFF_ALPHA // LANE)),
        ]

    out = jax.ShapeDtypeStruct((T, D_GLA_V), F32)
    return pl.pallas_call(
        functools.partial(_gla_kernel, lt=lt),
        grid=(GLA_HEADS // GLA_HPS, n_tiles),
        in_specs=stream(tf) + stream(tb) + [
            pl.BlockSpec((2, LANE, dk2), lambda h, t: (0, 0, h)),
            pl.BlockSpec((2, dk2), lambda h, t: (0, h)),
        ],
        out_specs=[pl.BlockSpec((lt, dv2), lambda h, t: (tf(t), h)),
                   pl.BlockSpec((lt, dv2), lambda h, t: (tb(t), h))],
        out_shape=[out, out],
        scratch_shapes=[pltpu.VMEM((2, GLA_HPS, GLA_DV, GLA_DK), F32)],
        compiler_params=_cparams(("arbitrary", "arbitrary")),
        name="gla_scan",
    )(p, p, p, p, p, p, p, p, wa, ba)


def _finish_kernel(u_ref, ys_f_ref, ys_b_ref, d_ref, wglu_ref, bglu_ref,
                   og_f_ref, og_b_ref, gate_ref, gn_ref,
                   x0_ref, conv_ref, s_ref, hb_ref, o_ref):
    u = u_ref[...]
    y = ys_f_ref[...] + ys_b_ref[...] + d_ref[...] * u
    zz = _gelu(y)
    s5 = zz * _sigmoid(_dot(zz.astype(BF16), wglu_ref[...]) + bglu_ref[...])
    o_ref[:, 0:D_S5] = s5.astype(BF16)

    gate = gate_ref[...]
    for h in range(GLA_HEADS):
        cols = slice(h * GLA_DV, (h + 1) * GLA_DV)
        o = og_f_ref[:, cols] + og_b_ref[:, cols]
        ms = jnp.mean(o * o, axis=-1, keepdims=True)
        on = o * lax.rsqrt(ms + EPS) * gn_ref[...]
        gt = gate[:, cols]
        o_ref[:, D_S5 + h * GLA_DV:D_S5 + (h + 1) * GLA_DV] = (on * (gt * _sigmoid(gt))).astype(BF16)

    s = s_ref[...]
    hy = x0_ref[...] * (conv_ref[...] + hb_ref[...] * s)
    o_ref[:, D_S5 + D_GLA_V:] = hy.astype(BF16)


def mixer_finish(p, ys_f, ys_b, s5_d, w_glu, b_glu, og_f, og_b, gn, x0, conv, s, hy_bias, *, tm):
    T = p.shape[0]
    row = lambda w, off=0: pl.BlockSpec((tm, w), lambda i: (i, off))
    const = lambda r, w: pl.BlockSpec((r, w), lambda i: (0, 0))
    return pl.pallas_call(
        _finish_kernel,
        grid=(T // tm,),
        in_specs=[
            row(D_S5, OFF_S5 // D_S5), row(D_S5), row(D_S5), const(1, D_S5), const(D_S5, D_S5), const(1, D_S5),
            row(D_GLA_V), row(D_GLA_V), row(D_GLA_V, OFF_GATE // D_GLA_V), const(1, GLA_DV),
            row(D_HY), row(D_HY), row(D_HY), const(1, D_HY),
        ],
        out_specs=pl.BlockSpec((tm, D_MODEL), lambda i: (i, 0)),
        out_shape=jax.ShapeDtypeStruct((T, D_MODEL), BF16),
        compiler_params=_cparams(("arbitrary",)),
        name="mixer_finish",
    )(p, ys_f, ys_b, s5_d.reshape(1, D_S5), w_glu, b_glu.reshape(1, D_S5),
      og_f, og_b, p, gn.reshape(1, GLA_DV), x0, conv, s, hy_bias.reshape(1, D_HY))


def _peer_kernel(x_ref, u_ref, vt_ref, s2_ref, e2_ref, s1_ref, e1_ref, tau_ref, o_ref,
                 act_ref, p_ref, *, n_i1, tm, cw):
    j = pl.program_id(1)

    @pl.when(j == 0)
    def _():
        o_ref[...] = jnp.zeros_like(o_ref)

    act_ref[...] = lax.dot_general(u_ref[...], x_ref[...], NT_DIMS, preferred_element_type=F32)
    K = PEER_KEYS
    sub = SUBLANE // n_i1
    base = (j % sub) * n_i1 if sub > 1 else 0
    for a in range(n_i1):
        r1 = pl.ds(base + a, 1)
        for cb in range(tm // cw):
            cols = slice(cb * cw, (cb + 1) * cw)
            w = jnp.zeros((K, cw), F32)
            for h in range(PEER_HEADS):
                tot = s1_ref[h, r1, cols] + s2_ref[h, :, cols]
                val = e1_ref[h, r1, cols] * e2_ref[h, :, cols]
                w = w + jnp.where(tot >= tau_ref[h, :, cols], val, 0.0)
            act = act_ref[a * K:(a + 1) * K, cols]
            p_ref[a * K:(a + 1) * K, cols] = (w * _gelu(act)).astype(BF16)
    o_ref[...] += lax.dot_general(vt_ref[...], p_ref[...], TN_DIMS, preferred_element_type=F32)


def peer_dense(xn, u_tab, vt_tab, s1t, e1t, s2t, e2t, taut, *, tm, te, cw=256):
    T, D = xn.shape
    E = u_tab.shape[0]
    H, K = PEER_HEADS, PEER_KEYS
    n_i1 = te // K
    sub = SUBLANE // n_i1
    i1_spec = pl.BlockSpec((H, SUBLANE, tm), lambda i, j: (0, j // sub, i))
    once = pl.Buffered(1)
    tab_spec = pl.BlockSpec((H, K, tm), lambda i, j: (0, 0, i), pipeline_mode=once)
    return pl.pallas_call(
        functools.partial(_peer_kernel, n_i1=n_i1, tm=tm, cw=cw),
        grid=(T // tm, E // te),
        in_specs=[
            pl.BlockSpec((tm, D), lambda i, j: (i, 0), pipeline_mode=once),
            pl.BlockSpec((te, D), lambda i, j: (j, 0)),
            pl.BlockSpec((te, D), lambda i, j: (j, 0)),
            tab_spec, tab_spec,
            i1_spec, i1_spec,
            pl.BlockSpec((H, 1, tm), lambda i, j: (0, 0, i), pipeline_mode=once),
        ],
        out_specs=pl.BlockSpec((D, tm), lambda i, j: (0, i)),
        out_shape=jax.ShapeDtypeStruct((D, T), F32),
        scratch_shapes=[pltpu.VMEM((te, tm), F32), pltpu.VMEM((te, tm), BF16)],
        compiler_params=_cparams(("arbitrary", "arbitrary")),
        name="peer_dense",
    )(xn, u_tab, vt_tab, s2t, e2t, s1t, e1t, taut)


NEG_BIG = -3.0e38
N_CAND = PEER_TOPK + 8 * 7 + 8


def _peer_prep_kernel(q_ref, k_ref, s1_ref, e1_ref, s2_ref, e2_ref, tau_ref, top_ref, cand_ref):
    R = PEER_TOPK
    half_w = PEER_DQ // 2

    def nt(a, b):
        return lax.dot_general(a, b, NT_DIMS, preferred_element_type=F32)

    def scores(half):
        qh = q_ref[:, half * half_w:(half + 1) * half_w]
        q_hi = qh.astype(BF16)
        q_lo = (qh - q_hi.astype(F32)).astype(BF16)
        kk = k_ref[half]
        k_hi = kk.astype(BF16)
        k_lo = (kk - k_hi.astype(F32)).astype(BF16)
        return nt(k_hi, q_hi) + (nt(k_hi, q_lo) + nt(k_lo, q_hi))

    def sorted_top(s, slot):
        work = s
        for r in range(R):
            m = jnp.max(work, axis=0, keepdims=True)
            top_ref[slot, r:r + 1, :] = m
            work = jnp.where(work == m, NEG_BIG, work)

    s1 = scores(0)
    s2 = scores(1)
    sorted_top(s1, 0)
    sorted_top(s2, 1)
    a = top_ref[0]
    b = top_ref[1]
    cand_ref[0:R, :] = a[0:1, :] + b
    for i in range(1, 8):
        cand_ref[R + 8 * (i - 1):R + 8 * i, :] = a[i:i + 1, :] + b[0:8, :]
    cand_ref[R + 56:R + 64, :] = a[8:16, :] + b[0:1, :]
    work = cand_ref[...]
    m0 = jnp.max(work, axis=0, keepdims=True)
    m = m0
    z = jnp.ones_like(m0)
    for r in range(1, R):
        work = jnp.where(work == m, NEG_BIG, work)
        m = jnp.max(work, axis=0, keepdims=True)
        z = z + jnp.exp(m - m0)
    tau_ref[0] = m
    s1_ref[0] = s1
    s2_ref[0] = s2
    e1_ref[0] = jnp.exp(s1 - a[0:1, :]) * (1.0 / z)
    e2_ref[0] = jnp.exp(s2 - b[0:1, :])


def peer_prep(q, keys, *, tm):
    T = q.shape[0]
    H, K = PEER_HEADS, PEER_KEYS
    tab = jax.ShapeDtypeStruct((H, K, T), F32)
    tab_spec = pl.BlockSpec((1, K, tm), lambda i, h: (h, 0, i))
    return pl.pallas_call(
        _peer_prep_kernel,
        grid=(T // tm, H),
        in_specs=[
            pl.BlockSpec((tm, PEER_DQ), lambda i, h: (i, h)),
            pl.BlockSpec((None, 2, K, PEER_DQ // 2), lambda i, h: (h, 0, 0, 0)),
        ],
        out_specs=[tab_spec, tab_spec, tab_spec, tab_spec, pl.BlockSpec((1, 1, tm), lambda i, h: (h, 0, i))],
        out_shape=[tab, tab, tab, tab, jax.ShapeDtypeStruct((H, 1, T), F32)],
        scratch_shapes=[pltpu.VMEM((2, PEER_TOPK, tm), F32), pltpu.VMEM((N_CAND, tm), F32)],
        compiler_params=_cparams(("arbitrary", "arbitrary")),
        name="peer_prep",
    )(q, keys)


def _transpose_gated_res_kernel(h_ref, ft_ref, gate_ref, o_ref, *, ctx_len, tm):
    i = pl.program_id(0)
    row = i * tm + lax.broadcasted_iota(jnp.int32, (tm, 1), 0)
    gate = jnp.where(row < ctx_len, gate_ref[0:1, :], gate_ref[1:2, :])
    o_ref[...] = h_ref[...] + gate * ft_ref[...].T


def transpose_gated_res(h, ft, gates, *, ctx_len, tm):
    T, D = h.shape
    return pl.pallas_call(
        functools.partial(_transpose_gated_res_kernel, ctx_len=ctx_len, tm=tm),
        grid=(T // tm,),
        in_specs=[
            pl.BlockSpec((tm, D), lambda i: (i, 0)),
            pl.BlockSpec((D, tm), lambda i: (0, i)),
            pl.BlockSpec((2, D), lambda i: (0, 0)),
        ],
        out_specs=pl.BlockSpec((tm, D), lambda i: (i, 0)),
        out_shape=jax.ShapeDtypeStruct((T, D), F32),
        compiler_params=_cparams(("arbitrary",)),
        name="transpose_gated_res",
    )(h, ft, gates)


def _rmsnorm_kernel(x_ref, g_ref, o_ref):
    x = x_ref[...]
    ms = jnp.mean(x * x, axis=-1, keepdims=True)
    o_ref[...] = x * lax.rsqrt(ms + EPS) * g_ref[...]


def rmsnorm_rows(x, g, *, tm):
    T, D = x.shape
    return pl.pallas_call(
        _rmsnorm_kernel,
        grid=(T // tm,),
        in_specs=[pl.BlockSpec((tm, D), lambda i: (i, 0)), pl.BlockSpec((1, D), lambda i: (0, 0))],
        out_specs=pl.BlockSpec((tm, D), lambda i: (i, 0)),
        out_shape=jax.ShapeDtypeStruct((T, D), F32),
        compiler_params=_cparams(("arbitrary",)),
        name="rmsnorm_rows",
    )(x, g.reshape(1, D))


def grid_sincos(n_tokens, dim):
    rows = n_tokens // GRID_W
    n_freq = dim // 4
    omega = 1.0 / (POS_BASE ** (jnp.arange(n_freq, dtype=F32) / n_freq))

    def enc(count):
        a = jnp.arange(count).astype(F32)[:, None] * omega[None, :]
        return jnp.concatenate([jnp.sin(a), jnp.cos(a)], axis=-1)

    row_code = jnp.broadcast_to(enc(rows)[:, None, :], (rows, GRID_W, dim // 2))
    col_code = jnp.broadcast_to(enc(GRID_W)[None, :, :], (rows, GRID_W, dim // 2))
    return jnp.concatenate([row_code, col_code], axis=-1).reshape(n_tokens, dim)


def _split_bf16(a):
    hi = a.astype(BF16)
    return hi, (a - hi.astype(F32)).astype(BF16)


def _dot3s(a, b_hi, b_lo):
    a_hi, a_lo = _split_bf16(a)
    return _dot(a_hi, b_hi) + (_dot(a_hi, b_lo) + _dot(a_lo, b_hi))


def _dot3(a, b):
    return _dot3s(a, *_split_bf16(b))


def _adaln_kernel(c_ref, w_ref, b_ref, o_ref):
    c = c_ref[...]
    o_ref[...] = _dot3(c * _sigmoid(c), w_ref[...]) + b_ref[...]


def adaln(cond, w, b, *, tn=1536):
    R, D = cond.shape
    N = w.shape[1]
    return pl.pallas_call(
        _adaln_kernel,
        grid=(N // tn,),
        in_specs=[pl.BlockSpec((R, D), lambda j: (0, 0)),
                  pl.BlockSpec((D, tn), lambda j: (0, j)),
                  pl.BlockSpec((1, tn), lambda j: (0, j))],
        out_specs=pl.BlockSpec((R, tn), lambda j: (0, j)),
        out_shape=jax.ShapeDtypeStruct((R, N), F32),
        compiler_params=_cparams(("arbitrary",)),
        name="adaln",
    )(cond, w, b.reshape(1, N))


def _hyena_pre_kernel(x0_ref, x1_ref, v_ref, w_ref, b_ref, x0o_ref, s_ref, *, ctx_len):
    T = x0_ref.shape[0]
    row = lax.broadcasted_iota(jnp.int32, (T, 1), 0)
    first = jnp.logical_or(row == 0, row == ctx_len)
    last = jnp.logical_or(row == ctx_len - 1, row == T - 1)

    def conv(ref, part):
        x = ref[...]
        prev = jnp.where(first, 0.0, pltpu.roll(x, 1, 0))
        nxt = jnp.where(last, 0.0, pltpu.roll(x, T - 1, 0))
        w = lambda tap: w_ref[tap, part:part + 1, :]
        return w(0) * prev + w(1) * x + w(2) * nxt + b_ref[part:part + 1, :]

    x0o_ref[...] = conv(x0_ref, 0)
    s_ref[...] = conv(x1_ref, 1) * conv(v_ref, 2)


def hyena_pre(p, conv_w, conv_b, *, ctx_len):
    T = p.shape[0]
    nb = D_HY // LANE
    col = lambda part: pl.BlockSpec((T, LANE), lambda c: (0, OFF_HY // LANE + part * nb + c))
    out = jax.ShapeDtypeStruct((T, D_HY), F32)
    return pl.pallas_call(
        functools.partial(_hyena_pre_kernel, ctx_len=ctx_len),
        grid=(nb,),
        in_specs=[col(0), col(1), col(2),
                  pl.BlockSpec((HY_SHORT, 3, LANE), lambda c: (0, 0, c)),
                  pl.BlockSpec((3, LANE), lambda c: (0, c))],
        out_specs=[pl.BlockSpec((T, LANE), lambda c: (0, c))] * 2,
        out_shape=[out, out],
        compiler_params=_cparams(("arbitrary",)),
        name="hyena_pre",
    )(p, p, p, conv_w.reshape(HY_SHORT, 3, D_HY), conv_b.reshape(3, D_HY))


def _hyena_filter_kernel(band_ref, w1_ref, b1_ref, w2_ref, b2_ref, w3_ref, fr_ref, dec_ref, o_ref, *, n, tr):
    i = pl.program_id(0)
    ri = (i * tr + lax.broadcasted_iota(jnp.int32, (tr, 1), 0)).astype(F32)
    t = ri * (1.0 / (n - 1))
    w = ri * (2.0 * math.pi / n)
    lane = lax.broadcasted_iota(jnp.int32, (1, LANE), 1)
    arg = w * band_ref[...]
    z = jnp.where(lane == 0, t,
                  jnp.where(lane <= HY_BANDS, jnp.cos(arg),
                            jnp.where(lane <= 2 * HY_BANDS, -jnp.sin(arg), 0.0)))
    h = jnp.sin(fr_ref[0:1, :] * (_dot3(z, w1_ref[...]) + b1_ref[...]))
    h = jnp.sin(fr_ref[1:2, :] * (_dot3(h, w2_ref[...]) + b2_ref[...]))
    h = _dot3(h, w3_ref[...]) * jnp.exp(-t * dec_ref[...])
    col = lax.broadcasted_iota(jnp.int32, (1, 2 * D_HY), 1)
    o_ref[...] = jnp.where(jnp.logical_and(ri == 0.0, col >= D_HY), 0.0, h)


def hyena_filter(n, w1, b1, w2, b2, w3, freq, decay, *, tr=256):
    bands = jnp.linspace(1e-4, HY_BANDS - 1, HY_BANDS, dtype=F32)
    band_row = jnp.zeros((1, LANE), F32).at[0, 1:1 + HY_BANDS].set(bands).at[0, 1 + HY_BANDS:1 + 2 * HY_BANDS].set(bands)
    w1p = jnp.zeros((LANE, HY_HIDDEN), F32).at[:HY_FEAT].set(w1)
    const = lambda a: pl.BlockSpec(a.shape, lambda i: (0,) * a.ndim)
    args = (band_row, w1p, b1.reshape(1, -1), w2, b2.reshape(1, -1), w3, freq,
            jnp.abs(decay).reshape(1, 2 * D_HY))
    return pl.pallas_call(
        functools.partial(_hyena_filter_kernel, n=n, tr=tr),
        grid=(n // tr,),
        in_specs=[const(a) for a in args],
        out_specs=pl.BlockSpec((tr, 2 * D_HY), lambda i: (i, 0)),
        out_shape=jax.ShapeDtypeStruct((n, 2 * D_HY), F32),
        compiler_params=_cparams(("arbitrary",)),
        name="hyena_filter",
    )(*args)


FFT_R = 128
FFT_N = FFT_R * FFT_R
FFT_CH = 32


def _dft_tables():
    r = np.arange(FFT_R)
    ang = 2.0 * np.pi * np.outer(r, r) / FFT_R
    c, s = np.cos(ang), np.sin(ang)
    angt = 2.0 * np.pi * np.outer(r, r) / FFT_N
    ct, st = np.cos(angt), np.sin(angt)

    def split(m):
        m = jnp.asarray(m, F32)
        hi = m.astype(BF16)
        return jnp.stack([hi, (m - hi.astype(F32)).astype(BF16)])

    fwd_a = split(np.concatenate([c, -s], axis=1))
    fwd_b = split(np.block([[c, -s], [s, c]]))
    inv_b = split(np.block([[c, s], [-s, c]]) / FFT_N)
    inv_a = split(np.concatenate([c, -s], axis=0))
    tw = (jnp.asarray(ct, F32), jnp.asarray(-st, F32))
    twc = (jnp.asarray(ct, F32), jnp.asarray(st, F32))
    return fwd_a, fwd_b, inv_b, inv_a, tw, twc


def _twiddle_transpose(y, twr, twi, dst_ref, nc):
    y = y.reshape(nc, FFT_R, 2 * FFT_R)
    yr, yi = y[:, :, :FFT_R], y[:, :, FFT_R:]
    zr = yr * twr - yi * twi
    zi = yr * twi + yi * twr
    for c in range(nc):
        dst_ref[c * FFT_R:(c + 1) * FFT_R, 0:FFT_R] = zr[c].T
        dst_ref[c * FFT_R:(c + 1) * FFT_R, FFT_R:2 * FFT_R] = zi[c].T


def _hyena_fwd_kernel(x_ref, fa_ref, twr_ref, twi_ref, fb_ref, o_ref, l_ref, yt_ref, *, nc):
    j = pl.program_id(1)
    half = FFT_R // 2

    @pl.when(j == 0)
    def _():
        zeros = jnp.zeros((half, LANE), F32)

        def body(b, carry):
            t = jnp.concatenate([x_ref[pl.ds(b, half, stride=FFT_R), :], zeros], axis=0)
            l_ref[pl.ds(b, LANE, stride=FFT_R), :] = t.T
            return carry

        lax.fori_loop(0, FFT_R, body, 0, unroll=8)

    rows = nc * FFT_R
    lc = l_ref[pl.ds(pl.multiple_of(j * rows, rows), rows), :]
    y = _dot3s(lc, fa_ref[0], fa_ref[1])
    _twiddle_transpose(y, twr_ref[...], twi_ref[...], yt_ref, nc)
    o_ref[...] = _dot3s(yt_ref[...], fb_ref[0], fb_ref[1])


def hyena_fwd_dft(x, tables):
    n, C = x.shape
    assert n * 2 == FFT_N and C % LANE == 0
    fwd_a, fwd_b, _, _, (twr, twi), _ = tables
    nc = FFT_CH
    steps = LANE // nc
    const = lambda a: pl.BlockSpec(a.shape, lambda cb, j: (0,) * a.ndim)
    return pl.pallas_call(
        functools.partial(_hyena_fwd_kernel, nc=nc),
        grid=(C // LANE, steps),
        in_specs=[pl.BlockSpec((n, LANE), lambda cb, j: (0, cb)), const(fwd_a), const(twr), const(twi), const(fwd_b)],
        out_specs=pl.BlockSpec((nc * FFT_R, 2 * FFT_R), lambda cb, j: (cb * steps + j, 0)),
        out_shape=jax.ShapeDtypeStruct((C * FFT_R, 2 * FFT_R), F32),
        scratch_shapes=[pltpu.VMEM((LANE * FFT_R, LANE), F32), pltpu.VMEM((nc * FFT_R, 2 * FFT_R), F32)],
        compiler_params=_cparams(("arbitrary", "arbitrary")),
        name="hyena_fwd_dft",
    )(x, fwd_a, twr, twi, fwd_b)


def _hyena_inv_kernel(s_ref, kf_ref, kb_ref, gb_ref, twr_ref, twi_ref, ga_ref, o_ref, lr_ref, at_ref, *, nc, steps):
    j = pl.program_id(1)
    R = FFT_R
    s, kf, kb = s_ref[...], kf_ref[...], kb_ref[...]
    sr, si = s[:, :R], s[:, R:]
    kr = kf[:, :R] + kb[:, :R]
    ki = kf[:, R:] - kb[:, R:]
    p = jnp.concatenate([sr * kr - si * ki, sr * ki + si * kr], axis=1)
    a = _dot3s(p, gb_ref[0], gb_ref[1])
    _twiddle_transpose(a, twr_ref[...], twi_ref[...], at_ref, nc)
    rows = nc * R
    lr_ref[pl.ds(pl.multiple_of(j * rows, rows), rows), :] = _dot3s(at_ref[...], ga_ref[0], ga_ref[1])

    @pl.when(j == steps - 1)
    def _():
        def body(b, carry):
            t = lr_ref[pl.ds(b, LANE, stride=R), :]
            o_ref[pl.ds(b, R // 2, stride=R), :] = t.T[0:R // 2, :]
            return carry

        lax.fori_loop(0, R, body, 0, unroll=8)


def hyena_inv_dft(s_f, k_f, tables):
    C = s_f.shape[0] // FFT_R
    _, _, inv_b, inv_a, _, (twr, twi) = tables
    nc = FFT_CH
    steps = LANE // nc
    kb_off = C // nc
    const = lambda a: pl.BlockSpec(a.shape, lambda cb, j: (0,) * a.ndim)
    blk = lambda off: pl.BlockSpec((nc * FFT_R, 2 * FFT_R), lambda cb, j: (off + cb * steps + j, 0))
    return pl.pallas_call(
        functools.partial(_hyena_inv_kernel, nc=nc, steps=steps),
        grid=(C // LANE, steps),
        in_specs=[blk(0), blk(0), blk(kb_off), const(inv_b), const(twr), const(twi), const(inv_a)],
        out_specs=pl.BlockSpec((FFT_N // 2, LANE), lambda cb, j: (0, cb)),
        out_shape=jax.ShapeDtypeStruct((FFT_N // 2, C), F32),
        scratch_shapes=[pltpu.VMEM((LANE * FFT_R, LANE), F32), pltpu.VMEM((nc * FFT_R, 2 * FFT_R), F32)],
        compiler_params=_cparams(("arbitrary", "arbitrary")),
        name="hyena_inv_dft",
    )(s_f, k_f, k_f, inv_b, twr, twi, inv_a)


def _hyena_ctx_kernel(s_ref, k_ref, fw_ref, iv_ref, o_ref):
    n = s_ref.shape[0]
    N = 2 * n
    xs = _dot3(fw_ref[...], s_ref[...])
    xk = _dot3(fw_ref[...], k_ref[...])
    sr, si = xs[:N], xs[N:]
    kr = xk[:N, :D_HY] + xk[:N, D_HY:]
    ki = xk[N:, :D_HY] - xk[N:, D_HY:]
    p = jnp.concatenate([sr * kr - si * ki, sr * ki + si * kr], axis=0)
    o_ref[...] = _dot3(iv_ref[...], p)


def hyena_ctx_conv(s_c, k_c):
    n = s_c.shape[0]
    N = 2 * n
    ang = 2.0 * np.pi * np.outer(np.arange(N), np.arange(N)) / N
    c, s = np.cos(ang), np.sin(ang)
    fw = jnp.asarray(np.concatenate([c[:, :n], -s[:, :n]], axis=0), F32)
    iv = jnp.asarray(np.concatenate([c[:n, :], -s[:n, :]], axis=1) / N, F32)
    full = lambda a: pl.BlockSpec(a.shape, lambda i: (0,) * a.ndim)
    return pl.pallas_call(
        _hyena_ctx_kernel,
        grid=(1,),
        in_specs=[full(s_c), full(k_c), full(fw), full(iv)],
        out_specs=pl.BlockSpec((n, D_HY), lambda i: (0, 0)),
        out_shape=jax.ShapeDtypeStruct((n, D_HY), F32),
        compiler_params=_cparams(("arbitrary",)),
        name="hyena_ctx_conv",
    )(s_c, k_c, fw, iv)


def hyena_mixer_parts(p, conv_w, conv_b, filt, tables, *, ctx_len):
    T = p.shape[0]
    x0, s = hyena_pre(p, conv_w, conv_b, ctx_len=ctx_len)
    k_lat = hyena_filter(T - ctx_len, *filt)
    k_ctx = hyena_filter(ctx_len, *filt)
    conv_l = hyena_inv_dft(hyena_fwd_dft(s[ctx_len:], tables), hyena_fwd_dft(k_lat, tables), tables)
    conv_c = hyena_ctx_conv(s[:ctx_len], k_ctx)
    return x0, jnp.concatenate([conv_c, conv_l], axis=0), s


def _pack_w_in(w_in_l):
    widths = (D_S5, D_GLA_K, D_GLA_K, D_GLA_V, D_GLA_V, 2 * GLA_RANK, 3 * D_HY)
    offs = [0]
    for wd in widths:
        offs.append(offs[-1] + wd)
    s5, q, k, v, gate, alpha, hy = (w_in_l[:, offs[i]:offs[i + 1]] for i in range(7))
    pad = jnp.zeros((w_in_l.shape[0], IN_PACKED - OFF_ALPHA - 2 * GLA_RANK), w_in_l.dtype)
    return jnp.concatenate([v, gate, s5, q, k, hy, alpha, pad], axis=1).astype(BF16)


def kernel(x, c, ctx, c_ctx, w_ada, b_ada, g_norm1, g_norm2, w_in, s5_a_re, s5_a_im, s5_log_step, s5_b_re, s5_b_im, s5_c_re, s5_c_im, s5_d, s5_w_glu, s5_b_glu, gla_w_alpha, gla_b_alpha, gla_g_norm, hy_conv_w, hy_conv_b, hy_f_w1, hy_f_b1, hy_f_w2, hy_f_b2, hy_f_w3, hy_f_freq, hy_decay, hy_bias, w_out, peer_w_q, peer_keys, peer_u, peer_v, g_final):
    L = x.shape[1]
    Lc = ctx.shape[1]
    T = L + Lc
    TM = 768
    h = jnp.concatenate([ctx[0], x[0] + grid_sincos(L, D_MODEL)], axis=0)
    cond = jnp.zeros((SUBLANE, D_MODEL), F32).at[0].set(c_ctx).at[1].set(c[0])
    tables = _dft_tables()

    for l in range(DEPTH):
        m = adaln(cond, w_ada[l], b_ada[l])[0:2]
        sh1, sc1, gt1, sh2, sc2, gt2 = jnp.split(m, 6, axis=-1)
        mod1 = jnp.stack([sh1[0], sc1[0], sh1[1], sc1[1]], axis=0)
        mod2 = jnp.stack([sh2[0], sc2[0], sh2[1], sc2[1]], axis=0)

        p = norm_mod_matmul(h, g_norm1[l], mod1, _pack_w_in(w_in[l]), ctx_len=Lc, tm=TM, tn=768)

        bm, cm, pw = s5_prepare(s5_a_re[l], s5_a_im[l], s5_log_step[l], s5_b_re[l], s5_b_im[l],
                                s5_c_re[l], s5_c_im[l])
        ys_f, ys_b = s5_scan(p, bm, cm, pw, ctx_len=Lc)

        wa = jnp.zeros((2, LANE, D_GLA_K), F32)
        wa = wa.at[0, 0:GLA_RANK].set(gla_w_alpha[l, 0]).at[1, GLA_RANK:2 * GLA_RANK].set(gla_w_alpha[l, 1])
        wa = wa.astype(BF16)
        og_f, og_b = gla_scan(p, wa, gla_b_alpha[l], ctx_len=Lc)

        filt = (hy_f_w1[l], hy_f_b1[l], hy_f_w2[l], hy_f_b2[l], hy_f_w3[l], hy_f_freq[l], hy_decay[l])
        x0, conv, s = hyena_mixer_parts(p, hy_conv_w[l], hy_conv_b[l], filt, tables, ctx_len=Lc)

        mix = mixer_finish(p, ys_f, ys_b, s5_d[l], s5_w_glu[l].astype(BF16), s5_b_glu[l],
                           og_f, og_b, gla_g_norm[l], x0, conv, s, hy_bias[l], tm=256)
        h = matmul_gated_res(mix, w_out[l].astype(BF16), h, gt1, ctx_len=Lc, tm=TM, tn=1024)

        q, xn = norm_mod_matmul(h, g_norm2[l], mod2, peer_w_q[l].astype(BF16),
                                ctx_len=Lc, tm=TM, tn=1024, emit_xn=True)
        s1t, e1t, s2t, e2t, taut = peer_prep(q, peer_keys[l], tm=TM)
        ft = peer_dense(xn, peer_u[l].astype(BF16), peer_v[l].astype(BF16),
                        s1t, e1t, s2t, e2t, taut, tm=TM, te=1024)
        h = transpose_gated_res(h, ft, gt2, ctx_len=Lc, tm=256)

    out = rmsnorm_rows(h[Lc:], g_final, tm=512)
    return out[None]
```

```python
import functools
import math

import jax
import jax.numpy as jnp
import numpy as np
from jax import lax
from jax.experimental import pallas as pl
from jax.experimental.pallas import tpu as pltpu

F32 = jnp.float32
BF16 = jnp.bfloat16

D_MODEL = 2048
DEPTH = 4
GRID_W = 64
EPS = 1e-6
POS_BASE = 10000.0

D_S5 = D_MODEL // 4
S5_GROUP = 16
S5_GROUPS = D_S5 // S5_GROUP
S5_STATE = 64
S5_BLK_GROUPS = 8
S5_BLK_CH = S5_BLK_GROUPS * S5_GROUP
S5_BLK_ST = S5_BLK_GROUPS * S5_STATE
S5_NBLK = S5_GROUPS // S5_BLK_GROUPS

GLA_HEADS = 4
D_GLA_K = D_MODEL // 4
D_GLA_V = D_MODEL // 2
GLA_DK = D_GLA_K // GLA_HEADS
GLA_DV = D_GLA_V // GLA_HEADS
GLA_RANK = 16
GLA_TAU = 16.0
GLA_CHUNK = 64
GLA_HPS = 4

D_HY = D_MODEL // 4
HY_SHORT = 3
HY_BANDS = 16
HY_FEAT = 1 + 2 * HY_BANDS
HY_HIDDEN = 64
HY_TARGET = 1e-2
HY_MIN_DECAY = -math.log(HY_TARGET) / 1.5
HY_MAX_DECAY = -math.log(HY_TARGET) / 0.3

PEER_HEADS = 8
PEER_KEYS = 128
PEER_EXPERTS = PEER_KEYS * PEER_KEYS
PEER_DQ = 256
PEER_TOPK = 16

OFF_V = 0
OFF_GATE = OFF_V + D_GLA_V
OFF_S5 = OFF_GATE + D_GLA_V
OFF_Q = OFF_S5 + D_S5
OFF_K = OFF_Q + D_GLA_K
OFF_HY = OFF_K + D_GLA_K
OFF_ALPHA = OFF_HY + 3 * D_HY
LANE = 128
SUBLANE = 8
S5_SCAN_SHIFTS = (1, 2, 4)
S5_TABLE_ROWS = SUBLANE * (1 + len(S5_SCAN_SHIFTS))
IN_PACKED = OFF_ALPHA + 2 * LANE

SEQ_TILE = 256
ROW_TILE = 768
SMALL_ROW_TILE = 256
OUT_ROW_TILE = 512
IN_PROJ_COLS = 768
PROJ_COLS = 1024
PEER_EXPERT_TILE = 1024
VMEM_LIMIT = 56 * 1024 * 1024

NT_DIMS = (((1,), (1,)), ((), ()))
TN_DIMS = (((0,), (0,)), ((), ()))


def _cparams(sem):
    return pltpu.CompilerParams(dimension_semantics=sem, vmem_limit_bytes=VMEM_LIMIT)


def _dot(a, b):
    return jnp.dot(a, b, preferred_element_type=F32)


def _gelu(x):
    k1 = -2.0 * math.sqrt(2.0 / math.pi) * math.log2(math.e)
    k2 = 0.044715 * k1
    return x / (1.0 + jnp.exp2(x * (k1 + k2 * (x * x))))


def _sigmoid(x):
    return 1.0 / (1.0 + jnp.exp(-x))


def _norm_mod_matmul_kernel(x_ref, g_ref, mod_ref, w_ref, o_ref, *rest, ctx_len, tm, emit_xn):
    if emit_xn:
        xo_ref, xn_ref = rest
    else:
        (xn_ref,) = rest
    i = pl.program_id(0)
    j = pl.program_id(1)

    @pl.when(j == 0)
    def _():
        x = x_ref[...]
        ms = jnp.mean(x * x, axis=-1, keepdims=True)
        y = x * lax.rsqrt(ms + EPS) * g_ref[...]
        row = i * tm + lax.broadcasted_iota(jnp.int32, (tm, 1), 0)
        is_ctx = row < ctx_len
        shift = jnp.where(is_ctx, mod_ref[0:1, :], mod_ref[2:3, :])
        scale = jnp.where(is_ctx, mod_ref[1:2, :], mod_ref[3:4, :])
        xn = (y * (1.0 + scale) + shift).astype(BF16)
        xn_ref[...] = xn
        if emit_xn:
            xo_ref[...] = xn

    o_ref[...] = _dot(xn_ref[...], w_ref[...])


def norm_mod_matmul(x, g, mod, w, *, ctx_len, tm, tn, emit_xn=False):
    T, D = x.shape
    N = w.shape[1]
    out_shape = [jax.ShapeDtypeStruct((T, N), F32)]
    out_specs = [pl.BlockSpec((tm, tn), lambda i, j: (i, j))]
    if emit_xn:
        out_shape.append(jax.ShapeDtypeStruct((T, D), BF16))
        out_specs.append(pl.BlockSpec((tm, D), lambda i, j: (i, 0)))
    res = pl.pallas_call(
        functools.partial(_norm_mod_matmul_kernel, ctx_len=ctx_len, tm=tm, emit_xn=emit_xn),
        grid=(T // tm, N // tn),
        in_specs=[
            pl.BlockSpec((tm, D), lambda i, j: (i, 0)),
            pl.BlockSpec((1, D), lambda i, j: (0, 0)),
            pl.BlockSpec((4, D), lambda i, j: (0, 0)),
            pl.BlockSpec((D, tn), lambda i, j: (0, j)),
        ],
        out_specs=out_specs,
        out_shape=out_shape,
        scratch_shapes=[pltpu.VMEM((tm, D), BF16)],
        compiler_params=_cparams(("arbitrary", "arbitrary")),
        name="norm_mod_matmul",
    )(x, g.reshape(1, D), mod, w)
    return res if emit_xn else res[0]


def _matmul_gated_res_kernel(a_ref, w_ref, r_ref, gate_ref, o_ref, *, ctx_len, tm):
    i = pl.program_id(0)
    row = i * tm + lax.broadcasted_iota(jnp.int32, (tm, 1), 0)
    gate = jnp.where(row < ctx_len, gate_ref[0:1, :], gate_ref[1:2, :])
    o_ref[...] = r_ref[...] + gate * _dot(a_ref[...], w_ref[...])


def matmul_gated_res(a, w, res, gates, *, ctx_len, tm, tn):
    T, K = a.shape
    N = w.shape[1]
    return pl.pallas_call(
        functools.partial(_matmul_gated_res_kernel, ctx_len=ctx_len, tm=tm),
        grid=(T // tm, N // tn),
        in_specs=[
            pl.BlockSpec((tm, K), lambda i, j: (i, 0)),
            pl.BlockSpec((K, tn), lambda i, j: (0, j)),
            pl.BlockSpec((tm, tn), lambda i, j: (i, j)),
            pl.BlockSpec((2, tn), lambda i, j: (0, j)),
        ],
        out_specs=pl.BlockSpec((tm, tn), lambda i, j: (i, j)),
        out_shape=jax.ShapeDtypeStruct((T, N), F32),
        compiler_params=_cparams(("arbitrary", "arbitrary")),
        name="matmul_gated_res",
    )(a, w, res, gates)


def _time_tile(t, n_tiles, n_ctx_tiles, rev):
    if not rev:
        return t
    return jnp.where(t < n_ctx_tiles, n_ctx_tiles - 1 - t, n_tiles - 1 - (t - n_ctx_tiles))


def _s5_kernel(uf_ref, ub_ref, bm_ref, cm_ref, pw_ref, yf_ref, yb_ref, h_ref, c_ref, *, lt):
    t = pl.program_id(1)
    ns = S5_BLK_ST
    u_refs = (uf_ref, ub_ref)
    y_refs = (yf_ref, yb_ref)

    @pl.when(t == 0)
    def _():
        c_ref[...] = jnp.zeros_like(c_ref)

    for d in range(2):
        bu = _dot(u_refs[d][...].astype(BF16), bm_ref[d])
        h_ref[d, 0] = bu[:, :ns]
        h_ref[d, 1] = bu[:, ns:]

    G = SUBLANE
    n_grp = lt // G

    def group_update(d, g, c_re, c_im):
        rev = d == 1
        gi = (n_grp - 1 - g) if rev else g
        rows = pl.ds(pl.multiple_of(gi * G, G), G)
        a_re = h_ref[d, 0, rows, :]
        a_im = h_ref[d, 1, rows, :]
        for k, s in enumerate(S5_SCAN_SHIFTS):
            level = slice(G * (k + 1), G * (k + 2))
            l_re = pw_ref[d, level, :ns]
            l_im = pw_ref[d, level, ns:]
            shift = (G - s) if rev else s
            s_re = pltpu.roll(a_re, shift, 0)
            s_im = pltpu.roll(a_im, shift, 0)
            a_re = a_re + (l_re * s_re - l_im * s_im)
            a_im = a_im + (l_re * s_im + l_im * s_re)
        p_re = pw_ref[d, 0:G, :ns]
        p_im = pw_ref[d, 0:G, ns:]
        a_re = a_re + (p_re * c_re - p_im * c_im)
        a_im = a_im + (p_re * c_im + p_im * c_re)
        h_ref[d, 0, rows, :] = a_re
        h_ref[d, 1, rows, :] = a_im
        edge = slice(0, 1) if rev else slice(G - 1, G)
        return a_re[edge, :], a_im[edge, :]

    def body(g, carry):
        f_re, f_im, b_re, b_im = carry
        f_re, f_im = group_update(0, g, f_re, f_im)
        b_re, b_im = group_update(1, g, b_re, b_im)
        return f_re, f_im, b_re, b_im

    carry = lax.fori_loop(0, n_grp, body, (c_ref[0, 0], c_ref[0, 1], c_ref[1, 0], c_ref[1, 1]), unroll=2)
    c_ref[0, 0], c_ref[0, 1], c_ref[1, 0], c_ref[1, 1] = carry

    for d in range(2):
        y_refs[d][...] = (_dot(h_ref[d, 0].astype(BF16), cm_ref[d, :ns, :])
                          + _dot(h_ref[d, 1].astype(BF16), cm_ref[d, ns:, :]))


def s5_scan(p, bm, cm, pw, *, ctx_len, lt=SEQ_TILE):
    T = p.shape[0]
    n_tiles = T // lt
    n_ctx = ctx_len // lt
    tf = functools.partial(_time_tile, n_tiles=n_tiles, n_ctx_tiles=n_ctx, rev=False)
    tb = functools.partial(_time_tile, n_tiles=n_tiles, n_ctx_tiles=n_ctx, rev=True)
    ucol = OFF_S5 // S5_BLK_CH
    out = jax.ShapeDtypeStruct((T, D_S5), F32)
    return pl.pallas_call(
        functools.partial(_s5_kernel, lt=lt),
        grid=(S5_NBLK, n_tiles),
        in_specs=[
            pl.BlockSpec((lt, S5_BLK_CH), lambda b, t: (tf(t), ucol + b)),
            pl.BlockSpec((lt, S5_BLK_CH), lambda b, t: (tb(t), ucol + b)),
            pl.BlockSpec((2, None, S5_BLK_CH, 2 * S5_BLK_ST), lambda b, t: (0, b, 0, 0)),
            pl.BlockSpec((2, None, 2 * S5_BLK_ST, S5_BLK_CH), lambda b, t: (0, b, 0, 0)),
            pl.BlockSpec((2, None, S5_TABLE_ROWS, 2 * S5_BLK_ST), lambda b, t: (0, b, 0, 0)),
        ],
        out_specs=[pl.BlockSpec((lt, S5_BLK_CH), lambda b, t: (tf(t), b)),
                   pl.BlockSpec((lt, S5_BLK_CH), lambda b, t: (tb(t), b))],
        out_shape=[out, out],
        scratch_shapes=[pltpu.VMEM((2, 2, lt, S5_BLK_ST), F32), pltpu.VMEM((2, 2, 1, S5_BLK_ST), F32)],
        compiler_params=_cparams(("arbitrary", "arbitrary")),
        name="s5_scan",
    )(p, p, bm, cm, pw)


def _cmul(a, b):
    return a[0] * b[0] - a[1] * b[1], a[0] * b[1] + a[1] * b[0]


def s5_prepare(a_re, a_im, log_step, b_re, b_im, c_re, c_im):
    G, P, Cg = S5_GROUPS, S5_STATE, S5_GROUP
    dt = jnp.exp(log_step)[..., None]
    er = jnp.exp(a_re * dt)
    lam1 = (er * jnp.cos(a_im * dt), er * jnp.sin(a_im * dt))
    den = a_re * a_re + a_im * a_im
    xr, xi = lam1[0] - 1.0, lam1[1]
    coef = ((xr * a_re + xi * a_im) / den, (xi * a_re - xr * a_im) / den)
    bb_re = coef[0][..., None] * b_re - coef[1][..., None] * b_im
    bb_im = coef[0][..., None] * b_im + coef[1][..., None] * b_re
    pows = [lam1]
    for _ in range(SUBLANE - 1):
        pows.append(_cmul(pows[-1], lam1))

    eye = jnp.eye(S5_BLK_GROUPS, dtype=F32)

    def blockdiag_in(m):
        m = m.reshape(2, S5_NBLK, S5_BLK_GROUPS, P, Cg)
        return jnp.einsum('dbgpc,gh->dbgchp', m, eye).reshape(2, S5_NBLK, S5_BLK_CH, S5_BLK_ST)

    def blockdiag_out(m):
        m = m.reshape(2, S5_NBLK, S5_BLK_GROUPS, Cg, P)
        return jnp.einsum('dbgcp,gh->dbgphc', m, eye).reshape(2, S5_NBLK, S5_BLK_ST, S5_BLK_CH)

    bm = jnp.concatenate([blockdiag_in(bb_re), blockdiag_in(bb_im)], axis=-1).astype(BF16)
    cm = jnp.concatenate([blockdiag_out(c_re), blockdiag_out(-c_im)], axis=-2).astype(BF16)

    def lay(v):
        return v.reshape(2, S5_NBLK, S5_BLK_ST)

    def table(rev):
        power = lambda k: jnp.concatenate([lay(pows[k][0]), lay(pows[k][1])], axis=-1)
        G = SUBLANE
        rows = [power(G - 1 - r if rev else r) for r in range(G)]
        zero = jnp.zeros_like(rows[0])
        for s in S5_SCAN_SHIFTS:
            for r in range(G):
                inside = (r < G - s) if rev else (r >= s)
                rows.append(power(s - 1) if inside else zero)
        return jnp.stack(rows, axis=2)

    return bm, cm, jnp.stack([table(False)[0], table(True)[1]])


def _log_sigmoid(z):
    return jnp.minimum(z, 0.0) - jnp.log(1.0 + jnp.exp(-jnp.abs(z)))


def _gla_kernel(qf_ref, kf_ref, vf_ref, af_ref, qb_ref, kb_ref, vb_ref, ab_ref, wa_ref, ba_ref,
                of_ref, ob_ref, st_ref, *, lt):
    t = pl.program_id(1)
    C = GLA_CHUNK

    @pl.when(t == 0)
    def _():
        st_ref[...] = jnp.zeros_like(st_ref)

    ri = lax.broadcasted_iota(jnp.int32, (C, C), 0)
    ci = lax.broadcasted_iota(jnp.int32, (C, C), 1)
    refs = ((qf_ref, kf_ref, vf_ref, af_ref, of_ref), (qb_ref, kb_ref, vb_ref, ab_ref, ob_ref))
    tris, tri_bs, g_alls = [], [], []
    for d in range(2):
        z = _dot(refs[d][3][...].astype(BF16), wa_ref[d]) + ba_ref[d:d + 1, :]
        g_alls.append(_log_sigmoid(z) * (1.0 / GLA_TAU))
        tri = (ci >= ri) if d == 1 else (ci <= ri)
        tris.append(tri)
        tri_bs.append(jnp.where(tri, 1.0, 0.0).astype(BF16))

    def chunk(d, hh, c):
        q_ref, k_ref, v_ref, _, o_ref = refs[d]
        rev = d == 1
        rows = slice(c * C, (c + 1) * C)
        kc = slice(hh * GLA_DK, (hh + 1) * GLA_DK)
        vc = slice(hh * GLA_DV, (hh + 1) * GLA_DV)
        g = g_alls[d][rows, kc]
        g_hi = g.astype(BF16)
        g_lo = (g - g_hi.astype(F32)).astype(BF16)
        b = _dot(tri_bs[d], g_hi) + _dot(tri_bs[d], g_lo)
        b_tot = b[0:1, :] if rev else b[C - 1:C, :]
        q = q_ref[rows, kc] * (GLA_DK ** -0.5)
        k = k_ref[rows, kc]
        v = v_ref[rows, vc].astype(BF16)
        q_d = (q * jnp.exp(b)).astype(BF16)
        k_d = (k * jnp.exp(-b)).astype(BF16)
        k_s = (k * jnp.exp(b_tot - b)).astype(BF16)
        att = lax.dot_general(q_d, k_d, NT_DIMS, preferred_element_type=F32)
        att = jnp.where(tris[d], att, 0.0).astype(BF16)
        s_t = st_ref[d, hh]
        o = _dot(att, v) + lax.dot_general(q_d, s_t.astype(BF16), NT_DIMS, preferred_element_type=F32)
        o_ref[rows, vc] = o
        st_ref[d, hh] = s_t * jnp.exp(b_tot) + lax.dot_general(v, k_s, TN_DIMS, preferred_element_type=F32)

    n_chunks = lt // C
    for c in range(n_chunks):
        for hh in range(GLA_HPS):
            chunk(0, hh, c)
            chunk(1, hh, n_chunks - 1 - c)


def gla_scan(p, wa, ba, *, ctx_len, lt=SEQ_TILE):
    T = p.shape[0]
    n_tiles = T // lt
    n_ctx = ctx_len // lt
    tf = functools.partial(_time_tile, n_tiles=n_tiles, n_ctx_tiles=n_ctx, rev=False)
    tb = functools.partial(_time_tile, n_tiles=n_tiles, n_ctx_tiles=n_ctx, rev=True)
    dk2, dv2 = GLA_HPS * GLA_DK, GLA_HPS * GLA_DV

    def stream(tt):
        return [
            pl.BlockSpec((lt, dk2), lambda h, t: (tt(t), OFF_Q // dk2 + h)),
            pl.BlockSpec((lt, dk2), lambda h, t: (tt(t), OFF_K // dk2 + h)),
            pl.BlockSpec((lt, dv2), lambda h, t: (tt(t), OFF_V // dv2 + h)),
            pl.BlockSpec((lt, LANE), lambda h, t: (tt(t), OFF_ALPHA // LANE)),
        ]

    out = jax.ShapeDtypeStruct((T, D_GLA_V), F32)
    return pl.pallas_call(
        functools.partial(_gla_kernel, lt=lt),
        grid=(GLA_HEADS // GLA_HPS, n_tiles),
        in_specs=stream(tf) + stream(tb) + [
            pl.BlockSpec((2, LANE, dk2), lambda h, t: (0, 0, h)),
            pl.BlockSpec((2, dk2), lambda h, t: (0, h)),
        ],
        out_specs=[pl.BlockSpec((lt, dv2), lambda h, t: (tf(t), h)),
                   pl.BlockSpec((lt, dv2), lambda h, t: (tb(t), h))],
        out_shape=[out, out],
        scratch_shapes=[pltpu.VMEM((2, GLA_HPS, GLA_DV, GLA_DK), F32)],
        compiler_params=_cparams(("arbitrary", "arbitrary")),
        name="gla_scan",
    )(p, p, p, p, p, p, p, p, wa, ba)


def _finish_kernel(u_ref, ys_f_ref, ys_b_ref, d_ref, wglu_ref, bglu_ref,
                   og_f_ref, og_b_ref, gate_ref, gn_ref,
                   x0_ref, conv_ref, s_ref, hb_ref, o_ref):
    u = u_ref[...]
    y = ys_f_ref[...] + ys_b_ref[...] + d_ref[...] * u
    zz = _gelu(y)
    s5 = zz * _sigmoid(_dot(zz.astype(BF16), wglu_ref[...]) + bglu_ref[...])
    o_ref[:, 0:D_S5] = s5.astype(BF16)

    gate = gate_ref[...]
    for h in range(GLA_HEADS):
        cols = slice(h * GLA_DV, (h + 1) * GLA_DV)
        o = og_f_ref[:, cols] + og_b_ref[:, cols]
        ms = jnp.mean(o * o, axis=-1, keepdims=True)
        on = o * lax.rsqrt(ms + EPS) * gn_ref[...]
        gt = gate[:, cols]
        o_ref[:, D_S5 + h * GLA_DV:D_S5 + (h + 1) * GLA_DV] = (on * (gt * _sigmoid(gt))).astype(BF16)

    s = s_ref[...]
    hy = x0_ref[...] * (conv_ref[...] + hb_ref[...] * s)
    o_ref[:, D_S5 + D_GLA_V:] = hy.astype(BF16)


def mixer_finish(p, ys_f, ys_b, s5_d, w_glu, b_glu, og_f, og_b, gn, x0, conv, s, hy_bias, *, tm):
    T = p.shape[0]
    row = lambda w, off=0: pl.BlockSpec((tm, w), lambda i: (i, off))
    const = lambda r, w: pl.BlockSpec((r, w), lambda i: (0, 0))
    return pl.pallas_call(
        _finish_kernel,
        grid=(T // tm,),
        in_specs=[
            row(D_S5, OFF_S5 // D_S5), row(D_S5), row(D_S5), const(1, D_S5), const(D_S5, D_S5), const(1, D_S5),
            row(D_GLA_V), row(D_GLA_V), row(D_GLA_V, OFF_GATE // D_GLA_V), const(1, GLA_DV),
            row(D_HY), row(D_HY), row(D_HY), const(1, D_HY),
        ],
        out_specs=pl.BlockSpec((tm, D_MODEL), lambda i: (i, 0)),
        out_shape=jax.ShapeDtypeStruct((T, D_MODEL), BF16),
        compiler_params=_cparams(("arbitrary",)),
        name="mixer_finish",
    )(p, ys_f, ys_b, s5_d.reshape(1, D_S5), w_glu, b_glu.reshape(1, D_S5),
      og_f, og_b, p, gn.reshape(1, GLA_DV), x0, conv, s, hy_bias.reshape(1, D_HY))


def _peer_kernel(x_ref, u_ref, vt_ref, s2_ref, e2_ref, s1_ref, e1_ref, tau_ref, o_ref,
                 act_ref, p_ref, *, n_i1, tm, cw):
    j = pl.program_id(1)

    @pl.when(j == 0)
    def _():
        o_ref[...] = jnp.zeros_like(o_ref)

    act_ref[...] = lax.dot_general(u_ref[...], x_ref[...], NT_DIMS, preferred_element_type=F32)
    K = PEER_KEYS
    sub = SUBLANE // n_i1
    base = (j % sub) * n_i1 if sub > 1 else 0
    for a in range(n_i1):
        r1 = pl.ds(base + a, 1)
        for cb in range(tm // cw):
            cols = slice(cb * cw, (cb + 1) * cw)
            w = jnp.zeros((K, cw), F32)
            for h in range(PEER_HEADS):
                tot = s1_ref[h, r1, cols] + s2_ref[h, :, cols]
                val = e1_ref[h, r1, cols] * e2_ref[h, :, cols]
                w = w + jnp.where(tot >= tau_ref[h, :, cols], val, 0.0)
            act = act_ref[a * K:(a + 1) * K, cols]
            p_ref[a * K:(a + 1) * K, cols] = (w * _gelu(act)).astype(BF16)
    o_ref[...] += lax.dot_general(vt_ref[...], p_ref[...], TN_DIMS, preferred_element_type=F32)


def peer_dense(xn, u_tab, vt_tab, s1t, e1t, s2t, e2t, taut, *, tm, te, cw=256):
    T, D = xn.shape
    E = u_tab.shape[0]
    H, K = PEER_HEADS, PEER_KEYS
    n_i1 = te // K
    sub = SUBLANE // n_i1
    i1_spec = pl.BlockSpec((H, SUBLANE, tm), lambda i, j: (0, j // sub, i))
    once = pl.Buffered(1)
    tab_spec = pl.BlockSpec((H, K, tm), lambda i, j: (0, 0, i), pipeline_mode=once)
    return pl.pallas_call(
        functools.partial(_peer_kernel, n_i1=n_i1, tm=tm, cw=cw),
        grid=(T // tm, E // te),
        in_specs=[
            pl.BlockSpec((tm, D), lambda i, j: (i, 0), pipeline_mode=once),
            pl.BlockSpec((te, D), lambda i, j: (j, 0)),
            pl.BlockSpec((te, D), lambda i, j: (j, 0)),
            tab_spec, tab_spec,
            i1_spec, i1_spec,
            pl.BlockSpec((H, 1, tm), lambda i, j: (0, 0, i), pipeline_mode=once),
        ],
        out_specs=pl.BlockSpec((D, tm), lambda i, j: (0, i)),
        out_shape=jax.ShapeDtypeStruct((D, T), F32),
        scratch_shapes=[pltpu.VMEM((te, tm), F32), pltpu.VMEM((te, tm), BF16)],
        compiler_params=_cparams(("arbitrary", "arbitrary")),
        name="peer_dense",
    )(xn, u_tab, vt_tab, s2t, e2t, s1t, e1t, taut)


NEG_BIG = -3.0e38
N_CAND = PEER_TOPK + 8 * 7 + 8


def _peer_prep_kernel(q_ref, k_ref, s1_ref, e1_ref, s2_ref, e2_ref, tau_ref, top_ref, cand_ref):
    R = PEER_TOPK
    half_w = PEER_DQ // 2

    def nt(a, b):
        return lax.dot_general(a, b, NT_DIMS, preferred_element_type=F32)

    def scores(half):
        qh = q_ref[:, half * half_w:(half + 1) * half_w]
        q_hi = qh.astype(BF16)
        q_lo = (qh - q_hi.astype(F32)).astype(BF16)
        kk = k_ref[half]
        k_hi = kk.astype(BF16)
        k_lo = (kk - k_hi.astype(F32)).astype(BF16)
        return nt(k_hi, q_hi) + (nt(k_hi, q_lo) + nt(k_lo, q_hi))

    def sorted_top(s, slot):
        work = s
        for r in range(R):
            m = jnp.max(work, axis=0, keepdims=True)
            top_ref[slot, r:r + 1, :] = m
            work = jnp.where(work == m, NEG_BIG, work)

    s1 = scores(0)
    s2 = scores(1)
    sorted_top(s1, 0)
    sorted_top(s2, 1)
    a = top_ref[0]
    b = top_ref[1]
    cand_ref[0:R, :] = a[0:1, :] + b
    for i in range(1, 8):
        cand_ref[R + 8 * (i - 1):R + 8 * i, :] = a[i:i + 1, :] + b[0:8, :]
    cand_ref[R + 56:R + 64, :] = a[8:16, :] + b[0:1, :]
    work = cand_ref[...]
    m0 = jnp.max(work, axis=0, keepdims=True)
    m = m0
    z = jnp.ones_like(m0)
    for r in range(1, R):
        work = jnp.where(work == m, NEG_BIG, work)
        m = jnp.max(work, axis=0, keepdims=True)
        z = z + jnp.exp(m - m0)
    tau_ref[0] = m
    s1_ref[0] = s1
    s2_ref[0] = s2
    e1_ref[0] = jnp.exp(s1 - a[0:1, :]) * (1.0 / z)
    e2_ref[0] = jnp.exp(s2 - b[0:1, :])


def peer_prep(q, keys, *, tm):
    T = q.shape[0]
    H, K = PEER_HEADS, PEER_KEYS
    tab = jax.ShapeDtypeStruct((H, K, T), F32)
    tab_spec = pl.BlockSpec((1, K, tm), lambda i, h: (h, 0, i))
    return pl.pallas_call(
        _peer_prep_kernel,
        grid=(T // tm, H),
        in_specs=[
            pl.BlockSpec((tm, PEER_DQ), lambda i, h: (i, h)),
            pl.BlockSpec((None, 2, K, PEER_DQ // 2), lambda i, h: (h, 0, 0, 0)),
        ],
        out_specs=[tab_spec, tab_spec, tab_spec, tab_spec, pl.BlockSpec((1, 1, tm), lambda i, h: (h, 0, i))],
        out_shape=[tab, tab, tab, tab, jax.ShapeDtypeStruct((H, 1, T), F32)],
        scratch_shapes=[pltpu.VMEM((2, PEER_TOPK, tm), F32), pltpu.VMEM((N_CAND, tm), F32)],
        compiler_params=_cparams(("arbitrary", "arbitrary")),
        name="peer_prep",
    )(q, keys)


def _transpose_gated_res_kernel(h_ref, ft_ref, gate_ref, o_ref, *, ctx_len, tm):
    i = pl.program_id(0)
    row = i * tm + lax.broadcasted_iota(jnp.int32, (tm, 1), 0)
    gate = jnp.where(row < ctx_len, gate_ref[0:1, :], gate_ref[1:2, :])
    o_ref[...] = h_ref[...] + gate * ft_ref[...].T


def transpose_gated_res(h, ft, gates, *, ctx_len, tm):
    T, D = h.shape
    return pl.pallas_call(
        functools.partial(_transpose_gated_res_kernel, ctx_len=ctx_len, tm=tm),
        grid=(T // tm,),
        in_specs=[
            pl.BlockSpec((tm, D), lambda i: (i, 0)),
            pl.BlockSpec((D, tm), lambda i: (0, i)),
            pl.BlockSpec((2, D), lambda i: (0, 0)),
        ],
        out_specs=pl.BlockSpec((tm, D), lambda i: (i, 0)),
        out_shape=jax.ShapeDtypeStruct((T, D), F32),
        compiler_params=_cparams(("arbitrary",)),
        name="transpose_gated_res",
    )(h, ft, gates)


def _rmsnorm_kernel(x_ref, g_ref, o_ref):
    x = x_ref[...]
    ms = jnp.mean(x * x, axis=-1, keepdims=True)
    o_ref[...] = x * lax.rsqrt(ms + EPS) * g_ref[...]


def rmsnorm_rows(x, g, *, tm):
    T, D = x.shape
    return pl.pallas_call(
        _rmsnorm_kernel,
        grid=(T // tm,),
        in_specs=[pl.BlockSpec((tm, D), lambda i: (i, 0)), pl.BlockSpec((1, D), lambda i: (0, 0))],
        out_specs=pl.BlockSpec((tm, D), lambda i: (i, 0)),
        out_shape=jax.ShapeDtypeStruct((T, D), F32),
        compiler_params=_cparams(("arbitrary",)),
        name="rmsnorm_rows",
    )(x, g.reshape(1, D))


def grid_sincos(n_tokens, dim):
    rows = n_tokens // GRID_W
    n_freq = dim // 4
    omega = 1.0 / (POS_BASE ** (jnp.arange(n_freq, dtype=F32) / n_freq))

    def enc(count):
        a = jnp.arange(count).astype(F32)[:, None] * omega[None, :]
        return jnp.concatenate([jnp.sin(a), jnp.cos(a)], axis=-1)

    row_code = jnp.broadcast_to(enc(rows)[:, None, :], (rows, GRID_W, dim // 2))
    col_code = jnp.broadcast_to(enc(GRID_W)[None, :, :], (rows, GRID_W, dim // 2))
    return jnp.concatenate([row_code, col_code], axis=-1).reshape(n_tokens, dim)


def _split_bf16(a):
    hi = a.astype(BF16)
    return hi, (a - hi.astype(F32)).astype(BF16)


def _dot3s(a, b_hi, b_lo):
    a_hi, a_lo = _split_bf16(a)
    return _dot(a_hi, b_hi) + (_dot(a_hi, b_lo) + _dot(a_lo, b_hi))


def _dot3(a, b):
    return _dot3s(a, *_split_bf16(b))


def _adaln_kernel(c_ref, w_ref, b_ref, o_ref):
    c = c_ref[...]
    o_ref[...] = _dot3(c * _sigmoid(c), w_ref[...]) + b_ref[...]


def adaln(cond, w, b, *, tn=1536):
    R, D = cond.shape
    N = w.shape[1]
    return pl.pallas_call(
        _adaln_kernel,
        grid=(N // tn,),
        in_specs=[pl.BlockSpec((R, D), lambda j: (0, 0)),
                  pl.BlockSpec((D, tn), lambda j: (0, j)),
                  pl.BlockSpec((1, tn), lambda j: (0, j))],
        out_specs=pl.BlockSpec((R, tn), lambda j: (0, j)),
        out_shape=jax.ShapeDtypeStruct((R, N), F32),
        compiler_params=_cparams(("arbitrary",)),
        name="adaln",
    )(cond, w, b.reshape(1, N))


def _hyena_pre_kernel(x0_ref, x1_ref, v_ref, w_ref, b_ref, x0o_ref, s_ref, *, ctx_len):
    T = x0_ref.shape[0]
    row = lax.broadcasted_iota(jnp.int32, (T, 1), 0)
    first = jnp.logical_or(row == 0, row == ctx_len)
    last = jnp.logical_or(row == ctx_len - 1, row == T - 1)

    def conv(ref, part):
        x = ref[...]
        prev = jnp.where(first, 0.0, pltpu.roll(x, 1, 0))
        nxt = jnp.where(last, 0.0, pltpu.roll(x, T - 1, 0))
        w = lambda tap: w_ref[tap, part:part + 1, :]
        return w(0) * prev + w(1) * x + w(2) * nxt + b_ref[part:part + 1, :]

    x0o_ref[...] = conv(x0_ref, 0)
    s_ref[...] = conv(x1_ref, 1) * conv(v_ref, 2)


def hyena_pre(p, conv_w, conv_b, *, ctx_len):
    T = p.shape[0]
    nb = D_HY // LANE
    col = lambda part: pl.BlockSpec((T, LANE), lambda c: (0, OFF_HY // LANE + part * nb + c))
    out = jax.ShapeDtypeStruct((T, D_HY), F32)
    return pl.pallas_call(
        functools.partial(_hyena_pre_kernel, ctx_len=ctx_len),
        grid=(nb,),
        in_specs=[col(0), col(1), col(2),
                  pl.BlockSpec((HY_SHORT, 3, LANE), lambda c: (0, 0, c)),
                  pl.BlockSpec((3, LANE), lambda c: (0, c))],
        out_specs=[pl.BlockSpec((T, LANE), lambda c: (0, c))] * 2,
        out_shape=[out, out],
        compiler_params=_cparams(("arbitrary",)),
        name="hyena_pre",
    )(p, p, p, conv_w.reshape(HY_SHORT, 3, D_HY), conv_b.reshape(3, D_HY))


def _hyena_filter_kernel(band_ref, w1_ref, b1_ref, w2_ref, b2_ref, w3_ref, fr_ref, dec_ref, o_ref, *, n, tr):
    i = pl.program_id(0)
    ri = (i * tr + lax.broadcasted_iota(jnp.int32, (tr, 1), 0)).astype(F32)
    t = ri * (1.0 / (n - 1))
    w = ri * (2.0 * math.pi / n)
    lane = lax.broadcasted_iota(jnp.int32, (1, LANE), 1)
    arg = w * band_ref[...]
    z = jnp.where(lane == 0, t,
                  jnp.where(lane <= HY_BANDS, jnp.cos(arg),
                            jnp.where(lane <= 2 * HY_BANDS, -jnp.sin(arg), 0.0)))
    h = jnp.sin(fr_ref[0:1, :] * (_dot3(z, w1_ref[...]) + b1_ref[...]))
    h = jnp.sin(fr_ref[1:2, :] * (_dot3(h, w2_ref[...]) + b2_ref[...]))
    h = _dot3(h, w3_ref[...]) * jnp.exp(-t * dec_ref[...])
    col = lax.broadcasted_iota(jnp.int32, (1, 2 * D_HY), 1)
    o_ref[...] = jnp.where(jnp.logical_and(ri == 0.0, col >= D_HY), 0.0, h)


def hyena_filter(n, w1, b1, w2, b2, w3, freq, decay, *, tr=256):
    bands = jnp.linspace(1e-4, HY_BANDS - 1, HY_BANDS, dtype=F32)
    band_row = jnp.zeros((1, LANE), F32).at[0, 1:1 + HY_BANDS].set(bands).at[0, 1 + HY_BANDS:1 + 2 * HY_BANDS].set(bands)
    w1p = jnp.zeros((LANE, HY_HIDDEN), F32).at[:HY_FEAT].set(w1)
    const = lambda a: pl.BlockSpec(a.shape, lambda i: (0,) * a.ndim)
    args = (band_row, w1p, b1.reshape(1, -1), w2, b2.reshape(1, -1), w3, freq,
            jnp.abs(decay).reshape(1, 2 * D_HY))
    return pl.pallas_call(
        functools.partial(_hyena_filter_kernel, n=n, tr=tr),
        grid=(n // tr,),
        in_specs=[const(a) for a in args],
        out_specs=pl.BlockSpec((tr, 2 * D_HY), lambda i: (i, 0)),
        out_shape=jax.ShapeDtypeStruct((n, 2 * D_HY), F32),
        compiler_params=_cparams(("arbitrary",)),
        name="hyena_filter",
    )(*args)


FFT_R = 128
FFT_N = FFT_R * FFT_R
FFT_CH = 32
FFT_PITCH = FFT_R + 4


def _dft_tables():
    r = np.arange(FFT_R)
    ang = 2.0 * np.pi * np.outer(r, r) / FFT_R
    c, s = np.cos(ang), np.sin(ang)
    angt = 2.0 * np.pi * np.outer(r, r) / FFT_N
    ct, st = np.cos(angt), np.sin(angt)

    def split(m):
        m = jnp.asarray(m, F32)
        hi = m.astype(BF16)
        return jnp.stack([hi, (m - hi.astype(F32)).astype(BF16)])

    fwd_a = split(np.concatenate([c, -s], axis=1))
    fwd_b = split(np.block([[c, -s], [s, c]]))
    inv_b = split(np.block([[c, s], [-s, c]]) / FFT_N)
    inv_a = split(np.concatenate([c, -s], axis=0))
    tw = (jnp.asarray(ct, F32), jnp.asarray(-st, F32))
    twc = (jnp.asarray(ct, F32), jnp.asarray(st, F32))
    return fwd_a, fwd_b, inv_b, inv_a, tw, twc


def _twiddle_transpose(y, twr, twi, dst_ref, nc):
    y = y.reshape(nc, FFT_R, 2 * FFT_R)
    yr, yi = y[:, :, :FFT_R], y[:, :, FFT_R:]
    zr = yr * twr - yi * twi
    zi = yr * twi + yi * twr
    for c in range(nc):
        dst_ref[c * FFT_R:(c + 1) * FFT_R, 0:FFT_R] = zr[c].T
        dst_ref[c * FFT_R:(c + 1) * FFT_R, FFT_R:2 * FFT_R] = zi[c].T


def _hyena_fwd_kernel(x_ref, fa_ref, twr_ref, twi_ref, fb_ref, o_ref, l_ref, yt_ref, *, nc):
    j = pl.program_id(1)
    half = FFT_R // 2

    @pl.when(j == 0)
    def _():
        zeros = jnp.zeros((half, LANE), F32)

        def body(b, carry):
            t = jnp.concatenate([x_ref[pl.ds(b, half, stride=FFT_R), :], zeros], axis=0)
            l_ref[pl.ds(b, LANE, stride=FFT_PITCH), :] = t.T
            return carry

        lax.fori_loop(0, FFT_R, body, 0, unroll=8)

    lc = jnp.concatenate(
        [l_ref[pl.ds(pl.multiple_of((j * nc + c) * FFT_PITCH, 4), FFT_R), :] for c in range(nc)], axis=0)
    y = _dot3s(lc, fa_ref[0], fa_ref[1])
    _twiddle_transpose(y, twr_ref[...], twi_ref[...], yt_ref, nc)
    o_ref[...] = _dot3s(yt_ref[...], fb_ref[0], fb_ref[1])


def hyena_fwd_dft(x, tables):
    n, C = x.shape
    assert n * 2 == FFT_N and C % LANE == 0
    fwd_a, fwd_b, _, _, (twr, twi), _ = tables
    nc = FFT_CH
    steps = LANE // nc
    const = lambda a: pl.BlockSpec(a.shape, lambda cb, j: (0,) * a.ndim)
    return pl.pallas_call(
        functools.partial(_hyena_fwd_kernel, nc=nc),
        grid=(C // LANE, steps),
        in_specs=[pl.BlockSpec((n, LANE), lambda cb, j: (0, cb)), const(fwd_a), const(twr), const(twi), const(fwd_b)],
        out_specs=pl.BlockSpec((nc * FFT_R, 2 * FFT_R), lambda cb, j: (cb * steps + j, 0)),
        out_shape=jax.ShapeDtypeStruct((C * FFT_R, 2 * FFT_R), F32),
        scratch_shapes=[pltpu.VMEM((LANE * FFT_PITCH, LANE), F32), pltpu.VMEM((nc * FFT_R, 2 * FFT_R), F32)],
        compiler_params=_cparams(("arbitrary", "arbitrary")),
        name="hyena_fwd_dft",
    )(x, fwd_a, twr, twi, fwd_b)


def _hyena_inv_kernel(s_ref, kf_ref, kb_ref, gb_ref, twr_ref, twi_ref, ga_ref, o_ref, lr_ref, at_ref, *, nc, steps):
    j = pl.program_id(1)
    R = FFT_R
    s, kf, kb = s_ref[...], kf_ref[...], kb_ref[...]
    sr, si = s[:, :R], s[:, R:]
    kr = kf[:, :R] + kb[:, :R]
    ki = kf[:, R:] - kb[:, R:]
    p = jnp.concatenate([sr * kr - si * ki, sr * ki + si * kr], axis=1)
    a = _dot3s(p, gb_ref[0], gb_ref[1])
    _twiddle_transpose(a, twr_ref[...], twi_ref[...], at_ref, nc)
    r = _dot3s(at_ref[...], ga_ref[0], ga_ref[1])
    for c in range(nc):
        lr_ref[pl.ds(pl.multiple_of((j * nc + c) * FFT_PITCH, 4), R), :] = r[c * R:(c + 1) * R, :]

    @pl.when(j == steps - 1)
    def _():
        def body(b, carry):
            t = lr_ref[pl.ds(b, LANE, stride=FFT_PITCH), :]
            o_ref[pl.ds(b, R // 2, stride=R), :] = t.T[0:R // 2, :]
            return carry

        lax.fori_loop(0, R, body, 0, unroll=8)


def hyena_inv_dft(s_f, k_f, tables):
    C = s_f.shape[0] // FFT_R
    _, _, inv_b, inv_a, _, (twr, twi) = tables
    nc = FFT_CH
    steps = LANE // nc
    kb_off = C // nc
    const = lambda a: pl.BlockSpec(a.shape, lambda cb, j: (0,) * a.ndim)
    blk = lambda off: pl.BlockSpec((nc * FFT_R, 2 * FFT_R), lambda cb, j: (off + cb * steps + j, 0))
    return pl.pallas_call(
        functools.partial(_hyena_inv_kernel, nc=nc, steps=steps),
        grid=(C // LANE, steps),
        in_specs=[blk(0), blk(0), blk(kb_off), const(inv_b), const(twr), const(twi), const(inv_a)],
        out_specs=pl.BlockSpec((FFT_N // 2, LANE), lambda cb, j: (0, cb)),
        out_shape=jax.ShapeDtypeStruct((FFT_N // 2, C), F32),
        scratch_shapes=[pltpu.VMEM((LANE * FFT_PITCH, LANE), F32), pltpu.VMEM((nc * FFT_R, 2 * FFT_R), F32)],
        compiler_params=_cparams(("arbitrary", "arbitrary")),
        name="hyena_inv_dft",
    )(s_f, k_f, k_f, inv_b, twr, twi, inv_a)


def _hyena_ctx_kernel(s_ref, k_ref, fw_ref, iv_ref, o_ref):
    n = s_ref.shape[0]
    N = 2 * n
    xs = _dot3(fw_ref[...], s_ref[...])
    xk = _dot3(fw_ref[...], k_ref[...])
    sr, si = xs[:N], xs[N:]
    kr = xk[:N, :D_HY] + xk[:N, D_HY:]
    ki = xk[N:, :D_HY] - xk[N:, D_HY:]
    p = jnp.concatenate([sr * kr - si * ki, sr * ki + si * kr], axis=0)
    o_ref[...] = _dot3(iv_ref[...], p)


def hyena_ctx_conv(s_c, k_c):
    n = s_c.shape[0]
    N = 2 * n
    ang = 2.0 * np.pi * np.outer(np.arange(N), np.arange(N)) / N
    c, s = np.cos(ang), np.sin(ang)
    fw = jnp.asarray(np.concatenate([c[:, :n], -s[:, :n]], axis=0), F32)
    iv = jnp.asarray(np.concatenate([c[:n, :], -s[:n, :]], axis=1) / N, F32)
    full = lambda a: pl.BlockSpec(a.shape, lambda i: (0,) * a.ndim)
    return pl.pallas_call(
        _hyena_ctx_kernel,
        grid=(1,),
        in_specs=[full(s_c), full(k_c), full(fw), full(iv)],
        out_specs=pl.BlockSpec((n, D_HY), lambda i: (0, 0)),
        out_shape=jax.ShapeDtypeStruct((n, D_HY), F32),
        compiler_params=_cparams(("arbitrary",)),
        name="hyena_ctx_conv",
    )(s_c, k_c, fw, iv)


def hyena_mixer_parts(p, conv_w, conv_b, filt, tables, *, ctx_len):
    T = p.shape[0]
    x0, s = hyena_pre(p, conv_w, conv_b, ctx_len=ctx_len)
    k_lat = hyena_filter(T - ctx_len, *filt)
    k_ctx = hyena_filter(ctx_len, *filt)
    conv_l = hyena_inv_dft(hyena_fwd_dft(s[ctx_len:], tables), hyena_fwd_dft(k_lat, tables), tables)
    conv_c = hyena_ctx_conv(s[:ctx_len], k_ctx)
    return x0, jnp.concatenate([conv_c, conv_l], axis=0), s


def _pack_w_in(w_in_l):
    widths = (D_S5, D_GLA_K, D_GLA_K, D_GLA_V, D_GLA_V, 2 * GLA_RANK, 3 * D_HY)
    offs = [0]
    for wd in widths:
        offs.append(offs[-1] + wd)
    s5, q, k, v, gate, alpha, hy = (w_in_l[:, offs[i]:offs[i + 1]] for i in range(7))
    pad = jnp.zeros((w_in_l.shape[0], IN_PACKED - OFF_ALPHA - 2 * GLA_RANK), w_in_l.dtype)
    return jnp.concatenate([v, gate, s5, q, k, hy, alpha, pad], axis=1).astype(BF16)


def kernel(x, c, ctx, c_ctx, w_ada, b_ada, g_norm1, g_norm2, w_in, s5_a_re, s5_a_im, s5_log_step, s5_b_re, s5_b_im, s5_c_re, s5_c_im, s5_d, s5_w_glu, s5_b_glu, gla_w_alpha, gla_b_alpha, gla_g_norm, hy_conv_w, hy_conv_b, hy_f_w1, hy_f_b1, hy_f_w2, hy_f_b2, hy_f_w3, hy_f_freq, hy_decay, hy_bias, w_out, peer_w_q, peer_keys, peer_u, peer_v, g_final):
    L = x.shape[1]
    Lc = ctx.shape[1]
    T = L + Lc
    h = jnp.concatenate([ctx[0], x[0] + grid_sincos(L, D_MODEL)], axis=0)
    cond = jnp.zeros((SUBLANE, D_MODEL), F32).at[0].set(c_ctx).at[1].set(c[0])
    tables = _dft_tables()

    for l in range(DEPTH):
        m = adaln(cond, w_ada[l], b_ada[l])[0:2]
        sh1, sc1, gt1, sh2, sc2, gt2 = jnp.split(m, 6, axis=-1)
        mod1 = jnp.stack([sh1[0], sc1[0], sh1[1], sc1[1]], axis=0)
        mod2 = jnp.stack([sh2[0], sc2[0], sh2[1], sc2[1]], axis=0)

        p = norm_mod_matmul(h, g_norm1[l], mod1, _pack_w_in(w_in[l]), ctx_len=Lc, tm=ROW_TILE, tn=IN_PROJ_COLS)

        bm, cm, pw = s5_prepare(s5_a_re[l], s5_a_im[l], s5_log_step[l], s5_b_re[l], s5_b_im[l],
                                s5_c_re[l], s5_c_im[l])
        ys_f, ys_b = s5_scan(p, bm, cm, pw, ctx_len=Lc)

        wa = jnp.zeros((2, LANE, D_GLA_K), F32)
        wa = wa.at[0, 0:GLA_RANK].set(gla_w_alpha[l, 0]).at[1, GLA_RANK:2 * GLA_RANK].set(gla_w_alpha[l, 1])
        wa = wa.astype(BF16)
        og_f, og_b = gla_scan(p, wa, gla_b_alpha[l], ctx_len=Lc)

        filt = (hy_f_w1[l], hy_f_b1[l], hy_f_w2[l], hy_f_b2[l], hy_f_w3[l], hy_f_freq[l], hy_decay[l])
        x0, conv, s = hyena_mixer_parts(p, hy_conv_w[l], hy_conv_b[l], filt, tables, ctx_len=Lc)

        mix = mixer_finish(p, ys_f, ys_b, s5_d[l], s5_w_glu[l].astype(BF16), s5_b_glu[l],
                           og_f, og_b, gla_g_norm[l], x0, conv, s, hy_bias[l], tm=SMALL_ROW_TILE)
        h = matmul_gated_res(mix, w_out[l].astype(BF16), h, gt1, ctx_len=Lc, tm=ROW_TILE, tn=PROJ_COLS)

        q, xn = norm_mod_matmul(h, g_norm2[l], mod2, peer_w_q[l].astype(BF16),
                                ctx_len=Lc, tm=ROW_TILE, tn=PROJ_COLS, emit_xn=True)
        s1t, e1t, s2t, e2t, taut = peer_prep(q, peer_keys[l], tm=ROW_TILE)
        ft = peer_dense(xn, peer_u[l].astype(BF16), peer_v[l].astype(BF16),
                        s1t, e1t, s2t, e2t, taut, tm=ROW_TILE, te=PEER_EXPERT_TILE)
        h = transpose_gated_res(h, ft, gt2, ctx_len=Lc, tm=SMALL_ROW_TILE)

    out = rmsnorm_rows(h[Lc:], g_final, tm=OUT_ROW_TILE)
    return out[None]
```

```python
import functools
import math

import jax
import jax.numpy as jnp
import numpy as np
from jax import lax
from jax.experimental import pallas as pl
from jax.experimental.pallas import tpu as pltpu

F32 = jnp.float32
BF16 = jnp.bfloat16

D_MODEL = 2048
DEPTH = 4
GRID_W = 64
EPS = 1e-6
POS_BASE = 10000.0

D_S5 = D_MODEL // 4
S5_GROUP = 16
S5_GROUPS = D_S5 // S5_GROUP
S5_STATE = 64
S5_BLK_GROUPS = 8
S5_BLK_CH = S5_BLK_GROUPS * S5_GROUP
S5_BLK_ST = S5_BLK_GROUPS * S5_STATE
S5_NBLK = S5_GROUPS // S5_BLK_GROUPS

GLA_HEADS = 4
D_GLA_K = D_MODEL // 4
D_GLA_V = D_MODEL // 2
GLA_DK = D_GLA_K // GLA_HEADS
GLA_DV = D_GLA_V // GLA_HEADS
GLA_RANK = 16
GLA_TAU = 16.0
GLA_CHUNK = 64
GLA_HPS = 4

D_HY = D_MODEL // 4
HY_SHORT = 3
HY_BANDS = 16
HY_FEAT = 1 + 2 * HY_BANDS
HY_HIDDEN = 64
HY_TARGET = 1e-2
HY_MIN_DECAY = -math.log(HY_TARGET) / 1.5
HY_MAX_DECAY = -math.log(HY_TARGET) / 0.3

PEER_HEADS = 8
PEER_KEYS = 128
PEER_EXPERTS = PEER_KEYS * PEER_KEYS
PEER_DQ = 256
PEER_TOPK = 16

OFF_V = 0
OFF_GATE = OFF_V + D_GLA_V
OFF_S5 = OFF_GATE + D_GLA_V
OFF_Q = OFF_S5 + D_S5
OFF_K = OFF_Q + D_GLA_K
OFF_HY = OFF_K + D_GLA_K
OFF_ALPHA = OFF_HY + 3 * D_HY
LANE = 128
SUBLANE = 8
S5_SCAN_SHIFTS = (1, 2, 4)
S5_TABLE_ROWS = SUBLANE * (1 + len(S5_SCAN_SHIFTS))
IN_PACKED = OFF_ALPHA + 2 * LANE

SEQ_TILE = 256
ROW_TILE = 768
SMALL_ROW_TILE = 256
OUT_ROW_TILE = 512
IN_PROJ_COLS = 768
PROJ_COLS = 1024
PEER_EXPERT_TILE = 1024
VMEM_LIMIT = 56 * 1024 * 1024

NT_DIMS = (((1,), (1,)), ((), ()))
TN_DIMS = (((0,), (0,)), ((), ()))


def _cparams(sem):
    return pltpu.CompilerParams(dimension_semantics=sem, vmem_limit_bytes=VMEM_LIMIT)


def _dot(a, b):
    return jnp.dot(a, b, preferred_element_type=F32)


def _gelu(x):
    k1 = -2.0 * math.sqrt(2.0 / math.pi) * math.log2(math.e)
    k2 = 0.044715 * k1
    return x / (1.0 + jnp.exp2(x * (k1 + k2 * (x * x))))


def _sigmoid(x):
    return 1.0 / (1.0 + jnp.exp(-x))


def _norm_modulate(x, g, mod_ref, row0, ctx_len):
    ms = jnp.mean(x * x, axis=-1, keepdims=True)
    y = x * lax.rsqrt(ms + EPS) * g
    row = row0 + lax.broadcasted_iota(jnp.int32, (x.shape[0], 1), 0)
    is_ctx = row < ctx_len
    shift = jnp.where(is_ctx, mod_ref[0:1, :], mod_ref[2:3, :])
    scale = jnp.where(is_ctx, mod_ref[1:2, :], mod_ref[3:4, :])
    return (y * (1.0 + scale) + shift).astype(BF16)


def _norm_mod_matmul_kernel(x_ref, g_ref, mod_ref, w_ref, o_ref, xn_ref, *, ctx_len, tm):
    i = pl.program_id(0)
    j = pl.program_id(1)

    @pl.when(j == 0)
    def _():
        xn_ref[...] = _norm_modulate(x_ref[...], g_ref[...], mod_ref, i * tm, ctx_len)

    o_ref[...] = _dot(xn_ref[...], w_ref[...])


def norm_mod_matmul(x, g, mod, w, *, ctx_len, tm, tn):
    T, D = x.shape
    N = w.shape[1]
    return pl.pallas_call(
        functools.partial(_norm_mod_matmul_kernel, ctx_len=ctx_len, tm=tm),
        grid=(T // tm, N // tn),
        in_specs=[
            pl.BlockSpec((tm, D), lambda i, j: (i, 0)),
            pl.BlockSpec((1, D), lambda i, j: (0, 0)),
            pl.BlockSpec((4, D), lambda i, j: (0, 0)),
            pl.BlockSpec((D, tn), lambda i, j: (0, j)),
        ],
        out_specs=pl.BlockSpec((tm, tn), lambda i, j: (i, j)),
        out_shape=jax.ShapeDtypeStruct((T, N), F32),
        scratch_shapes=[pltpu.VMEM((tm, D), BF16)],
        compiler_params=_cparams(("arbitrary", "arbitrary")),
        name="norm_mod_matmul",
    )(x, g.reshape(1, D), mod, w)


def _matmul_gated_res_kernel(a_ref, w_ref, r_ref, gate_ref, o_ref, *, ctx_len, tm):
    i = pl.program_id(0)
    row = i * tm + lax.broadcasted_iota(jnp.int32, (tm, 1), 0)
    gate = jnp.where(row < ctx_len, gate_ref[0:1, :], gate_ref[1:2, :])
    o_ref[...] = r_ref[...] + gate * _dot(a_ref[...], w_ref[...])


def matmul_gated_res(a, w, res, gates, *, ctx_len, tm, tn):
    T, K = a.shape
    N = w.shape[1]
    return pl.pallas_call(
        functools.partial(_matmul_gated_res_kernel, ctx_len=ctx_len, tm=tm),
        grid=(T // tm, N // tn),
        in_specs=[
            pl.BlockSpec((tm, K), lambda i, j: (i, 0)),
            pl.BlockSpec((K, tn), lambda i, j: (0, j)),
            pl.BlockSpec((tm, tn), lambda i, j: (i, j)),
            pl.BlockSpec((2, tn), lambda i, j: (0, j)),
        ],
        out_specs=pl.BlockSpec((tm, tn), lambda i, j: (i, j)),
        out_shape=jax.ShapeDtypeStruct((T, N), F32),
        compiler_params=_cparams(("arbitrary", "arbitrary")),
        name="matmul_gated_res",
    )(a, w, res, gates)


def _time_tile(t, n_tiles, n_ctx_tiles, rev):
    if not rev:
        return t
    return jnp.where(t < n_ctx_tiles, n_ctx_tiles - 1 - t, n_tiles - 1 - (t - n_ctx_tiles))


def _s5_kernel(uf_ref, ub_ref, bm_ref, cm_ref, pw_ref, yf_ref, yb_ref, h_ref, c_ref, *, lt):
    t = pl.program_id(1)
    ns = S5_BLK_ST
    u_refs = (uf_ref, ub_ref)
    y_refs = (yf_ref, yb_ref)

    @pl.when(t == 0)
    def _():
        c_ref[...] = jnp.zeros_like(c_ref)

    for d in range(2):
        bu = _dot(u_refs[d][...].astype(BF16), bm_ref[d])
        h_ref[d, 0] = bu[:, :ns]
        h_ref[d, 1] = bu[:, ns:]

    G = SUBLANE
    n_grp = lt // G

    def group_update(d, g, c_re, c_im):
        rev = d == 1
        gi = (n_grp - 1 - g) if rev else g
        rows = pl.ds(pl.multiple_of(gi * G, G), G)
        a_re = h_ref[d, 0, rows, :]
        a_im = h_ref[d, 1, rows, :]
        for k, s in enumerate(S5_SCAN_SHIFTS):
            level = slice(G * (k + 1), G * (k + 2))
            l_re = pw_ref[d, level, :ns]
            l_im = pw_ref[d, level, ns:]
            shift = (G - s) if rev else s
            s_re = pltpu.roll(a_re, shift, 0)
            s_im = pltpu.roll(a_im, shift, 0)
            a_re = a_re + (l_re * s_re - l_im * s_im)
            a_im = a_im + (l_re * s_im + l_im * s_re)
        p_re = pw_ref[d, 0:G, :ns]
        p_im = pw_ref[d, 0:G, ns:]
        a_re = a_re + (p_re * c_re - p_im * c_im)
        a_im = a_im + (p_re * c_im + p_im * c_re)
        h_ref[d, 0, rows, :] = a_re
        h_ref[d, 1, rows, :] = a_im
        edge = slice(0, 1) if rev else slice(G - 1, G)
        return a_re[edge, :], a_im[edge, :]

    def body(g, carry):
        f_re, f_im, b_re, b_im = carry
        f_re, f_im = group_update(0, g, f_re, f_im)
        b_re, b_im = group_update(1, g, b_re, b_im)
        return f_re, f_im, b_re, b_im

    carry = lax.fori_loop(0, n_grp, body, (c_ref[0, 0], c_ref[0, 1], c_ref[1, 0], c_ref[1, 1]), unroll=2)
    c_ref[0, 0], c_ref[0, 1], c_ref[1, 0], c_ref[1, 1] = carry

    for d in range(2):
        y_refs[d][...] = (_dot(h_ref[d, 0].astype(BF16), cm_ref[d, :ns, :])
                          + _dot(h_ref[d, 1].astype(BF16), cm_ref[d, ns:, :]))


def s5_scan(p, bm, cm, pw, *, ctx_len, lt=SEQ_TILE):
    T = p.shape[0]
    n_tiles = T // lt
    n_ctx = ctx_len // lt
    tf = functools.partial(_time_tile, n_tiles=n_tiles, n_ctx_tiles=n_ctx, rev=False)
    tb = functools.partial(_time_tile, n_tiles=n_tiles, n_ctx_tiles=n_ctx, rev=True)
    ucol = OFF_S5 // S5_BLK_CH
    out = jax.ShapeDtypeStruct((T, D_S5), F32)
    return pl.pallas_call(
        functools.partial(_s5_kernel, lt=lt),
        grid=(S5_NBLK, n_tiles),
        in_specs=[
            pl.BlockSpec((lt, S5_BLK_CH), lambda b, t: (tf(t), ucol + b)),
            pl.BlockSpec((lt, S5_BLK_CH), lambda b, t: (tb(t), ucol + b)),
            pl.BlockSpec((2, None, S5_BLK_CH, 2 * S5_BLK_ST), lambda b, t: (0, b, 0, 0)),
            pl.BlockSpec((2, None, 2 * S5_BLK_ST, S5_BLK_CH), lambda b, t: (0, b, 0, 0)),
            pl.BlockSpec((2, None, S5_TABLE_ROWS, 2 * S5_BLK_ST), lambda b, t: (0, b, 0, 0)),
        ],
        out_specs=[pl.BlockSpec((lt, S5_BLK_CH), lambda b, t: (tf(t), b)),
                   pl.BlockSpec((lt, S5_BLK_CH), lambda b, t: (tb(t), b))],
        out_shape=[out, out],
        scratch_shapes=[pltpu.VMEM((2, 2, lt, S5_BLK_ST), F32), pltpu.VMEM((2, 2, 1, S5_BLK_ST), F32)],
        compiler_params=_cparams(("arbitrary", "arbitrary")),
        name="s5_scan",
    )(p, p, bm, cm, pw)


def _cmul(a, b):
    return a[0] * b[0] - a[1] * b[1], a[0] * b[1] + a[1] * b[0]


def s5_prepare(a_re, a_im, log_step, b_re, b_im, c_re, c_im):
    G, P, Cg = S5_GROUPS, S5_STATE, S5_GROUP
    dt = jnp.exp(log_step)[..., None]
    er = jnp.exp(a_re * dt)
    lam1 = (er * jnp.cos(a_im * dt), er * jnp.sin(a_im * dt))
    den = a_re * a_re + a_im * a_im
    xr, xi = lam1[0] - 1.0, lam1[1]
    coef = ((xr * a_re + xi * a_im) / den, (xi * a_re - xr * a_im) / den)
    bb_re = coef[0][..., None] * b_re - coef[1][..., None] * b_im
    bb_im = coef[0][..., None] * b_im + coef[1][..., None] * b_re
    pows = [lam1]
    for _ in range(SUBLANE - 1):
        pows.append(_cmul(pows[-1], lam1))

    eye = jnp.eye(S5_BLK_GROUPS, dtype=F32)

    def blockdiag_in(m):
        m = m.reshape(2, S5_NBLK, S5_BLK_GROUPS, P, Cg)
        return jnp.einsum('dbgpc,gh->dbgchp', m, eye).reshape(2, S5_NBLK, S5_BLK_CH, S5_BLK_ST)

    def blockdiag_out(m):
        m = m.reshape(2, S5_NBLK, S5_BLK_GROUPS, Cg, P)
        return jnp.einsum('dbgcp,gh->dbgphc', m, eye).reshape(2, S5_NBLK, S5_BLK_ST, S5_BLK_CH)

    bm = jnp.concatenate([blockdiag_in(bb_re), blockdiag_in(bb_im)], axis=-1).astype(BF16)
    cm = jnp.concatenate([blockdiag_out(c_re), blockdiag_out(-c_im)], axis=-2).astype(BF16)

    def lay(v):
        return v.reshape(2, S5_NBLK, S5_BLK_ST)

    def table(rev):
        power = lambda k: jnp.concatenate([lay(pows[k][0]), lay(pows[k][1])], axis=-1)
        G = SUBLANE
        rows = [power(G - 1 - r if rev else r) for r in range(G)]
        zero = jnp.zeros_like(rows[0])
        for s in S5_SCAN_SHIFTS:
            for r in range(G):
                inside = (r < G - s) if rev else (r >= s)
                rows.append(power(s - 1) if inside else zero)
        return jnp.stack(rows, axis=2)

    return bm, cm, jnp.stack([table(False)[0], table(True)[1]])


def _log_sigmoid(z):
    return jnp.minimum(z, 0.0) - jnp.log(1.0 + jnp.exp(-jnp.abs(z)))


def _gla_kernel(qf_ref, kf_ref, vf_ref, af_ref, qb_ref, kb_ref, vb_ref, ab_ref, wa_ref, ba_ref,
                of_ref, ob_ref, st_ref, *, lt):
    t = pl.program_id(1)
    C = GLA_CHUNK

    @pl.when(t == 0)
    def _():
        st_ref[...] = jnp.zeros_like(st_ref)

    ri = lax.broadcasted_iota(jnp.int32, (C, C), 0)
    ci = lax.broadcasted_iota(jnp.int32, (C, C), 1)
    refs = ((qf_ref, kf_ref, vf_ref, af_ref, of_ref), (qb_ref, kb_ref, vb_ref, ab_ref, ob_ref))
    tris, tri_bs, g_alls = [], [], []
    for d in range(2):
        z = _dot(refs[d][3][...].astype(BF16), wa_ref[d]) + ba_ref[d:d + 1, :]
        g_alls.append(_log_sigmoid(z) * (1.0 / GLA_TAU))
        tri = (ci >= ri) if d == 1 else (ci <= ri)
        tris.append(tri)
        tri_bs.append(jnp.where(tri, 1.0, 0.0).astype(BF16))

    def chunk(d, hh, c):
        q_ref, k_ref, v_ref, _, o_ref = refs[d]
        rev = d == 1
        rows = slice(c * C, (c + 1) * C)
        kc = slice(hh * GLA_DK, (hh + 1) * GLA_DK)
        vc = slice(hh * GLA_DV, (hh + 1) * GLA_DV)
        g = g_alls[d][rows, kc]
        g_hi = g.astype(BF16)
        g_lo = (g - g_hi.astype(F32)).astype(BF16)
        b = _dot(tri_bs[d], g_hi) + _dot(tri_bs[d], g_lo)
        b_tot = b[0:1, :] if rev else b[C - 1:C, :]
        q = q_ref[rows, kc] * (GLA_DK ** -0.5)
        k = k_ref[rows, kc]
        v = v_ref[rows, vc].astype(BF16)
        q_d = (q * jnp.exp(b)).astype(BF16)
        k_d = (k * jnp.exp(-b)).astype(BF16)
        k_s = (k * jnp.exp(b_tot - b)).astype(BF16)
        att = lax.dot_general(q_d, k_d, NT_DIMS, preferred_element_type=F32)
        att = jnp.where(tris[d], att, 0.0).astype(BF16)
        s_t = st_ref[d, hh]
        o = _dot(att, v) + lax.dot_general(q_d, s_t.astype(BF16), NT_DIMS, preferred_element_type=F32)
        o_ref[rows, vc] = o
        st_ref[d, hh] = s_t * jnp.exp(b_tot) + lax.dot_general(v, k_s, TN_DIMS, preferred_element_type=F32)

    n_chunks = lt // C
    for c in range(n_chunks):
        for hh in range(GLA_HPS):
            chunk(0, hh, c)
            chunk(1, hh, n_chunks - 1 - c)


def gla_scan(p, wa, ba, *, ctx_len, lt=SEQ_TILE):
    T = p.shape[0]
    n_tiles = T // lt
    n_ctx = ctx_len // lt
    tf = functools.partial(_time_tile, n_tiles=n_tiles, n_ctx_tiles=n_ctx, rev=False)
    tb = functools.partial(_time_tile, n_tiles=n_tiles, n_ctx_tiles=n_ctx, rev=True)
    dk2, dv2 = GLA_HPS * GLA_DK, GLA_HPS * GLA_DV

    def stream(tt):
        return [
            pl.BlockSpec((lt, dk2), lambda h, t: (tt(t), OFF_Q // dk2 + h)),
            pl.BlockSpec((lt, dk2), lambda h, t: (tt(t), OFF_K // dk2 + h)),
            pl.BlockSpec((lt, dv2), lambda h, t: (tt(t), OFF_V // dv2 + h)),
            pl.BlockSpec((lt, LANE), lambda h, t: (tt(t), OFF_ALPHA // LANE)),
        ]

    out = jax.ShapeDtypeStruct((T, D_GLA_V), F32)
    return pl.pallas_call(
        functools.partial(_gla_kernel, lt=lt),
        grid=(GLA_HEADS // GLA_HPS, n_tiles),
        in_specs=stream(tf) + stream(tb) + [
            pl.BlockSpec((2, LANE, dk2), lambda h, t: (0, 0, h)),
            pl.BlockSpec((2, dk2), lambda h, t: (0, h)),
        ],
        out_specs=[pl.BlockSpec((lt, dv2), lambda h, t: (tf(t), h)),
                   pl.BlockSpec((lt, dv2), lambda h, t: (tb(t), h))],
        out_shape=[out, out],
        scratch_shapes=[pltpu.VMEM((2, GLA_HPS, GLA_DV, GLA_DK), F32)],
        compiler_params=_cparams(("arbitrary", "arbitrary")),
        name="gla_scan",
    )(p, p, p, p, p, p, p, p, wa, ba)


def _finish_kernel(u_ref, ys_f_ref, ys_b_ref, d_ref, wglu_ref, bglu_ref,
                   og_f_ref, og_b_ref, gate_ref, gn_ref,
                   x0_ref, conv_ref, s_ref, hb_ref, o_ref):
    u = u_ref[...]
    y = ys_f_ref[...] + ys_b_ref[...] + d_ref[...] * u
    zz = _gelu(y)
    s5 = zz * _sigmoid(_dot(zz.astype(BF16), wglu_ref[...]) + bglu_ref[...])
    o_ref[:, 0:D_S5] = s5.astype(BF16)

    gate = gate_ref[...]
    for h in range(GLA_HEADS):
        cols = slice(h * GLA_DV, (h + 1) * GLA_DV)
        o = og_f_ref[:, cols] + og_b_ref[:, cols]
        ms = jnp.mean(o * o, axis=-1, keepdims=True)
        on = o * lax.rsqrt(ms + EPS) * gn_ref[...]
        gt = gate[:, cols]
        o_ref[:, D_S5 + h * GLA_DV:D_S5 + (h + 1) * GLA_DV] = (on * (gt * _sigmoid(gt))).astype(BF16)

    s = s_ref[...]
    hy = x0_ref[...] * (conv_ref[...] + hb_ref[...] * s)
    o_ref[:, D_S5 + D_GLA_V:] = hy.astype(BF16)


def mixer_finish(p, ys_f, ys_b, s5_d, w_glu, b_glu, og_f, og_b, gn, x0, conv, s, hy_bias, *, tm):
    T = p.shape[0]
    row = lambda w, off=0: pl.BlockSpec((tm, w), lambda i: (i, off))
    const = lambda r, w: pl.BlockSpec((r, w), lambda i: (0, 0))
    return pl.pallas_call(
        _finish_kernel,
        grid=(T // tm,),
        in_specs=[
            row(D_S5, OFF_S5 // D_S5), row(D_S5), row(D_S5), const(1, D_S5), const(D_S5, D_S5), const(1, D_S5),
            row(D_GLA_V), row(D_GLA_V), row(D_GLA_V, OFF_GATE // D_GLA_V), const(1, GLA_DV),
            row(D_HY), row(D_HY), row(D_HY), const(1, D_HY),
        ],
        out_specs=pl.BlockSpec((tm, D_MODEL), lambda i: (i, 0)),
        out_shape=jax.ShapeDtypeStruct((T, D_MODEL), BF16),
        compiler_params=_cparams(("arbitrary",)),
        name="mixer_finish",
    )(p, ys_f, ys_b, s5_d.reshape(1, D_S5), w_glu, b_glu.reshape(1, D_S5),
      og_f, og_b, p, gn.reshape(1, GLA_DV), x0, conv, s, hy_bias.reshape(1, D_HY))


def _peer_kernel(x_ref, u_ref, vt_ref, s2_ref, e2_ref, s1_ref, e1_ref, tau_ref, o_ref,
                 act_ref, p_ref, *, n_i1, tm, cw):
    j = pl.program_id(1)

    @pl.when(j == 0)
    def _():
        o_ref[...] = jnp.zeros_like(o_ref)

    act_ref[...] = lax.dot_general(u_ref[...], x_ref[...], NT_DIMS, preferred_element_type=F32)
    K = PEER_KEYS
    sub = SUBLANE // n_i1
    base = (j % sub) * n_i1 if sub > 1 else 0
    for a in range(n_i1):
        r1 = pl.ds(base + a, 1)
        for cb in range(tm // cw):
            cols = slice(cb * cw, (cb + 1) * cw)
            w = jnp.zeros((K, cw), F32)
            for h in range(PEER_HEADS):
                tot = s1_ref[h, r1, cols] + s2_ref[h, :, cols]
                val = e1_ref[h, r1, cols] * e2_ref[h, :, cols]
                w = w + jnp.where(tot >= tau_ref[h, :, cols], val, 0.0)
            act = act_ref[a * K:(a + 1) * K, cols]
            p_ref[a * K:(a + 1) * K, cols] = (w * _gelu(act)).astype(BF16)
    o_ref[...] += lax.dot_general(vt_ref[...], p_ref[...], TN_DIMS, preferred_element_type=F32)


def peer_dense(xn, u_tab, vt_tab, s1t, e1t, s2t, e2t, taut, *, tm, te, cw=256):
    T, D = xn.shape
    E = u_tab.shape[0]
    H, K = PEER_HEADS, PEER_KEYS
    n_i1 = te // K
    sub = SUBLANE // n_i1
    i1_spec = pl.BlockSpec((H, SUBLANE, tm), lambda i, j: (0, j // sub, i))
    once = pl.Buffered(1)
    tab_spec = pl.BlockSpec((H, K, tm), lambda i, j: (0, 0, i), pipeline_mode=once)
    return pl.pallas_call(
        functools.partial(_peer_kernel, n_i1=n_i1, tm=tm, cw=cw),
        grid=(T // tm, E // te),
        in_specs=[
            pl.BlockSpec((tm, D), lambda i, j: (i, 0), pipeline_mode=once),
            pl.BlockSpec((te, D), lambda i, j: (j, 0)),
            pl.BlockSpec((te, D), lambda i, j: (j, 0)),
            tab_spec, tab_spec,
            i1_spec, i1_spec,
            pl.BlockSpec((H, 1, tm), lambda i, j: (0, 0, i), pipeline_mode=once),
        ],
        out_specs=pl.BlockSpec((D, tm), lambda i, j: (0, i)),
        out_shape=jax.ShapeDtypeStruct((D, T), F32),
        scratch_shapes=[pltpu.VMEM((te, tm), F32), pltpu.VMEM((te, tm), BF16)],
        compiler_params=_cparams(("arbitrary", "arbitrary")),
        name="peer_dense",
    )(xn, u_tab, vt_tab, s2t, e2t, s1t, e1t, taut)


NEG_BIG = -3.0e38
N_CAND = PEER_TOPK + 8 * 7 + 8


def _peer_prep_kernel(x_ref, g_ref, mod_ref, wq_ref, k_ref, xo_ref, s1_ref, e1_ref, s2_ref, e2_ref, tau_ref,
                      xn_ref, top_ref, cand_ref, *, ctx_len, tm):
    R = PEER_TOPK
    half_w = PEER_DQ // 2
    i = pl.program_id(0)

    @pl.when(pl.program_id(1) == 0)
    def _():
        xn = _norm_modulate(x_ref[...], g_ref[...], mod_ref, i * tm, ctx_len)
        xn_ref[...] = xn
        xo_ref[...] = xn

    q = _dot(xn_ref[...], wq_ref[...])

    def nt(a, b):
        return lax.dot_general(a, b, NT_DIMS, preferred_element_type=F32)

    def scores(half):
        qh = q[:, half * half_w:(half + 1) * half_w]
        q_hi = qh.astype(BF16)
        q_lo = (qh - q_hi.astype(F32)).astype(BF16)
        kk = k_ref[half]
        k_hi = kk.astype(BF16)
        k_lo = (kk - k_hi.astype(F32)).astype(BF16)
        return nt(k_hi, q_hi) + (nt(k_hi, q_lo) + nt(k_lo, q_hi))

    def sorted_top(s, slot):
        work = s
        for r in range(R):
            m = jnp.max(work, axis=0, keepdims=True)
            top_ref[slot, r:r + 1, :] = m
            work = jnp.where(work == m, NEG_BIG, work)

    s1 = scores(0)
    s2 = scores(1)
    sorted_top(s1, 0)
    sorted_top(s2, 1)
    a = top_ref[0]
    b = top_ref[1]
    cand_ref[0:R, :] = a[0:1, :] + b
    for i in range(1, 8):
        cand_ref[R + 8 * (i - 1):R + 8 * i, :] = a[i:i + 1, :] + b[0:8, :]
    cand_ref[R + 56:R + 64, :] = a[8:16, :] + b[0:1, :]
    work = cand_ref[...]
    m0 = jnp.max(work, axis=0, keepdims=True)
    m = m0
    z = jnp.ones_like(m0)
    for r in range(1, R):
        work = jnp.where(work == m, NEG_BIG, work)
        m = jnp.max(work, axis=0, keepdims=True)
        z = z + jnp.exp(m - m0)
    tau_ref[0] = m
    s1_ref[0] = s1
    s2_ref[0] = s2
    e1_ref[0] = jnp.exp(s1 - a[0:1, :]) * (1.0 / z)
    e2_ref[0] = jnp.exp(s2 - b[0:1, :])


def peer_prep(x, g, mod, w_q, keys, *, ctx_len, tm):
    T, D = x.shape
    H, K = PEER_HEADS, PEER_KEYS
    tab = jax.ShapeDtypeStruct((H, K, T), F32)
    tab_spec = pl.BlockSpec((1, K, tm), lambda i, h: (h, 0, i))
    return pl.pallas_call(
        functools.partial(_peer_prep_kernel, ctx_len=ctx_len, tm=tm),
        grid=(T // tm, H),
        in_specs=[
            pl.BlockSpec((tm, D), lambda i, h: (i, 0)),
            pl.BlockSpec((1, D), lambda i, h: (0, 0)),
            pl.BlockSpec((4, D), lambda i, h: (0, 0)),
            pl.BlockSpec((D, PEER_DQ), lambda i, h: (0, h)),
            pl.BlockSpec((None, 2, K, PEER_DQ // 2), lambda i, h: (h, 0, 0, 0)),
        ],
        out_specs=[pl.BlockSpec((tm, D), lambda i, h: (i, 0)),
                   tab_spec, tab_spec, tab_spec, tab_spec, pl.BlockSpec((1, 1, tm), lambda i, h: (h, 0, i))],
        out_shape=[jax.ShapeDtypeStruct((T, D), BF16), tab, tab, tab, tab, jax.ShapeDtypeStruct((H, 1, T), F32)],
        scratch_shapes=[pltpu.VMEM((tm, D), BF16), pltpu.VMEM((2, PEER_TOPK, tm), F32),
                        pltpu.VMEM((N_CAND, tm), F32)],
        compiler_params=_cparams(("arbitrary", "arbitrary")),
        name="peer_prep",
    )(x, g.reshape(1, D), mod, w_q, keys)


def _transpose_gated_res_kernel(h_ref, ft_ref, gate_ref, o_ref, *, ctx_len, tm):
    i = pl.program_id(0)
    row = i * tm + lax.broadcasted_iota(jnp.int32, (tm, 1), 0)
    gate = jnp.where(row < ctx_len, gate_ref[0:1, :], gate_ref[1:2, :])
    o_ref[...] = h_ref[...] + gate * ft_ref[...].T


def transpose_gated_res(h, ft, gates, *, ctx_len, tm):
    T, D = h.shape
    return pl.pallas_call(
        functools.partial(_transpose_gated_res_kernel, ctx_len=ctx_len, tm=tm),
        grid=(T // tm,),
        in_specs=[
            pl.BlockSpec((tm, D), lambda i: (i, 0)),
            pl.BlockSpec((D, tm), lambda i: (0, i)),
            pl.BlockSpec((2, D), lambda i: (0, 0)),
        ],
        out_specs=pl.BlockSpec((tm, D), lambda i: (i, 0)),
        out_shape=jax.ShapeDtypeStruct((T, D), F32),
        compiler_params=_cparams(("arbitrary",)),
        name="transpose_gated_res",
    )(h, ft, gates)


def _rmsnorm_kernel(x_ref, g_ref, o_ref):
    x = x_ref[...]
    ms = jnp.mean(x * x, axis=-1, keepdims=True)
    o_ref[...] = x * lax.rsqrt(ms + EPS) * g_ref[...]


def rmsnorm_rows(x, g, *, tm):
    T, D = x.shape
    return pl.pallas_call(
        _rmsnorm_kernel,
        grid=(T // tm,),
        in_specs=[pl.BlockSpec((tm, D), lambda i: (i, 0)), pl.BlockSpec((1, D), lambda i: (0, 0))],
        out_specs=pl.BlockSpec((tm, D), lambda i: (i, 0)),
        out_shape=jax.ShapeDtypeStruct((T, D), F32),
        compiler_params=_cparams(("arbitrary",)),
        name="rmsnorm_rows",
    )(x, g.reshape(1, D))


def grid_sincos(n_tokens, dim):
    rows = n_tokens // GRID_W
    n_freq = dim // 4
    omega = 1.0 / (POS_BASE ** (jnp.arange(n_freq, dtype=F32) / n_freq))

    def enc(count):
        a = jnp.arange(count).astype(F32)[:, None] * omega[None, :]
        return jnp.concatenate([jnp.sin(a), jnp.cos(a)], axis=-1)

    row_code = jnp.broadcast_to(enc(rows)[:, None, :], (rows, GRID_W, dim // 2))
    col_code = jnp.broadcast_to(enc(GRID_W)[None, :, :], (rows, GRID_W, dim // 2))
    return jnp.concatenate([row_code, col_code], axis=-1).reshape(n_tokens, dim)


def _split_bf16(a):
    hi = a.astype(BF16)
    return hi, (a - hi.astype(F32)).astype(BF16)


def _dot3s(a, b_hi, b_lo):
    a_hi, a_lo = _split_bf16(a)
    return _dot(a_hi, b_hi) + (_dot(a_hi, b_lo) + _dot(a_lo, b_hi))


def _dot3(a, b):
    return _dot3s(a, *_split_bf16(b))


def _adaln_kernel(c_ref, w_ref, b_ref, o_ref):
    c = c_ref[...]
    o_ref[...] = _dot3(c * _sigmoid(c), w_ref[...]) + b_ref[...]


def adaln(cond, w, b, *, tn=1536):
    R, D = cond.shape
    N = w.shape[1]
    return pl.pallas_call(
        _adaln_kernel,
        grid=(N // tn,),
        in_specs=[pl.BlockSpec((R, D), lambda j: (0, 0)),
                  pl.BlockSpec((D, tn), lambda j: (0, j)),
                  pl.BlockSpec((1, tn), lambda j: (0, j))],
        out_specs=pl.BlockSpec((R, tn), lambda j: (0, j)),
        out_shape=jax.ShapeDtypeStruct((R, N), F32),
        compiler_params=_cparams(("arbitrary",)),
        name="adaln",
    )(cond, w, b.reshape(1, N))


def _hyena_pre_kernel(x0_ref, x1_ref, v_ref, w_ref, b_ref, x0o_ref, s_ref, *, ctx_len):
    T = x0_ref.shape[0]
    row = lax.broadcasted_iota(jnp.int32, (T, 1), 0)
    first = jnp.logical_or(row == 0, row == ctx_len)
    last = jnp.logical_or(row == ctx_len - 1, row == T - 1)

    def conv(ref, part):
        x = ref[...]
        prev = jnp.where(first, 0.0, pltpu.roll(x, 1, 0))
        nxt = jnp.where(last, 0.0, pltpu.roll(x, T - 1, 0))
        w = lambda tap: w_ref[tap, part:part + 1, :]
        return w(0) * prev + w(1) * x + w(2) * nxt + b_ref[part:part + 1, :]

    x0o_ref[...] = conv(x0_ref, 0)
    s_ref[...] = conv(x1_ref, 1) * conv(v_ref, 2)


def hyena_pre(p, conv_w, conv_b, *, ctx_len):
    T = p.shape[0]
    nb = D_HY // LANE
    col = lambda part: pl.BlockSpec((T, LANE), lambda c: (0, OFF_HY // LANE + part * nb + c))
    out = jax.ShapeDtypeStruct((T, D_HY), F32)
    return pl.pallas_call(
        functools.partial(_hyena_pre_kernel, ctx_len=ctx_len),
        grid=(nb,),
        in_specs=[col(0), col(1), col(2),
                  pl.BlockSpec((HY_SHORT, 3, LANE), lambda c: (0, 0, c)),
                  pl.BlockSpec((3, LANE), lambda c: (0, c))],
        out_specs=[pl.BlockSpec((T, LANE), lambda c: (0, c))] * 2,
        out_shape=[out, out],
        compiler_params=_cparams(("arbitrary",)),
        name="hyena_pre",
    )(p, p, p, conv_w.reshape(HY_SHORT, 3, D_HY), conv_b.reshape(3, D_HY))


def _hyena_filter_kernel(band_ref, w1_ref, b1_ref, w2_ref, b2_ref, w3_ref, fr_ref, dec_ref, o_ref, *, n, tr):
    i = pl.program_id(0)
    ri = (i * tr + lax.broadcasted_iota(jnp.int32, (tr, 1), 0)).astype(F32)
    t = ri * (1.0 / (n - 1))
    w = ri * (2.0 * math.pi / n)
    lane = lax.broadcasted_iota(jnp.int32, (1, LANE), 1)
    arg = w * band_ref[...]
    z = jnp.where(lane == 0, t,
                  jnp.where(lane <= HY_BANDS, jnp.cos(arg),
                            jnp.where(lane <= 2 * HY_BANDS, -jnp.sin(arg), 0.0)))
    h = jnp.sin(fr_ref[0:1, :] * (_dot3(z, w1_ref[...]) + b1_ref[...]))
    h = jnp.sin(fr_ref[1:2, :] * (_dot3(h, w2_ref[...]) + b2_ref[...]))
    h = _dot3(h, w3_ref[...]) * jnp.exp(-t * dec_ref[...])
    col = lax.broadcasted_iota(jnp.int32, (1, 2 * D_HY), 1)
    o_ref[...] = jnp.where(jnp.logical_and(ri == 0.0, col >= D_HY), 0.0, h)


def hyena_filter(n, w1, b1, w2, b2, w3, freq, decay, *, tr=256):
    bands = jnp.linspace(1e-4, HY_BANDS - 1, HY_BANDS, dtype=F32)
    band_row = jnp.zeros((1, LANE), F32).at[0, 1:1 + HY_BANDS].set(bands).at[0, 1 + HY_BANDS:1 + 2 * HY_BANDS].set(bands)
    w1p = jnp.zeros((LANE, HY_HIDDEN), F32).at[:HY_FEAT].set(w1)
    const = lambda a: pl.BlockSpec(a.shape, lambda i: (0,) * a.ndim)
    args = (band_row, w1p, b1.reshape(1, -1), w2, b2.reshape(1, -1), w3, freq,
            jnp.abs(decay).reshape(1, 2 * D_HY))
    return pl.pallas_call(
        functools.partial(_hyena_filter_kernel, n=n, tr=tr),
        grid=(n // tr,),
        in_specs=[const(a) for a in args],
        out_specs=pl.BlockSpec((tr, 2 * D_HY), lambda i: (i, 0)),
        out_shape=jax.ShapeDtypeStruct((n, 2 * D_HY), F32),
        compiler_params=_cparams(("arbitrary",)),
        name="hyena_filter",
    )(*args)


FFT_R = 128
FFT_N = FFT_R * FFT_R
FFT_CH = 32
FFT_PITCH = FFT_R + 4


def _dft_tables():
    r = np.arange(FFT_R)
    ang = 2.0 * np.pi * np.outer(r, r) / FFT_R
    c, s = np.cos(ang), np.sin(ang)
    angt = 2.0 * np.pi * np.outer(r, r) / FFT_N
    ct, st = np.cos(angt), np.sin(angt)

    def split(m):
        m = jnp.asarray(m, F32)
        hi = m.astype(BF16)
        return jnp.stack([hi, (m - hi.astype(F32)).astype(BF16)])

    fwd_a = split(np.concatenate([c, -s], axis=1))
    fwd_b = split(np.block([[c, -s], [s, c]]))
    inv_b = split(np.block([[c, s], [-s, c]]) / FFT_N)
    inv_a = split(np.concatenate([c, -s], axis=0))
    tw = (jnp.asarray(ct, F32), jnp.asarray(-st, F32))
    twc = (jnp.asarray(ct, F32), jnp.asarray(st, F32))
    return fwd_a, fwd_b, inv_b, inv_a, tw, twc


def _twiddle_transpose(y, twr, twi, dst_ref, nc):
    y = y.reshape(nc, FFT_R, 2 * FFT_R)
    yr, yi = y[:, :, :FFT_R], y[:, :, FFT_R:]
    zr = yr * twr - yi * twi
    zi = yr * twi + yi * twr
    for c in range(nc):
        dst_ref[c * FFT_R:(c + 1) * FFT_R, 0:FFT_R] = zr[c].T
        dst_ref[c * FFT_R:(c + 1) * FFT_R, FFT_R:2 * FFT_R] = zi[c].T


def _hyena_fwd_kernel(x_ref, fa_ref, twr_ref, twi_ref, fb_ref, o_ref, l_ref, yt_ref, *, nc):
    j = pl.program_id(1)
    half = FFT_R // 2

    @pl.when(j == 0)
    def _():
        zeros = jnp.zeros((half, LANE), F32)

        def body(b, carry):
            t = jnp.concatenate([x_ref[pl.ds(b, half, stride=FFT_R), :], zeros], axis=0)
            l_ref[pl.ds(b, LANE, stride=FFT_PITCH), :] = t.T
            return carry

        lax.fori_loop(0, FFT_R, body, 0, unroll=8)

    lc = jnp.concatenate(
        [l_ref[pl.ds(pl.multiple_of((j * nc + c) * FFT_PITCH, 4), FFT_R), :] for c in range(nc)], axis=0)
    y = _dot3s(lc, fa_ref[0], fa_ref[1])
    _twiddle_transpose(y, twr_ref[...], twi_ref[...], yt_ref, nc)
    o_ref[...] = _dot3s(yt_ref[...], fb_ref[0], fb_ref[1])


def hyena_fwd_dft(x, tables):
    n, C = x.shape
    assert n * 2 == FFT_N and C % LANE == 0
    fwd_a, fwd_b, _, _, (twr, twi), _ = tables
    nc = FFT_CH
    steps = LANE // nc
    const = lambda a: pl.BlockSpec(a.shape, lambda cb, j: (0,) * a.ndim)
    return pl.pallas_call(
        functools.partial(_hyena_fwd_kernel, nc=nc),
        grid=(C // LANE, steps),
        in_specs=[pl.BlockSpec((n, LANE), lambda cb, j: (0, cb)), const(fwd_a), const(twr), const(twi), const(fwd_b)],
        out_specs=pl.BlockSpec((nc * FFT_R, 2 * FFT_R), lambda cb, j: (cb * steps + j, 0)),
        out_shape=jax.ShapeDtypeStruct((C * FFT_R, 2 * FFT_R), F32),
        scratch_shapes=[pltpu.VMEM((LANE * FFT_PITCH, LANE), F32), pltpu.VMEM((nc * FFT_R, 2 * FFT_R), F32)],
        compiler_params=_cparams(("arbitrary", "arbitrary")),
        name="hyena_fwd_dft",
    )(x, fwd_a, twr, twi, fwd_b)


def _hyena_inv_kernel(s_ref, kf_ref, kb_ref, gb_ref, twr_ref, twi_ref, ga_ref, o_ref, lr_ref, at_ref, *, nc, steps):
    j = pl.program_id(1)
    R = FFT_R
    s, kf, kb = s_ref[...], kf_ref[...], kb_ref[...]
    sr, si = s[:, :R], s[:, R:]
    kr = kf[:, :R] + kb[:, :R]
    ki = kf[:, R:] - kb[:, R:]
    p = jnp.concatenate([sr * kr - si * ki, sr * ki + si * kr], axis=1)
    a = _dot3s(p, gb_ref[0], gb_ref[1])
    _twiddle_transpose(a, twr_ref[...], twi_ref[...], at_ref, nc)
    r = _dot3s(at_ref[...], ga_ref[0], ga_ref[1])
    for c in range(nc):
        lr_ref[pl.ds(pl.multiple_of((j * nc + c) * FFT_PITCH, 4), R), :] = r[c * R:(c + 1) * R, :]

    @pl.when(j == steps - 1)
    def _():
        def body(b, carry):
            t = lr_ref[pl.ds(b, LANE, stride=FFT_PITCH), :]
            o_ref[pl.ds(b, R // 2, stride=R), :] = t.T[0:R // 2, :]
            return carry

        lax.fori_loop(0, R, body, 0, unroll=8)


def hyena_inv_dft(s_f, k_f, tables):
    C = s_f.shape[0] // FFT_R
    _, _, inv_b, inv_a, _, (twr, twi) = tables
    nc = FFT_CH
    steps = LANE // nc
    kb_off = C // nc
    const = lambda a: pl.BlockSpec(a.shape, lambda cb, j: (0,) * a.ndim)
    blk = lambda off: pl.BlockSpec((nc * FFT_R, 2 * FFT_R), lambda cb, j: (off + cb * steps + j, 0))
    return pl.pallas_call(
        functools.partial(_hyena_inv_kernel, nc=nc, steps=steps),
        grid=(C // LANE, steps),
        in_specs=[blk(0), blk(0), blk(kb_off), const(inv_b), const(twr), const(twi), const(inv_a)],
        out_specs=pl.BlockSpec((FFT_N // 2, LANE), lambda cb, j: (0, cb)),
        out_shape=jax.ShapeDtypeStruct((FFT_N // 2, C), F32),
        scratch_shapes=[pltpu.VMEM((LANE * FFT_PITCH, LANE), F32), pltpu.VMEM((nc * FFT_R, 2 * FFT_R), F32)],
        compiler_params=_cparams(("arbitrary", "arbitrary")),
        name="hyena_inv_dft",
    )(s_f, k_f, k_f, inv_b, twr, twi, inv_a)


def _hyena_ctx_kernel(s_ref, k_ref, fw_ref, iv_ref, o_ref):
    n = s_ref.shape[0]
    N = 2 * n
    xs = _dot3(fw_ref[...], s_ref[...])
    xk = _dot3(fw_ref[...], k_ref[...])
    sr, si = xs[:N], xs[N:]
    kr = xk[:N, :D_HY] + xk[:N, D_HY:]
    ki = xk[N:, :D_HY] - xk[N:, D_HY:]
    p = jnp.concatenate([sr * kr - si * ki, sr * ki + si * kr], axis=0)
    o_ref[...] = _dot3(iv_ref[...], p)


def hyena_ctx_conv(s_c, k_c):
    n = s_c.shape[0]
    N = 2 * n
    ang = 2.0 * np.pi * np.outer(np.arange(N), np.arange(N)) / N
    c, s = np.cos(ang), np.sin(ang)
    fw = jnp.asarray(np.concatenate([c[:, :n], -s[:, :n]], axis=0), F32)
    iv = jnp.asarray(np.concatenate([c[:n, :], -s[:n, :]], axis=1) / N, F32)
    full = lambda a: pl.BlockSpec(a.shape, lambda i: (0,) * a.ndim)
    return pl.pallas_call(
        _hyena_ctx_kernel,
        grid=(1,),
        in_specs=[full(s_c), full(k_c), full(fw), full(iv)],
        out_specs=pl.BlockSpec((n, D_HY), lambda i: (0, 0)),
        out_shape=jax.ShapeDtypeStruct((n, D_HY), F32),
        compiler_params=_cparams(("arbitrary",)),
        name="hyena_ctx_conv",
    )(s_c, k_c, fw, iv)


def hyena_mixer_parts(p, conv_w, conv_b, filt, tables, *, ctx_len):
    T = p.shape[0]
    x0, s = hyena_pre(p, conv_w, conv_b, ctx_len=ctx_len)
    k_lat = hyena_filter(T - ctx_len, *filt)
    k_ctx = hyena_filter(ctx_len, *filt)
    conv_l = hyena_inv_dft(hyena_fwd_dft(s[ctx_len:], tables), hyena_fwd_dft(k_lat, tables), tables)
    conv_c = hyena_ctx_conv(s[:ctx_len], k_ctx)
    return x0, jnp.concatenate([conv_c, conv_l], axis=0), s


def _pack_w_in(w_in_l):
    widths = (D_S5, D_GLA_K, D_GLA_K, D_GLA_V, D_GLA_V, 2 * GLA_RANK, 3 * D_HY)
    offs = [0]
    for wd in widths:
        offs.append(offs[-1] + wd)
    s5, q, k, v, gate, alpha, hy = (w_in_l[:, offs[i]:offs[i + 1]] for i in range(7))
    pad = jnp.zeros((w_in_l.shape[0], IN_PACKED - OFF_ALPHA - 2 * GLA_RANK), w_in_l.dtype)
    return jnp.concatenate([v, gate, s5, q, k, hy, alpha, pad], axis=1).astype(BF16)


def kernel(x, c, ctx, c_ctx, w_ada, b_ada, g_norm1, g_norm2, w_in, s5_a_re, s5_a_im, s5_log_step, s5_b_re, s5_b_im, s5_c_re, s5_c_im, s5_d, s5_w_glu, s5_b_glu, gla_w_alpha, gla_b_alpha, gla_g_norm, hy_conv_w, hy_conv_b, hy_f_w1, hy_f_b1, hy_f_w2, hy_f_b2, hy_f_w3, hy_f_freq, hy_decay, hy_bias, w_out, peer_w_q, peer_keys, peer_u, peer_v, g_final):
    L = x.shape[1]
    Lc = ctx.shape[1]
    T = L + Lc
    h = jnp.concatenate([ctx[0], x[0] + grid_sincos(L, D_MODEL)], axis=0)
    cond = jnp.zeros((SUBLANE, D_MODEL), F32).at[0].set(c_ctx).at[1].set(c[0])
    tables = _dft_tables()

    for l in range(DEPTH):
        m = adaln(cond, w_ada[l], b_ada[l])[0:2]
        sh1, sc1, gt1, sh2, sc2, gt2 = jnp.split(m, 6, axis=-1)
        mod1 = jnp.stack([sh1[0], sc1[0], sh1[1], sc1[1]], axis=0)
        mod2 = jnp.stack([sh2[0], sc2[0], sh2[1], sc2[1]], axis=0)

        p = norm_mod_matmul(h, g_norm1[l], mod1, _pack_w_in(w_in[l]), ctx_len=Lc, tm=ROW_TILE, tn=IN_PROJ_COLS)

        bm, cm, pw = s5_prepare(s5_a_re[l], s5_a_im[l], s5_log_step[l], s5_b_re[l], s5_b_im[l],
                                s5_c_re[l], s5_c_im[l])
        ys_f, ys_b = s5_scan(p, bm, cm, pw, ctx_len=Lc)

        wa = jnp.zeros((2, LANE, D_GLA_K), F32)
        wa = wa.at[0, 0:GLA_RANK].set(gla_w_alpha[l, 0]).at[1, GLA_RANK:2 * GLA_RANK].set(gla_w_alpha[l, 1])
        wa = wa.astype(BF16)
        og_f, og_b = gla_scan(p, wa, gla_b_alpha[l], ctx_len=Lc)

        filt = (hy_f_w1[l], hy_f_b1[l], hy_f_w2[l], hy_f_b2[l], hy_f_w3[l], hy_f_freq[l], hy_decay[l])
        x0, conv, s = hyena_mixer_parts(p, hy_conv_w[l], hy_conv_b[l], filt, tables, ctx_len=Lc)

        mix = mixer_finish(p, ys_f, ys_b, s5_d[l], s5_w_glu[l].astype(BF16), s5_b_glu[l],
                           og_f, og_b, gla_g_norm[l], x0, conv, s, hy_bias[l], tm=SMALL_ROW_TILE)
        h = matmul_gated_res(mix, w_out[l].astype(BF16), h, gt1, ctx_len=Lc, tm=ROW_TILE, tn=PROJ_COLS)

        xn, s1t, e1t, s2t, e2t, taut = peer_prep(h, g_norm2[l], mod2, peer_w_q[l].astype(BF16), peer_keys[l],
                                                 ctx_len=Lc, tm=ROW_TILE)
        ft = peer_dense(xn, peer_u[l].astype(BF16), peer_v[l].astype(BF16),
                        s1t, e1t, s2t, e2t, taut, tm=ROW_TILE, te=PEER_EXPERT_TILE)
        h = transpose_gated_res(h, ft, gt2, ctx_len=Lc, tm=SMALL_ROW_TILE)

    out = rmsnorm_rows(h[Lc:], g_final, tm=OUT_ROW_TILE)
    return out[None]
```
